```python
import jax
import jax.numpy as jnp
from jax import lax
import numpy as np

D_MODEL = 2048
BATCH = 32
SEQ = 256
DEPTH = 2
DEC_BATCH = 2
DEC_SEQ = 1024
PAST_LEN = 256

GRID_W = 64
NA_HEADS = 4
NA_HEAD_DIM = 128
NA_MAX_KH = 8
NA_KW = 16
NA_SPAN_W = 2 * NA_KW
SW_Q_HEADS = 8
SW_KV_HEADS = 2
SW_GROUP = SW_Q_HEADS // SW_KV_HEADS
SW_HEAD_DIM = 64
SW_WINDOW = 128
SW_BLOCK = 128
ROPE_THETA = 10000.0
POOL_WINDOWS = (2, 4, 8, 16)
POOL_GROUPS = 4
POOL_GROUP_DIM = 128
POOL_DIM = POOL_GROUPS * POOL_GROUP_DIM
A_DIM = NA_HEADS * NA_HEAD_DIM
B_Q_DIM = SW_Q_HEADS * SW_HEAD_DIM
B_KV_DIM = SW_KV_HEADS * SW_HEAD_DIM
N_BRANCH = 3
IN_SPLITS = (A_DIM, 2 * A_DIM, 3 * A_DIM, 3 * A_DIM + B_Q_DIM, 3 * A_DIM + B_Q_DIM + B_KV_DIM, 3 * A_DIM + B_Q_DIM + 2 * B_KV_DIM, 3 * A_DIM + B_Q_DIM + 2 * B_KV_DIM + POOL_DIM)
IN_DIM = 3 * A_DIM + B_Q_DIM + 2 * B_KV_DIM + POOL_DIM + N_BRANCH * D_MODEL
N_EXPERTS = 16
EC_FACTOR = 2
D_EXPERT = 1024
CTX_QBLOCK = 128
RMS_EPS = 1e-6
NEG_INF = -1e30

kernel_name = 'hybrid_diffusion_na_swa_pool_ec_step'


def rmsnorm(x, g):
    xf = x.astype(jnp.float32)
    y = xf * lax.rsqrt(jnp.mean(xf * xf, axis=-1, keepdims=True) + RMS_EPS)
    return (y * g.astype(jnp.float32)).astype(x.dtype)


def adaln(cond, w_ada_l, b_ada_l):
    mod = jax.nn.silu(cond) @ w_ada_l + b_ada_l
    return jnp.split(mod[..., None, :], 6, axis=-1)


def joint_softmax(*logits):
    sizes = [s.shape[-1] for s in logits]
    p = jax.nn.softmax(jnp.concatenate([s.astype(jnp.float32) for s in logits], axis=-1), axis=-1)
    return jnp.split(p, [int(i) for i in np.cumsum(sizes)[:-1]], axis=-1)


def axial_rope(x):
    L, dh = x.shape[1], x.shape[-1]
    nfreq = dh // 4
    inv = 1.0 / (ROPE_THETA ** (jnp.arange(nfreq, dtype=jnp.float32) / nfreq))
    t = jnp.arange(L, dtype=jnp.int32)
    pos = (t // GRID_W, t % GRID_W)
    xf = x.astype(jnp.float32)
    parts = []
    for a in range(2):
        ang = pos[a].astype(jnp.float32)[:, None] * inv[None, :]
        cos = jnp.cos(ang)[None, :, None, :]
        sin = jnp.sin(ang)[None, :, None, :]
        x1 = xf[..., 2 * a * nfreq:(2 * a + 1) * nfreq]
        x2 = xf[..., (2 * a + 1) * nfreq:(2 * a + 2) * nfreq]
        parts += [x1 * cos - x2 * sin, x2 * cos + x1 * sin]
    return jnp.concatenate(parts, axis=-1).astype(x.dtype)


def context_attention(q, k, v, sink):
    Bn, Lc, Hkv, G, Dh = q.shape
    scale = Dh ** -0.5
    qb = q.reshape(Bn, Lc // CTX_QBLOCK, CTX_QBLOCK, Hkv, G, Dh).swapaxes(0, 1)

    def one_block(qi):
        s = jnp.einsum('bqhgd,bkhd->bhgqk', qi, k).astype(jnp.float32) * scale
        if sink is None:
            p = jax.nn.softmax(s, axis=-1)
        else:
            sk = jnp.broadcast_to(sink.astype(jnp.float32)[None, :, :, None, None], s.shape[:-1] + (1,))
            p = joint_softmax(s, sk)[0]
        return jnp.einsum('bhgqk,bkhd->bqhgd', p.astype(v.dtype), v)

    o = lax.map(one_block, qb)
    return o.swapaxes(0, 1).reshape(Bn, Lc, Hkv * G * Dh)


def na_indices(L):
    rows = L // GRID_W
    kh = min(NA_MAX_KH, rows)
    ncb = GRID_W // NA_KW
    nk = kh * NA_SPAN_W
    nb = rows * ncb
    r = np.arange(rows).reshape(rows, 1, 1, 1)
    j = np.arange(ncb).reshape(1, ncb, 1, 1)
    qc = j * NA_KW + np.arange(NA_KW).reshape(1, 1, NA_KW, 1)
    m = np.arange(nk).reshape(1, 1, 1, nk)
    row0 = np.clip(r - kh // 2, 0, rows - kh)
    col0 = np.clip(qc - NA_KW // 2, 0, GRID_W - NA_KW)
    span0 = np.clip(j * NA_KW - NA_KW // 2, 0, GRID_W - NA_SPAN_W)
    kr = row0 + m // NA_SPAN_W
    kc = span0 + m % NA_SPAN_W
    full = (rows, ncb, NA_KW, nk)
    key_tok = (kr * GRID_W + kc).reshape(nb, nk)
    inside = np.broadcast_to((kc >= col0) & (kc < col0 + NA_KW), full).reshape(nb, NA_KW, nk)
    dr = np.broadcast_to(kr - r + NA_MAX_KH - 1, full).reshape(nb, NA_KW, nk)
    dc = np.broadcast_to(np.clip(kc - qc + NA_KW - 1, 0, 2 * NA_KW - 2), full).reshape(nb, NA_KW, nk)
    return key_tok, inside, dr, dc


def na_latent(q, k, v, k_ctx, v_ctx, rpb_l):
    Bn, L, H, Dh = q.shape
    scale = Dh ** -0.5
    key_tok, inside, dr, dc = na_indices(L)
    nb = key_tok.shape[0]
    qb = q.reshape(Bn, nb, NA_KW, H, Dh)
    kb = k[:, key_tok]
    vb = v[:, key_tok]
    bias = rpb_l[:, dr, dc].astype(jnp.float32)
    s_lat = jnp.einsum('bnqhd,bnkhd->bhnqk', qb, kb).astype(jnp.float32) * scale + bias[None]
    s_lat = jnp.where(inside[None, None], s_lat, NEG_INF)
    s_ctx = jnp.einsum('bnqhd,bchd->bhnqc', qb, k_ctx).astype(jnp.float32) * scale
    p_lat, p_ctx = joint_softmax(s_lat, s_ctx)
    o = (jnp.einsum('bhnqk,bnkhd->bnqhd', p_lat.astype(v.dtype), vb)
         + jnp.einsum('bhnqc,bchd->bnqhd', p_ctx.astype(v.dtype), v_ctx))
    return o.reshape(Bn, L, H * Dh)


def sw_latent(q, k, v, k_ctx, v_ctx, sink):
    Bn, L, Hkv, G, Dh = q.shape
    scale = Dh ** -0.5
    nb = L // SW_BLOCK
    qb = q.reshape(Bn, nb, SW_BLOCK, Hkv, G, Dh)
    pad = ((0, 0), (SW_BLOCK, SW_BLOCK), (0, 0), (0, 0))
    kp = jnp.pad(k, pad)
    vp = jnp.pad(v, pad)
    key_idx = np.arange(nb)[:, None] * SW_BLOCK + np.arange(3 * SW_BLOCK)[None, :]
    kb = kp[:, key_idx]
    vb = vp[:, key_idx]
    qpos = np.arange(L).reshape(nb, SW_BLOCK, 1)
    kpos = key_idx[:, None, :] - SW_BLOCK
    band = (kpos >= 0) & (kpos < L) & (np.abs(qpos - kpos) <= SW_WINDOW)
    s_lat = jnp.einsum('bnqhgd,bnkhd->bhgnqk', qb, kb).astype(jnp.float32) * scale
    s_lat = jnp.where(band[None, None, None], s_lat, NEG_INF)
    s_ctx = jnp.einsum('bnqhgd,bchd->bhgnqc', qb, k_ctx).astype(jnp.float32) * scale
    sk = jnp.broadcast_to(sink.astype(jnp.float32)[None, :, :, None, None, None], s_lat.shape[:-1] + (1,))
    p_lat, p_ctx, _ = joint_softmax(s_lat, s_ctx, sk)
    o = (jnp.einsum('bhgnqk,bnkhd->bnqhgd', p_lat.astype(v.dtype), vb)
         + jnp.einsum('bhgnqc,bchd->bnqhgd', p_ctx.astype(v.dtype), v_ctx))
    return o.reshape(Bn, L, Hkv * G * Dh)


def pool_mixer(u, pool_w_l, pool_scale_l):
    Bn, L, _ = u.shape
    cs = jnp.concatenate([jnp.zeros((Bn, 1, POOL_DIM), jnp.float32), jnp.cumsum(u.astype(jnp.float32), axis=1)], axis=1)
    t = np.arange(L)
    outs = []
    for gi, w in enumerate(POOL_WINDOWS):
        lo = np.clip(t - w // 2, 0, L)
        hi = np.clip(t - w // 2 + w, 0, L)
        cnt = (hi - lo).astype(np.float32)
        sl = slice(gi * POOL_GROUP_DIM, (gi + 1) * POOL_GROUP_DIM)
        outs.append((cs[:, hi, sl] - cs[:, lo, sl]) / cnt[None, :, None])
    pooled = jnp.concatenate(outs, axis=-1).astype(u.dtype) - u
    y = jnp.einsum('blgi,gio->blgo', pooled.reshape(Bn, L, POOL_GROUPS, POOL_GROUP_DIM), pool_w_l)
    return y.reshape(Bn, L, POOL_DIM) * pool_scale_l


def project_in(h, w_in_l):
    Bn, L, _ = h.shape
    qa, ka, va, qb, kb, vb, u, g = jnp.split(h @ w_in_l, IN_SPLITS, axis=-1)
    sa = (Bn, L, NA_HEADS, NA_HEAD_DIM)
    return (qa.reshape(sa), ka.reshape(sa), va.reshape(sa),
            qb.reshape(Bn, L, SW_Q_HEADS, SW_HEAD_DIM),
            kb.reshape(Bn, L, SW_KV_HEADS, SW_HEAD_DIM),
            vb.reshape(Bn, L, SW_KV_HEADS, SW_HEAD_DIM), u, g)


def merge_branches(o_a, o_b, o_c, g, wa_l, wb_l, wc_l, wo_l):
    ga, gb, gc = jnp.split(jax.nn.sigmoid(g), N_BRANCH, axis=-1)
    m = ga * (o_a @ wa_l) + gb * (o_b @ wb_l) + gc * (o_c @ wc_l)
    return m @ wo_l


def mixer_context(h, w_in_l, sink_l, pool_w_l, pool_scale_l, wa_l, wb_l, wc_l, wo_l):
    Bn, L, _ = h.shape
    qa, ka, va, qb, kb, vb, u, g = project_in(h, w_in_l)
    o_a = context_attention(qa[:, :, :, None, :], ka, va, None)
    o_b = context_attention(qb.reshape(Bn, L, SW_KV_HEADS, SW_GROUP, SW_HEAD_DIM), kb, vb,
                            sink_l.reshape(SW_KV_HEADS, SW_GROUP))
    o_c = pool_mixer(u, pool_w_l, pool_scale_l)
    return merge_branches(o_a, o_b, o_c, g, wa_l, wb_l, wc_l, wo_l), ka, va, kb, vb


def mixer_latent(h, ka_ctx, va_ctx, kb_ctx, vb_ctx, w_in_l, rpb_l, sink_l, pool_w_l, pool_scale_l, wa_l, wb_l, wc_l, wo_l):
    Bn, L, _ = h.shape
    qa, ka, va, qb, kb, vb, u, g = project_in(h, w_in_l)
    o_a = na_latent(qa, ka, va, ka_ctx, va_ctx, rpb_l)
    qb = axial_rope(qb).reshape(Bn, L, SW_KV_HEADS, SW_GROUP, SW_HEAD_DIM)
    kb = axial_rope(kb)
    o_b = sw_latent(qb, kb, vb, kb_ctx, vb_ctx, sink_l.reshape(SW_KV_HEADS, SW_GROUP))
    o_c = pool_mixer(u, pool_w_l, pool_scale_l)
    return merge_branches(o_a, o_b, o_c, g, wa_l, wb_l, wc_l, wo_l)


def expert_choice_ffn(h, w_router_l, w_gate_l, w_up_l, w_down_l):
    Bn, L, D = h.shape
    cap = EC_FACTOR * L // N_EXPERTS
    aff = jax.nn.softmax((h @ w_router_l).astype(jnp.float32), axis=-1)
    gates, idx = lax.top_k(aff.swapaxes(1, 2), cap)
    xs = jax.vmap(lambda hb, ib: hb[ib])(h, idx)
    hid = jax.nn.silu(jnp.einsum('becd,edf->becf', xs, w_gate_l)) * jnp.einsum('becd,edf->becf', xs, w_up_l)
    ye = jnp.einsum('becf,efd->becd', hid, w_down_l) * gates[..., None].astype(h.dtype)
    return jax.vmap(lambda ib, yb: jnp.zeros((L, D), yb.dtype).at[ib.reshape(-1)].add(yb.reshape(-1, D)))(idx, ye)


def setup_inputs(seed: int = 0) -> dict:
    key = jax.random.key(seed)
    ks = jax.random.split(key, 24)

    def nrm(k, shape, s):
        return jax.random.normal(k, shape, jnp.float32) * s

    return {
        'x_prompt': nrm(ks[0], (BATCH, SEQ, D_MODEL), 1.0),
        'x_sample': nrm(ks[1], (DEC_BATCH, DEC_SEQ, D_MODEL), 1.0),
        'c': nrm(ks[2], (DEC_BATCH, D_MODEL), 1.0),
        'cache_a_k': nrm(ks[3], (DEC_BATCH, DEPTH, PAST_LEN, NA_HEADS, NA_HEAD_DIM), 1.0),
        'cache_a_v': nrm(ks[4], (DEC_BATCH, DEPTH, PAST_LEN, NA_HEADS, NA_HEAD_DIM), 1.0),
        'cache_b_k': nrm(ks[5], (DEC_BATCH, DEPTH, PAST_LEN, SW_KV_HEADS, SW_HEAD_DIM), 1.0),
        'cache_b_v': nrm(ks[6], (DEC_BATCH, DEPTH, PAST_LEN, SW_KV_HEADS, SW_HEAD_DIM), 1.0),
        'c_ctx': nrm(ks[7], (D_MODEL,), 1.0),
        'norm_w': 1.0 + nrm(ks[8], (DEPTH, 4, D_MODEL), 0.05),
        'w_ada': nrm(ks[9], (DEPTH, D_MODEL, 6 * D_MODEL), 0.5 * D_MODEL ** -0.5),
        'b_ada': nrm(ks[10], (DEPTH, 6 * D_MODEL), 0.02),
        'w_in': nrm(ks[11], (DEPTH, D_MODEL, IN_DIM), D_MODEL ** -0.5),
        'a_rpb': nrm(ks[12], (DEPTH, NA_HEADS, 2 * NA_MAX_KH - 1, 2 * NA_KW - 1), 0.1),
        'b_sink': nrm(ks[13], (DEPTH, SW_Q_HEADS), 0.5),
        'c_pool_w': nrm(ks[14], (DEPTH, POOL_GROUPS, POOL_GROUP_DIM, POOL_GROUP_DIM), POOL_GROUP_DIM ** -0.5),
        'c_scale': 1.0 + nrm(ks[15], (DEPTH, POOL_DIM), 0.1),
        'w_branch_a': nrm(ks[16], (DEPTH, A_DIM, D_MODEL), A_DIM ** -0.5),
        'w_branch_b': nrm(ks[17], (DEPTH, B_Q_DIM, D_MODEL), B_Q_DIM ** -0.5),
        'w_branch_c': nrm(ks[18], (DEPTH, POOL_DIM, D_MODEL), POOL_DIM ** -0.5),
        'w_out': nrm(ks[19], (DEPTH, D_MODEL, D_MODEL), D_MODEL ** -0.5),
        'w_router': nrm(ks[20], (DEPTH, D_MODEL, N_EXPERTS), D_MODEL ** -0.5),
        'w_gate_e': nrm(ks[21], (DEPTH, N_EXPERTS, D_MODEL, D_EXPERT), D_MODEL ** -0.5),
        'w_up_e': nrm(ks[22], (DEPTH, N_EXPERTS, D_MODEL, D_EXPERT), D_MODEL ** -0.5),
        'w_down_e': nrm(ks[23], (DEPTH, N_EXPERTS, D_EXPERT, D_MODEL), D_EXPERT ** -0.5),
    }


def reference(x_prompt, x_sample, c, cache_a_k, cache_a_v, cache_b_k, cache_b_v, c_ctx, norm_w, w_ada, b_ada,
              w_in, a_rpb, b_sink, c_pool_w, c_scale, w_branch_a, w_branch_b, w_branch_c, w_out,
              w_router, w_gate_e, w_up_e, w_down_e):
    x = x_prompt
    ak_list, av_list, bk_list, bv_list = [], [], [], []
    for l in range(DEPTH):
        sh1, sc1, g1, sh2, sc2, g2 = adaln(c_ctx, w_ada[l], b_ada[l])
        h = rmsnorm(x, norm_w[l, 0]) * (1.0 + sc1) + sh1
        y, ka, va, kb, vb = mixer_context(h, w_in[l], b_sink[l], c_pool_w[l], c_scale[l],
                                          w_branch_a[l], w_branch_b[l], w_branch_c[l], w_out[l])
        x = x + g1 * rmsnorm(y, norm_w[l, 1])
        h = rmsnorm(x, norm_w[l, 2]) * (1.0 + sc2) + sh2
        x = x + g2 * rmsnorm(expert_choice_ffn(h, w_router[l], w_gate_e[l], w_up_e[l], w_down_e[l]), norm_w[l, 3])
        ak_list.append(ka)
        av_list.append(va)
        bk_list.append(kb)
        bv_list.append(vb)
    y_prompt = x
    new_a_k = jnp.stack(ak_list, axis=1)
    new_a_v = jnp.stack(av_list, axis=1)
    new_b_k = jnp.stack(bk_list, axis=1)
    new_b_v = jnp.stack(bv_list, axis=1)

    x = x_sample
    for l in range(DEPTH):
        sh1, sc1, g1, sh2, sc2, g2 = adaln(c, w_ada[l], b_ada[l])
        h = rmsnorm(x, norm_w[l, 0]) * (1.0 + sc1) + sh1
        y = mixer_latent(h, cache_a_k[:, l], cache_a_v[:, l], cache_b_k[:, l], cache_b_v[:, l],
                         w_in[l], a_rpb[l], b_sink[l], c_pool_w[l], c_scale[l],
                         w_branch_a[l], w_branch_b[l], w_branch_c[l], w_out[l])
        x = x + g1 * rmsnorm(y, norm_w[l, 1])
        h = rmsnorm(x, norm_w[l, 2]) * (1.0 + sc2) + sh2
        x = x + g2 * rmsnorm(expert_choice_ffn(h, w_router[l], w_gate_e[l], w_up_e[l], w_down_e[l]), norm_w[l, 3])
    y_sample = x
    return (y_prompt, y_sample, new_a_k, new_a_v, new_b_k, new_b_v)
```

```python
import functools

import numpy as np
import jax
import jax.numpy as jnp
from jax import lax
from jax.experimental import pallas as pl
from jax.experimental.pallas import tpu as pltpu

F32 = jnp.float32
BF16 = jnp.bfloat16

D_MODEL = 2048
DEPTH = 2
GRID_W = 64
NA_HEADS, NA_HEAD_DIM, NA_MAX_KH, NA_KW = 4, 128, 8, 16
SW_Q_HEADS, SW_KV_HEADS, SW_HEAD_DIM = 8, 2, 64
SW_GROUP = SW_Q_HEADS // SW_KV_HEADS
SW_WINDOW, SW_BLOCK = 128, 128
ROPE_THETA = 10000.0
POOL_WINDOWS = (2, 4, 8, 16)
POOL_GROUPS, POOL_GROUP_DIM = 4, 128
POOL_DIM = POOL_GROUPS * POOL_GROUP_DIM
A_DIM = NA_HEADS * NA_HEAD_DIM
B_Q_DIM = SW_Q_HEADS * SW_HEAD_DIM
B_KV_DIM = SW_KV_HEADS * SW_HEAD_DIM
N_BRANCH = 3
GATE_DIM = N_BRANCH * D_MODEL
QKVU_DIM = 3 * A_DIM + B_Q_DIM + 2 * B_KV_DIM + POOL_DIM
IN_DIM = QKVU_DIM + GATE_DIM
N_EXPERTS = 16
EC_FACTOR = 2
D_EXPERT = 1024
RMS_EPS = 1e-6
NEG_INF = -1e30

COL_G = 0
COL_QA = GATE_DIM // A_DIM
COL_KB = (GATE_DIM + 3 * A_DIM + B_Q_DIM) // B_KV_DIM
COL_U = COL_KB + 2

V7X_VMEM_BYTES = 64 * 1024 * 1024
V7X_VMEM_CEILING = 60000 * 1024
MIB = 1024 * 1024

ADA_ROWS = 16
ADA_TN = 1024
INPROJ_TM = 1024
INPROJ_TN = 896
MERGE_TM = 256
EXPERT_TF = 256
RANK_CHUNK = 256
COMBINE_TR = 256
COMBINE_TD = 512
ROW_CHUNK = 64


def _vmem_limit(estimate_bytes):
    return int(min(V7X_VMEM_CEILING, max(32 * MIB, estimate_bytes + 8 * MIB)))


def _params(semantics, estimate_bytes):
    return pltpu.CompilerParams(dimension_semantics=semantics,
                                vmem_limit_bytes=_vmem_limit(estimate_bytes))


def _rms(x, g):
    ms = jnp.mean(x * x, axis=-1, keepdims=True)
    return x * lax.rsqrt(ms + RMS_EPS) * g


def _dot(a, b):
    return jnp.dot(a, b, preferred_element_type=F32)


def _dot_nt(a, b):
    return lax.dot_general(a, b, (((1,), (1,)), ((), ())), preferred_element_type=F32)


def _adaln_kernel(c_ref, w_ref, b_ref, o_ref):
    c = c_ref[...]
    s = (c * jax.nn.sigmoid(c)).astype(BF16)
    o_ref[0] = _dot(s, w_ref[0].astype(BF16)) + b_ref[0]


def _adaln(cond, w_ada, b_ada):
    n_out = w_ada.shape[-1]
    return pl.pallas_call(
        _adaln_kernel,
        grid=(DEPTH, n_out // ADA_TN),
        in_specs=[pl.BlockSpec((ADA_ROWS, D_MODEL), lambda l, j: (0, 0)),
                  pl.BlockSpec((1, D_MODEL, ADA_TN), lambda l, j: (l, 0, j)),
                  pl.BlockSpec((1, 1, ADA_TN), lambda l, j: (l, 0, j))],
        out_specs=pl.BlockSpec((1, ADA_ROWS, ADA_TN), lambda l, j: (l, 0, j)),
        out_shape=jax.ShapeDtypeStruct((DEPTH, ADA_ROWS, n_out), F32),
        compiler_params=_params(("parallel", "parallel"), 2 * D_MODEL * ADA_TN * 4),
        name="adaln",
    )(cond, w_ada, b_ada.reshape(DEPTH, 1, n_out))


def _inproj_kernel(x_ref, mod_ref, nw_ref, w_ref, o_ref, h_scr):
    @pl.when(pl.program_id(1) == 0)
    def _():
        g = nw_ref[0:1, :]
        sc = 1.0 + mod_ref[1:2, :]
        sh = mod_ref[0:1, :]

        def body(r, carry):
            rows = pl.ds(pl.multiple_of(r * ROW_CHUNK, ROW_CHUNK), ROW_CHUNK)
            h_scr[rows, :] = (_rms(x_ref[rows, :], g) * sc + sh).astype(BF16)
            return carry

        lax.fori_loop(0, INPROJ_TM // ROW_CHUNK, body, 0)

    o_ref[...] = _dot(h_scr[...], w_ref[...])


def _inproj(x, mod_l, nw_l, w_in_l, group_of_tile):
    n = x.shape[0]
    est = (2 * INPROJ_TM * D_MODEL * 4 + 2 * D_MODEL * INPROJ_TN * 2 + 2 * INPROJ_TM * INPROJ_TN * 4
           + INPROJ_TM * D_MODEL * 2)
    return pl.pallas_call(
        _inproj_kernel,
        grid=(n // INPROJ_TM, IN_DIM // INPROJ_TN),
        in_specs=[pl.BlockSpec((INPROJ_TM, D_MODEL), lambda i, j: (i, 0)),
                  pl.BlockSpec((None, 6, D_MODEL), lambda i, j: (group_of_tile(i), 0, 0)),
                  pl.BlockSpec((4, D_MODEL), lambda i, j: (0, 0)),
                  pl.BlockSpec((D_MODEL, INPROJ_TN), lambda i, j: (0, j))],
        out_specs=pl.BlockSpec((INPROJ_TM, INPROJ_TN), lambda i, j: (i, j)),
        out_shape=jax.ShapeDtypeStruct((n, IN_DIM), F32),
        scratch_shapes=[pltpu.VMEM((INPROJ_TM, D_MODEL), BF16)],
        compiler_params=_params(("parallel", "arbitrary"), est),
        name="inproj",
    )(x, mod_l, nw_l, w_in_l)


def _joint_attention(q, segments, scale, sink=None):
    scores = []
    for k, _, bias, mask in segments:
        s = _dot_nt(q, k) * scale
        if bias is not None:
            s = s + bias
        if mask is not None:
            s = jnp.where(mask, s, NEG_INF)
        scores.append(s)
    m = scores[0].max(axis=-1, keepdims=True)
    for s in scores[1:]:
        m = jnp.maximum(m, s.max(axis=-1, keepdims=True))
    if sink is not None:
        m = jnp.maximum(m, sink)
    denom = jnp.exp(sink - m) if sink is not None else 0.0
    acc = None
    for s, (_, v, _, _) in zip(scores, segments):
        e = jnp.exp(s - m)
        denom = denom + e.sum(axis=-1, keepdims=True)
        pv = _dot(e.astype(BF16), v)
        acc = pv if acc is None else acc + pv
    return acc / denom


def _pool_group(u, window, pw_bf, scale_row):
    seq = u.shape[0]
    pad = 8
    n = seq + 2 * pad
    z = jnp.zeros((pad, POOL_GROUP_DIM), F32)
    p = jnp.concatenate([z, u, z], axis=0)
    k = 1
    while k < window:
        p = p + pltpu.roll(p, n - k, 0)
        k *= 2
    win = pltpu.roll(p, window // 2, 0)[pad:pad + seq]
    t = lax.broadcasted_iota(jnp.int32, (seq, 1), 0)
    lo = jnp.maximum(t - window // 2, 0)
    hi = jnp.minimum(t - window // 2 + window, seq)
    cnt = (hi - lo).astype(F32)
    pooled = win / cnt - u
    return _dot(pooled.astype(BF16), pw_bf) * scale_row


def _sink_column(sink_ref, kv_head, rows_per_head):
    r = lax.broadcasted_iota(jnp.int32, (SW_GROUP * rows_per_head, 1), 0)
    col = jnp.full((SW_GROUP * rows_per_head, 1), sink_ref[kv_head * SW_GROUP], F32)
    for g in range(1, SW_GROUP):
        col = jnp.where(r >= g * rows_per_head, sink_ref[kv_head * SW_GROUP + g], col)
    return col


def _ctx_mix_kernel(sink_ref, qa_ref, ka_ref, va_ref, qb_ref, kb_ref, vb_ref, u0_ref, u1_ref, u2_ref, u3_ref,
                    pw_ref, ps_ref, o_ref):
    seq = qa_ref.shape[0]
    for h in range(NA_HEADS):
        sl = slice(h * NA_HEAD_DIM, (h + 1) * NA_HEAD_DIM)
        o = _joint_attention(qa_ref[:, sl].astype(BF16),
                             [(ka_ref[:, sl].astype(BF16), va_ref[:, sl].astype(BF16), None, None)],
                             NA_HEAD_DIM ** -0.5)
        o_ref[:, sl] = o
    for hk in range(SW_KV_HEADS):
        ksl = slice(hk * SW_HEAD_DIM, (hk + 1) * SW_HEAD_DIM)
        q = jnp.concatenate(
            [qb_ref[:, (hk * SW_GROUP + g) * SW_HEAD_DIM:(hk * SW_GROUP + g + 1) * SW_HEAD_DIM] for g in range(SW_GROUP)],
            axis=0).astype(BF16)
        o = _joint_attention(q, [(kb_ref[:, ksl].astype(BF16), vb_ref[:, ksl].astype(BF16), None, None)],
                             SW_HEAD_DIM ** -0.5, sink=_sink_column(sink_ref, hk, seq))
        for g in range(SW_GROUP):
            c0 = A_DIM + (hk * SW_GROUP + g) * SW_HEAD_DIM
            o_ref[:, c0:c0 + SW_HEAD_DIM] = o[g * seq:(g + 1) * seq]
    for gi, u_ref in enumerate((u0_ref, u1_ref, u2_ref, u3_ref)):
        c0 = A_DIM + B_Q_DIM + gi * POOL_GROUP_DIM
        o_ref[:, c0:c0 + POOL_GROUP_DIM] = _pool_group(
            u_ref[...], POOL_WINDOWS[gi], pw_ref[gi].astype(BF16), ps_ref[:, gi * POOL_GROUP_DIM:(gi + 1) * POOL_GROUP_DIM])


def _ctx_mix(p, sink_l, pool_w_l, pool_scale_l, seq):
    n = p.shape[0]
    wide = lambda c: pl.BlockSpec((seq, A_DIM), lambda b, c=c: (b, c))
    narrow = lambda c: pl.BlockSpec((seq, B_KV_DIM), lambda b, c=c: (b, c))
    in_specs = [pl.BlockSpec(memory_space=pltpu.SMEM)]
    in_specs += [wide(COL_QA + i) for i in range(4)]
    in_specs += [narrow(COL_KB), narrow(COL_KB + 1)]
    in_specs += [narrow(COL_U + g) for g in range(POOL_GROUPS)]
    in_specs += [pl.BlockSpec((POOL_GROUPS, POOL_GROUP_DIM, POOL_GROUP_DIM), lambda b: (0, 0, 0)),
                 pl.BlockSpec((1, POOL_DIM), lambda b: (0, 0))]
    qa, ka, va, qb = (p,) * 4
    return pl.pallas_call(
        _ctx_mix_kernel,
        grid=(n // seq,),
        in_specs=in_specs,
        out_specs=pl.BlockSpec((seq, 3 * A_DIM), lambda b: (b, 0)),
        out_shape=jax.ShapeDtypeStruct((n, 3 * A_DIM), F32),
        compiler_params=_params(("parallel",), 16 * MIB),
        name="ctx_mix",
    )(sink_l, qa, ka, va, qb, p, p, p, p, p, p, pool_w_l, pool_scale_l.reshape(1, POOL_DIM))


NA_PAIR_ROWS = 2 * NA_MAX_KH - 2


def _rpb_table_kernel(rpb_ref, t_ref):
    lane = lax.broadcasted_iota(jnp.int32, (GRID_W, 2 * GRID_W), 1)
    qc = lax.broadcasted_iota(jnp.int32, (GRID_W, 2 * GRID_W), 0)
    kc = lane & (GRID_W - 1)
    upper = lane >= GRID_W
    dcm = jnp.clip(kc - qc + NA_KW - 1, 0, 2 * NA_KW - 2)
    col0 = jnp.clip(qc - NA_KW // 2, 0, GRID_W - NA_KW)
    inside = (kc >= col0) & (kc < col0 + NA_KW)
    n_dc = 2 * NA_KW - 1
    n_dr = 2 * NA_MAX_KH - 1

    def body(i, carry):
        h = i // NA_PAIR_ROWS
        dr = i - h * NA_PAIR_ROWS
        base = (h * n_dr + dr) * n_dc
        acc = jnp.zeros((GRID_W, 2 * GRID_W), F32)
        for dc in range(n_dc):
            val = jnp.where(upper, rpb_ref[base + n_dc + dc], rpb_ref[base + dc])
            acc = jnp.where(dcm == dc, val, acc)
        t_ref[i] = jnp.where(inside, acc, NEG_INF)
        return carry

    lax.fori_loop(0, NA_HEADS * NA_PAIR_ROWS, body, 0)


def _rpb_table(rpb_l):
    return pl.pallas_call(
        _rpb_table_kernel,
        in_specs=[pl.BlockSpec(memory_space=pltpu.SMEM)],
        out_specs=pl.BlockSpec(memory_space=pltpu.VMEM),
        out_shape=jax.ShapeDtypeStruct((NA_HEADS * NA_PAIR_ROWS, GRID_W, 2 * GRID_W), F32),
        name="rpb_table",
    )(rpb_l.reshape(-1))


def _na_kernel(q_ref, k_ref, v_ref, kc_ref, vc_ref, t_ref, o_ref, *, rows):
    qr = pl.program_id(1)
    row0 = jnp.clip(qr - NA_MAX_KH // 2, 0, rows - NA_MAX_KH)
    start = pl.multiple_of(row0 * GRID_W, GRID_W)
    nkeys = NA_MAX_KH * GRID_W
    d0 = row0 - qr + NA_MAX_KH - 1
    for h in range(NA_HEADS):
        sl = slice(h * NA_HEAD_DIM, (h + 1) * NA_HEAD_DIM)
        bias = jnp.concatenate([t_ref[h * NA_PAIR_ROWS + d0 + 2 * i] for i in range(NA_MAX_KH // 2)], axis=1)
        o = _joint_attention(
            q_ref[:, sl].astype(BF16),
            [(k_ref[pl.ds(start, nkeys), sl].astype(BF16), v_ref[pl.ds(start, nkeys), sl].astype(BF16), bias, None),
             (kc_ref[:, sl].astype(BF16), vc_ref[:, sl].astype(BF16), None, None)],
            NA_HEAD_DIM ** -0.5)
        o_ref[:, sl] = o


def _na_latent(p, cache_k, cache_v, table, layer, seq):
    n = p.shape[0]
    rows = seq // GRID_W
    past = cache_k.shape[2]
    ctx_spec = pl.BlockSpec((None, None, past, A_DIM), lambda b, r: (b, layer, 0, 0))
    return pl.pallas_call(
        functools.partial(_na_kernel, rows=rows),
        grid=(n // seq, rows),
        in_specs=[pl.BlockSpec((GRID_W, A_DIM), lambda b, r: (b * rows + r, COL_QA)),
                  pl.BlockSpec((seq, A_DIM), lambda b, r: (b, COL_QA + 1)),
                  pl.BlockSpec((seq, A_DIM), lambda b, r: (b, COL_QA + 2)),
                  ctx_spec, ctx_spec,
                  pl.BlockSpec(table.shape, lambda b, r: (0, 0, 0))],
        out_specs=pl.BlockSpec((GRID_W, A_DIM), lambda b, r: (b * rows + r, 0)),
        out_shape=jax.ShapeDtypeStruct((n, A_DIM), F32),
        compiler_params=_params(("parallel", "arbitrary"), 16 * MIB),
        name="na_latent",
    )(p, p, p, cache_k, cache_v, table)


def _rope_tables(seq):
    nfreq = SW_HEAD_DIM // 4
    inv = 1.0 / (ROPE_THETA ** (np.arange(nfreq, dtype=np.float32) / np.float32(nfreq)))
    t = np.arange(seq)
    pos = (t // GRID_W, t % GRID_W)
    cos = np.zeros((seq, SW_HEAD_DIM), np.float32)
    sin_next = np.zeros((seq, SW_HEAD_DIM), np.float32)
    sin_prev = np.zeros((seq, SW_HEAD_DIM), np.float32)
    for a in range(2):
        ang = pos[a].astype(np.float32)[:, None] * inv[None, :].astype(np.float32)
        c, s = np.cos(ang).astype(np.float32), np.sin(ang).astype(np.float32)
        lo = 2 * a * nfreq
        cos[:, lo:lo + nfreq] = c
        cos[:, lo + nfreq:lo + 2 * nfreq] = c
        sin_next[:, lo:lo + nfreq] = -s
        sin_prev[:, lo + nfreq:lo + 2 * nfreq] = s
    tile = lambda x: jnp.asarray(np.tile(x, (1, 128 // SW_HEAD_DIM)))
    return tile(cos), tile(sin_next), tile(sin_prev)


def _rope(x, cos, sin_next, sin_prev):
    nfreq = SW_HEAD_DIM // 4
    return x * cos + pltpu.roll(x, 128 - nfreq, 1) * sin_next + pltpu.roll(x, nfreq, 1) * sin_prev


def _sw_kernel(sink_ref, q_ref, k_ref, v_ref, kc_ref, vc_ref, cq_ref, snq_ref, spq_ref, ck_ref, snk_ref, spk_ref,
               o_ref, kr_scr, *, seq):
    n = pl.program_id(1)

    @pl.when(n == 0)
    def _():
        kr_scr[...] = _rope(k_ref[...], ck_ref[...], snk_ref[...], spk_ref[...]).astype(BF16)

    nwin = 3 * SW_BLOCK
    kstart = pl.multiple_of(jnp.clip((n - 1) * SW_BLOCK, 0, seq - nwin), SW_BLOCK)
    cq, snq, spq = cq_ref[...], snq_ref[...], spq_ref[...]
    q = jnp.concatenate([_rope(q_ref[:, c * 128:(c + 1) * 128], cq, snq, spq) for c in range(B_Q_DIM // 128)],
                        axis=1).astype(BF16)
    rows = SW_GROUP * SW_BLOCK
    qpos = n * SW_BLOCK + (lax.broadcasted_iota(jnp.int32, (rows, nwin), 0) & (SW_BLOCK - 1))
    kpos = kstart + lax.broadcasted_iota(jnp.int32, (rows, nwin), 1)
    band = jnp.abs(qpos - kpos) <= SW_WINDOW
    for hk in range(SW_KV_HEADS):
        ksl = slice(hk * SW_HEAD_DIM, (hk + 1) * SW_HEAD_DIM)
        qs = jnp.concatenate(
            [q[:, (hk * SW_GROUP + g) * SW_HEAD_DIM:(hk * SW_GROUP + g + 1) * SW_HEAD_DIM] for g in range(SW_GROUP)], axis=0)
        o = _joint_attention(
            qs,
            [(kr_scr[pl.ds(kstart, nwin), ksl], v_ref[pl.ds(kstart, nwin), ksl].astype(BF16), None, band),
             (kc_ref[:, ksl].astype(BF16), vc_ref[:, ksl].astype(BF16), None, None)],
            SW_HEAD_DIM ** -0.5, sink=_sink_column(sink_ref, hk, SW_BLOCK))
        for g in range(SW_GROUP):
            c0 = (hk * SW_GROUP + g) * SW_HEAD_DIM
            o_ref[:, c0:c0 + SW_HEAD_DIM] = o[g * SW_BLOCK:(g + 1) * SW_BLOCK]


def _sw_latent(p, cache_k, cache_v, sink_l, layer, seq):
    n = p.shape[0]
    nb = seq // SW_BLOCK
    past = cache_k.shape[2]
    cos, sin_next, sin_prev = _rope_tables(seq)
    ctx_spec = pl.BlockSpec((None, None, past, B_KV_DIM), lambda b, i: (b, layer, 0, 0))
    tab_q = pl.BlockSpec((SW_BLOCK, 128), lambda b, i: (i, 0))
    tab_k = pl.BlockSpec((seq, 128), lambda b, i: (0, 0))
    return pl.pallas_call(
        functools.partial(_sw_kernel, seq=seq),
        grid=(n // seq, nb),
        in_specs=[pl.BlockSpec(memory_space=pltpu.SMEM),
                  pl.BlockSpec((SW_BLOCK, B_Q_DIM), lambda b, i: (b * nb + i, COL_QA + 3)),
                  pl.BlockSpec((seq, B_KV_DIM), lambda b, i: (b, COL_KB)),
                  pl.BlockSpec((seq, B_KV_DIM), lambda b, i: (b, COL_KB + 1)),
                  ctx_spec, ctx_spec, tab_q, tab_q, tab_q, tab_k, tab_k, tab_k],
        out_specs=pl.BlockSpec((SW_BLOCK, B_Q_DIM), lambda b, i: (b * nb + i, 0)),
        out_shape=jax.ShapeDtypeStruct((n, B_Q_DIM), F32),
        scratch_shapes=[pltpu.VMEM((seq, B_KV_DIM), BF16)],
        compiler_params=_params(("parallel", "arbitrary"), 16 * MIB),
        name="sw_latent",
    )(sink_l, p, p, p, cache_k, cache_v, cos, sin_next, sin_prev, cos, sin_next, sin_prev)


def _pool_kernel(u0_ref, u1_ref, u2_ref, u3_ref, pw_ref, ps_ref, o_ref):
    for gi, u_ref in enumerate((u0_ref, u1_ref, u2_ref, u3_ref)):
        sl = slice(gi * POOL_GROUP_DIM, (gi + 1) * POOL_GROUP_DIM)
        o_ref[:, sl] = _pool_group(u_ref[...], POOL_WINDOWS[gi], pw_ref[gi].astype(BF16), ps_ref[:, sl])


def _pool_latent(p, pool_w_l, pool_scale_l, seq):
    n = p.shape[0]
    return pl.pallas_call(
        _pool_kernel,
        grid=(n // seq,),
        in_specs=[pl.BlockSpec((seq, POOL_GROUP_DIM), lambda b, g=g: (b, COL_U + g)) for g in range(POOL_GROUPS)]
        + [pl.BlockSpec((POOL_GROUPS, POOL_GROUP_DIM, POOL_GROUP_DIM), lambda b: (0, 0, 0)),
           pl.BlockSpec((1, POOL_DIM), lambda b: (0, 0))],
        out_specs=pl.BlockSpec((seq, POOL_DIM), lambda b: (b, 0)),
        out_shape=jax.ShapeDtypeStruct((n, POOL_DIM), F32),
        compiler_params=_params(("parallel",), 16 * MIB),
        name="pool_latent",
    )(p, p, p, p, pool_w_l, pool_scale_l.reshape(1, POOL_DIM))


def _split_bf16(x):
    hi = x.astype(BF16)
    return hi, (x - hi.astype(F32)).astype(BF16)


def _merge_kernel(x_ref, oa_ref, ob_ref, oc_ref, ga_ref, gb_ref, gc_ref, mod_ref, nw_ref,
                  wa_ref, wb_ref, wc_ref, wo_ref, wr_ref, x1_ref, h2_ref, lg_ref):
    m = (jax.nn.sigmoid(ga_ref[...]) * _dot(oa_ref[...].astype(BF16), wa_ref[...])
         + jax.nn.sigmoid(gb_ref[...]) * _dot(ob_ref[...].astype(BF16), wb_ref[...])
         + jax.nn.sigmoid(gc_ref[...]) * _dot(oc_ref[...].astype(BF16), wc_ref[...]))
    y = _dot(m.astype(BF16), wo_ref[...])
    x1 = x_ref[...] + mod_ref[2:3, :] * _rms(y, nw_ref[1:2, :])
    x1_ref[...] = x1
    h2 = _rms(x1, nw_ref[2:3, :]) * (1.0 + mod_ref[4:5, :]) + mod_ref[3:4, :]
    h2_ref[...] = h2.astype(BF16)
    h_hi, h_lo = _split_bf16(h2)
    w_hi, w_lo = _split_bf16(wr_ref[...])
    lg_ref[...] = _dot_nt(w_hi, h_hi) + (_dot_nt(w_hi, h_lo) + _dot_nt(w_lo, h_hi))


def _merge(x, branches, p, mod_l, nw_l, wa, wb, wc, wo, wr_t, group_of_tile):
    n = x.shape[0]
    tm = MERGE_TM
    row = lambda width, col=0: pl.BlockSpec((tm, width), lambda i, col=col: (i, col))
    const = lambda shape: pl.BlockSpec(shape, lambda i: (0,) * len(shape), pipeline_mode=pl.Buffered(1))
    oa, ob, oc = branches
    est = (2 * tm * D_MODEL * 4 * 6 + 2 * tm * 3 * A_DIM * 4 + 3 * A_DIM * D_MODEL * 2 + D_MODEL * D_MODEL * 2
           + 8 * tm * D_MODEL * 4)
    return pl.pallas_call(
        _merge_kernel,
        grid=(n // tm,),
        in_specs=[row(D_MODEL),
                  pl.BlockSpec((tm, A_DIM), oa[1]), pl.BlockSpec((tm, A_DIM), ob[1]), pl.BlockSpec((tm, A_DIM), oc[1]),
                  row(D_MODEL, COL_G), row(D_MODEL, COL_G + 1), row(D_MODEL, COL_G + 2),
                  pl.BlockSpec((None, 6, D_MODEL), lambda i: (group_of_tile(i), 0, 0)),
                  const((4, D_MODEL)),
                  const((A_DIM, D_MODEL)), const((B_Q_DIM, D_MODEL)), const((POOL_DIM, D_MODEL)),
                  const((D_MODEL, D_MODEL)), const((N_EXPERTS, D_MODEL))],
        out_specs=[row(D_MODEL), row(D_MODEL), pl.BlockSpec((N_EXPERTS, tm), lambda i: (0, i))],
        out_shape=[jax.ShapeDtypeStruct((n, D_MODEL), F32), jax.ShapeDtypeStruct((n, D_MODEL), BF16),
                   jax.ShapeDtypeStruct((N_EXPERTS, n), F32)],
        compiler_params=_params(("parallel",), est),
        name="merge",
    )(x, oa[0], ob[0], oc[0], p, p, p, mod_l, nw_l, wa, wb, wc, wo, wr_t)


def _dispatch_kernel(lg_ref, h_ref, xs_ref, gate_ref, rc_ref, aff_scr, sel_scr, *, seq, cap):
    lg = lg_ref[...]
    e = jnp.exp(lg - lg.max(axis=0, keepdims=True))
    aff_scr[...] = e / e.sum(axis=0, keepdims=True)
    rc_ref[...] = jnp.zeros(rc_ref.shape, F32)
    chunk = min(seq, RANK_CHUNK)
    lane_e = lax.broadcasted_iota(jnp.int32, (chunk, N_EXPERTS), 1)
    slot = lax.broadcasted_iota(jnp.int32, (cap, seq), 0).astype(F32)

    def body(ex, carry):
        row = aff_scr[pl.ds(ex, 1), :]
        rowb = jnp.broadcast_to(row, (chunk, seq))
        rank_row = jnp.zeros((1, seq), F32)
        for c in range(seq // chunk):
            tr = c * chunk + lax.broadcasted_iota(jnp.int32, (chunk, seq), 0)
            tc = lax.broadcasted_iota(jnp.int32, (chunk, seq), 1)
            col = jnp.where(tr == tc, rowb, 0.0).sum(axis=1, keepdims=True)
            beats = ((col > rowb) | ((col == rowb) & (tr < tc))).astype(F32)
            rank_row = rank_row + beats.sum(axis=0, keepdims=True)
            rank_col = (seq - 1.0) - beats.sum(axis=1, keepdims=True)
            rows = slice(c * chunk, (c + 1) * chunk)
            rc_ref[rows, :] = jnp.where(lane_e == ex, rank_col, rc_ref[rows, :])
        sel = slot == rank_row
        sel_scr[pl.ds(pl.multiple_of(ex * cap, cap), cap), :] = sel.astype(BF16)
        gate_ref[ex] = jnp.where(sel, jnp.broadcast_to(row, (cap, seq)), 0.0).sum(axis=1, keepdims=True)
        return carry

    lax.fori_loop(0, N_EXPERTS, body, 0)
    xs = _dot(sel_scr[...], h_ref[...]).astype(BF16)
    xs_ref[...] = xs.reshape(N_EXPERTS, cap, D_MODEL)


def _dispatch(lg_t, h2, seq):
    n = h2.shape[0]
    nb = n // seq
    cap = EC_FACTOR * seq // N_EXPERTS
    est = (2 * seq * D_MODEL * 2 + 2 * N_EXPERTS * cap * D_MODEL * 2 + N_EXPERTS * cap * seq * 2
           + N_EXPERTS * cap * D_MODEL * 4 + 12 * min(seq, RANK_CHUNK) * seq * 4)
    return pl.pallas_call(
        functools.partial(_dispatch_kernel, seq=seq, cap=cap),
        grid=(nb,),
        in_specs=[pl.BlockSpec((N_EXPERTS, seq), lambda b: (0, b)),
                  pl.BlockSpec((seq, D_MODEL), lambda b: (b, 0))],
        out_specs=[pl.BlockSpec((N_EXPERTS, cap, D_MODEL), lambda b: (0, b, 0)),
                   pl.BlockSpec((N_EXPERTS, cap, 1), lambda b: (0, b, 0)),
                   pl.BlockSpec((seq, N_EXPERTS), lambda b: (b, 0))],
        out_shape=[jax.ShapeDtypeStruct((N_EXPERTS, nb * cap, D_MODEL), BF16),
                   jax.ShapeDtypeStruct((N_EXPERTS, nb * cap, 1), F32),
                   jax.ShapeDtypeStruct((n, N_EXPERTS), F32)],
        scratch_shapes=[pltpu.VMEM((N_EXPERTS, seq), F32), pltpu.VMEM((N_EXPERTS * cap, seq), BF16)],
        compiler_params=_params(("parallel",), est),
        name="dispatch",
    )(lg_t, h2)


def _expert_kernel(xc_ref, xl_ref, gc_ref, gl_ref, wg_ref, wu_ref, wd_ref, yc_ref, yl_ref):
    f = pl.program_id(1)
    last = pl.num_programs(1) - 1
    wg = wg_ref[0].astype(BF16)
    wu = wu_ref[0].astype(BF16)
    wd = wd_ref[0].astype(BF16)
    for x_ref, g_ref, y_ref in ((xc_ref, gc_ref, yc_ref), (xl_ref, gl_ref, yl_ref)):
        x = x_ref[0]
        a = _dot(x, wg)
        hid = (a * jax.nn.sigmoid(a)) * _dot(x, wu)
        part = _dot(hid.astype(BF16), wd)

        @pl.when(f == 0)
        def _():
            y_ref[0] = part

        @pl.when(f > 0)
        def _():
            y_ref[0] += part

        @pl.when(f == last)
        def _():
            y_ref[0] = y_ref[0] * g_ref[0]


def _experts(xs_c, xs_l, gate_c, gate_l, w_gate_l, w_up_l, w_down_l):
    sc, sl = xs_c.shape[1], xs_l.shape[1]
    tf = EXPERT_TF
    slots = lambda s, width: pl.BlockSpec((1, s, width), lambda e, f: (e, 0, 0))
    est = (2 * (sc + sl) * D_MODEL * (2 + 4) + 2 * 3 * D_MODEL * tf * 4 + 3 * D_MODEL * tf * 2 + 4 * sc * tf * 4
           + sc * D_MODEL * 4)
    return pl.pallas_call(
        _expert_kernel,
        grid=(N_EXPERTS, D_EXPERT // tf),
        in_specs=[slots(sc, D_MODEL), slots(sl, D_MODEL), slots(sc, 1), slots(sl, 1),
                  pl.BlockSpec((1, D_MODEL, tf), lambda e, f: (e, 0, f)),
                  pl.BlockSpec((1, D_MODEL, tf), lambda e, f: (e, 0, f)),
                  pl.BlockSpec((1, tf, D_MODEL), lambda e, f: (e, f, 0))],
        out_specs=[slots(sc, D_MODEL), slots(sl, D_MODEL)],
        out_shape=[jax.ShapeDtypeStruct((N_EXPERTS, sc, D_MODEL), F32),
                   jax.ShapeDtypeStruct((N_EXPERTS, sl, D_MODEL), F32)],
        compiler_params=_params(("parallel", "arbitrary"), est),
        name="experts",
    )(xs_c, xs_l, gate_c, gate_l, w_gate_l, w_up_l, w_down_l)


def _combine_kernel(rc_ref, ye_ref, x1_ref, mod_ref, nw_ref, o_ref, m_scr, ffn_scr, *, cap):
    rc = rc_ref[...]
    slot = lax.broadcasted_iota(jnp.int32, (rc.shape[0], cap), 1).astype(F32)
    for ex in range(N_EXPERTS):
        m_scr[:, ex * cap:(ex + 1) * cap] = (rc[:, ex:ex + 1] == slot).astype(BF16)
    onehot = m_scr[...]
    for c in range(D_MODEL // COMBINE_TD):
        cols = slice(c * COMBINE_TD, (c + 1) * COMBINE_TD)
        ye = ye_ref[:, :, cols].reshape(N_EXPERTS * cap, COMBINE_TD)
        hi, lo = _split_bf16(ye)
        ffn_scr[:, cols] = _dot(onehot, hi) + _dot(onehot, lo)
    o_ref[...] = x1_ref[...] + mod_ref[5:6, :] * _rms(ffn_scr[...], nw_ref[3:4, :])


def _combine(rank_col, ye, x1, mod_l, nw_l, seq, group_of_tile):
    n = x1.shape[0]
    cap = EC_FACTOR * seq // N_EXPERTS
    tr = COMBINE_TR
    per = seq // tr
    est = (N_EXPERTS * cap * D_MODEL * 4 + 4 * tr * D_MODEL * 4 + tr * N_EXPERTS * cap * 2 + tr * D_MODEL * 4
           + 3 * N_EXPERTS * cap * COMBINE_TD * 4)
    return pl.pallas_call(
        functools.partial(_combine_kernel, cap=cap),
        grid=(n // seq, per),
        in_specs=[pl.BlockSpec((tr, N_EXPERTS), lambda b, i: (b * per + i, 0)),
                  pl.BlockSpec((N_EXPERTS, cap, D_MODEL), lambda b, i: (0, b, 0), pipeline_mode=pl.Buffered(1)),
                  pl.BlockSpec((tr, D_MODEL), lambda b, i: (b * per + i, 0)),
                  pl.BlockSpec((None, 6, D_MODEL), lambda b, i: (group_of_tile(b), 0, 0)),
                  pl.BlockSpec((4, D_MODEL), lambda b, i: (0, 0))],
        out_specs=pl.BlockSpec((tr, D_MODEL), lambda b, i: (b * per + i, 0)),
        out_shape=jax.ShapeDtypeStruct((n, D_MODEL), F32),
        scratch_shapes=[pltpu.VMEM((tr, N_EXPERTS * cap), BF16), pltpu.VMEM((tr, D_MODEL), F32)],
        compiler_params=_params(("parallel", "arbitrary"), est),
        name="combine",
    )(rank_col, ye, x1, mod_l, nw_l)


def kernel(x_prompt, x_sample, c, cache_a_k, cache_a_v, cache_b_k, cache_b_v, c_ctx, norm_w, w_ada, b_ada, w_in, a_rpb,
           b_sink, c_pool_w, c_scale, w_branch_a, w_branch_b, w_branch_c, w_out, w_router, w_gate_e, w_up_e, w_down_e):
    batch, seq_c, _ = x_prompt.shape
    dec_batch, seq_l, _ = x_sample.shape
    past = cache_a_k.shape[2]

    cond = jnp.zeros((ADA_ROWS, D_MODEL), F32).at[0].set(c_ctx).at[1:1 + dec_batch].set(c)
    mod = _adaln(cond, w_ada, b_ada).reshape(DEPTH, ADA_ROWS, 6, D_MODEL)

    ctx_group = lambda i: 0
    lat_group_inproj = lambda i: 1 + i // (seq_l // INPROJ_TM)
    lat_group_merge = lambda i: 1 + i // (seq_l // MERGE_TM)
    lat_group_batch = lambda b: 1 + b

    cak = cache_a_k.reshape(dec_batch, DEPTH, past, A_DIM)
    cav = cache_a_v.reshape(dec_batch, DEPTH, past, A_DIM)
    cbk = cache_b_k.reshape(dec_batch, DEPTH, past, B_KV_DIM)
    cbv = cache_b_v.reshape(dec_batch, DEPTH, past, B_KV_DIM)

    x_c = x_prompt.reshape(batch * seq_c, D_MODEL)
    x_l = x_sample.reshape(dec_batch * seq_l, D_MODEL)
    new_kv = [[], [], [], []]
    for l in range(DEPTH):
        w_in_l = jnp.concatenate([w_in[l, :, QKVU_DIM:], w_in[l, :, :QKVU_DIM]], axis=1).astype(BF16)
        wa, wb, wc, wo = (w[l].astype(BF16) for w in (w_branch_a, w_branch_b, w_branch_c, w_out))
        wr_t = w_router[l].T
        mod_l, nw_l = mod[l], norm_w[l]

        p_c = _inproj(x_c, mod_l, nw_l, w_in_l, ctx_group)
        o_c = _ctx_mix(p_c, b_sink[l], c_pool_w[l], c_scale[l], seq_c)
        branches_c = tuple((o_c, (lambda i, k=k: (i, k))) for k in range(N_BRANCH))
        x1_c, h2_c, lg_c = _merge(x_c, branches_c, p_c, mod_l, nw_l, wa, wb, wc, wo, wr_t, ctx_group)
        xs_c, gate_c, rc_c = _dispatch(lg_c, h2_c, seq_c)
        kv0 = GATE_DIM + A_DIM
        new_kv[0].append(p_c[:, kv0:kv0 + A_DIM])
        new_kv[1].append(p_c[:, kv0 + A_DIM:kv0 + 2 * A_DIM])
        kb0 = GATE_DIM + 3 * A_DIM + B_Q_DIM
        new_kv[2].append(p_c[:, kb0:kb0 + B_KV_DIM])
        new_kv[3].append(p_c[:, kb0 + B_KV_DIM:kb0 + 2 * B_KV_DIM])

        p_l = _inproj(x_l, mod_l, nw_l, w_in_l, lat_group_inproj)
        table = _rpb_table(a_rpb[l])
        o_a = _na_latent(p_l, cak, cav, table, l, seq_l)
        o_b = _sw_latent(p_l, cbk, cbv, b_sink[l], l, seq_l)
        o_p = _pool_latent(p_l, c_pool_w[l], c_scale[l], seq_l)
        first = lambda i: (i, 0)
        x1_l, h2_l, lg_l = _merge(x_l, ((o_a, first), (o_b, first), (o_p, first)), p_l, mod_l, nw_l,
                                  wa, wb, wc, wo, wr_t, lat_group_merge)
        xs_l, gate_l, rc_l = _dispatch(lg_l, h2_l, seq_l)

        ye_c, ye_l = _experts(xs_c, xs_l, gate_c, gate_l, w_gate_e[l], w_up_e[l], w_down_e[l])
        x_c = _combine(rc_c, ye_c, x1_c, mod_l, nw_l, seq_c, ctx_group)
        x_l = _combine(rc_l, ye_l, x1_l, mod_l, nw_l, seq_l, lat_group_batch)

    y_prompt = x_c.reshape(batch, seq_c, D_MODEL)
    y_sample = x_l.reshape(dec_batch, seq_l, D_MODEL)
    stack = lambda parts, heads, dim: jnp.stack(parts, axis=0).reshape(DEPTH, batch, seq_c, heads, dim).swapaxes(0, 1)
    return (y_prompt, y_sample,
            stack(new_kv[0], NA_HEADS, NA_HEAD_DIM), stack(new_kv[1], NA_HEADS, NA_HEAD_DIM),
            stack(new_kv[2], SW_KV_HEADS, SW_HEAD_DIM), stack(new_kv[3], SW_KV_HEADS, SW_HEAD_DIM))
```

```python
import functools

import numpy as np
import jax
import jax.numpy as jnp
from jax import lax
from jax.experimental import pallas as pl
from jax.experimental.pallas import tpu as pltpu

F32 = jnp.float32
BF16 = jnp.bfloat16

D_MODEL = 2048
DEPTH = 2
GRID_W = 64
NA_HEADS, NA_HEAD_DIM, NA_MAX_KH, NA_KW = 4, 128, 8, 16
SW_Q_HEADS, SW_KV_HEADS, SW_HEAD_DIM = 8, 2, 64
SW_GROUP = SW_Q_HEADS // SW_KV_HEADS
SW_WINDOW, SW_BLOCK = 128, 128
ROPE_THETA = 10000.0
POOL_WINDOWS = (2, 4, 8, 16)
POOL_GROUPS, POOL_GROUP_DIM = 4, 128
POOL_DIM = POOL_GROUPS * POOL_GROUP_DIM
A_DIM = NA_HEADS * NA_HEAD_DIM
B_Q_DIM = SW_Q_HEADS * SW_HEAD_DIM
B_KV_DIM = SW_KV_HEADS * SW_HEAD_DIM
N_BRANCH = 3
GATE_DIM = N_BRANCH * D_MODEL
QKVU_DIM = 3 * A_DIM + B_Q_DIM + 2 * B_KV_DIM + POOL_DIM
IN_DIM = QKVU_DIM + GATE_DIM
N_EXPERTS = 16
EC_FACTOR = 2
D_EXPERT = 1024
RMS_EPS = 1e-6
NEG_INF = -1e30

COL_QA = 0
COL_KB = (3 * A_DIM + B_Q_DIM) // B_KV_DIM
COL_U = COL_KB + 2

V7X_VMEM_BYTES = 64 * 1024 * 1024
V7X_VMEM_CEILING = 60000 * 1024
MIB = 1024 * 1024

ADA_ROWS = 16
ADA_TN = 1024
INPROJ_TM = 1024
INPROJ_TN = 896
MERGE_TM = 256
EXPERT_TF = 256
EXPERT_TD = 512
RANK_CHUNK = 256
COMBINE_TR = 256
COMBINE_TD = 512
ROW_CHUNK = 64


def _vmem_limit(estimate_bytes):
    return int(min(V7X_VMEM_CEILING, max(32 * MIB, estimate_bytes + 8 * MIB)))


def _params(semantics, estimate_bytes):
    return pltpu.CompilerParams(dimension_semantics=semantics,
                                vmem_limit_bytes=_vmem_limit(estimate_bytes))


def _rms(x, g):
    ms = jnp.mean(x * x, axis=-1, keepdims=True)
    return x * lax.rsqrt(ms + RMS_EPS) * g


def _dot(a, b):
    return jnp.dot(a, b, preferred_element_type=F32)


def _dot_nt(a, b):
    return lax.dot_general(a, b, (((1,), (1,)), ((), ())), preferred_element_type=F32)


def _adaln_kernel(c_ref, w_ref, b_ref, o_ref):
    c = c_ref[...]
    s = (c * jax.nn.sigmoid(c)).astype(BF16)
    o_ref[0] = _dot(s, w_ref[0].astype(BF16)) + b_ref[0]


def _adaln(cond, w_ada, b_ada):
    n_out = w_ada.shape[-1]
    return pl.pallas_call(
        _adaln_kernel,
        grid=(DEPTH, n_out // ADA_TN),
        in_specs=[pl.BlockSpec((ADA_ROWS, D_MODEL), lambda l, j: (0, 0)),
                  pl.BlockSpec((1, D_MODEL, ADA_TN), lambda l, j: (l, 0, j)),
                  pl.BlockSpec((1, 1, ADA_TN), lambda l, j: (l, 0, j))],
        out_specs=pl.BlockSpec((1, ADA_ROWS, ADA_TN), lambda l, j: (l, 0, j)),
        out_shape=jax.ShapeDtypeStruct((DEPTH, ADA_ROWS, n_out), F32),
        compiler_params=_params(("parallel", "parallel"), 2 * D_MODEL * ADA_TN * 4),
        name="adaln",
    )(cond, w_ada, b_ada.reshape(DEPTH, 1, n_out))


def _inproj_kernel(x_ref, mod_ref, nw_ref, w_ref, o_ref, h_scr):
    @pl.when(pl.program_id(1) == 0)
    def _():
        g = nw_ref[0:1, :]
        sc = 1.0 + mod_ref[1:2, :]
        sh = mod_ref[0:1, :]

        def body(r, carry):
            rows = pl.ds(pl.multiple_of(r * ROW_CHUNK, ROW_CHUNK), ROW_CHUNK)
            h_scr[rows, :] = (_rms(x_ref[rows, :], g) * sc + sh).astype(BF16)
            return carry

        lax.fori_loop(0, INPROJ_TM // ROW_CHUNK, body, 0)

    o_ref[...] = _dot(h_scr[...], w_ref[...])


def _mod_spec(layer, group_of, n_grid):
    if n_grid == 1:
        return pl.BlockSpec((None, None, 6, D_MODEL), lambda i: (layer, group_of(i), 0, 0))
    return pl.BlockSpec((None, None, 6, D_MODEL), lambda i, j: (layer, group_of(i), 0, 0))


def _inproj(x, mod, norm_w, w_in_bf, layer, group_of_tile):
    n = x.shape[0]
    est = (2 * INPROJ_TM * D_MODEL * 4 + 2 * D_MODEL * INPROJ_TN * 2 + 2 * INPROJ_TM * INPROJ_TN * 4
           + INPROJ_TM * D_MODEL * 2)
    return pl.pallas_call(
        _inproj_kernel,
        grid=(n // INPROJ_TM, IN_DIM // INPROJ_TN),
        in_specs=[pl.BlockSpec((INPROJ_TM, D_MODEL), lambda i, j: (i, 0)),
                  _mod_spec(layer, group_of_tile, 2),
                  pl.BlockSpec((None, 4, D_MODEL), lambda i, j: (layer, 0, 0)),
                  pl.BlockSpec((None, D_MODEL, INPROJ_TN), lambda i, j: (layer, 0, j))],
        out_specs=pl.BlockSpec((INPROJ_TM, INPROJ_TN), lambda i, j: (i, j)),
        out_shape=jax.ShapeDtypeStruct((n, IN_DIM), F32),
        scratch_shapes=[pltpu.VMEM((INPROJ_TM, D_MODEL), BF16)],
        compiler_params=_params(("parallel", "arbitrary"), est),
        name="inproj",
    )(x, mod, norm_w, w_in_bf)


def _joint_attention(q, segments, scale, sink=None):
    scores = []
    for k, _, bias, mask in segments:
        s = _dot_nt(q, k) * scale
        if bias is not None:
            s = s + bias
        if mask is not None:
            s = jnp.where(mask, s, NEG_INF)
        scores.append(s)
    m = scores[0].max(axis=-1, keepdims=True)
    for s in scores[1:]:
        m = jnp.maximum(m, s.max(axis=-1, keepdims=True))
    if sink is not None:
        m = jnp.maximum(m, sink)
    denom = jnp.exp(sink - m) if sink is not None else 0.0
    acc = None
    for s, (_, v, _, _) in zip(scores, segments):
        e = jnp.exp(s - m)
        denom = denom + e.sum(axis=-1, keepdims=True)
        pv = _dot(e.astype(BF16), v)
        acc = pv if acc is None else acc + pv
    return acc / denom


def _pool_group(u, window, pw_bf, scale_row):
    seq = u.shape[0]
    pad = 8
    n = seq + 2 * pad
    z = jnp.zeros((pad, POOL_GROUP_DIM), F32)
    p = jnp.concatenate([z, u, z], axis=0)
    k = 1
    while k < window:
        p = p + pltpu.roll(p, n - k, 0)
        k *= 2
    win = pltpu.roll(p, window // 2, 0)[pad:pad + seq]
    t = lax.broadcasted_iota(jnp.int32, (seq, 1), 0)
    lo = jnp.maximum(t - window // 2, 0)
    hi = jnp.minimum(t - window // 2 + window, seq)
    cnt = (hi - lo).astype(F32)
    pooled = win / cnt - u
    return _dot(pooled.astype(BF16), pw_bf) * scale_row


def _sink_column(sink_ref, kv_head, rows_per_head):
    r = lax.broadcasted_iota(jnp.int32, (SW_GROUP * rows_per_head, 1), 0)
    col = jnp.full((SW_GROUP * rows_per_head, 1), sink_ref[kv_head * SW_GROUP], F32)
    for g in range(1, SW_GROUP):
        col = jnp.where(r >= g * rows_per_head, sink_ref[kv_head * SW_GROUP + g], col)
    return col


def _ctx_mix_kernel(sink_ref, qa_ref, ka_ref, va_ref, qb_ref, kb_ref, vb_ref, u0_ref, u1_ref, u2_ref, u3_ref,
                    pw_ref, ps_ref, *rest):
    o_ref, nak_ref, nav_ref, nbk_ref, nbv_ref = rest[-5:]
    seq = qa_ref.shape[0]
    nak_ref[...] = ka_ref[...]
    nav_ref[...] = va_ref[...]
    nbk_ref[...] = kb_ref[...]
    nbv_ref[...] = vb_ref[...]
    for h in range(NA_HEADS):
        sl = slice(h * NA_HEAD_DIM, (h + 1) * NA_HEAD_DIM)
        o = _joint_attention(qa_ref[:, sl].astype(BF16),
                             [(ka_ref[:, sl].astype(BF16), va_ref[:, sl].astype(BF16), None, None)],
                             NA_HEAD_DIM ** -0.5)
        o_ref[:, sl] = o
    for hk in range(SW_KV_HEADS):
        ksl = slice(hk * SW_HEAD_DIM, (hk + 1) * SW_HEAD_DIM)
        q = jnp.concatenate(
            [qb_ref[:, (hk * SW_GROUP + g) * SW_HEAD_DIM:(hk * SW_GROUP + g + 1) * SW_HEAD_DIM] for g in range(SW_GROUP)],
            axis=0).astype(BF16)
        o = _joint_attention(q, [(kb_ref[:, ksl].astype(BF16), vb_ref[:, ksl].astype(BF16), None, None)],
                             SW_HEAD_DIM ** -0.5, sink=_sink_column(sink_ref, hk, seq))
        for g in range(SW_GROUP):
            c0 = A_DIM + (hk * SW_GROUP + g) * SW_HEAD_DIM
            o_ref[:, c0:c0 + SW_HEAD_DIM] = o[g * seq:(g + 1) * seq]
    for gi, u_ref in enumerate((u0_ref, u1_ref, u2_ref, u3_ref)):
        c0 = A_DIM + B_Q_DIM + gi * POOL_GROUP_DIM
        o_ref[:, c0:c0 + POOL_GROUP_DIM] = _pool_group(
            u_ref[...], POOL_WINDOWS[gi], pw_ref[gi].astype(BF16), ps_ref[:, gi * POOL_GROUP_DIM:(gi + 1) * POOL_GROUP_DIM])


def _ctx_mix(p, sink_l, pool_w, pool_scale, layer, seq, caches):
    n = p.shape[0]
    batch = n // seq
    wide = lambda c: pl.BlockSpec((seq, A_DIM), lambda b, c=c: (b, c))
    narrow = lambda c: pl.BlockSpec((seq, B_KV_DIM), lambda b, c=c: (b, c))
    in_specs = [pl.BlockSpec(memory_space=pltpu.SMEM)]
    in_specs += [wide(COL_QA + i) for i in range(4)]
    in_specs += [narrow(COL_KB), narrow(COL_KB + 1)]
    in_specs += [narrow(COL_U + g) for g in range(POOL_GROUPS)]
    in_specs += [pl.BlockSpec((None, POOL_GROUPS, POOL_GROUP_DIM, POOL_GROUP_DIM), lambda b: (layer, 0, 0, 0)),
                 pl.BlockSpec((None, 1, POOL_DIM), lambda b: (layer, 0, 0))]
    n_fixed = len(in_specs)
    in_specs += [pl.BlockSpec(memory_space=pl.ANY)] * len(caches)
    cache_spec = lambda width: pl.BlockSpec((None, None, seq, width), lambda b: (b, layer, 0, 0))
    cache_shape = lambda width: jax.ShapeDtypeStruct((batch, DEPTH, seq, width), F32)
    outs = pl.pallas_call(
        _ctx_mix_kernel,
        grid=(batch,),
        in_specs=in_specs,
        out_specs=[pl.BlockSpec((seq, 3 * A_DIM), lambda b: (b, 0)),
                   cache_spec(A_DIM), cache_spec(A_DIM), cache_spec(B_KV_DIM), cache_spec(B_KV_DIM)],
        out_shape=[jax.ShapeDtypeStruct((n, 3 * A_DIM), F32),
                   cache_shape(A_DIM), cache_shape(A_DIM), cache_shape(B_KV_DIM), cache_shape(B_KV_DIM)],
        input_output_aliases={n_fixed + k: 1 + k for k in range(len(caches))},
        compiler_params=_params(("parallel",), 24 * MIB),
        name="ctx_mix",
    )(sink_l, p, p, p, p, p, p, p, p, p, p, pool_w, pool_scale, *caches)
    return outs[0], tuple(outs[1:])


NA_PAIR_ROWS = 2 * NA_MAX_KH - 2


def _rpb_table_kernel(rpb_ref, t_ref):
    lane = lax.broadcasted_iota(jnp.int32, (GRID_W, 2 * GRID_W), 1)
    qc = lax.broadcasted_iota(jnp.int32, (GRID_W, 2 * GRID_W), 0)
    kc = lane & (GRID_W - 1)
    upper = lane >= GRID_W
    dcm = jnp.clip(kc - qc + NA_KW - 1, 0, 2 * NA_KW - 2)
    col0 = jnp.clip(qc - NA_KW // 2, 0, GRID_W - NA_KW)
    inside = (kc >= col0) & (kc < col0 + NA_KW)
    n_dc = 2 * NA_KW - 1
    n_dr = 2 * NA_MAX_KH - 1

    def body(i, carry):
        h = i // NA_PAIR_ROWS
        dr = i - h * NA_PAIR_ROWS
        base = (h * n_dr + dr) * n_dc
        acc = jnp.zeros((GRID_W, 2 * GRID_W), F32)
        for dc in range(n_dc):
            val = jnp.where(upper, rpb_ref[base + n_dc + dc], rpb_ref[base + dc])
            acc = jnp.where(dcm == dc, val, acc)
        t_ref[i] = jnp.where(inside, acc, NEG_INF)
        return carry

    lax.fori_loop(0, NA_HEADS * NA_PAIR_ROWS, body, 0)


def _rpb_table(rpb_l):
    return pl.pallas_call(
        _rpb_table_kernel,
        in_specs=[pl.BlockSpec(memory_space=pltpu.SMEM)],
        out_specs=pl.BlockSpec(memory_space=pltpu.VMEM),
        out_shape=jax.ShapeDtypeStruct((NA_HEADS * NA_PAIR_ROWS, GRID_W, 2 * GRID_W), F32),
        name="rpb_table",
    )(rpb_l.reshape(-1))


def _na_kernel(q_ref, k_ref, v_ref, kc_ref, vc_ref, t_ref, o_ref, *, rows):
    qr = pl.program_id(1)
    row0 = jnp.clip(qr - NA_MAX_KH // 2, 0, rows - NA_MAX_KH)
    start = pl.multiple_of(row0 * GRID_W, GRID_W)
    nkeys = NA_MAX_KH * GRID_W
    d0 = row0 - qr + NA_MAX_KH - 1
    for h in range(NA_HEADS):
        sl = slice(h * NA_HEAD_DIM, (h + 1) * NA_HEAD_DIM)
        bias = jnp.concatenate([t_ref[h * NA_PAIR_ROWS + d0 + 2 * i] for i in range(NA_MAX_KH // 2)], axis=1)
        o = _joint_attention(
            q_ref[:, sl].astype(BF16),
            [(k_ref[pl.ds(start, nkeys), sl].astype(BF16), v_ref[pl.ds(start, nkeys), sl].astype(BF16), bias, None),
             (kc_ref[:, sl].astype(BF16), vc_ref[:, sl].astype(BF16), None, None)],
            NA_HEAD_DIM ** -0.5)
        o_ref[:, sl] = o


def _na_latent(p, cache_k, cache_v, table, layer, seq):
    n = p.shape[0]
    rows = seq // GRID_W
    past = cache_k.shape[2]
    ctx_spec = pl.BlockSpec((None, None, past, A_DIM), lambda b, r: (b, layer, 0, 0))
    return pl.pallas_call(
        functools.partial(_na_kernel, rows=rows),
        grid=(n // seq, rows),
        in_specs=[pl.BlockSpec((GRID_W, A_DIM), lambda b, r: (b * rows + r, COL_QA)),
                  pl.BlockSpec((seq, A_DIM), lambda b, r: (b, COL_QA + 1)),
                  pl.BlockSpec((seq, A_DIM), lambda b, r: (b, COL_QA + 2)),
                  ctx_spec, ctx_spec,
                  pl.BlockSpec(table.shape, lambda b, r: (0, 0, 0))],
        out_specs=pl.BlockSpec((GRID_W, A_DIM), lambda b, r: (b * rows + r, 0)),
        out_shape=jax.ShapeDtypeStruct((n, A_DIM), F32),
        compiler_params=_params(("parallel", "arbitrary"), 16 * MIB),
        name="na_latent",
    )(p, p, p, cache_k, cache_v, table)


def _rope_tables(seq):
    nfreq = SW_HEAD_DIM // 4
    inv = 1.0 / (ROPE_THETA ** (np.arange(nfreq, dtype=np.float32) / np.float32(nfreq)))
    t = np.arange(seq)
    pos = (t // GRID_W, t % GRID_W)
    cos = np.zeros((seq, SW_HEAD_DIM), np.float32)
    sin_next = np.zeros((seq, SW_HEAD_DIM), np.float32)
    sin_prev = np.zeros((seq, SW_HEAD_DIM), np.float32)
    for a in range(2):
        ang = pos[a].astype(np.float32)[:, None] * inv[None, :].astype(np.float32)
        c, s = np.cos(ang).astype(np.float32), np.sin(ang).astype(np.float32)
        lo = 2 * a * nfreq
        cos[:, lo:lo + nfreq] = c
        cos[:, lo + nfreq:lo + 2 * nfreq] = c
        sin_next[:, lo:lo + nfreq] = -s
        sin_prev[:, lo + nfreq:lo + 2 * nfreq] = s
    tile = lambda x: jnp.asarray(np.tile(x, (1, 128 // SW_HEAD_DIM)))
    return tile(cos), tile(sin_next), tile(sin_prev)


def _rope(x, cos, sin_next, sin_prev):
    nfreq = SW_HEAD_DIM // 4
    return x * cos + pltpu.roll(x, 128 - nfreq, 1) * sin_next + pltpu.roll(x, nfreq, 1) * sin_prev


def _sw_kernel(sink_ref, q_ref, k_ref, v_ref, kc_ref, vc_ref, cq_ref, snq_ref, spq_ref, ck_ref, snk_ref, spk_ref,
               o_ref, kr_scr, *, seq):
    n = pl.program_id(1)

    @pl.when(n == 0)
    def _():
        kr_scr[...] = _rope(k_ref[...], ck_ref[...], snk_ref[...], spk_ref[...]).astype(BF16)

    nwin = 3 * SW_BLOCK
    kstart = pl.multiple_of(jnp.clip((n - 1) * SW_BLOCK, 0, seq - nwin), SW_BLOCK)
    cq, snq, spq = cq_ref[...], snq_ref[...], spq_ref[...]
    q = jnp.concatenate([_rope(q_ref[:, c * 128:(c + 1) * 128], cq, snq, spq) for c in range(B_Q_DIM // 128)],
                        axis=1).astype(BF16)
    rows = SW_GROUP * SW_BLOCK
    qpos = n * SW_BLOCK + (lax.broadcasted_iota(jnp.int32, (rows, nwin), 0) & (SW_BLOCK - 1))
    kpos = kstart + lax.broadcasted_iota(jnp.int32, (rows, nwin), 1)
    band = jnp.abs(qpos - kpos) <= SW_WINDOW
    for hk in range(SW_KV_HEADS):
        ksl = slice(hk * SW_HEAD_DIM, (hk + 1) * SW_HEAD_DIM)
        qs = jnp.concatenate(
            [q[:, (hk * SW_GROUP + g) * SW_HEAD_DIM:(hk * SW_GROUP + g + 1) * SW_HEAD_DIM] for g in range(SW_GROUP)], axis=0)
        o = _joint_attention(
            qs,
            [(kr_scr[pl.ds(kstart, nwin), ksl], v_ref[pl.ds(kstart, nwin), ksl].astype(BF16), None, band),
             (kc_ref[:, ksl].astype(BF16), vc_ref[:, ksl].astype(BF16), None, None)],
            SW_HEAD_DIM ** -0.5, sink=_sink_column(sink_ref, hk, SW_BLOCK))
        for g in range(SW_GROUP):
            c0 = (hk * SW_GROUP + g) * SW_HEAD_DIM
            o_ref[:, c0:c0 + SW_HEAD_DIM] = o[g * SW_BLOCK:(g + 1) * SW_BLOCK]


def _sw_latent(p, cache_k, cache_v, sink_l, layer, seq):
    n = p.shape[0]
    nb = seq // SW_BLOCK
    past = cache_k.shape[2]
    cos, sin_next, sin_prev = _rope_tables(seq)
    ctx_spec = pl.BlockSpec((None, None, past, B_KV_DIM), lambda b, i: (b, layer, 0, 0))
    tab_q = pl.BlockSpec((SW_BLOCK, 128), lambda b, i: (i, 0))
    tab_k = pl.BlockSpec((seq, 128), lambda b, i: (0, 0))
    return pl.pallas_call(
        functools.partial(_sw_kernel, seq=seq),
        grid=(n // seq, nb),
        in_specs=[pl.BlockSpec(memory_space=pltpu.SMEM),
                  pl.BlockSpec((SW_BLOCK, B_Q_DIM), lambda b, i: (b * nb + i, COL_QA + 3)),
                  pl.BlockSpec((seq, B_KV_DIM), lambda b, i: (b, COL_KB)),
                  pl.BlockSpec((seq, B_KV_DIM), lambda b, i: (b, COL_KB + 1)),
                  ctx_spec, ctx_spec, tab_q, tab_q, tab_q, tab_k, tab_k, tab_k],
        out_specs=pl.BlockSpec((SW_BLOCK, B_Q_DIM), lambda b, i: (b * nb + i, 0)),
        out_shape=jax.ShapeDtypeStruct((n, B_Q_DIM), F32),
        scratch_shapes=[pltpu.VMEM((seq, B_KV_DIM), BF16)],
        compiler_params=_params(("parallel", "arbitrary"), 16 * MIB),
        name="sw_latent",
    )(sink_l, p, p, p, cache_k, cache_v, cos, sin_next, sin_prev, cos, sin_next, sin_prev)


def _pool_kernel(u0_ref, u1_ref, u2_ref, u3_ref, pw_ref, ps_ref, o_ref):
    for gi, u_ref in enumerate((u0_ref, u1_ref, u2_ref, u3_ref)):
        sl = slice(gi * POOL_GROUP_DIM, (gi + 1) * POOL_GROUP_DIM)
        o_ref[:, sl] = _pool_group(u_ref[...], POOL_WINDOWS[gi], pw_ref[gi].astype(BF16), ps_ref[:, sl])


def _pool_latent(p, pool_w, pool_scale, layer, seq):
    n = p.shape[0]
    return pl.pallas_call(
        _pool_kernel,
        grid=(n // seq,),
        in_specs=[pl.BlockSpec((seq, POOL_GROUP_DIM), lambda b, g=g: (b, COL_U + g)) for g in range(POOL_GROUPS)]
        + [pl.BlockSpec((None, POOL_GROUPS, POOL_GROUP_DIM, POOL_GROUP_DIM), lambda b: (layer, 0, 0, 0)),
           pl.BlockSpec((None, 1, POOL_DIM), lambda b: (layer, 0, 0))],
        out_specs=pl.BlockSpec((seq, POOL_DIM), lambda b: (b, 0)),
        out_shape=jax.ShapeDtypeStruct((n, POOL_DIM), F32),
        compiler_params=_params(("parallel",), 16 * MIB),
        name="pool_latent",
    )(p, p, p, p, pool_w, pool_scale)


def _split_bf16(x):
    hi = x.astype(BF16)
    return hi, (x - hi.astype(F32)).astype(BF16)


def _merge_kernel(x_ref, oa_ref, ob_ref, oc_ref, ga_ref, gb_ref, gc_ref, mod_ref, nw_ref,
                  wa_ref, wb_ref, wc_ref, wo_ref, wr_ref, x1_ref, h2_ref, lg_ref):
    m = (jax.nn.sigmoid(ga_ref[...]) * _dot(oa_ref[...].astype(BF16), wa_ref[...])
         + jax.nn.sigmoid(gb_ref[...]) * _dot(ob_ref[...].astype(BF16), wb_ref[...])
         + jax.nn.sigmoid(gc_ref[...]) * _dot(oc_ref[...].astype(BF16), wc_ref[...]))
    y = _dot(m.astype(BF16), wo_ref[...])
    x1 = x_ref[...] + mod_ref[2:3, :] * _rms(y, nw_ref[1:2, :])
    x1_ref[...] = x1
    h2 = _rms(x1, nw_ref[2:3, :]) * (1.0 + mod_ref[4:5, :]) + mod_ref[3:4, :]
    h2_ref[...] = h2.astype(BF16)
    h_hi, h_lo = _split_bf16(h2)
    w_hi, w_lo = _split_bf16(wr_ref[...])
    lg_ref[...] = _dot_nt(w_hi, h_hi) + (_dot_nt(w_hi, h_lo) + _dot_nt(w_lo, h_hi))


def _merge(x, branches, p, mod, norm_w, wa, wb, wc, wo, wr_t, layer, group_of_tile):
    n = x.shape[0]
    tm = MERGE_TM
    row = lambda width: pl.BlockSpec((tm, width), lambda i: (i, 0))
    gate = lambda k: pl.BlockSpec((pl.Element(tm), pl.Element(D_MODEL)),
                                  lambda i, k=k: (i * tm, QKVU_DIM + k * D_MODEL))
    const = lambda *shape: pl.BlockSpec((None,) + shape, lambda i: (layer,) + (0,) * len(shape),
                                        pipeline_mode=pl.Buffered(1))
    oa, ob, oc = branches
    est = (2 * tm * D_MODEL * 4 * 6 + 2 * tm * 3 * A_DIM * 4 + 3 * A_DIM * D_MODEL * 2 + D_MODEL * D_MODEL * 2
           + 8 * tm * D_MODEL * 4)
    return pl.pallas_call(
        _merge_kernel,
        grid=(n // tm,),
        in_specs=[row(D_MODEL),
                  pl.BlockSpec((tm, A_DIM), oa[1]), pl.BlockSpec((tm, A_DIM), ob[1]), pl.BlockSpec((tm, A_DIM), oc[1]),
                  gate(0), gate(1), gate(2),
                  _mod_spec(layer, group_of_tile, 1),
                  const(4, D_MODEL),
                  const(A_DIM, D_MODEL), const(B_Q_DIM, D_MODEL), const(POOL_DIM, D_MODEL),
                  const(D_MODEL, D_MODEL), const(N_EXPERTS, D_MODEL)],
        out_specs=[row(D_MODEL), row(D_MODEL), pl.BlockSpec((N_EXPERTS, tm), lambda i: (0, i))],
        out_shape=[jax.ShapeDtypeStruct((n, D_MODEL), F32), jax.ShapeDtypeStruct((n, D_MODEL), BF16),
                   jax.ShapeDtypeStruct((N_EXPERTS, n), F32)],
        compiler_params=_params(("parallel",), est),
        name="merge",
    )(x, oa[0], ob[0], oc[0], p, p, p, mod, norm_w, wa, wb, wc, wo, wr_t)


def _dispatch_kernel(lg_ref, h_ref, xs_ref, gate_ref, rc_ref, aff_scr, sel_scr, *, seq, cap):
    lg = lg_ref[...]
    e = jnp.exp(lg - lg.max(axis=0, keepdims=True))
    aff_scr[...] = e / e.sum(axis=0, keepdims=True)
    rc_ref[...] = jnp.zeros(rc_ref.shape, F32)
    chunk = min(seq, RANK_CHUNK)
    lane_e = lax.broadcasted_iota(jnp.int32, (chunk, N_EXPERTS), 1)
    slot = lax.broadcasted_iota(jnp.int32, (cap, seq), 0).astype(F32)

    def body(ex, carry):
        row = aff_scr[pl.ds(ex, 1), :]
        rowb = jnp.broadcast_to(row, (chunk, seq))
        rank_row = jnp.zeros((1, seq), F32)
        for c in range(seq // chunk):
            tr = c * chunk + lax.broadcasted_iota(jnp.int32, (chunk, seq), 0)
            tc = lax.broadcasted_iota(jnp.int32, (chunk, seq), 1)
            col = jnp.where(tr == tc, rowb, 0.0).sum(axis=1, keepdims=True)
            beats = ((col > rowb) | ((col == rowb) & (tr < tc))).astype(F32)
            rank_row = rank_row + beats.sum(axis=0, keepdims=True)
            rank_col = (seq - 1.0) - beats.sum(axis=1, keepdims=True)
            rows = slice(c * chunk, (c + 1) * chunk)
            rc_ref[rows, :] = jnp.where(lane_e == ex, rank_col, rc_ref[rows, :])
        sel = slot == rank_row
        sel_scr[pl.ds(pl.multiple_of(ex * cap, cap), cap), :] = sel.astype(BF16)
        gate_ref[ex] = jnp.where(sel, jnp.broadcast_to(row, (cap, seq)), 0.0).sum(axis=1, keepdims=True)
        return carry

    lax.fori_loop(0, N_EXPERTS, body, 0, unroll=4 if seq <= RANK_CHUNK else 1)
    xs = _dot(sel_scr[...], h_ref[...]).astype(BF16)
    xs_ref[...] = xs.reshape(N_EXPERTS, cap, D_MODEL)


def _dispatch(lg_t, h2, seq):
    n = h2.shape[0]
    nb = n // seq
    cap = EC_FACTOR * seq // N_EXPERTS
    est = (2 * seq * D_MODEL * 2 + 2 * N_EXPERTS * cap * D_MODEL * 2 + N_EXPERTS * cap * seq * 2
           + N_EXPERTS * cap * D_MODEL * 4 + 12 * min(seq, RANK_CHUNK) * seq * 4)
    return pl.pallas_call(
        functools.partial(_dispatch_kernel, seq=seq, cap=cap),
        grid=(nb,),
        in_specs=[pl.BlockSpec((N_EXPERTS, seq), lambda b: (0, b)),
                  pl.BlockSpec((seq, D_MODEL), lambda b: (b, 0))],
        out_specs=[pl.BlockSpec((N_EXPERTS, cap, D_MODEL), lambda b: (0, b, 0)),
                   pl.BlockSpec((N_EXPERTS, cap, 1), lambda b: (0, b, 0)),
                   pl.BlockSpec((seq, N_EXPERTS), lambda b: (b, 0))],
        out_shape=[jax.ShapeDtypeStruct((N_EXPERTS, nb * cap, D_MODEL), BF16),
                   jax.ShapeDtypeStruct((N_EXPERTS, nb * cap, 1), F32),
                   jax.ShapeDtypeStruct((n, N_EXPERTS), F32)],
        scratch_shapes=[pltpu.VMEM((N_EXPERTS, seq), F32), pltpu.VMEM((N_EXPERTS * cap, seq), BF16)],
        compiler_params=_params(("parallel",), est),
        name="dispatch",
    )(lg_t, h2)


EXPERT_NF = D_EXPERT // EXPERT_TF
EXPERT_ND = D_MODEL // EXPERT_TD


def _expert_kernel(xc_ref, xl_ref, gc_ref, gl_ref, wg_ref, wu_ref, wd_ref, yc_ref, yl_ref, hc_scr, hl_scr):
    s = pl.program_id(1)

    @pl.when(s < EXPERT_NF)
    def _():
        wg = wg_ref[0].astype(BF16)
        wu = wu_ref[0].astype(BF16)
        for x_ref, h_scr in ((xc_ref, hc_scr), (xl_ref, hl_scr)):
            x = x_ref[0]
            a = _dot(x, wg)
            h_scr[s] = ((a * jax.nn.sigmoid(a)) * _dot(x, wu)).astype(BF16)

    @pl.when(s >= EXPERT_NF)
    def _():
        wd = wd_ref[0].astype(BF16)
        for h_scr, g_ref, y_ref in ((hc_scr, gc_ref, yc_ref), (hl_scr, gl_ref, yl_ref)):
            acc = _dot(h_scr[0], wd[0:EXPERT_TF])
            for f in range(1, EXPERT_NF):
                acc = acc + _dot(h_scr[f], wd[f * EXPERT_TF:(f + 1) * EXPERT_TF])
            y_ref[0] = acc * g_ref[0]


def _experts(xs_c, xs_l, gate_c, gate_l, w_gate, w_up, w_down, layer):
    sc, sl = xs_c.shape[1], xs_l.shape[1]
    tf, td = EXPERT_TF, EXPERT_TD
    slots = lambda s, width: pl.BlockSpec((1, s, width), lambda e, t: (e, 0, 0))
    up_tile = lambda e, t: (layer, e, 0, jnp.minimum(t, EXPERT_NF - 1))
    down_tile = lambda e, t: (e, 0, jnp.maximum(t - EXPERT_NF, 0))
    est = (2 * (sc + sl) * D_MODEL * 2 + 2 * 2 * D_MODEL * tf * 4 + 2 * D_EXPERT * td * 4 + 2 * (sc + sl) * td * 4
           + (sc + sl) * D_EXPERT * 2 + 2 * D_MODEL * tf * 2 + D_EXPERT * td * 2 + 6 * sc * max(tf, td) * 4)
    return pl.pallas_call(
        _expert_kernel,
        grid=(N_EXPERTS, EXPERT_NF + EXPERT_ND),
        in_specs=[slots(sc, D_MODEL), slots(sl, D_MODEL), slots(sc, 1), slots(sl, 1),
                  pl.BlockSpec((None, 1, D_MODEL, tf), up_tile),
                  pl.BlockSpec((None, 1, D_MODEL, tf), up_tile),
                  pl.BlockSpec((None, 1, D_EXPERT, td), lambda e, t: (layer,) + down_tile(e, t))],
        out_specs=[pl.BlockSpec((1, sc, td), down_tile), pl.BlockSpec((1, sl, td), down_tile)],
        out_shape=[jax.ShapeDtypeStruct((N_EXPERTS, sc, D_MODEL), F32),
                   jax.ShapeDtypeStruct((N_EXPERTS, sl, D_MODEL), F32)],
        scratch_shapes=[pltpu.VMEM((EXPERT_NF, sc, tf), BF16), pltpu.VMEM((EXPERT_NF, sl, tf), BF16)],
        compiler_params=_params(("parallel", "arbitrary"), est),
        name="experts",
    )(xs_c, xs_l, gate_c, gate_l, w_gate, w_up, w_down)


def _combine_kernel(rc_ref, ye_ref, x1_ref, mod_ref, nw_ref, o_ref, m_scr, ffn_scr, *, cap):
    rc = rc_ref[...]
    slot = lax.broadcasted_iota(jnp.int32, (rc.shape[0], cap), 1).astype(F32)
    for ex in range(N_EXPERTS):
        m_scr[:, ex * cap:(ex + 1) * cap] = (rc[:, ex:ex + 1] == slot).astype(BF16)
    onehot = m_scr[...]
    for c in range(D_MODEL // COMBINE_TD):
        cols = slice(c * COMBINE_TD, (c + 1) * COMBINE_TD)
        ye = ye_ref[:, :, cols].reshape(N_EXPERTS * cap, COMBINE_TD)
        hi, lo = _split_bf16(ye)
        ffn_scr[:, cols] = _dot(onehot, hi) + _dot(onehot, lo)
    o_ref[...] = x1_ref[...] + mod_ref[5:6, :] * _rms(ffn_scr[...], nw_ref[3:4, :])


def _combine(rank_col, ye, x1, mod, norm_w, layer, seq, group_of_batch):
    n = x1.shape[0]
    cap = EC_FACTOR * seq // N_EXPERTS
    tr = COMBINE_TR
    per = seq // tr
    ye_buffers = 2 if per == 1 else 1
    est = (ye_buffers * N_EXPERTS * cap * D_MODEL * 4 + 4 * tr * D_MODEL * 4 + tr * N_EXPERTS * cap * 2
           + tr * D_MODEL * 4 + 3 * N_EXPERTS * cap * COMBINE_TD * 4)
    return pl.pallas_call(
        functools.partial(_combine_kernel, cap=cap),
        grid=(n // seq, per),
        in_specs=[pl.BlockSpec((tr, N_EXPERTS), lambda b, i: (b * per + i, 0)),
                  pl.BlockSpec((N_EXPERTS, cap, D_MODEL), lambda b, i: (0, b, 0),
                               pipeline_mode=pl.Buffered(ye_buffers)),
                  pl.BlockSpec((tr, D_MODEL), lambda b, i: (b * per + i, 0)),
                  pl.BlockSpec((None, None, 6, D_MODEL), lambda b, i: (layer, group_of_batch(b), 0, 0)),
                  pl.BlockSpec((None, 4, D_MODEL), lambda b, i: (layer, 0, 0))],
        out_specs=pl.BlockSpec((tr, D_MODEL), lambda b, i: (b * per + i, 0)),
        out_shape=jax.ShapeDtypeStruct((n, D_MODEL), F32),
        scratch_shapes=[pltpu.VMEM((tr, N_EXPERTS * cap), BF16), pltpu.VMEM((tr, D_MODEL), F32)],
        compiler_params=_params(("parallel", "arbitrary"), est),
        name="combine",
    )(rank_col, ye, x1, mod, norm_w)


def kernel(x_prompt, x_sample, c, cache_a_k, cache_a_v, cache_b_k, cache_b_v, c_ctx, norm_w, w_ada, b_ada, w_in, a_rpb,
           b_sink, c_pool_w, c_scale, w_branch_a, w_branch_b, w_branch_c, w_out, w_router, w_gate_e, w_up_e, w_down_e):
    batch, seq_c, _ = x_prompt.shape
    dec_batch, seq_l, _ = x_sample.shape
    past = cache_a_k.shape[2]

    cond = jnp.zeros((ADA_ROWS, D_MODEL), F32).at[0].set(c_ctx).at[1:1 + dec_batch].set(c)
    mod = _adaln(cond, w_ada, b_ada).reshape(DEPTH, ADA_ROWS, 6, D_MODEL)

    ctx_group = lambda i: 0
    lat_group_inproj = lambda i: 1 + i // (seq_l // INPROJ_TM)
    lat_group_merge = lambda i: 1 + i // (seq_l // MERGE_TM)
    lat_group_batch = lambda b: 1 + b

    cak = cache_a_k.reshape(dec_batch, DEPTH, past, A_DIM)
    cav = cache_a_v.reshape(dec_batch, DEPTH, past, A_DIM)
    cbk = cache_b_k.reshape(dec_batch, DEPTH, past, B_KV_DIM)
    cbv = cache_b_v.reshape(dec_batch, DEPTH, past, B_KV_DIM)

    x_c = x_prompt.reshape(batch * seq_c, D_MODEL)
    x_l = x_sample.reshape(dec_batch * seq_l, D_MODEL)
    w_in_bf = w_in.astype(BF16)
    wa, wb, wc, wo = (w.astype(BF16) for w in (w_branch_a, w_branch_b, w_branch_c, w_out))
    wr_t = jnp.swapaxes(w_router, 1, 2)
    pool_scale = c_scale.reshape(DEPTH, 1, POOL_DIM)
    caches = ()
    for l in range(DEPTH):
        p_c = _inproj(x_c, mod, norm_w, w_in_bf, l, ctx_group)
        o_c, caches = _ctx_mix(p_c, b_sink[l], c_pool_w, pool_scale, l, seq_c, caches)
        branches_c = tuple((o_c, (lambda i, k=k: (i, k))) for k in range(N_BRANCH))
        x1_c, h2_c, lg_c = _merge(x_c, branches_c, p_c, mod, norm_w, wa, wb, wc, wo, wr_t, l, ctx_group)
        xs_c, gate_c, rc_c = _dispatch(lg_c, h2_c, seq_c)

        p_l = _inproj(x_l, mod, norm_w, w_in_bf, l, lat_group_inproj)
        table = _rpb_table(a_rpb[l])
        o_a = _na_latent(p_l, cak, cav, table, l, seq_l)
        o_b = _sw_latent(p_l, cbk, cbv, b_sink[l], l, seq_l)
        o_p = _pool_latent(p_l, c_pool_w, pool_scale, l, seq_l)
        first = lambda i: (i, 0)
        x1_l, h2_l, lg_l = _merge(x_l, ((o_a, first), (o_b, first), (o_p, first)), p_l, mod, norm_w,
                                  wa, wb, wc, wo, wr_t, l, lat_group_merge)
        xs_l, gate_l, rc_l = _dispatch(lg_l, h2_l, seq_l)

        ye_c, ye_l = _experts(xs_c, xs_l, gate_c, gate_l, w_gate_e, w_up_e, w_down_e, l)
        x_c = _combine(rc_c, ye_c, x1_c, mod, norm_w, l, seq_c, ctx_group)
        x_l = _combine(rc_l, ye_l, x1_l, mod, norm_w, l, seq_l, lat_group_batch)

    y_prompt = x_c.reshape(batch, seq_c, D_MODEL)
    y_sample = x_l.reshape(dec_batch, seq_l, D_MODEL)
    new_a_k, new_a_v, new_b_k, new_b_v = caches
    a_shape = (batch, DEPTH, seq_c, NA_HEADS, NA_HEAD_DIM)
    b_shape = (batch, DEPTH, seq_c, SW_KV_HEADS, SW_HEAD_DIM)
    return (y_prompt, y_sample, new_a_k.reshape(a_shape), new_a_v.reshape(a_shape),
            new_b_k.reshape(b_shape), new_b_v.reshape(b_shape))
```

```python
import functools

import numpy as np
import jax
import jax.numpy as jnp
from jax import lax
from jax.experimental import pallas as pl
from jax.experimental.pallas import tpu as pltpu

F32 = jnp.float32
BF16 = jnp.bfloat16

D_MODEL = 2048
DEPTH = 2
GRID_W = 64
NA_HEADS, NA_HEAD_DIM, NA_MAX_KH, NA_KW = 4, 128, 8, 16
SW_Q_HEADS, SW_KV_HEADS, SW_HEAD_DIM = 8, 2, 64
SW_GROUP = SW_Q_HEADS // SW_KV_HEADS
SW_WINDOW, SW_BLOCK = 128, 128
ROPE_THETA = 10000.0
POOL_WINDOWS = (2, 4, 8, 16)
POOL_GROUPS, POOL_GROUP_DIM = 4, 128
POOL_DIM = POOL_GROUPS * POOL_GROUP_DIM
A_DIM = NA_HEADS * NA_HEAD_DIM
B_Q_DIM = SW_Q_HEADS * SW_HEAD_DIM
B_KV_DIM = SW_KV_HEADS * SW_HEAD_DIM
N_BRANCH = 3
GATE_DIM = N_BRANCH * D_MODEL
QKVU_DIM = 3 * A_DIM + B_Q_DIM + 2 * B_KV_DIM + POOL_DIM
IN_DIM = QKVU_DIM + GATE_DIM
N_EXPERTS = 16
EC_FACTOR = 2
D_EXPERT = 1024
RMS_EPS = 1e-6
NEG_INF = -1e30

COL_QA = 0
COL_KB = (3 * A_DIM + B_Q_DIM) // B_KV_DIM
COL_U = COL_KB + 2

V7X_VMEM_BYTES = 64 * 1024 * 1024
V7X_VMEM_CEILING = 60000 * 1024
MIB = 1024 * 1024

ADA_ROWS = 16
ADA_TN = 1024
V7X_MXU_DIM = 256
LANES = 128
INPROJ_TM = 1024
INPROJ_TN = 1024
assert INPROJ_TN % V7X_MXU_DIM == 0 and GATE_DIM % INPROJ_TN == 0
QKVU_PAD = -(-QKVU_DIM // INPROJ_TN) * INPROJ_TN
INPROJ_MAIN_TILES = QKVU_PAD // INPROJ_TN
INPROJ_GATE_TILES = GATE_DIM // INPROJ_TN
MERGE_TM = 512
MERGE_CA = 1024
MERGE_CB = 512
MERGE_NA = D_MODEL // MERGE_CA
MERGE_NB = D_MODEL // MERGE_CB
EXPERT_TF = 256
EXPERT_TD = 512
RANK_CHUNK = 256
COMBINE_TR = 256
COMBINE_TD = 512
ROW_CHUNK = 64


def _vmem_limit(estimate_bytes):
    return int(min(V7X_VMEM_CEILING, max(32 * MIB, estimate_bytes + 8 * MIB)))


def _params(semantics, estimate_bytes):
    return pltpu.CompilerParams(dimension_semantics=semantics,
                                vmem_limit_bytes=_vmem_limit(estimate_bytes))


def _rms(x, g):
    ms = jnp.mean(x * x, axis=-1, keepdims=True)
    return x * lax.rsqrt(ms + RMS_EPS) * g


def _dot(a, b):
    return jnp.dot(a, b, preferred_element_type=F32)


def _dot_nt(a, b):
    return lax.dot_general(a, b, (((1,), (1,)), ((), ())), preferred_element_type=F32)


def _adaln_kernel(c_ref, w_ref, b_ref, o_ref):
    c = c_ref[...]
    s = (c * jax.nn.sigmoid(c)).astype(BF16)
    o_ref[0] = _dot(s, w_ref[0].astype(BF16)) + b_ref[0]


def _adaln(cond, w_ada, b_ada):
    n_out = w_ada.shape[-1]
    return pl.pallas_call(
        _adaln_kernel,
        grid=(DEPTH, n_out // ADA_TN),
        in_specs=[pl.BlockSpec((ADA_ROWS, D_MODEL), lambda l, j: (0, 0)),
                  pl.BlockSpec((1, D_MODEL, ADA_TN), lambda l, j: (l, 0, j)),
                  pl.BlockSpec((1, 1, ADA_TN), lambda l, j: (l, 0, j))],
        out_specs=pl.BlockSpec((1, ADA_ROWS, ADA_TN), lambda l, j: (l, 0, j)),
        out_shape=jax.ShapeDtypeStruct((DEPTH, ADA_ROWS, n_out), F32),
        compiler_params=_params(("parallel", "parallel"), 2 * D_MODEL * ADA_TN * 4),
        name="adaln",
    )(cond, w_ada, b_ada.reshape(DEPTH, 1, n_out))


def _inproj_kernel(x_ref, mod_ref, nw_ref, w_ref, o_ref, gate_ref, h_scr):
    j = pl.program_id(1)

    @pl.when(j == 0)
    def _():
        g = nw_ref[0:1, :]
        sc = 1.0 + mod_ref[1:2, :]
        sh = mod_ref[0:1, :]

        def body(r, carry):
            rows = pl.ds(pl.multiple_of(r * ROW_CHUNK, ROW_CHUNK), ROW_CHUNK)
            h_scr[rows, :] = (_rms(x_ref[rows, :], g) * sc + sh).astype(BF16)
            return carry

        lax.fori_loop(0, INPROJ_TM // ROW_CHUNK, body, 0)

    @pl.when(j < INPROJ_MAIN_TILES)
    def _():
        o_ref[...] = _dot(h_scr[...], w_ref[...])

    @pl.when(j >= INPROJ_MAIN_TILES)
    def _():
        gate_ref[...] = jax.nn.sigmoid(_dot(h_scr[...], w_ref[...])).astype(BF16)


def _mod_spec(layer, group_of, n_grid):
    if n_grid == 1:
        return pl.BlockSpec((None, None, 6, D_MODEL), lambda i: (layer, group_of(i), 0, 0))
    return pl.BlockSpec((None, None, 6, D_MODEL), lambda i, j: (layer, group_of(i), 0, 0))


def _inproj(x, mod, norm_w, w_in_bf, layer, group_of_tile):
    n = x.shape[0]
    est = (2 * INPROJ_TM * D_MODEL * 4 + 2 * D_MODEL * INPROJ_TN * 2 + 2 * INPROJ_TM * INPROJ_TN * (4 + 2)
           + INPROJ_TM * D_MODEL * 2 + 2 * INPROJ_TM * INPROJ_TN * 4)
    return pl.pallas_call(
        _inproj_kernel,
        grid=(n // INPROJ_TM, INPROJ_MAIN_TILES + INPROJ_GATE_TILES),
        in_specs=[pl.BlockSpec((INPROJ_TM, D_MODEL), lambda i, j: (i, 0)),
                  _mod_spec(layer, group_of_tile, 2),
                  pl.BlockSpec((None, 4, D_MODEL), lambda i, j: (layer, 0, 0)),
                  pl.BlockSpec((None, D_MODEL, INPROJ_TN), lambda i, j: (layer, 0, j))],
        out_specs=[pl.BlockSpec((INPROJ_TM, INPROJ_TN), lambda i, j: (i, jnp.minimum(j, INPROJ_MAIN_TILES - 1))),
                   pl.BlockSpec((INPROJ_TM, INPROJ_TN), lambda i, j: (i, jnp.maximum(j - INPROJ_MAIN_TILES, 0)))],
        out_shape=[jax.ShapeDtypeStruct((n, QKVU_PAD), F32), jax.ShapeDtypeStruct((n, GATE_DIM), BF16)],
        scratch_shapes=[pltpu.VMEM((INPROJ_TM, D_MODEL), BF16)],
        compiler_params=_params(("parallel", "arbitrary"), est),
        name="inproj",
    )(x, mod, norm_w, w_in_bf)


def _joint_attention(q, segments, scale, sink=None):
    scores = []
    for k, _, bias, mask in segments:
        s = _dot_nt(q, k) * scale
        if bias is not None:
            s = s + bias
        if mask is not None:
            s = jnp.where(mask, s, NEG_INF)
        scores.append(s)
    m = scores[0].max(axis=-1, keepdims=True)
    for s in scores[1:]:
        m = jnp.maximum(m, s.max(axis=-1, keepdims=True))
    if sink is not None:
        m = jnp.maximum(m, sink)
    denom = jnp.exp(sink - m) if sink is not None else 0.0
    acc = None
    for s, (_, v, _, _) in zip(scores, segments):
        e = jnp.exp(s - m)
        denom = denom + e.sum(axis=-1, keepdims=True)
        pv = _dot(e.astype(BF16), v)
        acc = pv if acc is None else acc + pv
    return acc / denom


def _pool_group(u, window, pw_bf, scale_row):
    seq = u.shape[0]
    pad = 8
    n = seq + 2 * pad
    z = jnp.zeros((pad, POOL_GROUP_DIM), F32)
    p = jnp.concatenate([z, u, z], axis=0)
    k = 1
    while k < window:
        p = p + pltpu.roll(p, n - k, 0)
        k *= 2
    win = pltpu.roll(p, window // 2, 0)[pad:pad + seq]
    t = lax.broadcasted_iota(jnp.int32, (seq, 1), 0)
    lo = jnp.maximum(t - window // 2, 0)
    hi = jnp.minimum(t - window // 2 + window, seq)
    cnt = (hi - lo).astype(F32)
    pooled = win / cnt - u
    return _dot(pooled.astype(BF16), pw_bf) * scale_row


def _sink_column(sink_ref, kv_head, rows_per_head):
    r = lax.broadcasted_iota(jnp.int32, (SW_GROUP * rows_per_head, 1), 0)
    col = jnp.full((SW_GROUP * rows_per_head, 1), sink_ref[kv_head * SW_GROUP], F32)
    for g in range(1, SW_GROUP):
        col = jnp.where(r >= g * rows_per_head, sink_ref[kv_head * SW_GROUP + g], col)
    return col


def _ctx_mix_kernel(sink_ref, qa_ref, ka_ref, va_ref, qb_ref, kb_ref, vb_ref, u0_ref, u1_ref, u2_ref, u3_ref,
                    pw_ref, ps_ref, *rest):
    o_ref, nak_ref, nav_ref, nbk_ref, nbv_ref = rest[-5:]
    seq = qa_ref.shape[0]
    nak_ref[...] = ka_ref[...]
    nav_ref[...] = va_ref[...]
    nbk_ref[...] = kb_ref[...]
    nbv_ref[...] = vb_ref[...]
    for h in range(NA_HEADS):
        sl = slice(h * NA_HEAD_DIM, (h + 1) * NA_HEAD_DIM)
        o = _joint_attention(qa_ref[:, sl].astype(BF16),
                             [(ka_ref[:, sl].astype(BF16), va_ref[:, sl].astype(BF16), None, None)],
                             NA_HEAD_DIM ** -0.5)
        o_ref[:, sl] = o
    for hk in range(SW_KV_HEADS):
        ksl = slice(hk * SW_HEAD_DIM, (hk + 1) * SW_HEAD_DIM)
        q = jnp.concatenate(
            [qb_ref[:, (hk * SW_GROUP + g) * SW_HEAD_DIM:(hk * SW_GROUP + g + 1) * SW_HEAD_DIM] for g in range(SW_GROUP)],
            axis=0).astype(BF16)
        o = _joint_attention(q, [(kb_ref[:, ksl].astype(BF16), vb_ref[:, ksl].astype(BF16), None, None)],
                             SW_HEAD_DIM ** -0.5, sink=_sink_column(sink_ref, hk, seq))
        for g in range(SW_GROUP):
            c0 = A_DIM + (hk * SW_GROUP + g) * SW_HEAD_DIM
            o_ref[:, c0:c0 + SW_HEAD_DIM] = o[g * seq:(g + 1) * seq]
    for gi, u_ref in enumerate((u0_ref, u1_ref, u2_ref, u3_ref)):
        c0 = A_DIM + B_Q_DIM + gi * POOL_GROUP_DIM
        o_ref[:, c0:c0 + POOL_GROUP_DIM] = _pool_group(
            u_ref[...], POOL_WINDOWS[gi], pw_ref[gi].astype(BF16), ps_ref[:, gi * POOL_GROUP_DIM:(gi + 1) * POOL_GROUP_DIM])


def _ctx_mix(p, sink_l, pool_w, pool_scale, layer, seq, caches):
    n = p.shape[0]
    batch = n // seq
    wide = lambda c: pl.BlockSpec((seq, A_DIM), lambda b, c=c: (b, c))
    narrow = lambda c: pl.BlockSpec((seq, B_KV_DIM), lambda b, c=c: (b, c))
    in_specs = [pl.BlockSpec(memory_space=pltpu.SMEM)]
    in_specs += [wide(COL_QA + i) for i in range(4)]
    in_specs += [narrow(COL_KB), narrow(COL_KB + 1)]
    in_specs += [narrow(COL_U + g) for g in range(POOL_GROUPS)]
    in_specs += [pl.BlockSpec((None, POOL_GROUPS, POOL_GROUP_DIM, POOL_GROUP_DIM), lambda b: (layer, 0, 0, 0)),
                 pl.BlockSpec((None, 1, POOL_DIM), lambda b: (layer, 0, 0))]
    n_fixed = len(in_specs)
    in_specs += [pl.BlockSpec(memory_space=pl.ANY)] * len(caches)
    cache_spec = lambda width: pl.BlockSpec((None, None, seq, width), lambda b: (b, layer, 0, 0))
    cache_shape = lambda width: jax.ShapeDtypeStruct((batch, DEPTH, seq, width), F32)
    outs = pl.pallas_call(
        _ctx_mix_kernel,
        grid=(batch,),
        in_specs=in_specs,
        out_specs=[pl.BlockSpec((seq, 3 * A_DIM), lambda b: (b, 0)),
                   cache_spec(A_DIM), cache_spec(A_DIM), cache_spec(B_KV_DIM), cache_spec(B_KV_DIM)],
        out_shape=[jax.ShapeDtypeStruct((n, 3 * A_DIM), F32),
                   cache_shape(A_DIM), cache_shape(A_DIM), cache_shape(B_KV_DIM), cache_shape(B_KV_DIM)],
        input_output_aliases={n_fixed + k: 1 + k for k in range(len(caches))},
        compiler_params=_params(("parallel",), 24 * MIB),
        name="ctx_mix",
    )(sink_l, p, p, p, p, p, p, p, p, p, p, pool_w, pool_scale, *caches)
    return outs[0], tuple(outs[1:])


NA_PAIR_ROWS = 2 * NA_MAX_KH - 2


def _rpb_table_kernel(rpb_ref, t_ref):
    lane = lax.broadcasted_iota(jnp.int32, (GRID_W, 2 * GRID_W), 1)
    qc = lax.broadcasted_iota(jnp.int32, (GRID_W, 2 * GRID_W), 0)
    kc = lane & (GRID_W - 1)
    upper = lane >= GRID_W
    dcm = jnp.clip(kc - qc + NA_KW - 1, 0, 2 * NA_KW - 2)
    col0 = jnp.clip(qc - NA_KW // 2, 0, GRID_W - NA_KW)
    inside = (kc >= col0) & (kc < col0 + NA_KW)
    n_dc = 2 * NA_KW - 1
    n_dr = 2 * NA_MAX_KH - 1

    def body(i, carry):
        h = i // NA_PAIR_ROWS
        dr = i - h * NA_PAIR_ROWS
        base = (h * n_dr + dr) * n_dc
        acc = jnp.zeros((GRID_W, 2 * GRID_W), F32)
        for dc in range(n_dc):
            val = jnp.where(upper, rpb_ref[base + n_dc + dc], rpb_ref[base + dc])
            acc = jnp.where(dcm == dc, val, acc)
        t_ref[i] = jnp.where(inside, acc, NEG_INF)
        return carry

    lax.fori_loop(0, NA_HEADS * NA_PAIR_ROWS, body, 0)


def _rpb_table(rpb_l):
    return pl.pallas_call(
        _rpb_table_kernel,
        in_specs=[pl.BlockSpec(memory_space=pltpu.SMEM)],
        out_specs=pl.BlockSpec(memory_space=pltpu.VMEM),
        out_shape=jax.ShapeDtypeStruct((NA_HEADS * NA_PAIR_ROWS, GRID_W, 2 * GRID_W), F32),
        name="rpb_table",
    )(rpb_l.reshape(-1))


def _na_kernel(q_ref, k_ref, v_ref, kc_ref, vc_ref, t_ref, o_ref, *, rows):
    qr = pl.program_id(1)
    row0 = jnp.clip(qr - NA_MAX_KH // 2, 0, rows - NA_MAX_KH)
    start = pl.multiple_of(row0 * GRID_W, GRID_W)
    nkeys = NA_MAX_KH * GRID_W
    d0 = row0 - qr + NA_MAX_KH - 1
    for h in range(NA_HEADS):
        sl = slice(h * NA_HEAD_DIM, (h + 1) * NA_HEAD_DIM)
        bias = jnp.concatenate([t_ref[h * NA_PAIR_ROWS + d0 + 2 * i] for i in range(NA_MAX_KH // 2)], axis=1)
        o = _joint_attention(
            q_ref[:, sl].astype(BF16),
            [(k_ref[pl.ds(start, nkeys), sl].astype(BF16), v_ref[pl.ds(start, nkeys), sl].astype(BF16), bias, None),
             (kc_ref[:, sl].astype(BF16), vc_ref[:, sl].astype(BF16), None, None)],
            NA_HEAD_DIM ** -0.5)
        o_ref[:, sl] = o


def _na_latent(p, cache_k, cache_v, table, layer, seq):
    n = p.shape[0]
    rows = seq // GRID_W
    past = cache_k.shape[2]
    ctx_spec = pl.BlockSpec((None, None, past, A_DIM), lambda b, r: (b, layer, 0, 0))
    return pl.pallas_call(
        functools.partial(_na_kernel, rows=rows),
        grid=(n // seq, rows),
        in_specs=[pl.BlockSpec((GRID_W, A_DIM), lambda b, r: (b * rows + r, COL_QA)),
                  pl.BlockSpec((seq, A_DIM), lambda b, r: (b, COL_QA + 1)),
                  pl.BlockSpec((seq, A_DIM), lambda b, r: (b, COL_QA + 2)),
                  ctx_spec, ctx_spec,
                  pl.BlockSpec(table.shape, lambda b, r: (0, 0, 0))],
        out_specs=pl.BlockSpec((GRID_W, A_DIM), lambda b, r: (b * rows + r, 0)),
        out_shape=jax.ShapeDtypeStruct((n, A_DIM), F32),
        compiler_params=_params(("parallel", "arbitrary"), 16 * MIB),
        name="na_latent",
    )(p, p, p, cache_k, cache_v, table)


def _rope_tables(seq):
    nfreq = SW_HEAD_DIM // 4
    inv = 1.0 / (ROPE_THETA ** (np.arange(nfreq, dtype=np.float32) / np.float32(nfreq)))
    t = np.arange(seq)
    pos = (t // GRID_W, t % GRID_W)
    cos = np.zeros((seq, SW_HEAD_DIM), np.float32)
    sin_next = np.zeros((seq, SW_HEAD_DIM), np.float32)
    sin_prev = np.zeros((seq, SW_HEAD_DIM), np.float32)
    for a in range(2):
        ang = pos[a].astype(np.float32)[:, None] * inv[None, :].astype(np.float32)
        c, s = np.cos(ang).astype(np.float32), np.sin(ang).astype(np.float32)
        lo = 2 * a * nfreq
        cos[:, lo:lo + nfreq] = c
        cos[:, lo + nfreq:lo + 2 * nfreq] = c
        sin_next[:, lo:lo + nfreq] = -s
        sin_prev[:, lo + nfreq:lo + 2 * nfreq] = s
    tile = lambda x: jnp.asarray(np.tile(x, (1, 128 // SW_HEAD_DIM)))
    return tile(cos), tile(sin_next), tile(sin_prev)


def _rope(x, cos, sin_next, sin_prev):
    nfreq = SW_HEAD_DIM // 4
    return x * cos + pltpu.roll(x, 128 - nfreq, 1) * sin_next + pltpu.roll(x, nfreq, 1) * sin_prev


def _sw_kernel(sink_ref, q_ref, k_ref, v_ref, kc_ref, vc_ref, cq_ref, snq_ref, spq_ref, ck_ref, snk_ref, spk_ref,
               o_ref, kr_scr, *, seq):
    n = pl.program_id(1)

    @pl.when(n == 0)
    def _():
        kr_scr[...] = _rope(k_ref[...], ck_ref[...], snk_ref[...], spk_ref[...]).astype(BF16)

    nwin = 3 * SW_BLOCK
    kstart = pl.multiple_of(jnp.clip((n - 1) * SW_BLOCK, 0, seq - nwin), SW_BLOCK)
    cq, snq, spq = cq_ref[...], snq_ref[...], spq_ref[...]
    q = jnp.concatenate([_rope(q_ref[:, c * 128:(c + 1) * 128], cq, snq, spq) for c in range(B_Q_DIM // 128)],
                        axis=1).astype(BF16)
    rows = SW_GROUP * SW_BLOCK
    qpos = n * SW_BLOCK + (lax.broadcasted_iota(jnp.int32, (rows, nwin), 0) & (SW_BLOCK - 1))
    kpos = kstart + lax.broadcasted_iota(jnp.int32, (rows, nwin), 1)
    band = jnp.abs(qpos - kpos) <= SW_WINDOW
    for hk in range(SW_KV_HEADS):
        ksl = slice(hk * SW_HEAD_DIM, (hk + 1) * SW_HEAD_DIM)
        qs = jnp.concatenate(
            [q[:, (hk * SW_GROUP + g) * SW_HEAD_DIM:(hk * SW_GROUP + g + 1) * SW_HEAD_DIM] for g in range(SW_GROUP)], axis=0)
        o = _joint_attention(
            qs,
            [(kr_scr[pl.ds(kstart, nwin), ksl], v_ref[pl.ds(kstart, nwin), ksl].astype(BF16), None, band),
             (kc_ref[:, ksl].astype(BF16), vc_ref[:, ksl].astype(BF16), None, None)],
            SW_HEAD_DIM ** -0.5, sink=_sink_column(sink_ref, hk, SW_BLOCK))
        for g in range(SW_GROUP):
            c0 = (hk * SW_GROUP + g) * SW_HEAD_DIM
            o_ref[:, c0:c0 + SW_HEAD_DIM] = o[g * SW_BLOCK:(g + 1) * SW_BLOCK]


def _sw_latent(p, cache_k, cache_v, sink_l, layer, seq):
    n = p.shape[0]
    nb = seq // SW_BLOCK
    past = cache_k.shape[2]
    cos, sin_next, sin_prev = _rope_tables(seq)
    ctx_spec = pl.BlockSpec((None, None, past, B_KV_DIM), lambda b, i: (b, layer, 0, 0))
    tab_q = pl.BlockSpec((SW_BLOCK, 128), lambda b, i: (i, 0))
    tab_k = pl.BlockSpec((seq, 128), lambda b, i: (0, 0))
    return pl.pallas_call(
        functools.partial(_sw_kernel, seq=seq),
        grid=(n // seq, nb),
        in_specs=[pl.BlockSpec(memory_space=pltpu.SMEM),
                  pl.BlockSpec((SW_BLOCK, B_Q_DIM), lambda b, i: (b * nb + i, COL_QA + 3)),
                  pl.BlockSpec((seq, B_KV_DIM), lambda b, i: (b, COL_KB)),
                  pl.BlockSpec((seq, B_KV_DIM), lambda b, i: (b, COL_KB + 1)),
                  ctx_spec, ctx_spec, tab_q, tab_q, tab_q, tab_k, tab_k, tab_k],
        out_specs=pl.BlockSpec((SW_BLOCK, B_Q_DIM), lambda b, i: (b * nb + i, 0)),
        out_shape=jax.ShapeDtypeStruct((n, B_Q_DIM), F32),
        scratch_shapes=[pltpu.VMEM((seq, B_KV_DIM), BF16)],
        compiler_params=_params(("parallel", "arbitrary"), 16 * MIB),
        name="sw_latent",
    )(sink_l, p, p, p, cache_k, cache_v, cos, sin_next, sin_prev, cos, sin_next, sin_prev)


def _pool_kernel(u0_ref, u1_ref, u2_ref, u3_ref, pw_ref, ps_ref, o_ref):
    for gi, u_ref in enumerate((u0_ref, u1_ref, u2_ref, u3_ref)):
        sl = slice(gi * POOL_GROUP_DIM, (gi + 1) * POOL_GROUP_DIM)
        o_ref[:, sl] = _pool_group(u_ref[...], POOL_WINDOWS[gi], pw_ref[gi].astype(BF16), ps_ref[:, sl])


def _pool_latent(p, pool_w, pool_scale, layer, seq):
    n = p.shape[0]
    return pl.pallas_call(
        _pool_kernel,
        grid=(n // seq,),
        in_specs=[pl.BlockSpec((seq, POOL_GROUP_DIM), lambda b, g=g: (b, COL_U + g)) for g in range(POOL_GROUPS)]
        + [pl.BlockSpec((None, POOL_GROUPS, POOL_GROUP_DIM, POOL_GROUP_DIM), lambda b: (layer, 0, 0, 0)),
           pl.BlockSpec((None, 1, POOL_DIM), lambda b: (layer, 0, 0))],
        out_specs=pl.BlockSpec((seq, POOL_DIM), lambda b: (b, 0)),
        out_shape=jax.ShapeDtypeStruct((n, POOL_DIM), F32),
        compiler_params=_params(("parallel",), 16 * MIB),
        name="pool_latent",
    )(p, p, p, p, pool_w, pool_scale)


def _split_bf16(x):
    hi = x.astype(BF16)
    return hi, (x - hi.astype(F32)).astype(BF16)


def _merge_kernel(x_ref, oa_ref, ob_ref, oc_ref, ga_ref, gb_ref, gc_ref, mod_ref, nw_ref,
                  wa_ref, wb_ref, wc_ref, wo_ref, wr_ref, x1_ref, h2_ref, lg_ref, m_scr, y_scr):
    s = pl.program_id(1)

    @pl.when(s < MERGE_NA)
    def _():
        m = (ga_ref[...].astype(F32) * _dot(oa_ref[...].astype(BF16), wa_ref[...])
             + gb_ref[...].astype(F32) * _dot(ob_ref[...].astype(BF16), wb_ref[...])
             + gc_ref[...].astype(F32) * _dot(oc_ref[...].astype(BF16), wc_ref[...]))
        m_scr[s] = m.astype(BF16)

    @pl.when(s >= MERGE_NA)
    def _():
        acc = _dot(m_scr[0], wo_ref[0:MERGE_CA, :])
        for f in range(1, MERGE_NA):
            acc = acc + _dot(m_scr[f], wo_ref[f * MERGE_CA:(f + 1) * MERGE_CA, :])
        y_scr[s - MERGE_NA] = acc

    @pl.when(s == MERGE_NA + MERGE_NB - 1)
    def _():
        tiles = [slice(c * MERGE_CB, (c + 1) * MERGE_CB) for c in range(MERGE_NB)]
        ss = jnp.zeros((x_ref.shape[0], 1), F32)
        for c in range(MERGE_NB):
            y = y_scr[c]
            ss = ss + (y * y).sum(axis=-1, keepdims=True)
        r1 = lax.rsqrt(ss / D_MODEL + RMS_EPS)
        ss = jnp.zeros((x_ref.shape[0], 1), F32)
        for c, cols in enumerate(tiles):
            x1 = x_ref[:, cols] + mod_ref[2:3, cols] * (y_scr[c] * r1 * nw_ref[1:2, cols])
            x1_ref[:, cols] = x1
            ss = ss + (x1 * x1).sum(axis=-1, keepdims=True)
        r2 = lax.rsqrt(ss / D_MODEL + RMS_EPS)
        lg = jnp.zeros(lg_ref.shape, F32)
        for cols in tiles:
            h2 = x1_ref[:, cols] * r2 * nw_ref[2:3, cols] * (1.0 + mod_ref[4:5, cols]) + mod_ref[3:4, cols]
            h2_ref[:, cols] = h2.astype(BF16)
            h_hi, h_lo = _split_bf16(h2)
            w_hi, w_lo = _split_bf16(wr_ref[:, cols])
            lg = lg + (_dot_nt(w_hi, h_hi) + (_dot_nt(w_hi, h_lo) + _dot_nt(w_lo, h_hi)))
        lg_ref[...] = lg


def _merge(x, branches, gates, mod, norm_w, wa, wb, wc, wo, wr_t, layer, group_of_tile):
    n = x.shape[0]
    tm, ca, cb = MERGE_TM, MERGE_CA, MERGE_CB
    row = lambda width: pl.BlockSpec((tm, width), lambda i, s: (i, 0))
    a_tile = lambda s: jnp.minimum(s, MERGE_NA - 1)
    b_tile = lambda s: jnp.maximum(s - MERGE_NA, 0)
    gate = lambda k: pl.BlockSpec((tm, ca), lambda i, s, k=k: (i, k * MERGE_NA + a_tile(s)))
    branch = lambda col: pl.BlockSpec((tm, A_DIM), lambda i, s: (i, col))
    w_branch = lambda rows: pl.BlockSpec((None, rows, ca), lambda i, s: (layer, 0, a_tile(s)))
    const = lambda *shape: pl.BlockSpec((None,) + shape, lambda i, s: (layer,) + (0,) * len(shape))
    oa, ob, oc = branches
    est = (2 * tm * D_MODEL * (4 + 4 + 2) + 2 * tm * 3 * A_DIM * 4 + 2 * 3 * tm * ca * 2 + 2 * 3 * A_DIM * ca * 2
           + 2 * D_MODEL * cb * 2 + tm * D_MODEL * (2 + 4) + 4 * tm * ca * 4)
    return pl.pallas_call(
        _merge_kernel,
        grid=(n // tm, MERGE_NA + MERGE_NB),
        in_specs=[row(D_MODEL), branch(oa[1]), branch(ob[1]), branch(oc[1]),
                  gate(0), gate(1), gate(2),
                  _mod_spec(layer, group_of_tile, 2),
                  const(4, D_MODEL),
                  w_branch(A_DIM), w_branch(B_Q_DIM), w_branch(POOL_DIM),
                  pl.BlockSpec((None, D_MODEL, cb), lambda i, s: (layer, 0, b_tile(s))),
                  const(N_EXPERTS, D_MODEL)],
        out_specs=[row(D_MODEL), row(D_MODEL), pl.BlockSpec((N_EXPERTS, tm), lambda i, s: (0, i))],
        out_shape=[jax.ShapeDtypeStruct((n, D_MODEL), F32), jax.ShapeDtypeStruct((n, D_MODEL), BF16),
                   jax.ShapeDtypeStruct((N_EXPERTS, n), F32)],
        scratch_shapes=[pltpu.VMEM((MERGE_NA, tm, ca), BF16), pltpu.VMEM((MERGE_NB, tm, cb), F32)],
        compiler_params=_params(("parallel", "arbitrary"), est),
        name="merge",
    )(x, oa[0], ob[0], oc[0], gates, gates, gates, mod, norm_w, wa, wb, wc, wo, wr_t)


def _dispatch_kernel(lg_ref, h_ref, xs_ref, gate_ref, rc_ref, aff_scr, sel_scr, *, seq, cap):
    lg = lg_ref[...]
    e = jnp.exp(lg - lg.max(axis=0, keepdims=True))
    aff_scr[...] = e / e.sum(axis=0, keepdims=True)
    rc_ref[...] = jnp.zeros(rc_ref.shape, F32)
    chunk = min(seq, RANK_CHUNK)
    lane_e = lax.broadcasted_iota(jnp.int32, (chunk, LANES), 1)
    slot = lax.broadcasted_iota(jnp.int32, (cap, seq), 0).astype(F32)

    def body(ex, carry):
        row = aff_scr[pl.ds(ex, 1), :]
        rowb = jnp.broadcast_to(row, (chunk, seq))
        rank_row = jnp.zeros((1, seq), F32)
        for c in range(seq // chunk):
            tr = c * chunk + lax.broadcasted_iota(jnp.int32, (chunk, seq), 0)
            tc = lax.broadcasted_iota(jnp.int32, (chunk, seq), 1)
            col = jnp.where(tr == tc, rowb, 0.0).sum(axis=1, keepdims=True)
            beats = ((col > rowb) | ((col == rowb) & (tr < tc))).astype(F32)
            rank_row = rank_row + beats.sum(axis=0, keepdims=True)
            rank_col = (seq - 1.0) - beats.sum(axis=1, keepdims=True)
            rows = slice(c * chunk, (c + 1) * chunk)
            rc_ref[rows, :] = jnp.where(lane_e == ex, rank_col, rc_ref[rows, :])
        sel = slot == rank_row
        sel_scr[pl.ds(pl.multiple_of(ex * cap, cap), cap), :] = sel.astype(BF16)
        gate = jnp.where(sel, jnp.broadcast_to(row, (cap, seq)), 0.0).sum(axis=1, keepdims=True)
        gate_ref[ex] = jnp.broadcast_to(gate, (cap, LANES))
        return carry

    lax.fori_loop(0, N_EXPERTS, body, 0, unroll=4 if seq <= RANK_CHUNK else 1)
    xs = _dot(sel_scr[...], h_ref[...]).astype(BF16)
    xs_ref[...] = xs.reshape(N_EXPERTS, cap, D_MODEL)


def _dispatch(lg_t, h2, seq):
    n = h2.shape[0]
    nb = n // seq
    cap = EC_FACTOR * seq // N_EXPERTS
    est = (2 * seq * D_MODEL * 2 + 2 * N_EXPERTS * cap * D_MODEL * 2 + N_EXPERTS * cap * seq * 2
           + N_EXPERTS * cap * D_MODEL * 4 + 12 * min(seq, RANK_CHUNK) * seq * 4)
    return pl.pallas_call(
        functools.partial(_dispatch_kernel, seq=seq, cap=cap),
        grid=(nb,),
        in_specs=[pl.BlockSpec((N_EXPERTS, seq), lambda b: (0, b)),
                  pl.BlockSpec((seq, D_MODEL), lambda b: (b, 0))],
        out_specs=[pl.BlockSpec((N_EXPERTS, cap, D_MODEL), lambda b: (0, b, 0)),
                   pl.BlockSpec((N_EXPERTS, cap, LANES), lambda b: (0, b, 0)),
                   pl.BlockSpec((seq, LANES), lambda b: (b, 0))],
        out_shape=[jax.ShapeDtypeStruct((N_EXPERTS, nb * cap, D_MODEL), BF16),
                   jax.ShapeDtypeStruct((N_EXPERTS, nb * cap, LANES), F32),
                   jax.ShapeDtypeStruct((n, LANES), F32)],
        scratch_shapes=[pltpu.VMEM((N_EXPERTS, seq), F32), pltpu.VMEM((N_EXPERTS * cap, seq), BF16)],
        compiler_params=_params(("parallel",), est),
        name="dispatch",
    )(lg_t, h2)


EXPERT_NF = D_EXPERT // EXPERT_TF
EXPERT_ND = D_MODEL // EXPERT_TD


def _expert_kernel(xc_ref, xl_ref, gc_ref, gl_ref, wg_ref, wu_ref, wd_ref, yc_ref, yl_ref, hc_scr, hl_scr):
    s = pl.program_id(1)

    @pl.when(s < EXPERT_NF)
    def _():
        wg = wg_ref[0].astype(BF16)
        wu = wu_ref[0].astype(BF16)
        for x_ref, h_scr in ((xc_ref, hc_scr), (xl_ref, hl_scr)):
            x = x_ref[0]
            a = _dot(x, wg)
            h_scr[s] = ((a * jax.nn.sigmoid(a)) * _dot(x, wu)).astype(BF16)

    @pl.when(s >= EXPERT_NF)
    def _():
        wd = wd_ref[0].astype(BF16)
        for h_scr, g_ref, y_ref in ((hc_scr, gc_ref, yc_ref), (hl_scr, gl_ref, yl_ref)):
            acc = _dot(h_scr[0], wd[0:EXPERT_TF])
            for f in range(1, EXPERT_NF):
                acc = acc + _dot(h_scr[f], wd[f * EXPERT_TF:(f + 1) * EXPERT_TF])
            y_ref[0] = acc * g_ref[0, :, 0:1]


def _experts(xs_c, xs_l, gate_c, gate_l, w_gate, w_up, w_down, layer):
    sc, sl = xs_c.shape[1], xs_l.shape[1]
    tf, td = EXPERT_TF, EXPERT_TD
    slots = lambda s, width: pl.BlockSpec((1, s, width), lambda e, t: (e, 0, 0))
    up_tile = lambda e, t: (layer, e, 0, jnp.minimum(t, EXPERT_NF - 1))
    down_tile = lambda e, t: (e, 0, jnp.maximum(t - EXPERT_NF, 0))
    est = (2 * (sc + sl) * D_MODEL * 2 + 2 * 2 * D_MODEL * tf * 4 + 2 * D_EXPERT * td * 4 + 2 * (sc + sl) * td * 4
           + (sc + sl) * D_EXPERT * 2 + 2 * D_MODEL * tf * 2 + D_EXPERT * td * 2 + 6 * sc * max(tf, td) * 4)
    return pl.pallas_call(
        _expert_kernel,
        grid=(N_EXPERTS, EXPERT_NF + EXPERT_ND),
        in_specs=[slots(sc, D_MODEL), slots(sl, D_MODEL), slots(sc, LANES), slots(sl, LANES),
                  pl.BlockSpec((None, 1, D_MODEL, tf), up_tile),
                  pl.BlockSpec((None, 1, D_MODEL, tf), up_tile),
                  pl.BlockSpec((None, 1, D_EXPERT, td), lambda e, t: (layer,) + down_tile(e, t))],
        out_specs=[pl.BlockSpec((1, sc, td), down_tile), pl.BlockSpec((1, sl, td), down_tile)],
        out_shape=[jax.ShapeDtypeStruct((N_EXPERTS, sc, D_MODEL), F32),
                   jax.ShapeDtypeStruct((N_EXPERTS, sl, D_MODEL), F32)],
        scratch_shapes=[pltpu.VMEM((EXPERT_NF, sc, tf), BF16), pltpu.VMEM((EXPERT_NF, sl, tf), BF16)],
        compiler_params=_params(("parallel", "arbitrary"), est),
        name="experts",
    )(xs_c, xs_l, gate_c, gate_l, w_gate, w_up, w_down)


def _combine_kernel(rc_ref, ye_ref, x1_ref, mod_ref, nw_ref, o_ref, m_scr, ffn_scr, *, cap):
    rc = rc_ref[...]
    slot = lax.broadcasted_iota(jnp.int32, (rc.shape[0], cap), 1).astype(F32)
    for ex in range(N_EXPERTS):
        m_scr[:, ex * cap:(ex + 1) * cap] = (rc[:, ex:ex + 1] == slot).astype(BF16)
    onehot = m_scr[...]
    for c in range(D_MODEL // COMBINE_TD):
        cols = slice(c * COMBINE_TD, (c + 1) * COMBINE_TD)
        ye = ye_ref[:, :, cols].reshape(N_EXPERTS * cap, COMBINE_TD)
        hi, lo = _split_bf16(ye)
        ffn_scr[:, cols] = _dot(onehot, hi) + _dot(onehot, lo)
    o_ref[...] = x1_ref[...] + mod_ref[5:6, :] * _rms(ffn_scr[...], nw_ref[3:4, :])


def _combine(rank_col, ye, x1, mod, norm_w, layer, seq, group_of_batch):
    n = x1.shape[0]
    cap = EC_FACTOR * seq // N_EXPERTS
    tr = COMBINE_TR
    per = seq // tr
    ye_buffers = 2 if per == 1 else 1
    est = (ye_buffers * N_EXPERTS * cap * D_MODEL * 4 + 4 * tr * D_MODEL * 4 + tr * N_EXPERTS * cap * 2
           + tr * D_MODEL * 4 + 3 * N_EXPERTS * cap * COMBINE_TD * 4)
    return pl.pallas_call(
        functools.partial(_combine_kernel, cap=cap),
        grid=(n // seq, per),
        in_specs=[pl.BlockSpec((tr, LANES), lambda b, i: (b * per + i, 0)),
                  pl.BlockSpec((N_EXPERTS, cap, D_MODEL), lambda b, i: (0, b, 0),
                               pipeline_mode=pl.Buffered(ye_buffers)),
                  pl.BlockSpec((tr, D_MODEL), lambda b, i: (b * per + i, 0)),
                  pl.BlockSpec((None, None, 6, D_MODEL), lambda b, i: (layer, group_of_batch(b), 0, 0)),
                  pl.BlockSpec((None, 4, D_MODEL), lambda b, i: (layer, 0, 0))],
        out_specs=pl.BlockSpec((tr, D_MODEL), lambda b, i: (b * per + i, 0)),
        out_shape=jax.ShapeDtypeStruct((n, D_MODEL), F32),
        scratch_shapes=[pltpu.VMEM((tr, N_EXPERTS * cap), BF16), pltpu.VMEM((tr, D_MODEL), F32)],
        compiler_params=_params(("parallel", "arbitrary"), est),
        name="combine",
    )(rank_col, ye, x1, mod, norm_w)


def kernel(x_prompt, x_sample, c, cache_a_k, cache_a_v, cache_b_k, cache_b_v, c_ctx, norm_w, w_ada, b_ada, w_in, a_rpb,
           b_sink, c_pool_w, c_scale, w_branch_a, w_branch_b, w_branch_c, w_out, w_router, w_gate_e, w_up_e, w_down_e):
    batch, seq_c, _ = x_prompt.shape
    dec_batch, seq_l, _ = x_sample.shape
    past = cache_a_k.shape[2]

    cond = jnp.zeros((ADA_ROWS, D_MODEL), F32).at[0].set(c_ctx).at[1:1 + dec_batch].set(c)
    mod = _adaln(cond, w_ada, b_ada).reshape(DEPTH, ADA_ROWS, 6, D_MODEL)

    ctx_group = lambda i: 0
    lat_group_inproj = lambda i: 1 + i // (seq_l // INPROJ_TM)
    lat_group_merge = lambda i: 1 + i // (seq_l // MERGE_TM)
    lat_group_batch = lambda b: 1 + b

    cak = cache_a_k.reshape(dec_batch, DEPTH, past, A_DIM)
    cav = cache_a_v.reshape(dec_batch, DEPTH, past, A_DIM)
    cbk = cache_b_k.reshape(dec_batch, DEPTH, past, B_KV_DIM)
    cbv = cache_b_v.reshape(dec_batch, DEPTH, past, B_KV_DIM)

    x_c = x_prompt.reshape(batch * seq_c, D_MODEL)
    x_l = x_sample.reshape(dec_batch * seq_l, D_MODEL)
    w_in_bf = jnp.pad(w_in[:, :, :QKVU_DIM].astype(BF16), ((0, 0), (0, 0), (0, QKVU_PAD - QKVU_DIM)))
    w_in_bf = jnp.concatenate([w_in_bf, w_in[:, :, QKVU_DIM:].astype(BF16)], axis=2)
    wa, wb, wc, wo = (w.astype(BF16) for w in (w_branch_a, w_branch_b, w_branch_c, w_out))
    wr_t = jnp.swapaxes(w_router, 1, 2)
    pool_scale = c_scale.reshape(DEPTH, 1, POOL_DIM)
    caches = ()
    for l in range(DEPTH):
        p_c, g_c = _inproj(x_c, mod, norm_w, w_in_bf, l, ctx_group)
        o_c, caches = _ctx_mix(p_c, b_sink[l], c_pool_w, pool_scale, l, seq_c, caches)
        branches_c = tuple((o_c, k) for k in range(N_BRANCH))
        x1_c, h2_c, lg_c = _merge(x_c, branches_c, g_c, mod, norm_w, wa, wb, wc, wo, wr_t, l, ctx_group)
        xs_c, gate_c, rc_c = _dispatch(lg_c, h2_c, seq_c)

        p_l, g_l = _inproj(x_l, mod, norm_w, w_in_bf, l, lat_group_inproj)
        table = _rpb_table(a_rpb[l])
        o_a = _na_latent(p_l, cak, cav, table, l, seq_l)
        o_b = _sw_latent(p_l, cbk, cbv, b_sink[l], l, seq_l)
        o_p = _pool_latent(p_l, c_pool_w, pool_scale, l, seq_l)
        x1_l, h2_l, lg_l = _merge(x_l, ((o_a, 0), (o_b, 0), (o_p, 0)), g_l, mod, norm_w,
                                  wa, wb, wc, wo, wr_t, l, lat_group_merge)
        xs_l, gate_l, rc_l = _dispatch(lg_l, h2_l, seq_l)

        ye_c, ye_l = _experts(xs_c, xs_l, gate_c, gate_l, w_gate_e, w_up_e, w_down_e, l)
        x_c = _combine(rc_c, ye_c, x1_c, mod, norm_w, l, seq_c, ctx_group)
        x_l = _combine(rc_l, ye_l, x1_l, mod, norm_w, l, seq_l, lat_group_batch)

    y_prompt = x_c.reshape(batch, seq_c, D_MODEL)
    y_sample = x_l.reshape(dec_batch, seq_l, D_MODEL)
    new_a_k, new_a_v, new_b_k, new_b_v = caches
    a_shape = (batch, DEPTH, seq_c, NA_HEADS, NA_HEAD_DIM)
    b_shape = (batch, DEPTH, seq_c, SW_KV_HEADS, SW_HEAD_DIM)
    return (y_prompt, y_sample, new_a_k.reshape(a_shape), new_a_v.reshape(a_shape),
            new_b_k.reshape(b_shape), new_b_v.reshape(b_shape))
```

```python
import functools

import numpy as np
import jax
import jax.numpy as jnp
from jax import lax
from jax.experimental import pallas as pl
from jax.experimental.pallas import tpu as pltpu

F32 = jnp.float32
BF16 = jnp.bfloat16

D_MODEL = 2048
DEPTH = 2
GRID_W = 64
NA_HEADS, NA_HEAD_DIM, NA_MAX_KH, NA_KW = 4, 128, 8, 16
SW_Q_HEADS, SW_KV_HEADS, SW_HEAD_DIM = 8, 2, 64
SW_GROUP = SW_Q_HEADS // SW_KV_HEADS
SW_WINDOW, SW_BLOCK = 128, 128
ROPE_THETA = 10000.0
POOL_WINDOWS = (2, 4, 8, 16)
POOL_GROUPS, POOL_GROUP_DIM = 4, 128
POOL_DIM = POOL_GROUPS * POOL_GROUP_DIM
A_DIM = NA_HEADS * NA_HEAD_DIM
B_Q_DIM = SW_Q_HEADS * SW_HEAD_DIM
B_KV_DIM = SW_KV_HEADS * SW_HEAD_DIM
N_BRANCH = 3
GATE_DIM = N_BRANCH * D_MODEL
QKVU_DIM = 3 * A_DIM + B_Q_DIM + 2 * B_KV_DIM + POOL_DIM
IN_DIM = QKVU_DIM + GATE_DIM
N_EXPERTS = 16
EC_FACTOR = 2
D_EXPERT = 1024
RMS_EPS = 1e-6
NEG_INF = -1e30

COL_QA = 0
COL_KB = (3 * A_DIM + B_Q_DIM) // B_KV_DIM
COL_U = COL_KB + 2

V7X_VMEM_BYTES = 64 * 1024 * 1024
V7X_VMEM_CEILING = 60000 * 1024
MIB = 1024 * 1024

ADA_ROWS = 16
ADA_TN = 1024
V7X_MXU_DIM = 256
LANES = 128
INPROJ_TM = 1024
INPROJ_TN = 1280
assert INPROJ_TN % V7X_MXU_DIM == 0 and IN_DIM % INPROJ_TN == 0
INPROJ_TILES = IN_DIM // INPROJ_TN
INPROJ_SPLIT = QKVU_DIM // INPROJ_TN
P_WIDTH = (INPROJ_SPLIT + 1) * INPROJ_TN
GATES_WIDTH = (INPROJ_TILES - INPROJ_SPLIT) * INPROJ_TN
GATES_COL0 = QKVU_DIM - INPROJ_SPLIT * INPROJ_TN
MERGE_TM = 512
MERGE_CA = 512
MERGE_CB = 512
MERGE_NA = D_MODEL // MERGE_CA
MERGE_NB = D_MODEL // MERGE_CB
EXPERT_TF = 256
EXPERT_TD = 512
RANK_CHUNK = 256
COMBINE_TR = 256
COMBINE_TD = 512
ROW_CHUNK = 64


def _vmem_limit(estimate_bytes):
    return int(min(V7X_VMEM_CEILING, max(32 * MIB, estimate_bytes + 8 * MIB)))


def _params(semantics, estimate_bytes):
    return pltpu.CompilerParams(dimension_semantics=semantics,
                                vmem_limit_bytes=_vmem_limit(estimate_bytes))


def _rms(x, g):
    ms = jnp.mean(x * x, axis=-1, keepdims=True)
    return x * lax.rsqrt(ms + RMS_EPS) * g


def _dot(a, b):
    return jnp.dot(a, b, preferred_element_type=F32)


def _dot_nt(a, b):
    return lax.dot_general(a, b, (((1,), (1,)), ((), ())), preferred_element_type=F32)


def _adaln_kernel(c_ref, w_ref, b_ref, o_ref):
    c = c_ref[...]
    s = (c * jax.nn.sigmoid(c)).astype(BF16)
    o_ref[0] = _dot(s, w_ref[0].astype(BF16)) + b_ref[0]


def _adaln(cond, w_ada, b_ada):
    n_out = w_ada.shape[-1]
    return pl.pallas_call(
        _adaln_kernel,
        grid=(DEPTH, n_out // ADA_TN),
        in_specs=[pl.BlockSpec((ADA_ROWS, D_MODEL), lambda l, j: (0, 0)),
                  pl.BlockSpec((1, D_MODEL, ADA_TN), lambda l, j: (l, 0, j)),
                  pl.BlockSpec((1, 1, ADA_TN), lambda l, j: (l, 0, j))],
        out_specs=pl.BlockSpec((1, ADA_ROWS, ADA_TN), lambda l, j: (l, 0, j)),
        out_shape=jax.ShapeDtypeStruct((DEPTH, ADA_ROWS, n_out), F32),
        compiler_params=_params(("parallel", "parallel"), 2 * D_MODEL * ADA_TN * 4),
        name="adaln",
    )(cond, w_ada, b_ada.reshape(DEPTH, 1, n_out))


def _inproj_kernel(x_ref, mod_ref, nw_ref, w_ref, o_ref, gate_ref, h_scr):
    j = pl.program_id(1)

    @pl.when(j == 0)
    def _():
        g = nw_ref[0:1, :]
        sc = 1.0 + mod_ref[1:2, :]
        sh = mod_ref[0:1, :]

        def body(r, carry):
            rows = pl.ds(pl.multiple_of(r * ROW_CHUNK, ROW_CHUNK), ROW_CHUNK)
            h_scr[rows, :] = (_rms(x_ref[rows, :], g) * sc + sh).astype(BF16)
            return carry

        lax.fori_loop(0, INPROJ_TM // ROW_CHUNK, body, 0)

    @pl.when(j < INPROJ_SPLIT)
    def _():
        o_ref[...] = _dot(h_scr[...], w_ref[...])

    @pl.when(j == INPROJ_SPLIT)
    def _():
        acc = _dot(h_scr[...], w_ref[...])
        o_ref[...] = acc
        gate_ref[...] = jax.nn.sigmoid(acc).astype(BF16)

    @pl.when(j > INPROJ_SPLIT)
    def _():
        gate_ref[...] = jax.nn.sigmoid(_dot(h_scr[...], w_ref[...])).astype(BF16)


def _mod_spec(layer, group_of, n_grid):
    if n_grid == 1:
        return pl.BlockSpec((None, None, 6, D_MODEL), lambda i: (layer, group_of(i), 0, 0))
    return pl.BlockSpec((None, None, 6, D_MODEL), lambda i, j: (layer, group_of(i), 0, 0))


def _inproj(x, mod, norm_w, w_in_bf, layer, group_of_tile):
    n = x.shape[0]
    est = (2 * INPROJ_TM * D_MODEL * 4 + 2 * D_MODEL * INPROJ_TN * 2 + 2 * INPROJ_TM * INPROJ_TN * (4 + 2)
           + INPROJ_TM * D_MODEL * 2 + 2 * INPROJ_TM * INPROJ_TN * 4)
    return pl.pallas_call(
        _inproj_kernel,
        grid=(n // INPROJ_TM, INPROJ_TILES),
        in_specs=[pl.BlockSpec((INPROJ_TM, D_MODEL), lambda i, j: (i, 0)),
                  _mod_spec(layer, group_of_tile, 2),
                  pl.BlockSpec((None, 4, D_MODEL), lambda i, j: (layer, 0, 0)),
                  pl.BlockSpec((None, D_MODEL, INPROJ_TN), lambda i, j: (layer, 0, j))],
        out_specs=[pl.BlockSpec((INPROJ_TM, INPROJ_TN), lambda i, j: (i, jnp.minimum(j, INPROJ_SPLIT))),
                   pl.BlockSpec((INPROJ_TM, INPROJ_TN), lambda i, j: (i, jnp.maximum(j - INPROJ_SPLIT, 0)))],
        out_shape=[jax.ShapeDtypeStruct((n, P_WIDTH), F32), jax.ShapeDtypeStruct((n, GATES_WIDTH), BF16)],
        scratch_shapes=[pltpu.VMEM((INPROJ_TM, D_MODEL), BF16)],
        compiler_params=_params(("parallel", "arbitrary"), est),
        name="inproj",
    )(x, mod, norm_w, w_in_bf)


def _joint_attention(q, segments, scale, sink=None):
    scores = []
    for k, _, bias, mask in segments:
        s = _dot_nt(q, k) * scale
        if bias is not None:
            s = s + bias
        if mask is not None:
            s = jnp.where(mask, s, NEG_INF)
        scores.append(s)
    m = scores[0].max(axis=-1, keepdims=True)
    for s in scores[1:]:
        m = jnp.maximum(m, s.max(axis=-1, keepdims=True))
    if sink is not None:
        m = jnp.maximum(m, sink)
    denom = jnp.exp(sink - m) if sink is not None else 0.0
    acc = None
    for s, (_, v, _, _) in zip(scores, segments):
        e = jnp.exp(s - m)
        denom = denom + e.sum(axis=-1, keepdims=True)
        pv = _dot(e.astype(BF16), v)
        acc = pv if acc is None else acc + pv
    return acc / denom


def _pool_group(u, window, pw_bf, scale_row):
    seq = u.shape[0]
    pad = 8
    n = seq + 2 * pad
    z = jnp.zeros((pad, POOL_GROUP_DIM), F32)
    p = jnp.concatenate([z, u, z], axis=0)
    k = 1
    while k < window:
        p = p + pltpu.roll(p, n - k, 0)
        k *= 2
    win = pltpu.roll(p, window // 2, 0)[pad:pad + seq]
    t = lax.broadcasted_iota(jnp.int32, (seq, 1), 0)
    lo = jnp.maximum(t - window // 2, 0)
    hi = jnp.minimum(t - window // 2 + window, seq)
    cnt = (hi - lo).astype(F32)
    pooled = win / cnt - u
    return _dot(pooled.astype(BF16), pw_bf) * scale_row


def _sink_column(sink_ref, kv_head, rows_per_head):
    r = lax.broadcasted_iota(jnp.int32, (SW_GROUP * rows_per_head, 1), 0)
    col = jnp.full((SW_GROUP * rows_per_head, 1), sink_ref[kv_head * SW_GROUP], F32)
    for g in range(1, SW_GROUP):
        col = jnp.where(r >= g * rows_per_head, sink_ref[kv_head * SW_GROUP + g], col)
    return col


def _ctx_mix_kernel(sink_ref, qa_ref, ka_ref, va_ref, qb_ref, kb_ref, vb_ref, u0_ref, u1_ref, u2_ref, u3_ref,
                    pw_ref, ps_ref, *rest):
    o_ref, nak_ref, nav_ref, nbk_ref, nbv_ref = rest[-5:]
    seq = qa_ref.shape[0]
    nak_ref[...] = ka_ref[...]
    nav_ref[...] = va_ref[...]
    nbk_ref[...] = kb_ref[...]
    nbv_ref[...] = vb_ref[...]
    for h in range(NA_HEADS):
        sl = slice(h * NA_HEAD_DIM, (h + 1) * NA_HEAD_DIM)
        o = _joint_attention(qa_ref[:, sl].astype(BF16),
                             [(ka_ref[:, sl].astype(BF16), va_ref[:, sl].astype(BF16), None, None)],
                             NA_HEAD_DIM ** -0.5)
        o_ref[:, sl] = o
    for hk in range(SW_KV_HEADS):
        ksl = slice(hk * SW_HEAD_DIM, (hk + 1) * SW_HEAD_DIM)
        q = jnp.concatenate(
            [qb_ref[:, (hk * SW_GROUP + g) * SW_HEAD_DIM:(hk * SW_GROUP + g + 1) * SW_HEAD_DIM] for g in range(SW_GROUP)],
            axis=0).astype(BF16)
        o = _joint_attention(q, [(kb_ref[:, ksl].astype(BF16), vb_ref[:, ksl].astype(BF16), None, None)],
                             SW_HEAD_DIM ** -0.5, sink=_sink_column(sink_ref, hk, seq))
        for g in range(SW_GROUP):
            c0 = A_DIM + (hk * SW_GROUP + g) * SW_HEAD_DIM
            o_ref[:, c0:c0 + SW_HEAD_DIM] = o[g * seq:(g + 1) * seq]
    for gi, u_ref in enumerate((u0_ref, u1_ref, u2_ref, u3_ref)):
        c0 = A_DIM + B_Q_DIM + gi * POOL_GROUP_DIM
        o_ref[:, c0:c0 + POOL_GROUP_DIM] = _pool_group(
            u_ref[...], POOL_WINDOWS[gi], pw_ref[gi].astype(BF16), ps_ref[:, gi * POOL_GROUP_DIM:(gi + 1) * POOL_GROUP_DIM])


def _ctx_mix(p, sink_l, pool_w, pool_scale, layer, seq, caches):
    n = p.shape[0]
    batch = n // seq
    wide = lambda c: pl.BlockSpec((seq, A_DIM), lambda b, c=c: (b, c))
    narrow = lambda c: pl.BlockSpec((seq, B_KV_DIM), lambda b, c=c: (b, c))
    in_specs = [pl.BlockSpec(memory_space=pltpu.SMEM)]
    in_specs += [wide(COL_QA + i) for i in range(4)]
    in_specs += [narrow(COL_KB), narrow(COL_KB + 1)]
    in_specs += [narrow(COL_U + g) for g in range(POOL_GROUPS)]
    in_specs += [pl.BlockSpec((None, POOL_GROUPS, POOL_GROUP_DIM, POOL_GROUP_DIM), lambda b: (layer, 0, 0, 0)),
                 pl.BlockSpec((None, 1, POOL_DIM), lambda b: (layer, 0, 0))]
    n_fixed = len(in_specs)
    in_specs += [pl.BlockSpec(memory_space=pl.ANY)] * len(caches)
    cache_spec = lambda width: pl.BlockSpec((None, None, seq, width), lambda b: (b, layer, 0, 0))
    cache_shape = lambda width: jax.ShapeDtypeStruct((batch, DEPTH, seq, width), F32)
    outs = pl.pallas_call(
        _ctx_mix_kernel,
        grid=(batch,),
        in_specs=in_specs,
        out_specs=[pl.BlockSpec((seq, 3 * A_DIM), lambda b: (b, 0)),
                   cache_spec(A_DIM), cache_spec(A_DIM), cache_spec(B_KV_DIM), cache_spec(B_KV_DIM)],
        out_shape=[jax.ShapeDtypeStruct((n, 3 * A_DIM), F32),
                   cache_shape(A_DIM), cache_shape(A_DIM), cache_shape(B_KV_DIM), cache_shape(B_KV_DIM)],
        input_output_aliases={n_fixed + k: 1 + k for k in range(len(caches))},
        compiler_params=_params(("parallel",), 24 * MIB),
        name="ctx_mix",
    )(sink_l, p, p, p, p, p, p, p, p, p, p, pool_w, pool_scale, *caches)
    return outs[0], tuple(outs[1:])


NA_PAIR_ROWS = 2 * NA_MAX_KH - 2


def _rpb_table_kernel(rpb_ref, t_ref):
    lane = lax.broadcasted_iota(jnp.int32, (GRID_W, 2 * GRID_W), 1)
    qc = lax.broadcasted_iota(jnp.int32, (GRID_W, 2 * GRID_W), 0)
    kc = lane & (GRID_W - 1)
    upper = lane >= GRID_W
    dcm = jnp.clip(kc - qc + NA_KW - 1, 0, 2 * NA_KW - 2)
    col0 = jnp.clip(qc - NA_KW // 2, 0, GRID_W - NA_KW)
    inside = (kc >= col0) & (kc < col0 + NA_KW)
    n_dc = 2 * NA_KW - 1
    n_dr = 2 * NA_MAX_KH - 1

    def body(i, carry):
        h = i // NA_PAIR_ROWS
        dr = i - h * NA_PAIR_ROWS
        base = (h * n_dr + dr) * n_dc
        acc = jnp.zeros((GRID_W, 2 * GRID_W), F32)
        for dc in range(n_dc):
            val = jnp.where(upper, rpb_ref[base + n_dc + dc], rpb_ref[base + dc])
            acc = jnp.where(dcm == dc, val, acc)
        t_ref[i] = jnp.where(inside, acc, NEG_INF)
        return carry

    lax.fori_loop(0, NA_HEADS * NA_PAIR_ROWS, body, 0)


def _rpb_table(rpb_l):
    return pl.pallas_call(
        _rpb_table_kernel,
        in_specs=[pl.BlockSpec(memory_space=pltpu.SMEM)],
        out_specs=pl.BlockSpec(memory_space=pltpu.VMEM),
        out_shape=jax.ShapeDtypeStruct((NA_HEADS * NA_PAIR_ROWS, GRID_W, 2 * GRID_W), F32),
        name="rpb_table",
    )(rpb_l.reshape(-1))


def _na_kernel(q_ref, k_ref, v_ref, kc_ref, vc_ref, t_ref, o_ref, *, rows):
    qr = pl.program_id(1)
    row0 = jnp.clip(qr - NA_MAX_KH // 2, 0, rows - NA_MAX_KH)
    start = pl.multiple_of(row0 * GRID_W, GRID_W)
    nkeys = NA_MAX_KH * GRID_W
    d0 = row0 - qr + NA_MAX_KH - 1
    for h in range(NA_HEADS):
        sl = slice(h * NA_HEAD_DIM, (h + 1) * NA_HEAD_DIM)
        bias = jnp.concatenate([t_ref[h * NA_PAIR_ROWS + d0 + 2 * i] for i in range(NA_MAX_KH // 2)], axis=1)
        o = _joint_attention(
            q_ref[:, sl].astype(BF16),
            [(k_ref[pl.ds(start, nkeys), sl].astype(BF16), v_ref[pl.ds(start, nkeys), sl].astype(BF16), bias, None),
             (kc_ref[:, sl].astype(BF16), vc_ref[:, sl].astype(BF16), None, None)],
            NA_HEAD_DIM ** -0.5)
        o_ref[:, sl] = o


def _na_latent(p, cache_k, cache_v, table, layer, seq):
    n = p.shape[0]
    rows = seq // GRID_W
    past = cache_k.shape[2]
    ctx_spec = pl.BlockSpec((None, None, past, A_DIM), lambda b, r: (b, layer, 0, 0))
    return pl.pallas_call(
        functools.partial(_na_kernel, rows=rows),
        grid=(n // seq, rows),
        in_specs=[pl.BlockSpec((GRID_W, A_DIM), lambda b, r: (b * rows + r, COL_QA)),
                  pl.BlockSpec((seq, A_DIM), lambda b, r: (b, COL_QA + 1)),
                  pl.BlockSpec((seq, A_DIM), lambda b, r: (b, COL_QA + 2)),
                  ctx_spec, ctx_spec,
                  pl.BlockSpec(table.shape, lambda b, r: (0, 0, 0))],
        out_specs=pl.BlockSpec((GRID_W, A_DIM), lambda b, r: (b * rows + r, 0)),
        out_shape=jax.ShapeDtypeStruct((n, A_DIM), F32),
        compiler_params=_params(("parallel", "arbitrary"), 16 * MIB),
        name="na_latent",
    )(p, p, p, cache_k, cache_v, table)


def _rope_tables(seq):
    nfreq = SW_HEAD_DIM // 4
    inv = 1.0 / (ROPE_THETA ** (np.arange(nfreq, dtype=np.float32) / np.float32(nfreq)))
    t = np.arange(seq)
    pos = (t // GRID_W, t % GRID_W)
    cos = np.zeros((seq, SW_HEAD_DIM), np.float32)
    sin_next = np.zeros((seq, SW_HEAD_DIM), np.float32)
    sin_prev = np.zeros((seq, SW_HEAD_DIM), np.float32)
    for a in range(2):
        ang = pos[a].astype(np.float32)[:, None] * inv[None, :].astype(np.float32)
        c, s = np.cos(ang).astype(np.float32), np.sin(ang).astype(np.float32)
        lo = 2 * a * nfreq
        cos[:, lo:lo + nfreq] = c
        cos[:, lo + nfreq:lo + 2 * nfreq] = c
        sin_next[:, lo:lo + nfreq] = -s
        sin_prev[:, lo + nfreq:lo + 2 * nfreq] = s
    tile = lambda x: jnp.asarray(np.tile(x, (1, 128 // SW_HEAD_DIM)))
    return tile(cos), tile(sin_next), tile(sin_prev)


def _rope(x, cos, sin_next, sin_prev):
    nfreq = SW_HEAD_DIM // 4
    return x * cos + pltpu.roll(x, 128 - nfreq, 1) * sin_next + pltpu.roll(x, nfreq, 1) * sin_prev


def _sw_kernel(sink_ref, q_ref, k_ref, v_ref, kc_ref, vc_ref, cq_ref, snq_ref, spq_ref, ck_ref, snk_ref, spk_ref,
               o_ref, kr_scr, *, seq):
    n = pl.program_id(1)

    @pl.when(n == 0)
    def _():
        kr_scr[...] = _rope(k_ref[...], ck_ref[...], snk_ref[...], spk_ref[...]).astype(BF16)

    nwin = 3 * SW_BLOCK
    kstart = pl.multiple_of(jnp.clip((n - 1) * SW_BLOCK, 0, seq - nwin), SW_BLOCK)
    cq, snq, spq = cq_ref[...], snq_ref[...], spq_ref[...]
    q = jnp.concatenate([_rope(q_ref[:, c * 128:(c + 1) * 128], cq, snq, spq) for c in range(B_Q_DIM // 128)],
                        axis=1).astype(BF16)
    rows = SW_GROUP * SW_BLOCK
    qpos = n * SW_BLOCK + (lax.broadcasted_iota(jnp.int32, (rows, nwin), 0) & (SW_BLOCK - 1))
    kpos = kstart + lax.broadcasted_iota(jnp.int32, (rows, nwin), 1)
    band = jnp.abs(qpos - kpos) <= SW_WINDOW
    for hk in range(SW_KV_HEADS):
        ksl = slice(hk * SW_HEAD_DIM, (hk + 1) * SW_HEAD_DIM)
        qs = jnp.concatenate(
            [q[:, (hk * SW_GROUP + g) * SW_HEAD_DIM:(hk * SW_GROUP + g + 1) * SW_HEAD_DIM] for g in range(SW_GROUP)], axis=0)
        o = _joint_attention(
            qs,
            [(kr_scr[pl.ds(kstart, nwin), ksl], v_ref[pl.ds(kstart, nwin), ksl].astype(BF16), None, band),
             (kc_ref[:, ksl].astype(BF16), vc_ref[:, ksl].astype(BF16), None, None)],
            SW_HEAD_DIM ** -0.5, sink=_sink_column(sink_ref, hk, SW_BLOCK))
        for g in range(SW_GROUP):
            c0 = (hk * SW_GROUP + g) * SW_HEAD_DIM
            o_ref[:, c0:c0 + SW_HEAD_DIM] = o[g * SW_BLOCK:(g + 1) * SW_BLOCK]


def _sw_latent(p, cache_k, cache_v, sink_l, layer, seq):
    n = p.shape[0]
    nb = seq // SW_BLOCK
    past = cache_k.shape[2]
    cos, sin_next, sin_prev = _rope_tables(seq)
    ctx_spec = pl.BlockSpec((None, None, past, B_KV_DIM), lambda b, i: (b, layer, 0, 0))
    tab_q = pl.BlockSpec((SW_BLOCK, 128), lambda b, i: (i, 0))
    tab_k = pl.BlockSpec((seq, 128), lambda b, i: (0, 0))
    return pl.pallas_call(
        functools.partial(_sw_kernel, seq=seq),
        grid=(n // seq, nb),
        in_specs=[pl.BlockSpec(memory_space=pltpu.SMEM),
                  pl.BlockSpec((SW_BLOCK, B_Q_DIM), lambda b, i: (b * nb + i, COL_QA + 3)),
                  pl.BlockSpec((seq, B_KV_DIM), lambda b, i: (b, COL_KB)),
                  pl.BlockSpec((seq, B_KV_DIM), lambda b, i: (b, COL_KB + 1)),
                  ctx_spec, ctx_spec, tab_q, tab_q, tab_q, tab_k, tab_k, tab_k],
        out_specs=pl.BlockSpec((SW_BLOCK, B_Q_DIM), lambda b, i: (b * nb + i, 0)),
        out_shape=jax.ShapeDtypeStruct((n, B_Q_DIM), F32),
        scratch_shapes=[pltpu.VMEM((seq, B_KV_DIM), BF16)],
        compiler_params=_params(("parallel", "arbitrary"), 16 * MIB),
        name="sw_latent",
    )(sink_l, p, p, p, cache_k, cache_v, cos, sin_next, sin_prev, cos, sin_next, sin_prev)


def _pool_kernel(u0_ref, u1_ref, u2_ref, u3_ref, pw_ref, ps_ref, o_ref):
    for gi, u_ref in enumerate((u0_ref, u1_ref, u2_ref, u3_ref)):
        sl = slice(gi * POOL_GROUP_DIM, (gi + 1) * POOL_GROUP_DIM)
        o_ref[:, sl] = _pool_group(u_ref[...], POOL_WINDOWS[gi], pw_ref[gi].astype(BF16), ps_ref[:, sl])


def _pool_latent(p, pool_w, pool_scale, layer, seq):
    n = p.shape[0]
    return pl.pallas_call(
        _pool_kernel,
        grid=(n // seq,),
        in_specs=[pl.BlockSpec((seq, POOL_GROUP_DIM), lambda b, g=g: (b, COL_U + g)) for g in range(POOL_GROUPS)]
        + [pl.BlockSpec((None, POOL_GROUPS, POOL_GROUP_DIM, POOL_GROUP_DIM), lambda b: (layer, 0, 0, 0)),
           pl.BlockSpec((None, 1, POOL_DIM), lambda b: (layer, 0, 0))],
        out_specs=pl.BlockSpec((seq, POOL_DIM), lambda b: (b, 0)),
        out_shape=jax.ShapeDtypeStruct((n, POOL_DIM), F32),
        compiler_params=_params(("parallel",), 16 * MIB),
        name="pool_latent",
    )(p, p, p, p, pool_w, pool_scale)


def _split_bf16(x):
    hi = x.astype(BF16)
    return hi, (x - hi.astype(F32)).astype(BF16)


def _merge_kernel(x_ref, oa_ref, ob_ref, oc_ref, ga_ref, gb_ref, gc_ref, mod_ref, nw_ref,
                  wa_ref, wb_ref, wc_ref, wo_ref, wr_ref, x1_ref, h2_ref, lg_ref, m_scr, y_scr):
    s = pl.program_id(1)

    for t in range(MERGE_NA):
        @pl.when(s == t)
        def _(t=t):
            cols = slice(t * MERGE_CA, (t + 1) * MERGE_CA)
            m = (ga_ref[...].astype(F32) * _dot(oa_ref[...].astype(BF16), wa_ref[:, cols])
                 + gb_ref[...].astype(F32) * _dot(ob_ref[...].astype(BF16), wb_ref[:, cols])
                 + gc_ref[...].astype(F32) * _dot(oc_ref[...].astype(BF16), wc_ref[:, cols]))
            m_scr[t] = m.astype(BF16)

    for t in range(MERGE_NB):
        @pl.when(s == MERGE_NA + t)
        def _(t=t):
            cols = slice(t * MERGE_CB, (t + 1) * MERGE_CB)
            acc = _dot(m_scr[0], wo_ref[0:MERGE_CA, cols])
            for f in range(1, MERGE_NA):
                acc = acc + _dot(m_scr[f], wo_ref[f * MERGE_CA:(f + 1) * MERGE_CA, cols])
            y_scr[t] = acc

    @pl.when(s == MERGE_NA + MERGE_NB - 1)
    def _():
        tiles = [slice(c * MERGE_CB, (c + 1) * MERGE_CB) for c in range(MERGE_NB)]
        ss = jnp.zeros((x_ref.shape[0], 1), F32)
        for c in range(MERGE_NB):
            y = y_scr[c]
            ss = ss + (y * y).sum(axis=-1, keepdims=True)
        r1 = lax.rsqrt(ss / D_MODEL + RMS_EPS)
        ss = jnp.zeros((x_ref.shape[0], 1), F32)
        for c, cols in enumerate(tiles):
            x1 = x_ref[:, cols] + mod_ref[2:3, cols] * (y_scr[c] * r1 * nw_ref[1:2, cols])
            x1_ref[:, cols] = x1
            ss = ss + (x1 * x1).sum(axis=-1, keepdims=True)
        r2 = lax.rsqrt(ss / D_MODEL + RMS_EPS)
        lg = jnp.zeros(lg_ref.shape, F32)
        for cols in tiles:
            h2 = x1_ref[:, cols] * r2 * nw_ref[2:3, cols] * (1.0 + mod_ref[4:5, cols]) + mod_ref[3:4, cols]
            h2_ref[:, cols] = h2.astype(BF16)
            h_hi, h_lo = _split_bf16(h2)
            w_hi, w_lo = _split_bf16(wr_ref[:, cols])
            lg = lg + (_dot_nt(w_hi, h_hi) + (_dot_nt(w_hi, h_lo) + _dot_nt(w_lo, h_hi)))
        lg_ref[...] = lg


def _merge(x, branches, gates, mod, norm_w, wa, wb, wc, wo, wr_t, layer, group_of_tile):
    n = x.shape[0]
    tm, ca, cb = MERGE_TM, MERGE_CA, MERGE_CB
    row = lambda width: pl.BlockSpec((tm, width), lambda i, s: (i, 0))
    a_tile = lambda s: jnp.minimum(s, MERGE_NA - 1)
    gate = lambda k: pl.BlockSpec((pl.Element(tm), pl.Element(ca)),
                                  lambda i, s, k=k: (i * tm, pl.multiple_of(
                                      GATES_COL0 + k * D_MODEL + a_tile(s) * ca, LANES)))
    branch = lambda col: pl.BlockSpec((tm, A_DIM), lambda i, s: (i, col))
    const = lambda *shape: pl.BlockSpec((None,) + shape, lambda i, s: (layer,) + (0,) * len(shape))
    resident = lambda *shape: pl.BlockSpec((None,) + shape, lambda i, s: (layer,) + (0,) * len(shape),
                                           pipeline_mode=pl.Buffered(1))
    oa, ob, oc = branches
    est = (2 * tm * D_MODEL * (4 + 4 + 2) + 2 * tm * 3 * A_DIM * 4 + 2 * 3 * tm * ca * 2 + 3 * A_DIM * D_MODEL * 2
           + D_MODEL * D_MODEL * 2 + tm * D_MODEL * (2 + 4) + 4 * tm * ca * 4)
    return pl.pallas_call(
        _merge_kernel,
        grid=(n // tm, MERGE_NA + MERGE_NB),
        in_specs=[row(D_MODEL), branch(oa[1]), branch(ob[1]), branch(oc[1]),
                  gate(0), gate(1), gate(2),
                  _mod_spec(layer, group_of_tile, 2),
                  const(4, D_MODEL),
                  resident(A_DIM, D_MODEL), resident(B_Q_DIM, D_MODEL), resident(POOL_DIM, D_MODEL),
                  resident(D_MODEL, D_MODEL),
                  const(N_EXPERTS, D_MODEL)],
        out_specs=[row(D_MODEL), row(D_MODEL), pl.BlockSpec((N_EXPERTS, tm), lambda i, s: (0, i))],
        out_shape=[jax.ShapeDtypeStruct((n, D_MODEL), F32), jax.ShapeDtypeStruct((n, D_MODEL), BF16),
                   jax.ShapeDtypeStruct((N_EXPERTS, n), F32)],
        scratch_shapes=[pltpu.VMEM((MERGE_NA, tm, ca), BF16), pltpu.VMEM((MERGE_NB, tm, cb), F32)],
        compiler_params=_params(("parallel", "arbitrary"), est),
        name="merge",
    )(x, oa[0], ob[0], oc[0], gates, gates, gates, mod, norm_w, wa, wb, wc, wo, wr_t)


def _dispatch_kernel(lg_ref, h_ref, xs_ref, gate_ref, rc_ref, aff_scr, sel_scr, *, seq, cap):
    lg = lg_ref[...]
    e = jnp.exp(lg - lg.max(axis=0, keepdims=True))
    aff_scr[...] = e / e.sum(axis=0, keepdims=True)
    rc_ref[...] = jnp.zeros(rc_ref.shape, F32)
    chunk = min(seq, RANK_CHUNK)
    lane_e = lax.broadcasted_iota(jnp.int32, (chunk, LANES), 1)
    slot = lax.broadcasted_iota(jnp.int32, (cap, seq), 0).astype(F32)

    def body(ex, carry):
        row = aff_scr[pl.ds(ex, 1), :]
        rowb = jnp.broadcast_to(row, (chunk, seq))
        rank_row = jnp.zeros((1, seq), F32)
        for c in range(seq // chunk):
            tr = c * chunk + lax.broadcasted_iota(jnp.int32, (chunk, seq), 0)
            tc = lax.broadcasted_iota(jnp.int32, (chunk, seq), 1)
            col = jnp.where(tr == tc, rowb, 0.0).sum(axis=1, keepdims=True)
            beats = ((col > rowb) | ((col == rowb) & (tr < tc))).astype(F32)
            rank_row = rank_row + beats.sum(axis=0, keepdims=True)
            rank_col = (seq - 1.0) - beats.sum(axis=1, keepdims=True)
            rows = slice(c * chunk, (c + 1) * chunk)
            rc_ref[rows, :] = jnp.where(lane_e == ex, rank_col, rc_ref[rows, :])
        sel = slot == rank_row
        sel_scr[pl.ds(pl.multiple_of(ex * cap, cap), cap), :] = sel.astype(BF16)
        gate = jnp.where(sel, jnp.broadcast_to(row, (cap, seq)), 0.0).sum(axis=1, keepdims=True)
        gate_ref[ex] = jnp.broadcast_to(gate, (cap, LANES))
        return carry

    lax.fori_loop(0, N_EXPERTS, body, 0, unroll=4 if seq <= RANK_CHUNK else 1)
    xs = _dot(sel_scr[...], h_ref[...]).astype(BF16)
    xs_ref[...] = xs.reshape(N_EXPERTS, cap, D_MODEL)


def _dispatch(lg_t, h2, seq):
    n = h2.shape[0]
    nb = n // seq
    cap = EC_FACTOR * seq // N_EXPERTS
    est = (2 * seq * D_MODEL * 2 + 2 * N_EXPERTS * cap * D_MODEL * 2 + N_EXPERTS * cap * seq * 2
           + N_EXPERTS * cap * D_MODEL * 4 + 12 * min(seq, RANK_CHUNK) * seq * 4)
    return pl.pallas_call(
        functools.partial(_dispatch_kernel, seq=seq, cap=cap),
        grid=(nb,),
        in_specs=[pl.BlockSpec((N_EXPERTS, seq), lambda b: (0, b)),
                  pl.BlockSpec((seq, D_MODEL), lambda b: (b, 0))],
        out_specs=[pl.BlockSpec((N_EXPERTS, cap, D_MODEL), lambda b: (0, b, 0)),
                   pl.BlockSpec((N_EXPERTS, cap, LANES), lambda b: (0, b, 0)),
                   pl.BlockSpec((seq, LANES), lambda b: (b, 0))],
        out_shape=[jax.ShapeDtypeStruct((N_EXPERTS, nb * cap, D_MODEL), BF16),
                   jax.ShapeDtypeStruct((N_EXPERTS, nb * cap, LANES), F32),
                   jax.ShapeDtypeStruct((n, LANES), F32)],
        scratch_shapes=[pltpu.VMEM((N_EXPERTS, seq), F32), pltpu.VMEM((N_EXPERTS * cap, seq), BF16)],
        compiler_params=_params(("parallel",), est),
        name="dispatch",
    )(lg_t, h2)


EXPERT_NF = D_EXPERT // EXPERT_TF
EXPERT_ND = D_MODEL // EXPERT_TD


assert EXPERT_NF == EXPERT_ND


def _expert_kernel(xc_ref, xl_ref, gc_ref, gl_ref, wg_ref, wu_ref, wd_ref, yc_ref, yl_ref, hc_scr, hl_scr):
    e = pl.program_id(0)
    t = pl.program_id(1)
    cur = e % 2

    @pl.when(e >= 1)
    def _():
        wd = wd_ref[0].astype(BF16)
        for h_scr, g_ref, y_ref in ((hc_scr, gc_ref, yc_ref), (hl_scr, gl_ref, yl_ref)):
            acc = _dot(h_scr[1 - cur, 0], wd[0:EXPERT_TF])
            for f in range(1, EXPERT_NF):
                acc = acc + _dot(h_scr[1 - cur, f], wd[f * EXPERT_TF:(f + 1) * EXPERT_TF])
            y_ref[0] = acc * g_ref[0, :, 0:1]

    @pl.when(e < N_EXPERTS)
    def _():
        wg = wg_ref[0].astype(BF16)
        wu = wu_ref[0].astype(BF16)
        for x_ref, h_scr in ((xc_ref, hc_scr), (xl_ref, hl_scr)):
            x = x_ref[0]
            a = _dot(x, wg)
            h_scr[cur, t] = ((a * jax.nn.sigmoid(a)) * _dot(x, wu)).astype(BF16)


def _experts(xs_c, xs_l, gate_c, gate_l, w_gate, w_up, w_down, layer):
    sc, sl = xs_c.shape[1], xs_l.shape[1]
    tf, td = EXPERT_TF, EXPERT_TD
    last = N_EXPERTS - 1
    up_expert = lambda e: jnp.minimum(e, last)
    down_expert = lambda e: jnp.maximum(e - 1, 0)
    up_tile = lambda e, t: (layer, up_expert(e), 0, jnp.where(e > last, EXPERT_NF - 1, t))
    down_tile = lambda e, t: (down_expert(e), 0, jnp.where(e == 0, 0, t))
    x_spec = lambda s: pl.BlockSpec((1, s, D_MODEL), lambda e, t: (up_expert(e), 0, 0))
    g_spec = lambda s: pl.BlockSpec((1, s, LANES), lambda e, t: (down_expert(e), 0, 0))
    est = (2 * (sc + sl) * D_MODEL * 2 + 2 * 2 * D_MODEL * tf * 4 + 2 * D_EXPERT * td * 4 + 2 * (sc + sl) * td * 4
           + 2 * (sc + sl) * D_EXPERT * 2 + 2 * D_MODEL * tf * 2 + D_EXPERT * td * 2 + 6 * sc * max(tf, td) * 4)
    return pl.pallas_call(
        _expert_kernel,
        grid=(N_EXPERTS + 1, EXPERT_NF),
        in_specs=[x_spec(sc), x_spec(sl), g_spec(sc), g_spec(sl),
                  pl.BlockSpec((None, 1, D_MODEL, tf), up_tile),
                  pl.BlockSpec((None, 1, D_MODEL, tf), up_tile),
                  pl.BlockSpec((None, 1, D_EXPERT, td), lambda e, t: (layer,) + down_tile(e, t))],
        out_specs=[pl.BlockSpec((1, sc, td), down_tile), pl.BlockSpec((1, sl, td), down_tile)],
        out_shape=[jax.ShapeDtypeStruct((N_EXPERTS, sc, D_MODEL), F32),
                   jax.ShapeDtypeStruct((N_EXPERTS, sl, D_MODEL), F32)],
        scratch_shapes=[pltpu.VMEM((2, EXPERT_NF, sc, tf), BF16), pltpu.VMEM((2, EXPERT_NF, sl, tf), BF16)],
        compiler_params=_params(("arbitrary", "arbitrary"), est),
        name="experts",
    )(xs_c, xs_l, gate_c, gate_l, w_gate, w_up, w_down)


def _combine_kernel(rc_ref, ye_ref, x1_ref, mod_ref, nw_ref, o_ref, m_scr, ffn_scr, *, cap):
    rc = rc_ref[...]
    slot = lax.broadcasted_iota(jnp.int32, (rc.shape[0], cap), 1).astype(F32)
    for ex in range(N_EXPERTS):
        m_scr[:, ex * cap:(ex + 1) * cap] = (rc[:, ex:ex + 1] == slot).astype(BF16)
    onehot = m_scr[...]
    for c in range(D_MODEL // COMBINE_TD):
        cols = slice(c * COMBINE_TD, (c + 1) * COMBINE_TD)
        ye = ye_ref[:, :, cols].reshape(N_EXPERTS * cap, COMBINE_TD)
        hi, lo = _split_bf16(ye)
        ffn_scr[:, cols] = _dot(onehot, hi) + _dot(onehot, lo)
    o_ref[...] = x1_ref[...] + mod_ref[5:6, :] * _rms(ffn_scr[...], nw_ref[3:4, :])


def _combine(rank_col, ye, x1, mod, norm_w, layer, seq, group_of_batch):
    n = x1.shape[0]
    cap = EC_FACTOR * seq // N_EXPERTS
    tr = COMBINE_TR
    per = seq // tr
    ye_buffers = 2 if per == 1 else 1
    est = (ye_buffers * N_EXPERTS * cap * D_MODEL * 4 + 4 * tr * D_MODEL * 4 + tr * N_EXPERTS * cap * 2
           + tr * D_MODEL * 4 + 3 * N_EXPERTS * cap * COMBINE_TD * 4)
    return pl.pallas_call(
        functools.partial(_combine_kernel, cap=cap),
        grid=(n // seq, per),
        in_specs=[pl.BlockSpec((tr, LANES), lambda b, i: (b * per + i, 0)),
                  pl.BlockSpec((N_EXPERTS, cap, D_MODEL), lambda b, i: (0, b, 0),
                               pipeline_mode=pl.Buffered(ye_buffers)),
                  pl.BlockSpec((tr, D_MODEL), lambda b, i: (b * per + i, 0)),
                  pl.BlockSpec((None, None, 6, D_MODEL), lambda b, i: (layer, group_of_batch(b), 0, 0)),
                  pl.BlockSpec((None, 4, D_MODEL), lambda b, i: (layer, 0, 0))],
        out_specs=pl.BlockSpec((tr, D_MODEL), lambda b, i: (b * per + i, 0)),
        out_shape=jax.ShapeDtypeStruct((n, D_MODEL), F32),
        scratch_shapes=[pltpu.VMEM((tr, N_EXPERTS * cap), BF16), pltpu.VMEM((tr, D_MODEL), F32)],
        compiler_params=_params(("parallel", "arbitrary"), est),
        name="combine",
    )(rank_col, ye, x1, mod, norm_w)


def kernel(x_prompt, x_sample, c, cache_a_k, cache_a_v, cache_b_k, cache_b_v, c_ctx, norm_w, w_ada, b_ada, w_in, a_rpb,
           b_sink, c_pool_w, c_scale, w_branch_a, w_branch_b, w_branch_c, w_out, w_router, w_gate_e, w_up_e, w_down_e):
    batch, seq_c, _ = x_prompt.shape
    dec_batch, seq_l, _ = x_sample.shape
    past = cache_a_k.shape[2]

    cond = jnp.zeros((ADA_ROWS, D_MODEL), F32).at[0].set(c_ctx).at[1:1 + dec_batch].set(c)
    mod = _adaln(cond, w_ada, b_ada).reshape(DEPTH, ADA_ROWS, 6, D_MODEL)

    ctx_group = lambda i: 0
    lat_group_inproj = lambda i: 1 + i // (seq_l // INPROJ_TM)
    lat_group_merge = lambda i: 1 + i // (seq_l // MERGE_TM)
    lat_group_batch = lambda b: 1 + b

    cak = cache_a_k.reshape(dec_batch, DEPTH, past, A_DIM)
    cav = cache_a_v.reshape(dec_batch, DEPTH, past, A_DIM)
    cbk = cache_b_k.reshape(dec_batch, DEPTH, past, B_KV_DIM)
    cbv = cache_b_v.reshape(dec_batch, DEPTH, past, B_KV_DIM)

    x_c = x_prompt.reshape(batch * seq_c, D_MODEL)
    x_l = x_sample.reshape(dec_batch * seq_l, D_MODEL)
    w_in_bf = w_in.astype(BF16)
    wa, wb, wc, wo = (w.astype(BF16) for w in (w_branch_a, w_branch_b, w_branch_c, w_out))
    wr_t = jnp.swapaxes(w_router, 1, 2)
    pool_scale = c_scale.reshape(DEPTH, 1, POOL_DIM)
    caches = ()
    for l in range(DEPTH):
        p_c, g_c = _inproj(x_c, mod, norm_w, w_in_bf, l, ctx_group)
        o_c, caches = _ctx_mix(p_c, b_sink[l], c_pool_w, pool_scale, l, seq_c, caches)
        branches_c = tuple((o_c, k) for k in range(N_BRANCH))
        x1_c, h2_c, lg_c = _merge(x_c, branches_c, g_c, mod, norm_w, wa, wb, wc, wo, wr_t, l, ctx_group)
        xs_c, gate_c, rc_c = _dispatch(lg_c, h2_c, seq_c)

        p_l, g_l = _inproj(x_l, mod, norm_w, w_in_bf, l, lat_group_inproj)
        table = _rpb_table(a_rpb[l])
        o_a = _na_latent(p_l, cak, cav, table, l, seq_l)
        o_b = _sw_latent(p_l, cbk, cbv, b_sink[l], l, seq_l)
        o_p = _pool_latent(p_l, c_pool_w, pool_scale, l, seq_l)
        x1_l, h2_l, lg_l = _merge(x_l, ((o_a, 0), (o_b, 0), (o_p, 0)), g_l, mod, norm_w,
                                  wa, wb, wc, wo, wr_t, l, lat_group_merge)
        xs_l, gate_l, rc_l = _dispatch(lg_l, h2_l, seq_l)

        ye_c, ye_l = _experts(xs_c, xs_l, gate_c, gate_l, w_gate_e, w_up_e, w_down_e, l)
        x_c = _combine(rc_c, ye_c, x1_c, mod, norm_w, l, seq_c, ctx_group)
        x_l = _combine(rc_l, ye_l, x1_l, mod, norm_w, l, seq_l, lat_group_batch)

    y_prompt = x_c.reshape(batch, seq_c, D_MODEL)
    y_sample = x_l.reshape(dec_batch, seq_l, D_MODEL)
    new_a_k, new_a_v, new_b_k, new_b_v = caches
    a_shape = (batch, DEPTH, seq_c, NA_HEADS, NA_HEAD_DIM)
    b_shape = (batch, DEPTH, seq_c, SW_KV_HEADS, SW_HEAD_DIM)
    return (y_prompt, y_sample, new_a_k.reshape(a_shape), new_a_v.reshape(a_shape),
            new_b_k.reshape(b_shape), new_b_v.reshape(b_shape))
```

```python
import functools

import numpy as np
import jax
import jax.numpy as jnp
from jax import lax
from jax.experimental import pallas as pl
from jax.experimental.pallas import tpu as pltpu

F32 = jnp.float32
BF16 = jnp.bfloat16

D_MODEL = 2048
DEPTH = 2
GRID_W = 64
NA_HEADS, NA_HEAD_DIM, NA_MAX_KH, NA_KW = 4, 128, 8, 16
SW_Q_HEADS, SW_KV_HEADS, SW_HEAD_DIM = 8, 2, 64
SW_GROUP = SW_Q_HEADS // SW_KV_HEADS
SW_WINDOW, SW_BLOCK = 128, 128
ROPE_THETA = 10000.0
POOL_WINDOWS = (2, 4, 8, 16)
POOL_GROUPS, POOL_GROUP_DIM = 4, 128
POOL_DIM = POOL_GROUPS * POOL_GROUP_DIM
A_DIM = NA_HEADS * NA_HEAD_DIM
B_Q_DIM = SW_Q_HEADS * SW_HEAD_DIM
B_KV_DIM = SW_KV_HEADS * SW_HEAD_DIM
N_BRANCH = 3
GATE_DIM = N_BRANCH * D_MODEL
QKVU_DIM = 3 * A_DIM + B_Q_DIM + 2 * B_KV_DIM + POOL_DIM
IN_DIM = QKVU_DIM + GATE_DIM
N_EXPERTS = 16
EC_FACTOR = 2
D_EXPERT = 1024
RMS_EPS = 1e-6
NEG_INF = -1e30

COL_QA = 0
COL_KB = (3 * A_DIM + B_Q_DIM) // B_KV_DIM
COL_U = COL_KB + 2

V7X_VMEM_BYTES = 64 * 1024 * 1024
V7X_VMEM_CEILING = 60000 * 1024
MIB = 1024 * 1024

ADA_ROWS = 16
ADA_TN = 1024
V7X_MXU_DIM = 256
LANES = 128
INPROJ_TM = 1024
INPROJ_TN = 1280
assert INPROJ_TN % V7X_MXU_DIM == 0 and IN_DIM % INPROJ_TN == 0
INPROJ_TILES = IN_DIM // INPROJ_TN
INPROJ_SPLIT = QKVU_DIM // INPROJ_TN
P_WIDTH = (INPROJ_SPLIT + 1) * INPROJ_TN
GATES_WIDTH = (INPROJ_TILES - INPROJ_SPLIT) * INPROJ_TN
GATES_COL0 = QKVU_DIM - INPROJ_SPLIT * INPROJ_TN
BMERGE_TM = 1024
BMERGE_CA = 1024
BMERGE_NA = D_MODEL // BMERGE_CA
OUTPROJ_TM = 512
OUTPROJ_CB = 512
EXPERT_TF = 256
EXPERT_TD = 512
RANK_CHUNK = 256
COMBINE_TR = 256
COMBINE_TD = 512
ROW_CHUNK = 64


def _vmem_limit(estimate_bytes):
    return int(min(V7X_VMEM_CEILING, max(32 * MIB, estimate_bytes + 8 * MIB)))


def _params(semantics, estimate_bytes):
    return pltpu.CompilerParams(dimension_semantics=semantics,
                                vmem_limit_bytes=_vmem_limit(estimate_bytes))


def _rms(x, g):
    ms = jnp.mean(x * x, axis=-1, keepdims=True)
    return x * lax.rsqrt(ms + RMS_EPS) * g


def _dot(a, b):
    return jnp.dot(a, b, preferred_element_type=F32)


def _dot_nt(a, b):
    return lax.dot_general(a, b, (((1,), (1,)), ((), ())), preferred_element_type=F32)


def _adaln_kernel(c_ref, w_ref, b_ref, o_ref):
    c = c_ref[...]
    s = (c * jax.nn.sigmoid(c)).astype(BF16)
    o_ref[0] = _dot(s, w_ref[0].astype(BF16)) + b_ref[0]


def _adaln(cond, w_ada, b_ada):
    n_out = w_ada.shape[-1]
    return pl.pallas_call(
        _adaln_kernel,
        grid=(DEPTH, n_out // ADA_TN),
        in_specs=[pl.BlockSpec((ADA_ROWS, D_MODEL), lambda l, j: (0, 0)),
                  pl.BlockSpec((1, D_MODEL, ADA_TN), lambda l, j: (l, 0, j)),
                  pl.BlockSpec((1, 1, ADA_TN), lambda l, j: (l, 0, j))],
        out_specs=pl.BlockSpec((1, ADA_ROWS, ADA_TN), lambda l, j: (l, 0, j)),
        out_shape=jax.ShapeDtypeStruct((DEPTH, ADA_ROWS, n_out), F32),
        compiler_params=_params(("parallel", "parallel"), 2 * D_MODEL * ADA_TN * 4),
        name="adaln",
    )(cond, w_ada, b_ada.reshape(DEPTH, 1, n_out))


def _inproj_kernel(x_ref, mod_ref, nw_ref, w_ref, o_ref, gate_ref, h_scr):
    j = pl.program_id(1)

    @pl.when(j == 0)
    def _():
        g = nw_ref[0:1, :]
        sc = 1.0 + mod_ref[1:2, :]
        sh = mod_ref[0:1, :]

        def body(r, carry):
            rows = pl.ds(pl.multiple_of(r * ROW_CHUNK, ROW_CHUNK), ROW_CHUNK)
            h_scr[rows, :] = (_rms(x_ref[rows, :], g) * sc + sh).astype(BF16)
            return carry

        lax.fori_loop(0, INPROJ_TM // ROW_CHUNK, body, 0)

    @pl.when(j < INPROJ_SPLIT)
    def _():
        o_ref[...] = _dot(h_scr[...], w_ref[...])

    @pl.when(j == INPROJ_SPLIT)
    def _():
        acc = _dot(h_scr[...], w_ref[...])
        o_ref[...] = acc
        gate_ref[...] = jax.nn.sigmoid(acc).astype(BF16)

    @pl.when(j > INPROJ_SPLIT)
    def _():
        gate_ref[...] = jax.nn.sigmoid(_dot(h_scr[...], w_ref[...])).astype(BF16)


def _mod_spec(layer, group_of, n_grid):
    if n_grid == 1:
        return pl.BlockSpec((None, None, 6, D_MODEL), lambda i: (layer, group_of(i), 0, 0))
    return pl.BlockSpec((None, None, 6, D_MODEL), lambda i, j: (layer, group_of(i), 0, 0))


def _inproj(x, mod, norm_w, w_in_bf, layer, group_of_tile):
    n = x.shape[0]
    est = (2 * INPROJ_TM * D_MODEL * 4 + 2 * D_MODEL * INPROJ_TN * 2 + 2 * INPROJ_TM * INPROJ_TN * (4 + 2)
           + INPROJ_TM * D_MODEL * 2 + 2 * INPROJ_TM * INPROJ_TN * 4)
    return pl.pallas_call(
        _inproj_kernel,
        grid=(n // INPROJ_TM, INPROJ_TILES),
        in_specs=[pl.BlockSpec((INPROJ_TM, D_MODEL), lambda i, j: (i, 0)),
                  _mod_spec(layer, group_of_tile, 2),
                  pl.BlockSpec((None, 4, D_MODEL), lambda i, j: (layer, 0, 0)),
                  pl.BlockSpec((None, D_MODEL, INPROJ_TN), lambda i, j: (layer, 0, j))],
        out_specs=[pl.BlockSpec((INPROJ_TM, INPROJ_TN), lambda i, j: (i, jnp.minimum(j, INPROJ_SPLIT))),
                   pl.BlockSpec((INPROJ_TM, INPROJ_TN), lambda i, j: (i, jnp.maximum(j - INPROJ_SPLIT, 0)))],
        out_shape=[jax.ShapeDtypeStruct((n, P_WIDTH), F32), jax.ShapeDtypeStruct((n, GATES_WIDTH), BF16)],
        scratch_shapes=[pltpu.VMEM((INPROJ_TM, D_MODEL), BF16)],
        compiler_params=_params(("parallel", "arbitrary"), est),
        name="inproj",
    )(x, mod, norm_w, w_in_bf)


def _joint_attention(q, segments, scale, sink=None):
    scores = []
    for k, _, bias, mask in segments:
        s = _dot_nt(q, k) * scale
        if bias is not None:
            s = s + bias
        if mask is not None:
            s = jnp.where(mask, s, NEG_INF)
        scores.append(s)
    m = scores[0].max(axis=-1, keepdims=True)
    for s in scores[1:]:
        m = jnp.maximum(m, s.max(axis=-1, keepdims=True))
    if sink is not None:
        m = jnp.maximum(m, sink)
    denom = jnp.exp(sink - m) if sink is not None else 0.0
    acc = None
    for s, (_, v, _, _) in zip(scores, segments):
        e = jnp.exp(s - m)
        denom = denom + e.sum(axis=-1, keepdims=True)
        pv = _dot(e.astype(BF16), v)
        acc = pv if acc is None else acc + pv
    return acc / denom


def _pool_group(u, window, pw_bf, scale_row):
    seq = u.shape[0]
    pad = 8
    n = seq + 2 * pad
    z = jnp.zeros((pad, POOL_GROUP_DIM), F32)
    p = jnp.concatenate([z, u, z], axis=0)
    k = 1
    while k < window:
        p = p + pltpu.roll(p, n - k, 0)
        k *= 2
    win = pltpu.roll(p, window // 2, 0)[pad:pad + seq]
    t = lax.broadcasted_iota(jnp.int32, (seq, 1), 0)
    lo = jnp.maximum(t - window // 2, 0)
    hi = jnp.minimum(t - window // 2 + window, seq)
    cnt = (hi - lo).astype(F32)
    pooled = win / cnt - u
    return _dot(pooled.astype(BF16), pw_bf) * scale_row


def _unstack_heads(o, rows_per_head):
    return jnp.concatenate([o[g * rows_per_head:(g + 1) * rows_per_head] for g in range(SW_GROUP)], axis=1).astype(BF16)


def _sink_column(sink_ref, kv_head, rows_per_head):
    r = lax.broadcasted_iota(jnp.int32, (SW_GROUP * rows_per_head, 1), 0)
    col = jnp.full((SW_GROUP * rows_per_head, 1), sink_ref[kv_head * SW_GROUP], F32)
    for g in range(1, SW_GROUP):
        col = jnp.where(r >= g * rows_per_head, sink_ref[kv_head * SW_GROUP + g], col)
    return col


def _ctx_mix_kernel(sink_ref, qa_ref, ka_ref, va_ref, qb_ref, kb_ref, vb_ref, u0_ref, u1_ref, u2_ref, u3_ref,
                    pw_ref, ps_ref, *rest):
    o_ref, nak_ref, nav_ref, nbk_ref, nbv_ref = rest[-5:]
    seq = qa_ref.shape[0]
    nak_ref[...] = ka_ref[...]
    nav_ref[...] = va_ref[...]
    nbk_ref[...] = kb_ref[...]
    nbv_ref[...] = vb_ref[...]
    for h in range(NA_HEADS):
        sl = slice(h * NA_HEAD_DIM, (h + 1) * NA_HEAD_DIM)
        o = _joint_attention(qa_ref[:, sl].astype(BF16),
                             [(ka_ref[:, sl].astype(BF16), va_ref[:, sl].astype(BF16), None, None)],
                             NA_HEAD_DIM ** -0.5)
        o_ref[:, sl] = o.astype(BF16)
    for hk in range(SW_KV_HEADS):
        ksl = slice(hk * SW_HEAD_DIM, (hk + 1) * SW_HEAD_DIM)
        q = jnp.concatenate(
            [qb_ref[:, (hk * SW_GROUP + g) * SW_HEAD_DIM:(hk * SW_GROUP + g + 1) * SW_HEAD_DIM] for g in range(SW_GROUP)],
            axis=0).astype(BF16)
        o = _joint_attention(q, [(kb_ref[:, ksl].astype(BF16), vb_ref[:, ksl].astype(BF16), None, None)],
                             SW_HEAD_DIM ** -0.5, sink=_sink_column(sink_ref, hk, seq))
        c0 = A_DIM + hk * SW_GROUP * SW_HEAD_DIM
        o_ref[:, c0:c0 + SW_GROUP * SW_HEAD_DIM] = _unstack_heads(o, seq)
    for gi, u_ref in enumerate((u0_ref, u1_ref, u2_ref, u3_ref)):
        c0 = A_DIM + B_Q_DIM + gi * POOL_GROUP_DIM
        o_ref[:, c0:c0 + POOL_GROUP_DIM] = _pool_group(
            u_ref[...], POOL_WINDOWS[gi], pw_ref[gi].astype(BF16),
            ps_ref[:, gi * POOL_GROUP_DIM:(gi + 1) * POOL_GROUP_DIM]).astype(BF16)


def _ctx_mix(p, sink_l, pool_w, pool_scale, layer, seq, caches):
    n = p.shape[0]
    batch = n // seq
    wide = lambda c: pl.BlockSpec((seq, A_DIM), lambda b, c=c: (b, c))
    narrow = lambda c: pl.BlockSpec((seq, B_KV_DIM), lambda b, c=c: (b, c))
    in_specs = [pl.BlockSpec(memory_space=pltpu.SMEM)]
    in_specs += [wide(COL_QA + i) for i in range(4)]
    in_specs += [narrow(COL_KB), narrow(COL_KB + 1)]
    in_specs += [narrow(COL_U + g) for g in range(POOL_GROUPS)]
    in_specs += [pl.BlockSpec((None, POOL_GROUPS, POOL_GROUP_DIM, POOL_GROUP_DIM), lambda b: (layer, 0, 0, 0)),
                 pl.BlockSpec((None, 1, POOL_DIM), lambda b: (layer, 0, 0))]
    n_fixed = len(in_specs)
    in_specs += [pl.BlockSpec(memory_space=pl.ANY)] * len(caches)
    cache_spec = lambda width: pl.BlockSpec((None, None, seq, width), lambda b: (b, layer, 0, 0))
    cache_shape = lambda width: jax.ShapeDtypeStruct((batch, DEPTH, seq, width), F32)
    outs = pl.pallas_call(
        _ctx_mix_kernel,
        grid=(batch,),
        in_specs=in_specs,
        out_specs=[pl.BlockSpec((seq, 3 * A_DIM), lambda b: (b, 0)),
                   cache_spec(A_DIM), cache_spec(A_DIM), cache_spec(B_KV_DIM), cache_spec(B_KV_DIM)],
        out_shape=[jax.ShapeDtypeStruct((n, 3 * A_DIM), BF16),
                   cache_shape(A_DIM), cache_shape(A_DIM), cache_shape(B_KV_DIM), cache_shape(B_KV_DIM)],
        input_output_aliases={n_fixed + k: 1 + k for k in range(len(caches))},
        compiler_params=_params(("parallel",), 24 * MIB),
        name="ctx_mix",
    )(sink_l, p, p, p, p, p, p, p, p, p, p, pool_w, pool_scale, *caches)
    return outs[0], tuple(outs[1:])


NA_PAIR_ROWS = 2 * NA_MAX_KH - 2


def _rpb_table_kernel(rpb_ref, t_ref):
    lane = lax.broadcasted_iota(jnp.int32, (GRID_W, 2 * GRID_W), 1)
    qc = lax.broadcasted_iota(jnp.int32, (GRID_W, 2 * GRID_W), 0)
    kc = lane & (GRID_W - 1)
    upper = lane >= GRID_W
    dcm = jnp.clip(kc - qc + NA_KW - 1, 0, 2 * NA_KW - 2)
    col0 = jnp.clip(qc - NA_KW // 2, 0, GRID_W - NA_KW)
    inside = (kc >= col0) & (kc < col0 + NA_KW)
    n_dc = 2 * NA_KW - 1
    n_dr = 2 * NA_MAX_KH - 1

    def body(i, carry):
        h = i // NA_PAIR_ROWS
        dr = i - h * NA_PAIR_ROWS
        base = (h * n_dr + dr) * n_dc
        acc = jnp.zeros((GRID_W, 2 * GRID_W), F32)
        for dc in range(n_dc):
            val = jnp.where(upper, rpb_ref[base + n_dc + dc], rpb_ref[base + dc])
            acc = jnp.where(dcm == dc, val, acc)
        t_ref[i] = jnp.where(inside, acc, NEG_INF)
        return carry

    lax.fori_loop(0, NA_HEADS * NA_PAIR_ROWS, body, 0)


def _rpb_table(rpb_l):
    return pl.pallas_call(
        _rpb_table_kernel,
        in_specs=[pl.BlockSpec(memory_space=pltpu.SMEM)],
        out_specs=pl.BlockSpec(memory_space=pltpu.VMEM),
        out_shape=jax.ShapeDtypeStruct((NA_HEADS * NA_PAIR_ROWS, GRID_W, 2 * GRID_W), F32),
        name="rpb_table",
    )(rpb_l.reshape(-1))


def _na_kernel(q_ref, k_ref, v_ref, kc_ref, vc_ref, t_ref, o_ref, *, rows):
    qr = pl.program_id(1)
    row0 = jnp.clip(qr - NA_MAX_KH // 2, 0, rows - NA_MAX_KH)
    start = pl.multiple_of(row0 * GRID_W, GRID_W)
    nkeys = NA_MAX_KH * GRID_W
    d0 = row0 - qr + NA_MAX_KH - 1
    for h in range(NA_HEADS):
        sl = slice(h * NA_HEAD_DIM, (h + 1) * NA_HEAD_DIM)
        bias = jnp.concatenate([t_ref[h * NA_PAIR_ROWS + d0 + 2 * i] for i in range(NA_MAX_KH // 2)], axis=1)
        o = _joint_attention(
            q_ref[:, sl].astype(BF16),
            [(k_ref[pl.ds(start, nkeys), sl].astype(BF16), v_ref[pl.ds(start, nkeys), sl].astype(BF16), bias, None),
             (kc_ref[:, sl].astype(BF16), vc_ref[:, sl].astype(BF16), None, None)],
            NA_HEAD_DIM ** -0.5)
        o_ref[:, sl] = o.astype(BF16)


def _na_latent(p, cache_k, cache_v, table, layer, seq):
    n = p.shape[0]
    rows = seq // GRID_W
    past = cache_k.shape[2]
    ctx_spec = pl.BlockSpec((None, None, past, A_DIM), lambda b, r: (b, layer, 0, 0))
    return pl.pallas_call(
        functools.partial(_na_kernel, rows=rows),
        grid=(n // seq, rows),
        in_specs=[pl.BlockSpec((GRID_W, A_DIM), lambda b, r: (b * rows + r, COL_QA)),
                  pl.BlockSpec((seq, A_DIM), lambda b, r: (b, COL_QA + 1)),
                  pl.BlockSpec((seq, A_DIM), lambda b, r: (b, COL_QA + 2)),
                  ctx_spec, ctx_spec,
                  pl.BlockSpec(table.shape, lambda b, r: (0, 0, 0))],
        out_specs=pl.BlockSpec((GRID_W, A_DIM), lambda b, r: (b * rows + r, 0)),
        out_shape=jax.ShapeDtypeStruct((n, A_DIM), BF16),
        compiler_params=_params(("parallel", "arbitrary"), 16 * MIB),
        name="na_latent",
    )(p, p, p, cache_k, cache_v, table)


def _rope_tables(seq):
    nfreq = SW_HEAD_DIM // 4
    inv = 1.0 / (ROPE_THETA ** (np.arange(nfreq, dtype=np.float32) / np.float32(nfreq)))
    t = np.arange(seq)
    pos = (t // GRID_W, t % GRID_W)
    cos = np.zeros((seq, SW_HEAD_DIM), np.float32)
    sin_next = np.zeros((seq, SW_HEAD_DIM), np.float32)
    sin_prev = np.zeros((seq, SW_HEAD_DIM), np.float32)
    for a in range(2):
        ang = pos[a].astype(np.float32)[:, None] * inv[None, :].astype(np.float32)
        c, s = np.cos(ang).astype(np.float32), np.sin(ang).astype(np.float32)
        lo = 2 * a * nfreq
        cos[:, lo:lo + nfreq] = c
        cos[:, lo + nfreq:lo + 2 * nfreq] = c
        sin_next[:, lo:lo + nfreq] = -s
        sin_prev[:, lo + nfreq:lo + 2 * nfreq] = s
    tile = lambda x: jnp.asarray(np.tile(x, (1, 128 // SW_HEAD_DIM)))
    return tile(cos), tile(sin_next), tile(sin_prev)


def _rope(x, cos, sin_next, sin_prev):
    nfreq = SW_HEAD_DIM // 4
    return x * cos + pltpu.roll(x, 128 - nfreq, 1) * sin_next + pltpu.roll(x, nfreq, 1) * sin_prev


def _sw_kernel(sink_ref, q_ref, k_ref, v_ref, kc_ref, vc_ref, cq_ref, snq_ref, spq_ref, ck_ref, snk_ref, spk_ref,
               o_ref, kr_scr, *, seq):
    n = pl.program_id(1)

    @pl.when(n == 0)
    def _():
        kr_scr[...] = _rope(k_ref[...], ck_ref[...], snk_ref[...], spk_ref[...]).astype(BF16)

    nwin = 3 * SW_BLOCK
    kstart = pl.multiple_of(jnp.clip((n - 1) * SW_BLOCK, 0, seq - nwin), SW_BLOCK)
    cq, snq, spq = cq_ref[...], snq_ref[...], spq_ref[...]
    q = jnp.concatenate([_rope(q_ref[:, c * 128:(c + 1) * 128], cq, snq, spq) for c in range(B_Q_DIM // 128)],
                        axis=1).astype(BF16)
    rows = SW_GROUP * SW_BLOCK
    qpos = n * SW_BLOCK + (lax.broadcasted_iota(jnp.int32, (rows, nwin), 0) & (SW_BLOCK - 1))
    kpos = kstart + lax.broadcasted_iota(jnp.int32, (rows, nwin), 1)
    band = jnp.abs(qpos - kpos) <= SW_WINDOW
    for hk in range(SW_KV_HEADS):
        ksl = slice(hk * SW_HEAD_DIM, (hk + 1) * SW_HEAD_DIM)
        qs = jnp.concatenate(
            [q[:, (hk * SW_GROUP + g) * SW_HEAD_DIM:(hk * SW_GROUP + g + 1) * SW_HEAD_DIM] for g in range(SW_GROUP)], axis=0)
        o = _joint_attention(
            qs,
            [(kr_scr[pl.ds(kstart, nwin), ksl], v_ref[pl.ds(kstart, nwin), ksl].astype(BF16), None, band),
             (kc_ref[:, ksl].astype(BF16), vc_ref[:, ksl].astype(BF16), None, None)],
            SW_HEAD_DIM ** -0.5, sink=_sink_column(sink_ref, hk, SW_BLOCK))
        c0 = hk * SW_GROUP * SW_HEAD_DIM
        o_ref[:, c0:c0 + SW_GROUP * SW_HEAD_DIM] = _unstack_heads(o, SW_BLOCK)


def _sw_latent(p, cache_k, cache_v, sink_l, layer, seq):
    n = p.shape[0]
    nb = seq // SW_BLOCK
    past = cache_k.shape[2]
    cos, sin_next, sin_prev = _rope_tables(seq)
    ctx_spec = pl.BlockSpec((None, None, past, B_KV_DIM), lambda b, i: (b, layer, 0, 0))
    tab_q = pl.BlockSpec((SW_BLOCK, 128), lambda b, i: (i, 0))
    tab_k = pl.BlockSpec((seq, 128), lambda b, i: (0, 0))
    return pl.pallas_call(
        functools.partial(_sw_kernel, seq=seq),
        grid=(n // seq, nb),
        in_specs=[pl.BlockSpec(memory_space=pltpu.SMEM),
                  pl.BlockSpec((SW_BLOCK, B_Q_DIM), lambda b, i: (b * nb + i, COL_QA + 3)),
                  pl.BlockSpec((seq, B_KV_DIM), lambda b, i: (b, COL_KB)),
                  pl.BlockSpec((seq, B_KV_DIM), lambda b, i: (b, COL_KB + 1)),
                  ctx_spec, ctx_spec, tab_q, tab_q, tab_q, tab_k, tab_k, tab_k],
        out_specs=pl.BlockSpec((SW_BLOCK, B_Q_DIM), lambda b, i: (b * nb + i, 0)),
        out_shape=jax.ShapeDtypeStruct((n, B_Q_DIM), BF16),
        scratch_shapes=[pltpu.VMEM((seq, B_KV_DIM), BF16)],
        compiler_params=_params(("parallel", "arbitrary"), 16 * MIB),
        name="sw_latent",
    )(sink_l, p, p, p, cache_k, cache_v, cos, sin_next, sin_prev, cos, sin_next, sin_prev)


def _pool_kernel(u0_ref, u1_ref, u2_ref, u3_ref, pw_ref, ps_ref, o_ref):
    for gi, u_ref in enumerate((u0_ref, u1_ref, u2_ref, u3_ref)):
        sl = slice(gi * POOL_GROUP_DIM, (gi + 1) * POOL_GROUP_DIM)
        o_ref[:, sl] = _pool_group(u_ref[...], POOL_WINDOWS[gi], pw_ref[gi].astype(BF16), ps_ref[:, sl]).astype(BF16)


def _pool_latent(p, pool_w, pool_scale, layer, seq):
    n = p.shape[0]
    return pl.pallas_call(
        _pool_kernel,
        grid=(n // seq,),
        in_specs=[pl.BlockSpec((seq, POOL_GROUP_DIM), lambda b, g=g: (b, COL_U + g)) for g in range(POOL_GROUPS)]
        + [pl.BlockSpec((None, POOL_GROUPS, POOL_GROUP_DIM, POOL_GROUP_DIM), lambda b: (layer, 0, 0, 0)),
           pl.BlockSpec((None, 1, POOL_DIM), lambda b: (layer, 0, 0))],
        out_specs=pl.BlockSpec((seq, POOL_DIM), lambda b: (b, 0)),
        out_shape=jax.ShapeDtypeStruct((n, POOL_DIM), BF16),
        compiler_params=_params(("parallel",), 16 * MIB),
        name="pool_latent",
    )(p, p, p, p, pool_w, pool_scale)


def _split_bf16(x):
    hi = x.astype(BF16)
    return hi, (x - hi.astype(F32)).astype(BF16)


def _branch_merge_kernel(oa_ref, ob_ref, oc_ref, ga_ref, gb_ref, gc_ref, wa_ref, wb_ref, wc_ref, m_ref):
    s = pl.program_id(1)
    for t in range(BMERGE_NA):
        @pl.when(s == t)
        def _(t=t):
            cols = slice(t * BMERGE_CA, (t + 1) * BMERGE_CA)
            m = (ga_ref[...].astype(F32) * _dot(oa_ref[...], wa_ref[:, cols])
                 + gb_ref[...].astype(F32) * _dot(ob_ref[...], wb_ref[:, cols])
                 + gc_ref[...].astype(F32) * _dot(oc_ref[...], wc_ref[:, cols]))
            m_ref[...] = m.astype(BF16)


def _branch_merge(branches, gates, wa, wb, wc, layer):
    n = gates.shape[0]
    tm, ca = BMERGE_TM, BMERGE_CA
    gate = lambda k: pl.BlockSpec((pl.Element(tm), pl.Element(ca)),
                                  lambda i, s, k=k: (i * tm, pl.multiple_of(GATES_COL0 + k * D_MODEL + s * ca, LANES)))
    branch = lambda col: pl.BlockSpec((tm, A_DIM), lambda i, s: (i, col))
    resident = lambda rows: pl.BlockSpec((None, rows, D_MODEL), lambda i, s: (layer, 0, 0),
                                         pipeline_mode=pl.Buffered(1))
    oa, ob, oc = branches
    est = (2 * 3 * tm * A_DIM * 2 + 2 * 3 * tm * ca * 2 + 2 * tm * ca * 2 + 3 * A_DIM * D_MODEL * 2 + 8 * tm * ca * 4)
    return pl.pallas_call(
        _branch_merge_kernel,
        grid=(n // tm, BMERGE_NA),
        in_specs=[branch(oa[1]), branch(ob[1]), branch(oc[1]), gate(0), gate(1), gate(2),
                  resident(A_DIM), resident(B_Q_DIM), resident(POOL_DIM)],
        out_specs=pl.BlockSpec((tm, ca), lambda i, s: (i, s)),
        out_shape=jax.ShapeDtypeStruct((n, D_MODEL), BF16),
        compiler_params=_params(("parallel", "arbitrary"), est),
        name="branch_merge",
    )(oa[0], ob[0], oc[0], gates, gates, gates, wa, wb, wc)


def _outproj_epilogue(y_ref, x_ref, mod_ref, nw_ref, wr_ref, x1_ref, h2_ref, lg_ref):
    tiles = [slice(c * OUTPROJ_CB, (c + 1) * OUTPROJ_CB) for c in range(D_MODEL // OUTPROJ_CB)]
    ss = jnp.zeros((x_ref.shape[0], 1), F32)
    for cols in tiles:
        y = y_ref[:, cols]
        ss = ss + (y * y).sum(axis=-1, keepdims=True)
    r1 = lax.rsqrt(ss / D_MODEL + RMS_EPS)
    ss = jnp.zeros((x_ref.shape[0], 1), F32)
    for cols in tiles:
        x1 = x_ref[:, cols] + mod_ref[2:3, cols] * (y_ref[:, cols] * r1 * nw_ref[1:2, cols])
        x1_ref[:, cols] = x1
        ss = ss + (x1 * x1).sum(axis=-1, keepdims=True)
    r2 = lax.rsqrt(ss / D_MODEL + RMS_EPS)
    lg = jnp.zeros(lg_ref.shape, F32)
    for cols in tiles:
        h2 = x1_ref[:, cols] * r2 * nw_ref[2:3, cols] * (1.0 + mod_ref[4:5, cols]) + mod_ref[3:4, cols]
        h2_ref[:, cols] = h2.astype(BF16)
        h_hi, h_lo = _split_bf16(h2)
        w_hi, w_lo = _split_bf16(wr_ref[:, cols])
        lg = lg + (_dot_nt(w_hi, h_hi) + (_dot_nt(w_hi, h_lo) + _dot_nt(w_lo, h_hi)))
    lg_ref[...] = lg


def _outproj_kernel(m_ref, x_ref, mod_ref, nw_ref, wo_ref, wr_ref, x1_ref, h2_ref, lg_ref, ya_scr, yb_scr):
    i = pl.program_id(0)

    @pl.when(i == 0)
    def _():
        yb_scr[...] = jnp.zeros(yb_scr.shape, F32)

    def step(y_new, y_old):
        y_new[...] = _dot(m_ref[...], wo_ref[...])
        _outproj_epilogue(y_old, x_ref, mod_ref, nw_ref, wr_ref, x1_ref, h2_ref, lg_ref)

    @pl.when(i % 2 == 0)
    def _():
        step(ya_scr, yb_scr)

    @pl.when(i % 2 == 1)
    def _():
        step(yb_scr, ya_scr)


def _outproj(m, x, mod, norm_w, wo, wr_t, layer, group_of_tile):
    n = x.shape[0]
    tm = OUTPROJ_TM
    nt = n // tm
    lag = lambda i: jnp.maximum(i - 1, 0)
    row = lambda: pl.BlockSpec((tm, D_MODEL), lambda i: (lag(i), 0))
    est = (2 * tm * D_MODEL * (2 + 4 + 4 + 2) + D_MODEL * D_MODEL * 2 + 2 * tm * D_MODEL * 4 + 4 * tm * OUTPROJ_CB * 4)
    return pl.pallas_call(
        _outproj_kernel,
        grid=(nt + 1,),
        in_specs=[pl.BlockSpec((tm, D_MODEL), lambda i: (jnp.minimum(i, nt - 1), 0)),
                  row(),
                  pl.BlockSpec((None, None, 6, D_MODEL), lambda i: (layer, group_of_tile(lag(i)), 0, 0)),
                  pl.BlockSpec((None, 4, D_MODEL), lambda i: (layer, 0, 0)),
                  pl.BlockSpec((None, D_MODEL, D_MODEL), lambda i: (layer, 0, 0), pipeline_mode=pl.Buffered(1)),
                  pl.BlockSpec((None, N_EXPERTS, D_MODEL), lambda i: (layer, 0, 0))],
        out_specs=[row(), row(), pl.BlockSpec((N_EXPERTS, tm), lambda i: (0, lag(i)))],
        out_shape=[jax.ShapeDtypeStruct((n, D_MODEL), F32), jax.ShapeDtypeStruct((n, D_MODEL), BF16),
                   jax.ShapeDtypeStruct((N_EXPERTS, n), F32)],
        scratch_shapes=[pltpu.VMEM((tm, D_MODEL), F32), pltpu.VMEM((tm, D_MODEL), F32)],
        compiler_params=_params(("arbitrary",), est),
        name="outproj",
    )(m, x, mod, norm_w, wo, wr_t)


def _dispatch_kernel(lg_ref, h_ref, xs_ref, gate_ref, rc_ref, aff_scr, sel_scr, *, seq, cap):
    lg = lg_ref[...]
    e = jnp.exp(lg - lg.max(axis=0, keepdims=True))
    aff_scr[...] = e / e.sum(axis=0, keepdims=True)
    rc_ref[...] = jnp.zeros(rc_ref.shape, F32)
    chunk = min(seq, RANK_CHUNK)
    lane_e = lax.broadcasted_iota(jnp.int32, (chunk, LANES), 1)
    slot = lax.broadcasted_iota(jnp.int32, (cap, seq), 0).astype(F32)

    def body(ex, carry):
        row = aff_scr[pl.ds(ex, 1), :]
        rowb = jnp.broadcast_to(row, (chunk, seq))
        rank_row = jnp.zeros((1, seq), F32)
        for c in range(seq // chunk):
            tr = c * chunk + lax.broadcasted_iota(jnp.int32, (chunk, seq), 0)
            tc = lax.broadcasted_iota(jnp.int32, (chunk, seq), 1)
            col = jnp.where(tr == tc, rowb, 0.0).sum(axis=1, keepdims=True)
            beats = ((col > rowb) | ((col == rowb) & (tr < tc))).astype(F32)
            rank_row = rank_row + beats.sum(axis=0, keepdims=True)
            rank_col = (seq - 1.0) - beats.sum(axis=1, keepdims=True)
            rows = slice(c * chunk, (c + 1) * chunk)
            rc_ref[rows, :] = jnp.where(lane_e == ex, rank_col, rc_ref[rows, :])
        sel = slot == rank_row
        sel_scr[pl.ds(pl.multiple_of(ex * cap, cap), cap), :] = sel.astype(BF16)
        gate = jnp.where(sel, jnp.broadcast_to(row, (cap, seq)), 0.0).sum(axis=1, keepdims=True)
        gate_ref[ex] = jnp.broadcast_to(gate, (cap, LANES))
        return carry

    lax.fori_loop(0, N_EXPERTS, body, 0, unroll=4 if seq <= RANK_CHUNK else 1)
    xs = _dot(sel_scr[...], h_ref[...]).astype(BF16)
    xs_ref[...] = xs.reshape(N_EXPERTS, cap, D_MODEL)


def _dispatch(lg_t, h2, seq):
    n = h2.shape[0]
    nb = n // seq
    cap = EC_FACTOR * seq // N_EXPERTS
    est = (2 * seq * D_MODEL * 2 + 2 * N_EXPERTS * cap * D_MODEL * 2 + N_EXPERTS * cap * seq * 2
           + N_EXPERTS * cap * D_MODEL * 4 + 12 * min(seq, RANK_CHUNK) * seq * 4)
    return pl.pallas_call(
        functools.partial(_dispatch_kernel, seq=seq, cap=cap),
        grid=(nb,),
        in_specs=[pl.BlockSpec((N_EXPERTS, seq), lambda b: (0, b)),
                  pl.BlockSpec((seq, D_MODEL), lambda b: (b, 0))],
        out_specs=[pl.BlockSpec((N_EXPERTS, cap, D_MODEL), lambda b: (0, b, 0)),
                   pl.BlockSpec((N_EXPERTS, cap, LANES), lambda b: (0, b, 0)),
                   pl.BlockSpec((seq, LANES), lambda b: (b, 0))],
        out_shape=[jax.ShapeDtypeStruct((N_EXPERTS, nb * cap, D_MODEL), BF16),
                   jax.ShapeDtypeStruct((N_EXPERTS, nb * cap, LANES), F32),
                   jax.ShapeDtypeStruct((n, LANES), F32)],
        scratch_shapes=[pltpu.VMEM((N_EXPERTS, seq), F32), pltpu.VMEM((N_EXPERTS * cap, seq), BF16)],
        compiler_params=_params(("parallel",), est),
        name="dispatch",
    )(lg_t, h2)


EXPERT_NF = D_EXPERT // EXPERT_TF
EXPERT_ND = D_MODEL // EXPERT_TD


assert EXPERT_NF == EXPERT_ND


def _expert_kernel(xc_ref, xl_ref, gc_ref, gl_ref, wg_ref, wu_ref, wd_ref, yc_ref, yl_ref, hc_scr, hl_scr):
    e = pl.program_id(0)
    t = pl.program_id(1)
    cur = e % 2

    @pl.when(e >= 1)
    def _():
        wd = wd_ref[0].astype(BF16)
        for h_scr, g_ref, y_ref in ((hc_scr, gc_ref, yc_ref), (hl_scr, gl_ref, yl_ref)):
            acc = _dot(h_scr[1 - cur, 0], wd[0:EXPERT_TF])
            for f in range(1, EXPERT_NF):
                acc = acc + _dot(h_scr[1 - cur, f], wd[f * EXPERT_TF:(f + 1) * EXPERT_TF])
            y_ref[0] = acc * g_ref[0, :, 0:1]

    @pl.when(e < N_EXPERTS)
    def _():
        wg = wg_ref[0].astype(BF16)
        wu = wu_ref[0].astype(BF16)
        for x_ref, h_scr in ((xc_ref, hc_scr), (xl_ref, hl_scr)):
            x = x_ref[0]
            a = _dot(x, wg)
            h_scr[cur, t] = ((a * jax.nn.sigmoid(a)) * _dot(x, wu)).astype(BF16)


def _experts(xs_c, xs_l, gate_c, gate_l, w_gate, w_up, w_down, layer):
    sc, sl = xs_c.shape[1], xs_l.shape[1]
    tf, td = EXPERT_TF, EXPERT_TD
    last = N_EXPERTS - 1
    up_expert = lambda e: jnp.minimum(e, last)
    down_expert = lambda e: jnp.maximum(e - 1, 0)
    up_tile = lambda e, t: (layer, up_expert(e), 0, jnp.where(e > last, EXPERT_NF - 1, t))
    down_tile = lambda e, t: (down_expert(e), 0, jnp.where(e == 0, 0, t))
    x_spec = lambda s: pl.BlockSpec((1, s, D_MODEL), lambda e, t: (up_expert(e), 0, 0))
    g_spec = lambda s: pl.BlockSpec((1, s, LANES), lambda e, t: (down_expert(e), 0, 0))
    est = (2 * (sc + sl) * D_MODEL * 2 + 2 * 2 * D_MODEL * tf * 4 + 2 * D_EXPERT * td * 4 + 2 * (sc + sl) * td * 4
           + 2 * (sc + sl) * D_EXPERT * 2 + 2 * D_MODEL * tf * 2 + D_EXPERT * td * 2 + 6 * sc * max(tf, td) * 4)
    return pl.pallas_call(
        _expert_kernel,
        grid=(N_EXPERTS + 1, EXPERT_NF),
        in_specs=[x_spec(sc), x_spec(sl), g_spec(sc), g_spec(sl),
                  pl.BlockSpec((None, 1, D_MODEL, tf), up_tile),
                  pl.BlockSpec((None, 1, D_MODEL, tf), up_tile),
                  pl.BlockSpec((None, 1, D_EXPERT, td), lambda e, t: (layer,) + down_tile(e, t))],
        out_specs=[pl.BlockSpec((1, sc, td), down_tile), pl.BlockSpec((1, sl, td), down_tile)],
        out_shape=[jax.ShapeDtypeStruct((N_EXPERTS, sc, D_MODEL), F32),
                   jax.ShapeDtypeStruct((N_EXPERTS, sl, D_MODEL), F32)],
        scratch_shapes=[pltpu.VMEM((2, EXPERT_NF, sc, tf), BF16), pltpu.VMEM((2, EXPERT_NF, sl, tf), BF16)],
        compiler_params=_params(("arbitrary", "arbitrary"), est),
        name="experts",
    )(xs_c, xs_l, gate_c, gate_l, w_gate, w_up, w_down)


def _combine_kernel(rc_ref, ye_ref, x1_ref, mod_ref, nw_ref, o_ref, m_scr, ffn_scr, *, cap):
    rc = rc_ref[...]
    slot = lax.broadcasted_iota(jnp.int32, (rc.shape[0], cap), 1).astype(F32)
    for ex in range(N_EXPERTS):
        m_scr[:, ex * cap:(ex + 1) * cap] = (rc[:, ex:ex + 1] == slot).astype(BF16)
    onehot = m_scr[...]
    for c in range(D_MODEL // COMBINE_TD):
        cols = slice(c * COMBINE_TD, (c + 1) * COMBINE_TD)
        ye = ye_ref[:, :, cols].reshape(N_EXPERTS * cap, COMBINE_TD)
        hi, lo = _split_bf16(ye)
        ffn_scr[:, cols] = _dot(onehot, hi) + _dot(onehot, lo)
    o_ref[...] = x1_ref[...] + mod_ref[5:6, :] * _rms(ffn_scr[...], nw_ref[3:4, :])


def _combine(rank_col, ye, x1, mod, norm_w, layer, seq, group_of_batch):
    n = x1.shape[0]
    cap = EC_FACTOR * seq // N_EXPERTS
    tr = COMBINE_TR
    per = seq // tr
    ye_buffers = 2 if per == 1 else 1
    est = (ye_buffers * N_EXPERTS * cap * D_MODEL * 4 + 4 * tr * D_MODEL * 4 + tr * N_EXPERTS * cap * 2
           + tr * D_MODEL * 4 + 3 * N_EXPERTS * cap * COMBINE_TD * 4)
    return pl.pallas_call(
        functools.partial(_combine_kernel, cap=cap),
        grid=(n // seq, per),
        in_specs=[pl.BlockSpec((tr, LANES), lambda b, i: (b * per + i, 0)),
                  pl.BlockSpec((N_EXPERTS, cap, D_MODEL), lambda b, i: (0, b, 0),
                               pipeline_mode=pl.Buffered(ye_buffers)),
                  pl.BlockSpec((tr, D_MODEL), lambda b, i: (b * per + i, 0)),
                  pl.BlockSpec((None, None, 6, D_MODEL), lambda b, i: (layer, group_of_batch(b), 0, 0)),
                  pl.BlockSpec((None, 4, D_MODEL), lambda b, i: (layer, 0, 0))],
        out_specs=pl.BlockSpec((tr, D_MODEL), lambda b, i: (b * per + i, 0)),
        out_shape=jax.ShapeDtypeStruct((n, D_MODEL), F32),
        scratch_shapes=[pltpu.VMEM((tr, N_EXPERTS * cap), BF16), pltpu.VMEM((tr, D_MODEL), F32)],
        compiler_params=_params(("parallel", "arbitrary"), est),
        name="combine",
    )(rank_col, ye, x1, mod, norm_w)


def kernel(x_prompt, x_sample, c, cache_a_k, cache_a_v, cache_b_k, cache_b_v, c_ctx, norm_w, w_ada, b_ada, w_in, a_rpb,
           b_sink, c_pool_w, c_scale, w_branch_a, w_branch_b, w_branch_c, w_out, w_router, w_gate_e, w_up_e, w_down_e):
    batch, seq_c, _ = x_prompt.shape
    dec_batch, seq_l, _ = x_sample.shape
    past = cache_a_k.shape[2]

    cond = jnp.zeros((ADA_ROWS, D_MODEL), F32).at[0].set(c_ctx).at[1:1 + dec_batch].set(c)
    mod = _adaln(cond, w_ada, b_ada).reshape(DEPTH, ADA_ROWS, 6, D_MODEL)

    ctx_group = lambda i: 0
    lat_group_inproj = lambda i: 1 + i // (seq_l // INPROJ_TM)
    lat_group_outproj = lambda i: 1 + i // (seq_l // OUTPROJ_TM)
    lat_group_batch = lambda b: 1 + b

    cak = cache_a_k.reshape(dec_batch, DEPTH, past, A_DIM)
    cav = cache_a_v.reshape(dec_batch, DEPTH, past, A_DIM)
    cbk = cache_b_k.reshape(dec_batch, DEPTH, past, B_KV_DIM)
    cbv = cache_b_v.reshape(dec_batch, DEPTH, past, B_KV_DIM)

    x_c = x_prompt.reshape(batch * seq_c, D_MODEL)
    x_l = x_sample.reshape(dec_batch * seq_l, D_MODEL)
    w_in_bf = w_in.astype(BF16)
    wa, wb, wc, wo = (w.astype(BF16) for w in (w_branch_a, w_branch_b, w_branch_c, w_out))
    wr_t = jnp.swapaxes(w_router, 1, 2)
    pool_scale = c_scale.reshape(DEPTH, 1, POOL_DIM)
    caches = ()
    for l in range(DEPTH):
        p_c, g_c = _inproj(x_c, mod, norm_w, w_in_bf, l, ctx_group)
        o_c, caches = _ctx_mix(p_c, b_sink[l], c_pool_w, pool_scale, l, seq_c, caches)
        branches_c = tuple((o_c, k) for k in range(N_BRANCH))
        m_c = _branch_merge(branches_c, g_c, wa, wb, wc, l)
        x1_c, h2_c, lg_c = _outproj(m_c, x_c, mod, norm_w, wo, wr_t, l, ctx_group)
        xs_c, gate_c, rc_c = _dispatch(lg_c, h2_c, seq_c)

        p_l, g_l = _inproj(x_l, mod, norm_w, w_in_bf, l, lat_group_inproj)
        table = _rpb_table(a_rpb[l])
        o_a = _na_latent(p_l, cak, cav, table, l, seq_l)
        o_b = _sw_latent(p_l, cbk, cbv, b_sink[l], l, seq_l)
        o_p = _pool_latent(p_l, c_pool_w, pool_scale, l, seq_l)
        m_l = _branch_merge(((o_a, 0), (o_b, 0), (o_p, 0)), g_l, wa, wb, wc, l)
        x1_l, h2_l, lg_l = _outproj(m_l, x_l, mod, norm_w, wo, wr_t, l, lat_group_outproj)
        xs_l, gate_l, rc_l = _dispatch(lg_l, h2_l, seq_l)

        ye_c, ye_l = _experts(xs_c, xs_l, gate_c, gate_l, w_gate_e, w_up_e, w_down_e, l)
        x_c = _combine(rc_c, ye_c, x1_c, mod, norm_w, l, seq_c, ctx_group)
        x_l = _combine(rc_l, ye_l, x1_l, mod, norm_w, l, seq_l, lat_group_batch)

    y_prompt = x_c.reshape(batch, seq_c, D_MODEL)
    y_sample = x_l.reshape(dec_batch, seq_l, D_MODEL)
    new_a_k, new_a_v, new_b_k, new_b_v = caches
    a_shape = (batch, DEPTH, seq_c, NA_HEADS, NA_HEAD_DIM)
    b_shape = (batch, DEPTH, seq_c, SW_KV_HEADS, SW_HEAD_DIM)
    return (y_prompt, y_sample, new_a_k.reshape(a_shape), new_a_v.reshape(a_shape),
            new_b_k.reshape(b_shape), new_b_v.reshape(b_shape))
```

```python
import functools

import numpy as np
import jax
import jax.numpy as jnp
from jax import lax
from jax.experimental import pallas as pl
from jax.experimental.pallas import tpu as pltpu

F32 = jnp.float32
BF16 = jnp.bfloat16

D_MODEL = 2048
DEPTH = 2
GRID_W = 64
NA_HEADS, NA_HEAD_DIM, NA_MAX_KH, NA_KW = 4, 128, 8, 16
SW_Q_HEADS, SW_KV_HEADS, SW_HEAD_DIM = 8, 2, 64
SW_GROUP = SW_Q_HEADS // SW_KV_HEADS
SW_WINDOW, SW_BLOCK = 128, 128
ROPE_THETA = 10000.0
POOL_WINDOWS = (2, 4, 8, 16)
POOL_GROUPS, POOL_GROUP_DIM = 4, 128
POOL_DIM = POOL_GROUPS * POOL_GROUP_DIM
A_DIM = NA_HEADS * NA_HEAD_DIM
B_Q_DIM = SW_Q_HEADS * SW_HEAD_DIM
B_KV_DIM = SW_KV_HEADS * SW_HEAD_DIM
N_BRANCH = 3
GATE_DIM = N_BRANCH * D_MODEL
QKVU_DIM = 3 * A_DIM + B_Q_DIM + 2 * B_KV_DIM + POOL_DIM
IN_DIM = QKVU_DIM + GATE_DIM
N_EXPERTS = 16
EC_FACTOR = 2
D_EXPERT = 1024
RMS_EPS = 1e-6
NEG_INF = -1e30

COL_QA = 0
COL_KB = (3 * A_DIM + B_Q_DIM) // B_KV_DIM
COL_U = COL_KB + 2

V7X_VMEM_BYTES = 64 * 1024 * 1024
V7X_VMEM_CEILING = 60000 * 1024
MIB = 1024 * 1024

ADA_ROWS = 16
ADA_TN = 1024
V7X_MXU_DIM = 256
LANES = 128
INPROJ_TM = 1024
INPROJ_TN = 1280
assert INPROJ_TN % V7X_MXU_DIM == 0 and IN_DIM % INPROJ_TN == 0
INPROJ_TILES = IN_DIM // INPROJ_TN
INPROJ_SPLIT = QKVU_DIM // INPROJ_TN
P_WIDTH = (INPROJ_SPLIT + 1) * INPROJ_TN
GATES_WIDTH = (INPROJ_TILES - INPROJ_SPLIT) * INPROJ_TN
GATES_COL0 = QKVU_DIM - INPROJ_SPLIT * INPROJ_TN
BMERGE_TM = 1024
BMERGE_CA = 1024
BMERGE_NA = D_MODEL // BMERGE_CA
OUTPROJ_TM = 512
OUTPROJ_CB = 512
EXPERT_TF = 256
EXPERT_TD = 512
COMBINE_TR = 256
ROW_CHUNK = 64


def _vmem_limit(estimate_bytes):
    return int(min(V7X_VMEM_CEILING, max(32 * MIB, estimate_bytes + 8 * MIB)))


def _params(semantics, estimate_bytes):
    return pltpu.CompilerParams(dimension_semantics=semantics,
                                vmem_limit_bytes=_vmem_limit(estimate_bytes))


def _rms(x, g):
    ms = jnp.mean(x * x, axis=-1, keepdims=True)
    return x * lax.rsqrt(ms + RMS_EPS) * g


def _dot(a, b):
    return jnp.dot(a, b, preferred_element_type=F32)


def _dot_nt(a, b):
    return lax.dot_general(a, b, (((1,), (1,)), ((), ())), preferred_element_type=F32)


def _adaln_kernel(c_ref, w_ref, b_ref, o_ref):
    c = c_ref[...]
    s = (c * jax.nn.sigmoid(c)).astype(BF16)
    o_ref[0] = _dot(s, w_ref[0].astype(BF16)) + b_ref[0]


def _adaln(cond, w_ada, b_ada):
    n_out = w_ada.shape[-1]
    return pl.pallas_call(
        _adaln_kernel,
        grid=(DEPTH, n_out // ADA_TN),
        in_specs=[pl.BlockSpec((ADA_ROWS, D_MODEL), lambda l, j: (0, 0)),
                  pl.BlockSpec((1, D_MODEL, ADA_TN), lambda l, j: (l, 0, j)),
                  pl.BlockSpec((1, 1, ADA_TN), lambda l, j: (l, 0, j))],
        out_specs=pl.BlockSpec((1, ADA_ROWS, ADA_TN), lambda l, j: (l, 0, j)),
        out_shape=jax.ShapeDtypeStruct((DEPTH, ADA_ROWS, n_out), F32),
        compiler_params=_params(("parallel", "parallel"), 2 * D_MODEL * ADA_TN * 4),
        name="adaln",
    )(cond, w_ada, b_ada.reshape(DEPTH, 1, n_out))


def _inproj_kernel(x_ref, mod_ref, nw_ref, w_ref, o_ref, gate_ref, h_scr):
    j = pl.program_id(1)

    @pl.when(j == 0)
    def _():
        g = nw_ref[0:1, :]
        sc = 1.0 + mod_ref[1:2, :]
        sh = mod_ref[0:1, :]

        def body(r, carry):
            rows = pl.ds(pl.multiple_of(r * ROW_CHUNK, ROW_CHUNK), ROW_CHUNK)
            h_scr[rows, :] = (_rms(x_ref[rows, :], g) * sc + sh).astype(BF16)
            return carry

        lax.fori_loop(0, INPROJ_TM // ROW_CHUNK, body, 0)

    @pl.when(j < INPROJ_SPLIT)
    def _():
        o_ref[...] = _dot(h_scr[...], w_ref[...])

    @pl.when(j == INPROJ_SPLIT)
    def _():
        acc = _dot(h_scr[...], w_ref[...])
        o_ref[...] = acc
        gate_ref[...] = jax.nn.sigmoid(acc).astype(BF16)

    @pl.when(j > INPROJ_SPLIT)
    def _():
        gate_ref[...] = jax.nn.sigmoid(_dot(h_scr[...], w_ref[...])).astype(BF16)


def _mod_spec(layer, group_of):
    return pl.BlockSpec((None, None, 6, D_MODEL), lambda i, j: (layer, group_of(i), 0, 0))


def _inproj(x, mod, norm_w, w_in_bf, layer, group_of_tile):
    n = x.shape[0]
    est = (2 * INPROJ_TM * D_MODEL * 4 + 2 * D_MODEL * INPROJ_TN * 2 + 2 * INPROJ_TM * INPROJ_TN * (4 + 2)
           + INPROJ_TM * D_MODEL * 2 + 2 * INPROJ_TM * INPROJ_TN * 4)
    return pl.pallas_call(
        _inproj_kernel,
        grid=(n // INPROJ_TM, INPROJ_TILES),
        in_specs=[pl.BlockSpec((INPROJ_TM, D_MODEL), lambda i, j: (i, 0)),
                  _mod_spec(layer, group_of_tile),
                  pl.BlockSpec((None, 4, D_MODEL), lambda i, j: (layer, 0, 0)),
                  pl.BlockSpec((None, D_MODEL, INPROJ_TN), lambda i, j: (layer, 0, j))],
        out_specs=[pl.BlockSpec((INPROJ_TM, INPROJ_TN), lambda i, j: (i, jnp.minimum(j, INPROJ_SPLIT))),
                   pl.BlockSpec((INPROJ_TM, INPROJ_TN), lambda i, j: (i, jnp.maximum(j - INPROJ_SPLIT, 0)))],
        out_shape=[jax.ShapeDtypeStruct((n, P_WIDTH), F32), jax.ShapeDtypeStruct((n, GATES_WIDTH), BF16)],
        scratch_shapes=[pltpu.VMEM((INPROJ_TM, D_MODEL), BF16)],
        compiler_params=_params(("parallel", "arbitrary"), est),
        name="inproj",
    )(x, mod, norm_w, w_in_bf)


def _joint_attention(q, segments, scale, sink=None):
    scores = []
    for k, _, bias, mask in segments:
        s = _dot_nt(q, k) * scale
        if bias is not None:
            s = s + bias
        if mask is not None:
            s = jnp.where(mask, s, NEG_INF)
        scores.append(s)
    m = scores[0].max(axis=-1, keepdims=True)
    for s in scores[1:]:
        m = jnp.maximum(m, s.max(axis=-1, keepdims=True))
    if sink is not None:
        m = jnp.maximum(m, sink)
    denom = jnp.exp(sink - m) if sink is not None else 0.0
    acc = None
    for s, (_, v, _, _) in zip(scores, segments):
        e = jnp.exp(s - m)
        denom = denom + e.sum(axis=-1, keepdims=True)
        pv = _dot(e.astype(BF16), v)
        acc = pv if acc is None else acc + pv
    return acc / denom


def _pool_group(u, window, pw_bf, scale_row):
    seq = u.shape[0]
    pad = 8
    n = seq + 2 * pad
    z = jnp.zeros((pad, POOL_GROUP_DIM), F32)
    p = jnp.concatenate([z, u, z], axis=0)
    k = 1
    while k < window:
        p = p + pltpu.roll(p, n - k, 0)
        k *= 2
    win = pltpu.roll(p, window // 2, 0)[pad:pad + seq]
    t = lax.broadcasted_iota(jnp.int32, (seq, 1), 0)
    lo = jnp.maximum(t - window // 2, 0)
    hi = jnp.minimum(t - window // 2 + window, seq)
    cnt = (hi - lo).astype(F32)
    pooled = win / cnt - u
    return _dot(pooled.astype(BF16), pw_bf) * scale_row


def _unstack_heads(o, rows_per_head):
    return jnp.concatenate([o[g * rows_per_head:(g + 1) * rows_per_head] for g in range(SW_GROUP)], axis=1).astype(BF16)


def _sink_column(sink_ref, kv_head, rows_per_head):
    r = lax.broadcasted_iota(jnp.int32, (SW_GROUP * rows_per_head, 1), 0)
    col = jnp.full((SW_GROUP * rows_per_head, 1), sink_ref[kv_head * SW_GROUP], F32)
    for g in range(1, SW_GROUP):
        col = jnp.where(r >= g * rows_per_head, sink_ref[kv_head * SW_GROUP + g], col)
    return col


def _ctx_mix_kernel(sink_ref, qa_ref, ka_ref, va_ref, qb_ref, kb_ref, vb_ref, u0_ref, u1_ref, u2_ref, u3_ref,
                    pw_ref, ps_ref, *rest, layer):
    o_ref = rest[-5]
    seq = qa_ref.shape[0]
    for cache_ref, src_ref in zip(rest[-4:], (ka_ref, va_ref, kb_ref, vb_ref)):
        if len(cache_ref.shape) == 2:
            cache_ref[...] = src_ref[...]
        else:
            for d in range(cache_ref.shape[0]):
                cache_ref[d] = src_ref[...] if d == layer else jnp.zeros(src_ref.shape, F32)
    for h in range(NA_HEADS):
        sl = slice(h * NA_HEAD_DIM, (h + 1) * NA_HEAD_DIM)
        o = _joint_attention(qa_ref[:, sl].astype(BF16),
                             [(ka_ref[:, sl].astype(BF16), va_ref[:, sl].astype(BF16), None, None)],
                             NA_HEAD_DIM ** -0.5)
        o_ref[:, sl] = o.astype(BF16)
    for hk in range(SW_KV_HEADS):
        ksl = slice(hk * SW_HEAD_DIM, (hk + 1) * SW_HEAD_DIM)
        q = jnp.concatenate(
            [qb_ref[:, (hk * SW_GROUP + g) * SW_HEAD_DIM:(hk * SW_GROUP + g + 1) * SW_HEAD_DIM] for g in range(SW_GROUP)],
            axis=0).astype(BF16)
        o = _joint_attention(q, [(kb_ref[:, ksl].astype(BF16), vb_ref[:, ksl].astype(BF16), None, None)],
                             SW_HEAD_DIM ** -0.5, sink=_sink_column(sink_ref, hk, seq))
        c0 = A_DIM + hk * SW_GROUP * SW_HEAD_DIM
        o_ref[:, c0:c0 + SW_GROUP * SW_HEAD_DIM] = _unstack_heads(o, seq)
    for gi, u_ref in enumerate((u0_ref, u1_ref, u2_ref, u3_ref)):
        c0 = A_DIM + B_Q_DIM + gi * POOL_GROUP_DIM
        o_ref[:, c0:c0 + POOL_GROUP_DIM] = _pool_group(
            u_ref[...], POOL_WINDOWS[gi], pw_ref[gi].astype(BF16),
            ps_ref[:, gi * POOL_GROUP_DIM:(gi + 1) * POOL_GROUP_DIM]).astype(BF16)


def _ctx_mix(p, sink_l, pool_w, pool_scale, layer, seq, caches):
    n = p.shape[0]
    batch = n // seq
    wide = lambda c: pl.BlockSpec((seq, A_DIM), lambda b, c=c: (b, c))
    narrow = lambda c: pl.BlockSpec((seq, B_KV_DIM), lambda b, c=c: (b, c))
    in_specs = [pl.BlockSpec(memory_space=pltpu.SMEM)]
    in_specs += [wide(COL_QA + i) for i in range(4)]
    in_specs += [narrow(COL_KB), narrow(COL_KB + 1)]
    in_specs += [narrow(COL_U + g) for g in range(POOL_GROUPS)]
    in_specs += [pl.BlockSpec((None, POOL_GROUPS, POOL_GROUP_DIM, POOL_GROUP_DIM), lambda b: (layer, 0, 0, 0)),
                 pl.BlockSpec((None, 1, POOL_DIM), lambda b: (layer, 0, 0))]
    n_fixed = len(in_specs)
    in_specs += [pl.BlockSpec(memory_space=pl.ANY)] * len(caches)
    if caches:
        cache_spec = lambda width: pl.BlockSpec((None, None, seq, width), lambda b: (b, layer, 0, 0))
    else:
        cache_spec = lambda width: pl.BlockSpec((None, DEPTH, seq, width), lambda b: (b, 0, 0, 0))
    cache_shape = lambda width: jax.ShapeDtypeStruct((batch, DEPTH, seq, width), F32)
    outs = pl.pallas_call(
        functools.partial(_ctx_mix_kernel, layer=layer),
        grid=(batch,),
        in_specs=in_specs,
        out_specs=[pl.BlockSpec((seq, 3 * A_DIM), lambda b: (b, 0)),
                   cache_spec(A_DIM), cache_spec(A_DIM), cache_spec(B_KV_DIM), cache_spec(B_KV_DIM)],
        out_shape=[jax.ShapeDtypeStruct((n, 3 * A_DIM), BF16),
                   cache_shape(A_DIM), cache_shape(A_DIM), cache_shape(B_KV_DIM), cache_shape(B_KV_DIM)],
        input_output_aliases={n_fixed + k: 1 + k for k in range(len(caches))},
        compiler_params=_params(("parallel",), 24 * MIB),
        name="ctx_mix",
    )(sink_l, p, p, p, p, p, p, p, p, p, p, pool_w, pool_scale, *caches)
    return outs[0], tuple(outs[1:])


NA_PAIR_ROWS = 2 * NA_MAX_KH - 2


def _rpb_table_kernel(rpb_ref, t_ref):
    lane = lax.broadcasted_iota(jnp.int32, (GRID_W, 2 * GRID_W), 1)
    qc = lax.broadcasted_iota(jnp.int32, (GRID_W, 2 * GRID_W), 0)
    kc = lane & (GRID_W - 1)
    upper = lane >= GRID_W
    dcm = jnp.clip(kc - qc + NA_KW - 1, 0, 2 * NA_KW - 2)
    col0 = jnp.clip(qc - NA_KW // 2, 0, GRID_W - NA_KW)
    inside = (kc >= col0) & (kc < col0 + NA_KW)
    n_dc = 2 * NA_KW - 1
    n_dr = 2 * NA_MAX_KH - 1

    def body(i, carry):
        h = i // NA_PAIR_ROWS
        dr = i - h * NA_PAIR_ROWS
        base = (h * n_dr + dr) * n_dc
        acc = jnp.zeros((GRID_W, 2 * GRID_W), F32)
        for dc in range(n_dc):
            val = jnp.where(upper, rpb_ref[base + n_dc + dc], rpb_ref[base + dc])
            acc = jnp.where(dcm == dc, val, acc)
        t_ref[i] = jnp.where(inside, acc, NEG_INF)
        return carry

    lax.fori_loop(0, NA_HEADS * NA_PAIR_ROWS, body, 0)


def _rpb_table(rpb_l):
    return pl.pallas_call(
        _rpb_table_kernel,
        in_specs=[pl.BlockSpec(memory_space=pltpu.SMEM)],
        out_specs=pl.BlockSpec(memory_space=pltpu.VMEM),
        out_shape=jax.ShapeDtypeStruct((NA_HEADS * NA_PAIR_ROWS, GRID_W, 2 * GRID_W), F32),
        name="rpb_table",
    )(rpb_l.reshape(-1))


def _na_kernel(q_ref, k_ref, v_ref, kc_ref, vc_ref, t_ref, o_ref, *, rows):
    qr = pl.program_id(1)
    row0 = jnp.clip(qr - NA_MAX_KH // 2, 0, rows - NA_MAX_KH)
    start = pl.multiple_of(row0 * GRID_W, GRID_W)
    nkeys = NA_MAX_KH * GRID_W
    d0 = row0 - qr + NA_MAX_KH - 1
    for h in range(NA_HEADS):
        sl = slice(h * NA_HEAD_DIM, (h + 1) * NA_HEAD_DIM)
        bias = jnp.concatenate([t_ref[h * NA_PAIR_ROWS + d0 + 2 * i] for i in range(NA_MAX_KH // 2)], axis=1)
        o = _joint_attention(
            q_ref[:, sl].astype(BF16),
            [(k_ref[pl.ds(start, nkeys), sl].astype(BF16), v_ref[pl.ds(start, nkeys), sl].astype(BF16), bias, None),
             (kc_ref[:, h, :].astype(BF16), vc_ref[:, h, :].astype(BF16), None, None)],
            NA_HEAD_DIM ** -0.5)
        o_ref[:, sl] = o.astype(BF16)


def _na_latent(p, cache_k, cache_v, table, layer, seq):
    n = p.shape[0]
    rows = seq // GRID_W
    past = cache_k.shape[2]
    ctx_spec = pl.BlockSpec((None, None, past, NA_HEADS, NA_HEAD_DIM), lambda b, r: (b, layer, 0, 0, 0))
    return pl.pallas_call(
        functools.partial(_na_kernel, rows=rows),
        grid=(n // seq, rows),
        in_specs=[pl.BlockSpec((GRID_W, A_DIM), lambda b, r: (b * rows + r, COL_QA)),
                  pl.BlockSpec((seq, A_DIM), lambda b, r: (b, COL_QA + 1)),
                  pl.BlockSpec((seq, A_DIM), lambda b, r: (b, COL_QA + 2)),
                  ctx_spec, ctx_spec,
                  pl.BlockSpec(table.shape, lambda b, r: (0, 0, 0))],
        out_specs=pl.BlockSpec((GRID_W, A_DIM), lambda b, r: (b * rows + r, 0)),
        out_shape=jax.ShapeDtypeStruct((n, A_DIM), BF16),
        compiler_params=_params(("parallel", "arbitrary"), 16 * MIB),
        name="na_latent",
    )(p, p, p, cache_k, cache_v, table)


def _rope_tables(seq):
    nfreq = SW_HEAD_DIM // 4
    inv = 1.0 / (ROPE_THETA ** (np.arange(nfreq, dtype=np.float32) / np.float32(nfreq)))
    t = np.arange(seq)
    pos = (t // GRID_W, t % GRID_W)
    cos = np.zeros((seq, SW_HEAD_DIM), np.float32)
    sin_next = np.zeros((seq, SW_HEAD_DIM), np.float32)
    sin_prev = np.zeros((seq, SW_HEAD_DIM), np.float32)
    for a in range(2):
        ang = pos[a].astype(np.float32)[:, None] * inv[None, :].astype(np.float32)
        c, s = np.cos(ang).astype(np.float32), np.sin(ang).astype(np.float32)
        lo = 2 * a * nfreq
        cos[:, lo:lo + nfreq] = c
        cos[:, lo + nfreq:lo + 2 * nfreq] = c
        sin_next[:, lo:lo + nfreq] = -s
        sin_prev[:, lo + nfreq:lo + 2 * nfreq] = s
    tile = lambda x: jnp.asarray(np.tile(x, (1, 128 // SW_HEAD_DIM)))
    return tile(cos), tile(sin_next), tile(sin_prev)


def _rope(x, cos, sin_next, sin_prev):
    nfreq = SW_HEAD_DIM // 4
    return x * cos + pltpu.roll(x, 128 - nfreq, 1) * sin_next + pltpu.roll(x, nfreq, 1) * sin_prev


def _sw_kernel(sink_ref, q_ref, k_ref, v_ref, kc_ref, vc_ref, cq_ref, snq_ref, spq_ref, ck_ref, snk_ref, spk_ref,
               o_ref, kr_scr, *, seq):
    n = pl.program_id(1)

    @pl.when(n == 0)
    def _():
        kr_scr[...] = _rope(k_ref[...], ck_ref[...], snk_ref[...], spk_ref[...]).astype(BF16)

    nwin = 3 * SW_BLOCK
    kstart = pl.multiple_of(jnp.clip((n - 1) * SW_BLOCK, 0, seq - nwin), SW_BLOCK)
    cq, snq, spq = cq_ref[...], snq_ref[...], spq_ref[...]
    q = jnp.concatenate([_rope(q_ref[:, c * 128:(c + 1) * 128], cq, snq, spq) for c in range(B_Q_DIM // 128)],
                        axis=1).astype(BF16)
    rows = SW_GROUP * SW_BLOCK
    qpos = n * SW_BLOCK + (lax.broadcasted_iota(jnp.int32, (rows, nwin), 0) & (SW_BLOCK - 1))
    kpos = kstart + lax.broadcasted_iota(jnp.int32, (rows, nwin), 1)
    band = jnp.abs(qpos - kpos) <= SW_WINDOW
    for hk in range(SW_KV_HEADS):
        ksl = slice(hk * SW_HEAD_DIM, (hk + 1) * SW_HEAD_DIM)
        qs = jnp.concatenate(
            [q[:, (hk * SW_GROUP + g) * SW_HEAD_DIM:(hk * SW_GROUP + g + 1) * SW_HEAD_DIM] for g in range(SW_GROUP)], axis=0)
        o = _joint_attention(
            qs,
            [(kr_scr[pl.ds(kstart, nwin), ksl], v_ref[pl.ds(kstart, nwin), ksl].astype(BF16), None, band),
             (kc_ref[:, ksl].astype(BF16), vc_ref[:, ksl].astype(BF16), None, None)],
            SW_HEAD_DIM ** -0.5, sink=_sink_column(sink_ref, hk, SW_BLOCK))
        c0 = hk * SW_GROUP * SW_HEAD_DIM
        o_ref[:, c0:c0 + SW_GROUP * SW_HEAD_DIM] = _unstack_heads(o, SW_BLOCK)


def _sw_latent(p, cache_k, cache_v, sink_l, layer, seq):
    n = p.shape[0]
    nb = seq // SW_BLOCK
    past = cache_k.shape[2]
    cos, sin_next, sin_prev = _rope_tables(seq)
    ctx_spec = pl.BlockSpec((None, None, past, B_KV_DIM), lambda b, i: (b, layer, 0, 0))
    tab_q = pl.BlockSpec((SW_BLOCK, 128), lambda b, i: (i, 0))
    tab_k = pl.BlockSpec((seq, 128), lambda b, i: (0, 0))
    return pl.pallas_call(
        functools.partial(_sw_kernel, seq=seq),
        grid=(n // seq, nb),
        in_specs=[pl.BlockSpec(memory_space=pltpu.SMEM),
                  pl.BlockSpec((SW_BLOCK, B_Q_DIM), lambda b, i: (b * nb + i, COL_QA + 3)),
                  pl.BlockSpec((seq, B_KV_DIM), lambda b, i: (b, COL_KB)),
                  pl.BlockSpec((seq, B_KV_DIM), lambda b, i: (b, COL_KB + 1)),
                  ctx_spec, ctx_spec, tab_q, tab_q, tab_q, tab_k, tab_k, tab_k],
        out_specs=pl.BlockSpec((SW_BLOCK, B_Q_DIM), lambda b, i: (b * nb + i, 0)),
        out_shape=jax.ShapeDtypeStruct((n, B_Q_DIM), BF16),
        scratch_shapes=[pltpu.VMEM((seq, B_KV_DIM), BF16)],
        compiler_params=_params(("parallel", "arbitrary"), 16 * MIB),
        name="sw_latent",
    )(sink_l, p, p, p, cache_k, cache_v, cos, sin_next, sin_prev, cos, sin_next, sin_prev)


def _pool_kernel(u0_ref, u1_ref, u2_ref, u3_ref, pw_ref, ps_ref, o_ref):
    for gi, u_ref in enumerate((u0_ref, u1_ref, u2_ref, u3_ref)):
        sl = slice(gi * POOL_GROUP_DIM, (gi + 1) * POOL_GROUP_DIM)
        o_ref[:, sl] = _pool_group(u_ref[...], POOL_WINDOWS[gi], pw_ref[gi].astype(BF16), ps_ref[:, sl]).astype(BF16)


def _pool_latent(p, pool_w, pool_scale, layer, seq):
    n = p.shape[0]
    return pl.pallas_call(
        _pool_kernel,
        grid=(n // seq,),
        in_specs=[pl.BlockSpec((seq, POOL_GROUP_DIM), lambda b, g=g: (b, COL_U + g)) for g in range(POOL_GROUPS)]
        + [pl.BlockSpec((None, POOL_GROUPS, POOL_GROUP_DIM, POOL_GROUP_DIM), lambda b: (layer, 0, 0, 0)),
           pl.BlockSpec((None, 1, POOL_DIM), lambda b: (layer, 0, 0))],
        out_specs=pl.BlockSpec((seq, POOL_DIM), lambda b: (b, 0)),
        out_shape=jax.ShapeDtypeStruct((n, POOL_DIM), BF16),
        compiler_params=_params(("parallel",), 16 * MIB),
        name="pool_latent",
    )(p, p, p, p, pool_w, pool_scale)


def _split_bf16(x):
    hi = x.astype(BF16)
    return hi, (x - hi.astype(F32)).astype(BF16)


def _branch_merge_kernel(oa_ref, ob_ref, oc_ref, ga_ref, gb_ref, gc_ref, wa_ref, wb_ref, wc_ref, m_ref):
    s = pl.program_id(1)
    for t in range(BMERGE_NA):
        @pl.when(s == t)
        def _(t=t):
            cols = slice(t * BMERGE_CA, (t + 1) * BMERGE_CA)
            m = (ga_ref[...].astype(F32) * _dot(oa_ref[...], wa_ref[:, cols])
                 + gb_ref[...].astype(F32) * _dot(ob_ref[...], wb_ref[:, cols])
                 + gc_ref[...].astype(F32) * _dot(oc_ref[...], wc_ref[:, cols]))
            m_ref[...] = m.astype(BF16)


def _branch_merge(branches, gates, wa, wb, wc, layer):
    n = gates.shape[0]
    tm, ca = BMERGE_TM, BMERGE_CA
    gate = lambda k: pl.BlockSpec((pl.Element(tm), pl.Element(ca)),
                                  lambda i, s, k=k: (i * tm, pl.multiple_of(GATES_COL0 + k * D_MODEL + s * ca, LANES)))
    branch = lambda col: pl.BlockSpec((tm, A_DIM), lambda i, s: (i, col))
    resident = lambda rows: pl.BlockSpec((None, rows, D_MODEL), lambda i, s: (layer, 0, 0),
                                         pipeline_mode=pl.Buffered(1))
    oa, ob, oc = branches
    est = (2 * 3 * tm * A_DIM * 2 + 2 * 3 * tm * ca * 2 + 2 * tm * ca * 2 + 3 * A_DIM * D_MODEL * 2 + 8 * tm * ca * 4)
    return pl.pallas_call(
        _branch_merge_kernel,
        grid=(n // tm, BMERGE_NA),
        in_specs=[branch(oa[1]), branch(ob[1]), branch(oc[1]), gate(0), gate(1), gate(2),
                  resident(A_DIM), resident(B_Q_DIM), resident(POOL_DIM)],
        out_specs=pl.BlockSpec((tm, ca), lambda i, s: (i, s)),
        out_shape=jax.ShapeDtypeStruct((n, D_MODEL), BF16),
        compiler_params=_params(("parallel", "arbitrary"), est),
        name="branch_merge",
    )(oa[0], ob[0], oc[0], gates, gates, gates, wa, wb, wc)


def _outproj_epilogue(y_ref, x_ref, mod_ref, nw_ref, wr_ref, x1_ref, h2_ref, lg_ref):
    tiles = [slice(c * OUTPROJ_CB, (c + 1) * OUTPROJ_CB) for c in range(D_MODEL // OUTPROJ_CB)]
    ss = jnp.zeros((x_ref.shape[0], 1), F32)
    for cols in tiles:
        y = y_ref[:, cols]
        ss = ss + (y * y).sum(axis=-1, keepdims=True)
    r1 = lax.rsqrt(ss / D_MODEL + RMS_EPS)
    ss = jnp.zeros((x_ref.shape[0], 1), F32)
    for cols in tiles:
        x1 = x_ref[:, cols] + mod_ref[2:3, cols] * (y_ref[:, cols] * r1 * nw_ref[1:2, cols])
        x1_ref[:, cols] = x1
        ss = ss + (x1 * x1).sum(axis=-1, keepdims=True)
    r2 = lax.rsqrt(ss / D_MODEL + RMS_EPS)
    lg = jnp.zeros(lg_ref.shape, F32)
    for cols in tiles:
        h2 = x1_ref[:, cols] * r2 * nw_ref[2:3, cols] * (1.0 + mod_ref[4:5, cols]) + mod_ref[3:4, cols]
        h2_ref[:, cols] = h2.astype(BF16)
        h_hi, h_lo = _split_bf16(h2)
        w_hi, w_lo = _split_bf16(wr_ref[:, cols])
        lg = lg + (_dot_nt(w_hi, h_hi) + (_dot_nt(w_hi, h_lo) + _dot_nt(w_lo, h_hi)))
    lg_ref[...] = lg


def _outproj_kernel(m_ref, x_ref, mod_ref, nw_ref, wo_ref, wr_ref, x1_ref, h2_ref, lg_ref, ya_scr, yb_scr):
    i = pl.program_id(0)

    @pl.when(i == 0)
    def _():
        yb_scr[...] = jnp.zeros(yb_scr.shape, F32)

    def step(y_new, y_old):
        y_new[...] = _dot(m_ref[...], wo_ref[...])
        _outproj_epilogue(y_old, x_ref, mod_ref, nw_ref, wr_ref, x1_ref, h2_ref, lg_ref)

    @pl.when(i % 2 == 0)
    def _():
        step(ya_scr, yb_scr)

    @pl.when(i % 2 == 1)
    def _():
        step(yb_scr, ya_scr)


def _outproj(m, x, mod, norm_w, wo, wr_t, layer, group_of_tile):
    n = x.shape[0]
    tm = OUTPROJ_TM
    nt = n // tm
    lag = lambda i: jnp.maximum(i - 1, 0)
    row = lambda: pl.BlockSpec((tm, D_MODEL), lambda i: (lag(i), 0))
    est = (2 * tm * D_MODEL * (2 + 4 + 4 + 2) + D_MODEL * D_MODEL * 2 + 2 * tm * D_MODEL * 4 + 4 * tm * OUTPROJ_CB * 4)
    return pl.pallas_call(
        _outproj_kernel,
        grid=(nt + 1,),
        in_specs=[pl.BlockSpec((tm, D_MODEL), lambda i: (jnp.minimum(i, nt - 1), 0)),
                  row(),
                  pl.BlockSpec((None, None, 6, D_MODEL), lambda i: (layer, group_of_tile(lag(i)), 0, 0)),
                  pl.BlockSpec((None, 4, D_MODEL), lambda i: (layer, 0, 0)),
                  pl.BlockSpec((None, D_MODEL, D_MODEL), lambda i: (layer, 0, 0), pipeline_mode=pl.Buffered(1)),
                  pl.BlockSpec((None, N_EXPERTS, D_MODEL), lambda i: (layer, 0, 0))],
        out_specs=[row(), row(), pl.BlockSpec((N_EXPERTS, tm), lambda i: (0, lag(i)))],
        out_shape=[jax.ShapeDtypeStruct((n, D_MODEL), F32), jax.ShapeDtypeStruct((n, D_MODEL), BF16),
                   jax.ShapeDtypeStruct((N_EXPERTS, n), F32)],
        scratch_shapes=[pltpu.VMEM((tm, D_MODEL), F32), pltpu.VMEM((tm, D_MODEL), F32)],
        compiler_params=_params(("arbitrary",), est),
        name="outproj",
    )(m, x, mod, norm_w, wo, wr_t)


RANK_TILE = LANES


def _dispatch_kernel(lg_ref, h_ref, xs_ref, gate_ref, rc_ref, aff_scr, sel_scr, *, seq, cap):
    t = RANK_TILE
    nt = seq // t
    lg = lg_ref[...]
    e = jnp.exp(lg - lg.max(axis=0, keepdims=True))
    aff_scr[...] = e / e.sum(axis=0, keepdims=True)
    aff = aff_scr[...]
    ident = jnp.where(lax.broadcasted_iota(jnp.int32, (seq, seq), 0) == lax.broadcasted_iota(jnp.int32, (seq, seq), 1),
                      1.0, 0.0).astype(BF16)
    a1 = aff.astype(BF16)
    r1 = aff - a1.astype(F32)
    a2 = r1.astype(BF16)
    a3 = (r1 - a2.astype(F32)).astype(BF16)
    aff_col = (_dot_nt(ident, a1) + _dot_nt(ident, a2)) + _dot_nt(ident, a3)

    earlier = lax.broadcasted_iota(jnp.int32, (t, t), 0) < lax.broadcasted_iota(jnp.int32, (t, t), 1)
    slot = lax.broadcasted_iota(jnp.int32, (cap, seq), 0).astype(F32)
    ranks = []
    for ex in range(N_EXPERTS):
        row = aff[ex:ex + 1, :]
        cols = [jnp.broadcast_to(aff_col[r * t:(r + 1) * t, ex:ex + 1], (t, t)) for r in range(nt)]
        counts = []
        for c in range(nt):
            rowb = jnp.broadcast_to(row[:, c * t:(c + 1) * t], (t, t))
            acc = jnp.zeros((t, t), F32)
            for r in range(nt):
                if r < c:
                    beats = cols[r] >= rowb
                elif r > c:
                    beats = cols[r] > rowb
                else:
                    beats = (cols[r] > rowb) | (earlier & (cols[r] == rowb))
                acc = acc + jnp.where(beats, 1.0, 0.0)
            counts.append(acc.sum(axis=0, keepdims=True))
        rank_row = jnp.concatenate(counts, axis=1) if nt > 1 else counts[0]
        ranks.append(rank_row)
        sel = slot == rank_row
        sel_scr[ex * cap:(ex + 1) * cap, :] = sel.astype(BF16)
        gate = jnp.where(sel, jnp.broadcast_to(row, (cap, seq)), 0.0).sum(axis=1, keepdims=True)
        gate_ref[ex] = jnp.broadcast_to(gate, (cap, LANES))
    rank = jnp.concatenate(ranks + [jnp.zeros((LANES - N_EXPERTS, seq), F32)], axis=0)
    rc_ref[...] = _dot_nt(ident, jnp.minimum(rank, float(cap)).astype(BF16))
    xs = _dot(sel_scr[...], h_ref[...]).astype(BF16)
    xs_ref[...] = xs.reshape(N_EXPERTS, cap, D_MODEL)


def _dispatch(lg_t, h2, seq):
    n = h2.shape[0]
    nb = n // seq
    cap = EC_FACTOR * seq // N_EXPERTS
    est = (2 * seq * D_MODEL * 2 + 2 * N_EXPERTS * cap * D_MODEL * 2 + N_EXPERTS * cap * seq * 2
           + N_EXPERTS * cap * D_MODEL * 4 + seq * seq * (2 + 4) + 4 * seq * LANES * 4)
    return pl.pallas_call(
        functools.partial(_dispatch_kernel, seq=seq, cap=cap),
        grid=(nb,),
        in_specs=[pl.BlockSpec((N_EXPERTS, seq), lambda b: (0, b)),
                  pl.BlockSpec((seq, D_MODEL), lambda b: (b, 0))],
        out_specs=[pl.BlockSpec((N_EXPERTS, cap, D_MODEL), lambda b: (0, b, 0)),
                   pl.BlockSpec((N_EXPERTS, cap, LANES), lambda b: (0, b, 0)),
                   pl.BlockSpec((seq, LANES), lambda b: (b, 0))],
        out_shape=[jax.ShapeDtypeStruct((N_EXPERTS, nb * cap, D_MODEL), BF16),
                   jax.ShapeDtypeStruct((N_EXPERTS, nb * cap, LANES), F32),
                   jax.ShapeDtypeStruct((n, LANES), F32)],
        scratch_shapes=[pltpu.VMEM((N_EXPERTS, seq), F32), pltpu.VMEM((N_EXPERTS * cap, seq), BF16)],
        compiler_params=_params(("parallel",), est),
        name="dispatch",
    )(lg_t, h2)


EXPERT_NF = D_EXPERT // EXPERT_TF
EXPERT_ND = D_MODEL // EXPERT_TD


assert EXPERT_NF == EXPERT_ND


def _expert_kernel(xc_ref, xl_ref, gc_ref, gl_ref, wg_ref, wu_ref, wd_ref, yc_ref, yl_ref, hc_scr, hl_scr):
    e = pl.program_id(0)
    t = pl.program_id(1)
    cur = e % 2

    @pl.when(e >= 1)
    def _():
        wd = wd_ref[0].astype(BF16)
        for h_scr, g_ref, y_ref in ((hc_scr, gc_ref, yc_ref), (hl_scr, gl_ref, yl_ref)):
            acc = _dot(h_scr[1 - cur, 0], wd[0:EXPERT_TF])
            for f in range(1, EXPERT_NF):
                acc = acc + _dot(h_scr[1 - cur, f], wd[f * EXPERT_TF:(f + 1) * EXPERT_TF])
            y_ref[0] = (acc * g_ref[0, :, 0:1]).astype(BF16)

    @pl.when(e < N_EXPERTS)
    def _():
        wg = wg_ref[0].astype(BF16)
        wu = wu_ref[0].astype(BF16)
        for x_ref, h_scr in ((xc_ref, hc_scr), (xl_ref, hl_scr)):
            x = x_ref[0]
            a = _dot(x, wg)
            h_scr[cur, t] = ((a * jax.nn.sigmoid(a)) * _dot(x, wu)).astype(BF16)


def _experts(xs_c, xs_l, gate_c, gate_l, w_gate, w_up, w_down, layer):
    sc, sl = xs_c.shape[1], xs_l.shape[1]
    tf, td = EXPERT_TF, EXPERT_TD
    last = N_EXPERTS - 1
    up_expert = lambda e: jnp.minimum(e, last)
    down_expert = lambda e: jnp.maximum(e - 1, 0)
    up_tile = lambda e, t: (layer, up_expert(e), 0, jnp.where(e > last, EXPERT_NF - 1, t))
    down_tile = lambda e, t: (down_expert(e), 0, jnp.where(e == 0, 0, t))
    x_spec = lambda s: pl.BlockSpec((1, s, D_MODEL), lambda e, t: (up_expert(e), 0, 0))
    g_spec = lambda s: pl.BlockSpec((1, s, LANES), lambda e, t: (down_expert(e), 0, 0))
    est = (2 * (sc + sl) * D_MODEL * 2 + 2 * 2 * D_MODEL * tf * 4 + 2 * D_EXPERT * td * 4 + 2 * (sc + sl) * td * 4
           + 2 * (sc + sl) * D_EXPERT * 2 + 2 * D_MODEL * tf * 2 + D_EXPERT * td * 2 + 6 * sc * max(tf, td) * 4)
    return pl.pallas_call(
        _expert_kernel,
        grid=(N_EXPERTS + 1, EXPERT_NF),
        in_specs=[x_spec(sc), x_spec(sl), g_spec(sc), g_spec(sl),
                  pl.BlockSpec((None, 1, D_MODEL, tf), up_tile),
                  pl.BlockSpec((None, 1, D_MODEL, tf), up_tile),
                  pl.BlockSpec((None, 1, D_EXPERT, td), lambda e, t: (layer,) + down_tile(e, t))],
        out_specs=[pl.BlockSpec((1, sc, td), down_tile), pl.BlockSpec((1, sl, td), down_tile)],
        out_shape=[jax.ShapeDtypeStruct((N_EXPERTS, sc, D_MODEL), BF16),
                   jax.ShapeDtypeStruct((N_EXPERTS, sl, D_MODEL), BF16)],
        scratch_shapes=[pltpu.VMEM((2, EXPERT_NF, sc, tf), BF16), pltpu.VMEM((2, EXPERT_NF, sl, tf), BF16)],
        compiler_params=_params(("arbitrary", "arbitrary"), est),
        name="experts",
    )(xs_c, xs_l, gate_c, gate_l, w_gate, w_up, w_down)


def _combine_kernel(rc_ref, ye_ref, x1_ref, mod_ref, nw_ref, o_ref, m_scr, *, cap):
    rc = rc_ref[...]
    slot = lax.broadcasted_iota(jnp.int32, (rc.shape[0], cap), 1).astype(F32)
    for ex in range(N_EXPERTS):
        m_scr[:, ex * cap:(ex + 1) * cap] = (rc[:, ex:ex + 1] == slot).astype(BF16)
    ffn = _dot(m_scr[...], ye_ref[...].reshape(N_EXPERTS * cap, D_MODEL))
    o_ref[...] = x1_ref[...] + mod_ref[5:6, :] * _rms(ffn, nw_ref[3:4, :])


def _combine(rank_col, ye, x1, mod, norm_w, layer, seq, group_of_batch):
    n = x1.shape[0]
    cap = EC_FACTOR * seq // N_EXPERTS
    tr = COMBINE_TR
    per = seq // tr
    est = (2 * N_EXPERTS * cap * D_MODEL * 2 + 4 * tr * D_MODEL * 4 + tr * N_EXPERTS * cap * 2 + 3 * tr * D_MODEL * 4)
    return pl.pallas_call(
        functools.partial(_combine_kernel, cap=cap),
        grid=(n // seq, per),
        in_specs=[pl.BlockSpec((tr, LANES), lambda b, i: (b * per + i, 0)),
                  pl.BlockSpec((N_EXPERTS, cap, D_MODEL), lambda b, i: (0, b, 0)),
                  pl.BlockSpec((tr, D_MODEL), lambda b, i: (b * per + i, 0)),
                  pl.BlockSpec((None, None, 6, D_MODEL), lambda b, i: (layer, group_of_batch(b), 0, 0)),
                  pl.BlockSpec((None, 4, D_MODEL), lambda b, i: (layer, 0, 0))],
        out_specs=pl.BlockSpec((tr, D_MODEL), lambda b, i: (b * per + i, 0)),
        out_shape=jax.ShapeDtypeStruct((n, D_MODEL), F32),
        scratch_shapes=[pltpu.VMEM((tr, N_EXPERTS * cap), BF16)],
        compiler_params=_params(("parallel", "arbitrary"), est),
        name="combine",
    )(rank_col, ye, x1, mod, norm_w)


def kernel(x_prompt, x_sample, c, cache_a_k, cache_a_v, cache_b_k, cache_b_v, c_ctx, norm_w, w_ada, b_ada, w_in, a_rpb,
           b_sink, c_pool_w, c_scale, w_branch_a, w_branch_b, w_branch_c, w_out, w_router, w_gate_e, w_up_e, w_down_e):
    batch, seq_c, _ = x_prompt.shape
    dec_batch, seq_l, _ = x_sample.shape
    past = cache_a_k.shape[2]

    cond = jnp.zeros((ADA_ROWS, D_MODEL), F32).at[0].set(c_ctx).at[1:1 + dec_batch].set(c)
    mod = _adaln(cond, w_ada, b_ada).reshape(DEPTH, ADA_ROWS, 6, D_MODEL)

    ctx_group = lambda i: 0
    lat_group_inproj = lambda i: 1 + i // (seq_l // INPROJ_TM)
    lat_group_outproj = lambda i: 1 + i // (seq_l // OUTPROJ_TM)
    lat_group_batch = lambda b: 1 + b

    cbk = cache_b_k.reshape(dec_batch, DEPTH, past, B_KV_DIM)
    cbv = cache_b_v.reshape(dec_batch, DEPTH, past, B_KV_DIM)

    x_c = x_prompt.reshape(batch * seq_c, D_MODEL)
    x_l = x_sample.reshape(dec_batch * seq_l, D_MODEL)
    w_in_bf = w_in.astype(BF16)
    wa, wb, wc, wo = (w.astype(BF16) for w in (w_branch_a, w_branch_b, w_branch_c, w_out))
    wr_t = jnp.swapaxes(w_router, 1, 2)
    pool_scale = c_scale.reshape(DEPTH, 1, POOL_DIM)
    caches = ()
    for l in range(DEPTH):
        p_c, g_c = _inproj(x_c, mod, norm_w, w_in_bf, l, ctx_group)
        o_c, caches = _ctx_mix(p_c, b_sink[l], c_pool_w, pool_scale, l, seq_c, caches)
        branches_c = tuple((o_c, k) for k in range(N_BRANCH))
        m_c = _branch_merge(branches_c, g_c, wa, wb, wc, l)
        x1_c, h2_c, lg_c = _outproj(m_c, x_c, mod, norm_w, wo, wr_t, l, ctx_group)
        xs_c, gate_c, rc_c = _dispatch(lg_c, h2_c, seq_c)

        p_l, g_l = _inproj(x_l, mod, norm_w, w_in_bf, l, lat_group_inproj)
        table = _rpb_table(a_rpb[l])
        o_a = _na_latent(p_l, cache_a_k, cache_a_v, table, l, seq_l)
        o_b = _sw_latent(p_l, cbk, cbv, b_sink[l], l, seq_l)
        o_p = _pool_latent(p_l, c_pool_w, pool_scale, l, seq_l)
        m_l = _branch_merge(((o_a, 0), (o_b, 0), (o_p, 0)), g_l, wa, wb, wc, l)
        x1_l, h2_l, lg_l = _outproj(m_l, x_l, mod, norm_w, wo, wr_t, l, lat_group_outproj)
        xs_l, gate_l, rc_l = _dispatch(lg_l, h2_l, seq_l)

        ye_c, ye_l = _experts(xs_c, xs_l, gate_c, gate_l, w_gate_e, w_up_e, w_down_e, l)
        x_c = _combine(rc_c, ye_c, x1_c, mod, norm_w, l, seq_c, ctx_group)
        x_l = _combine(rc_l, ye_l, x1_l, mod, norm_w, l, seq_l, lat_group_batch)

    y_prompt = x_c.reshape(batch, seq_c, D_MODEL)
    y_sample = x_l.reshape(dec_batch, seq_l, D_MODEL)
    new_a_k, new_a_v, new_b_k, new_b_v = caches
    a_shape = (batch, DEPTH, seq_c, NA_HEADS, NA_HEAD_DIM)
    b_shape = (batch, DEPTH, seq_c, SW_KV_HEADS, SW_HEAD_DIM)
    return (y_prompt, y_sample, new_a_k.reshape(a_shape), new_a_v.reshape(a_shape),
            new_b_k.reshape(b_shape), new_b_v.reshape(b_shape))
```

```python
import functools

import numpy as np
import jax
import jax.numpy as jnp
from jax import lax
from jax.experimental import pallas as pl
from jax.experimental.pallas import tpu as pltpu

F32 = jnp.float32
BF16 = jnp.bfloat16

D_MODEL = 2048
DEPTH = 2
GRID_W = 64
NA_HEADS, NA_HEAD_DIM, NA_MAX_KH, NA_KW = 4, 128, 8, 16
SW_Q_HEADS, SW_KV_HEADS, SW_HEAD_DIM = 8, 2, 64
SW_GROUP = SW_Q_HEADS // SW_KV_HEADS
SW_WINDOW, SW_BLOCK = 128, 128
ROPE_THETA = 10000.0
POOL_WINDOWS = (2, 4, 8, 16)
POOL_GROUPS, POOL_GROUP_DIM = 4, 128
POOL_DIM = POOL_GROUPS * POOL_GROUP_DIM
A_DIM = NA_HEADS * NA_HEAD_DIM
B_Q_DIM = SW_Q_HEADS * SW_HEAD_DIM
B_KV_DIM = SW_KV_HEADS * SW_HEAD_DIM
N_BRANCH = 3
GATE_DIM = N_BRANCH * D_MODEL
QKVU_DIM = 3 * A_DIM + B_Q_DIM + 2 * B_KV_DIM + POOL_DIM
IN_DIM = QKVU_DIM + GATE_DIM
N_EXPERTS = 16
EC_FACTOR = 2
D_EXPERT = 1024
RMS_EPS = 1e-6
NEG_INF = -1e30

COL_QA = 0
COL_KB = (3 * A_DIM + B_Q_DIM) // B_KV_DIM
COL_U = COL_KB + 2

V7X_VMEM_BYTES = 64 * 1024 * 1024
V7X_VMEM_CEILING = 60000 * 1024
MIB = 1024 * 1024

ADA_ROWS = 16
ADA_TN = 1024
V7X_MXU_DIM = 256
LANES = 128
INPROJ_TM = 1024
INPROJ_TN = 1280
assert INPROJ_TN % V7X_MXU_DIM == 0 and IN_DIM % INPROJ_TN == 0
INPROJ_TILES = IN_DIM // INPROJ_TN
INPROJ_SPLIT = QKVU_DIM // INPROJ_TN
P_WIDTH = (INPROJ_SPLIT + 1) * INPROJ_TN
GATES_WIDTH = (INPROJ_TILES - INPROJ_SPLIT) * INPROJ_TN
GATES_COL0 = QKVU_DIM - INPROJ_SPLIT * INPROJ_TN
BMERGE_TM = 1024
BMERGE_CA = 1024
BMERGE_NA = D_MODEL // BMERGE_CA
OUTPROJ_TM = 512
OUTPROJ_CB = 512
EXPERT_TF = 256
EXPERT_TD = 512
COMBINE_TR = 256
ROW_CHUNK = 64


def _vmem_limit(estimate_bytes):
    return int(min(V7X_VMEM_CEILING, max(32 * MIB, estimate_bytes + 8 * MIB)))


def _params(semantics, estimate_bytes):
    return pltpu.CompilerParams(dimension_semantics=semantics,
                                vmem_limit_bytes=_vmem_limit(estimate_bytes))


def _rms(x, g):
    ms = jnp.mean(x * x, axis=-1, keepdims=True)
    return x * lax.rsqrt(ms + RMS_EPS) * g


def _sigmoid(x):
    return 0.5 * jnp.tanh(0.5 * x) + 0.5


def _dot(a, b):
    return jnp.dot(a, b, preferred_element_type=F32)


def _dot_nt(a, b):
    return lax.dot_general(a, b, (((1,), (1,)), ((), ())), preferred_element_type=F32)


def _adaln_kernel(c_ref, w_ref, b_ref, o_ref):
    c = c_ref[...]
    s = (c * jax.nn.sigmoid(c)).astype(BF16)
    o_ref[0] = _dot(s, w_ref[0].astype(BF16)) + b_ref[0]


def _adaln(cond, w_ada, b_ada):
    n_out = w_ada.shape[-1]
    return pl.pallas_call(
        _adaln_kernel,
        grid=(DEPTH, n_out // ADA_TN),
        in_specs=[pl.BlockSpec((ADA_ROWS, D_MODEL), lambda l, j: (0, 0)),
                  pl.BlockSpec((1, D_MODEL, ADA_TN), lambda l, j: (l, 0, j)),
                  pl.BlockSpec((1, 1, ADA_TN), lambda l, j: (l, 0, j))],
        out_specs=pl.BlockSpec((1, ADA_ROWS, ADA_TN), lambda l, j: (l, 0, j)),
        out_shape=jax.ShapeDtypeStruct((DEPTH, ADA_ROWS, n_out), F32),
        compiler_params=_params(("parallel", "parallel"), 2 * D_MODEL * ADA_TN * 4),
        name="adaln",
    )(cond, w_ada, b_ada.reshape(DEPTH, 1, n_out))


def _inproj_kernel(x_ref, mod_ref, nw_ref, w_ref, o_ref, gate_ref, h_scr):
    j = pl.program_id(1)

    @pl.when(j == 0)
    def _():
        g = nw_ref[0:1, :]
        sc = 1.0 + mod_ref[1:2, :]
        sh = mod_ref[0:1, :]

        def body(r, carry):
            rows = pl.ds(pl.multiple_of(r * ROW_CHUNK, ROW_CHUNK), ROW_CHUNK)
            h_scr[rows, :] = (_rms(x_ref[rows, :], g) * sc + sh).astype(BF16)
            return carry

        lax.fori_loop(0, INPROJ_TM // ROW_CHUNK, body, 0)

    @pl.when(j < INPROJ_SPLIT)
    def _():
        o_ref[...] = _dot(h_scr[...], w_ref[...])

    @pl.when(j == INPROJ_SPLIT)
    def _():
        acc = _dot(h_scr[...], w_ref[...])
        o_ref[...] = acc
        gate_ref[...] = _sigmoid(acc).astype(BF16)

    @pl.when(j > INPROJ_SPLIT)
    def _():
        gate_ref[...] = _sigmoid(_dot(h_scr[...], w_ref[...])).astype(BF16)


def _mod_spec(layer, group_of):
    return pl.BlockSpec((None, None, 6, D_MODEL), lambda i, j: (layer, group_of(i), 0, 0))


def _inproj(x, mod, norm_w, w_in_bf, layer, group_of_tile):
    n = x.shape[0]
    est = (2 * INPROJ_TM * D_MODEL * 4 + 2 * D_MODEL * INPROJ_TN * 2 + 2 * INPROJ_TM * INPROJ_TN * (4 + 2)
           + INPROJ_TM * D_MODEL * 2 + 2 * INPROJ_TM * INPROJ_TN * 4)
    return pl.pallas_call(
        _inproj_kernel,
        grid=(n // INPROJ_TM, INPROJ_TILES),
        in_specs=[pl.BlockSpec((INPROJ_TM, D_MODEL), lambda i, j: (i, 0)),
                  _mod_spec(layer, group_of_tile),
                  pl.BlockSpec((None, 4, D_MODEL), lambda i, j: (layer, 0, 0)),
                  pl.BlockSpec((None, D_MODEL, INPROJ_TN), lambda i, j: (layer, 0, j))],
        out_specs=[pl.BlockSpec((INPROJ_TM, INPROJ_TN), lambda i, j: (i, jnp.minimum(j, INPROJ_SPLIT))),
                   pl.BlockSpec((INPROJ_TM, INPROJ_TN), lambda i, j: (i, jnp.maximum(j - INPROJ_SPLIT, 0)))],
        out_shape=[jax.ShapeDtypeStruct((n, P_WIDTH), F32), jax.ShapeDtypeStruct((n, GATES_WIDTH), BF16)],
        scratch_shapes=[pltpu.VMEM((INPROJ_TM, D_MODEL), BF16)],
        compiler_params=_params(("parallel", "arbitrary"), est),
        name="inproj",
    )(x, mod, norm_w, w_in_bf)


def _joint_attention(q, segments, scale, sink=None):
    scores = []
    for k, _, bias, mask in segments:
        s = _dot_nt(q, k) * scale
        if bias is not None:
            s = s + bias
        if mask is not None:
            s = jnp.where(mask, s, NEG_INF)
        scores.append(s)
    m = scores[0].max(axis=-1, keepdims=True)
    for s in scores[1:]:
        m = jnp.maximum(m, s.max(axis=-1, keepdims=True))
    if sink is not None:
        m = jnp.maximum(m, sink)
    denom = jnp.exp(sink - m) if sink is not None else 0.0
    acc = None
    for s, (_, v, _, _) in zip(scores, segments):
        e = jnp.exp(s - m)
        denom = denom + e.sum(axis=-1, keepdims=True)
        pv = _dot(e.astype(BF16), v)
        acc = pv if acc is None else acc + pv
    return acc / denom


def _pool_group(u, window, pw_bf, scale_row):
    seq = u.shape[0]
    pad = 8
    n = seq + 2 * pad
    z = jnp.zeros((pad, POOL_GROUP_DIM), F32)
    p = jnp.concatenate([z, u, z], axis=0)
    k = 1
    while k < window:
        p = p + pltpu.roll(p, n - k, 0)
        k *= 2
    win = pltpu.roll(p, window // 2, 0)[pad:pad + seq]
    t = lax.broadcasted_iota(jnp.int32, (seq, 1), 0)
    lo = jnp.maximum(t - window // 2, 0)
    hi = jnp.minimum(t - window // 2 + window, seq)
    cnt = (hi - lo).astype(F32)
    pooled = win / cnt - u
    return _dot(pooled.astype(BF16), pw_bf) * scale_row


def _unstack_heads(o, rows_per_head):
    return jnp.concatenate([o[g * rows_per_head:(g + 1) * rows_per_head] for g in range(SW_GROUP)], axis=1).astype(BF16)


def _sink_column(sink_ref, kv_head, rows_per_head):
    r = lax.broadcasted_iota(jnp.int32, (SW_GROUP * rows_per_head, 1), 0)
    col = jnp.full((SW_GROUP * rows_per_head, 1), sink_ref[kv_head * SW_GROUP], F32)
    for g in range(1, SW_GROUP):
        col = jnp.where(r >= g * rows_per_head, sink_ref[kv_head * SW_GROUP + g], col)
    return col


def _ctx_mix_kernel(sink_ref, qa_ref, ka_ref, va_ref, qb_ref, kb_ref, vb_ref, u0_ref, u1_ref, u2_ref, u3_ref,
                    pw_ref, ps_ref, *rest, layer):
    o_ref = rest[-5]
    seq = qa_ref.shape[0]
    for cache_ref, src_ref in zip(rest[-4:], (ka_ref, va_ref, kb_ref, vb_ref)):
        if len(cache_ref.shape) == 2:
            cache_ref[...] = src_ref[...]
        else:
            for d in range(cache_ref.shape[0]):
                cache_ref[d] = src_ref[...] if d == layer else jnp.zeros(src_ref.shape, F32)
    for h in range(NA_HEADS):
        sl = slice(h * NA_HEAD_DIM, (h + 1) * NA_HEAD_DIM)
        o = _joint_attention(qa_ref[:, sl].astype(BF16),
                             [(ka_ref[:, sl].astype(BF16), va_ref[:, sl].astype(BF16), None, None)],
                             NA_HEAD_DIM ** -0.5)
        o_ref[:, sl] = o.astype(BF16)
    for hk in range(SW_KV_HEADS):
        ksl = slice(hk * SW_HEAD_DIM, (hk + 1) * SW_HEAD_DIM)
        q = jnp.concatenate(
            [qb_ref[:, (hk * SW_GROUP + g) * SW_HEAD_DIM:(hk * SW_GROUP + g + 1) * SW_HEAD_DIM] for g in range(SW_GROUP)],
            axis=0).astype(BF16)
        o = _joint_attention(q, [(kb_ref[:, ksl].astype(BF16), vb_ref[:, ksl].astype(BF16), None, None)],
                             SW_HEAD_DIM ** -0.5, sink=_sink_column(sink_ref, hk, seq))
        c0 = A_DIM + hk * SW_GROUP * SW_HEAD_DIM
        o_ref[:, c0:c0 + SW_GROUP * SW_HEAD_DIM] = _unstack_heads(o, seq)
    for gi, u_ref in enumerate((u0_ref, u1_ref, u2_ref, u3_ref)):
        c0 = A_DIM + B_Q_DIM + gi * POOL_GROUP_DIM
        o_ref[:, c0:c0 + POOL_GROUP_DIM] = _pool_group(
            u_ref[...], POOL_WINDOWS[gi], pw_ref[gi].astype(BF16),
            ps_ref[:, gi * POOL_GROUP_DIM:(gi + 1) * POOL_GROUP_DIM]).astype(BF16)


def _ctx_mix(p, sink_l, pool_w, pool_scale, layer, seq, caches):
    n = p.shape[0]
    batch = n // seq
    wide = lambda c: pl.BlockSpec((seq, A_DIM), lambda b, c=c: (b, c))
    narrow = lambda c: pl.BlockSpec((seq, B_KV_DIM), lambda b, c=c: (b, c))
    in_specs = [pl.BlockSpec(memory_space=pltpu.SMEM)]
    in_specs += [wide(COL_QA + i) for i in range(4)]
    in_specs += [narrow(COL_KB), narrow(COL_KB + 1)]
    in_specs += [narrow(COL_U + g) for g in range(POOL_GROUPS)]
    in_specs += [pl.BlockSpec((None, POOL_GROUPS, POOL_GROUP_DIM, POOL_GROUP_DIM), lambda b: (layer, 0, 0, 0)),
                 pl.BlockSpec((None, 1, POOL_DIM), lambda b: (layer, 0, 0))]
    n_fixed = len(in_specs)
    in_specs += [pl.BlockSpec(memory_space=pl.ANY)] * len(caches)
    if caches:
        cache_spec = lambda width: pl.BlockSpec((None, None, seq, width), lambda b: (b, layer, 0, 0))
    else:
        cache_spec = lambda width: pl.BlockSpec((None, DEPTH, seq, width), lambda b: (b, 0, 0, 0))
    cache_shape = lambda width: jax.ShapeDtypeStruct((batch, DEPTH, seq, width), F32)
    outs = pl.pallas_call(
        functools.partial(_ctx_mix_kernel, layer=layer),
        grid=(batch,),
        in_specs=in_specs,
        out_specs=[pl.BlockSpec((seq, 3 * A_DIM), lambda b: (b, 0)),
                   cache_spec(A_DIM), cache_spec(A_DIM), cache_spec(B_KV_DIM), cache_spec(B_KV_DIM)],
        out_shape=[jax.ShapeDtypeStruct((n, 3 * A_DIM), BF16),
                   cache_shape(A_DIM), cache_shape(A_DIM), cache_shape(B_KV_DIM), cache_shape(B_KV_DIM)],
        input_output_aliases={n_fixed + k: 1 + k for k in range(len(caches))},
        compiler_params=_params(("parallel",), 24 * MIB),
        name="ctx_mix",
    )(sink_l, p, p, p, p, p, p, p, p, p, p, pool_w, pool_scale, *caches)
    return outs[0], tuple(outs[1:])


NA_PAIR_ROWS = 2 * NA_MAX_KH - 2


def _rpb_table_kernel(rpb_ref, t_ref):
    lane = lax.broadcasted_iota(jnp.int32, (GRID_W, 2 * GRID_W), 1)
    qc = lax.broadcasted_iota(jnp.int32, (GRID_W, 2 * GRID_W), 0)
    kc = lane & (GRID_W - 1)
    upper = lane >= GRID_W
    dcm = jnp.clip(kc - qc + NA_KW - 1, 0, 2 * NA_KW - 2)
    col0 = jnp.clip(qc - NA_KW // 2, 0, GRID_W - NA_KW)
    inside = (kc >= col0) & (kc < col0 + NA_KW)
    n_dc = 2 * NA_KW - 1
    n_dr = 2 * NA_MAX_KH - 1

    def body(i, carry):
        h = i // NA_PAIR_ROWS
        dr = i - h * NA_PAIR_ROWS
        base = (h * n_dr + dr) * n_dc
        acc = jnp.zeros((GRID_W, 2 * GRID_W), F32)
        for dc in range(n_dc):
            val = jnp.where(upper, rpb_ref[base + n_dc + dc], rpb_ref[base + dc])
            acc = jnp.where(dcm == dc, val, acc)
        t_ref[i] = jnp.where(inside, acc, NEG_INF)
        return carry

    lax.fori_loop(0, NA_HEADS * NA_PAIR_ROWS, body, 0)


def _rpb_table(rpb_l):
    return pl.pallas_call(
        _rpb_table_kernel,
        in_specs=[pl.BlockSpec(memory_space=pltpu.SMEM)],
        out_specs=pl.BlockSpec(memory_space=pltpu.VMEM),
        out_shape=jax.ShapeDtypeStruct((NA_HEADS * NA_PAIR_ROWS, GRID_W, 2 * GRID_W), F32),
        name="rpb_table",
    )(rpb_l.reshape(-1))


def _na_kernel(q_ref, k_ref, v_ref, kc_ref, vc_ref, t_ref, o_ref, *, rows):
    qr = pl.program_id(1)
    row0 = jnp.clip(qr - NA_MAX_KH // 2, 0, rows - NA_MAX_KH)
    start = pl.multiple_of(row0 * GRID_W, GRID_W)
    nkeys = NA_MAX_KH * GRID_W
    d0 = row0 - qr + NA_MAX_KH - 1
    for h in range(NA_HEADS):
        sl = slice(h * NA_HEAD_DIM, (h + 1) * NA_HEAD_DIM)
        bias = jnp.concatenate([t_ref[h * NA_PAIR_ROWS + d0 + 2 * i] for i in range(NA_MAX_KH // 2)], axis=1)
        o = _joint_attention(
            q_ref[:, sl].astype(BF16),
            [(k_ref[pl.ds(start, nkeys), sl].astype(BF16), v_ref[pl.ds(start, nkeys), sl].astype(BF16), bias, None),
             (kc_ref[:, h, :].astype(BF16), vc_ref[:, h, :].astype(BF16), None, None)],
            NA_HEAD_DIM ** -0.5)
        o_ref[:, sl] = o.astype(BF16)


def _na_latent(p, cache_k, cache_v, table, layer, seq):
    n = p.shape[0]
    rows = seq // GRID_W
    past = cache_k.shape[2]
    ctx_spec = pl.BlockSpec((None, None, past, NA_HEADS, NA_HEAD_DIM), lambda b, r: (b, layer, 0, 0, 0))
    return pl.pallas_call(
        functools.partial(_na_kernel, rows=rows),
        grid=(n // seq, rows),
        in_specs=[pl.BlockSpec((GRID_W, A_DIM), lambda b, r: (b * rows + r, COL_QA)),
                  pl.BlockSpec((seq, A_DIM), lambda b, r: (b, COL_QA + 1)),
                  pl.BlockSpec((seq, A_DIM), lambda b, r: (b, COL_QA + 2)),
                  ctx_spec, ctx_spec,
                  pl.BlockSpec(table.shape, lambda b, r: (0, 0, 0))],
        out_specs=pl.BlockSpec((GRID_W, A_DIM), lambda b, r: (b * rows + r, 0)),
        out_shape=jax.ShapeDtypeStruct((n, A_DIM), BF16),
        compiler_params=_params(("parallel", "arbitrary"), 16 * MIB),
        name="na_latent",
    )(p, p, p, cache_k, cache_v, table)


def _rope_tables(seq):
    nfreq = SW_HEAD_DIM // 4
    inv = 1.0 / (ROPE_THETA ** (np.arange(nfreq, dtype=np.float32) / np.float32(nfreq)))
    t = np.arange(seq)
    pos = (t // GRID_W, t % GRID_W)
    cos = np.zeros((seq, SW_HEAD_DIM), np.float32)
    sin_next = np.zeros((seq, SW_HEAD_DIM), np.float32)
    sin_prev = np.zeros((seq, SW_HEAD_DIM), np.float32)
    for a in range(2):
        ang = pos[a].astype(np.float32)[:, None] * inv[None, :].astype(np.float32)
        c, s = np.cos(ang).astype(np.float32), np.sin(ang).astype(np.float32)
        lo = 2 * a * nfreq
        cos[:, lo:lo + nfreq] = c
        cos[:, lo + nfreq:lo + 2 * nfreq] = c
        sin_next[:, lo:lo + nfreq] = -s
        sin_prev[:, lo + nfreq:lo + 2 * nfreq] = s
    tile = lambda x: jnp.asarray(np.tile(x, (1, 128 // SW_HEAD_DIM)))
    return tile(cos), tile(sin_next), tile(sin_prev)


def _rope(x, cos, sin_next, sin_prev):
    nfreq = SW_HEAD_DIM // 4
    return x * cos + pltpu.roll(x, 128 - nfreq, 1) * sin_next + pltpu.roll(x, nfreq, 1) * sin_prev


def _sw_kernel(sink_ref, q_ref, k_ref, v_ref, kc_ref, vc_ref, cq_ref, snq_ref, spq_ref, ck_ref, snk_ref, spk_ref,
               o_ref, kr_scr, *, seq):
    n = pl.program_id(1)

    @pl.when(n == 0)
    def _():
        kr_scr[...] = _rope(k_ref[...], ck_ref[...], snk_ref[...], spk_ref[...]).astype(BF16)

    nwin = 3 * SW_BLOCK
    kstart = pl.multiple_of(jnp.clip((n - 1) * SW_BLOCK, 0, seq - nwin), SW_BLOCK)
    cq, snq, spq = cq_ref[...], snq_ref[...], spq_ref[...]
    q = jnp.concatenate([_rope(q_ref[:, c * 128:(c + 1) * 128], cq, snq, spq) for c in range(B_Q_DIM // 128)],
                        axis=1).astype(BF16)
    rows = SW_GROUP * SW_BLOCK
    qpos = n * SW_BLOCK + (lax.broadcasted_iota(jnp.int32, (rows, nwin), 0) & (SW_BLOCK - 1))
    kpos = kstart + lax.broadcasted_iota(jnp.int32, (rows, nwin), 1)
    band = jnp.abs(qpos - kpos) <= SW_WINDOW
    for hk in range(SW_KV_HEADS):
        ksl = slice(hk * SW_HEAD_DIM, (hk + 1) * SW_HEAD_DIM)
        qs = jnp.concatenate(
            [q[:, (hk * SW_GROUP + g) * SW_HEAD_DIM:(hk * SW_GROUP + g + 1) * SW_HEAD_DIM] for g in range(SW_GROUP)], axis=0)
        o = _joint_attention(
            qs,
            [(kr_scr[pl.ds(kstart, nwin), ksl], v_ref[pl.ds(kstart, nwin), ksl].astype(BF16), None, band),
             (kc_ref[:, ksl].astype(BF16), vc_ref[:, ksl].astype(BF16), None, None)],
            SW_HEAD_DIM ** -0.5, sink=_sink_column(sink_ref, hk, SW_BLOCK))
        c0 = hk * SW_GROUP * SW_HEAD_DIM
        o_ref[:, c0:c0 + SW_GROUP * SW_HEAD_DIM] = _unstack_heads(o, SW_BLOCK)


def _sw_latent(p, cache_k, cache_v, sink_l, layer, seq):
    n = p.shape[0]
    nb = seq // SW_BLOCK
    past = cache_k.shape[2]
    cos, sin_next, sin_prev = _rope_tables(seq)
    ctx_spec = pl.BlockSpec((None, None, past, B_KV_DIM), lambda b, i: (b, layer, 0, 0))
    tab_q = pl.BlockSpec((SW_BLOCK, 128), lambda b, i: (i, 0))
    tab_k = pl.BlockSpec((seq, 128), lambda b, i: (0, 0))
    return pl.pallas_call(
        functools.partial(_sw_kernel, seq=seq),
        grid=(n // seq, nb),
        in_specs=[pl.BlockSpec(memory_space=pltpu.SMEM),
                  pl.BlockSpec((SW_BLOCK, B_Q_DIM), lambda b, i: (b * nb + i, COL_QA + 3)),
                  pl.BlockSpec((seq, B_KV_DIM), lambda b, i: (b, COL_KB)),
                  pl.BlockSpec((seq, B_KV_DIM), lambda b, i: (b, COL_KB + 1)),
                  ctx_spec, ctx_spec, tab_q, tab_q, tab_q, tab_k, tab_k, tab_k],
        out_specs=pl.BlockSpec((SW_BLOCK, B_Q_DIM), lambda b, i: (b * nb + i, 0)),
        out_shape=jax.ShapeDtypeStruct((n, B_Q_DIM), BF16),
        scratch_shapes=[pltpu.VMEM((seq, B_KV_DIM), BF16)],
        compiler_params=_params(("parallel", "arbitrary"), 16 * MIB),
        name="sw_latent",
    )(sink_l, p, p, p, cache_k, cache_v, cos, sin_next, sin_prev, cos, sin_next, sin_prev)


def _pool_kernel(u0_ref, u1_ref, u2_ref, u3_ref, pw_ref, ps_ref, o_ref):
    for gi, u_ref in enumerate((u0_ref, u1_ref, u2_ref, u3_ref)):
        sl = slice(gi * POOL_GROUP_DIM, (gi + 1) * POOL_GROUP_DIM)
        o_ref[:, sl] = _pool_group(u_ref[...], POOL_WINDOWS[gi], pw_ref[gi].astype(BF16), ps_ref[:, sl]).astype(BF16)


def _pool_latent(p, pool_w, pool_scale, layer, seq):
    n = p.shape[0]
    return pl.pallas_call(
        _pool_kernel,
        grid=(n // seq,),
        in_specs=[pl.BlockSpec((seq, POOL_GROUP_DIM), lambda b, g=g: (b, COL_U + g)) for g in range(POOL_GROUPS)]
        + [pl.BlockSpec((None, POOL_GROUPS, POOL_GROUP_DIM, POOL_GROUP_DIM), lambda b: (layer, 0, 0, 0)),
           pl.BlockSpec((None, 1, POOL_DIM), lambda b: (layer, 0, 0))],
        out_specs=pl.BlockSpec((seq, POOL_DIM), lambda b: (b, 0)),
        out_shape=jax.ShapeDtypeStruct((n, POOL_DIM), BF16),
        compiler_params=_params(("parallel",), 16 * MIB),
        name="pool_latent",
    )(p, p, p, p, pool_w, pool_scale)


def _split_bf16(x):
    hi = x.astype(BF16)
    return hi, (x - hi.astype(F32)).astype(BF16)


def _branch_merge_kernel(oa_ref, ob_ref, oc_ref, ga_ref, gb_ref, gc_ref, wa_ref, wb_ref, wc_ref, m_ref):
    s = pl.program_id(1)
    for t in range(BMERGE_NA):
        @pl.when(s == t)
        def _(t=t):
            cols = slice(t * BMERGE_CA, (t + 1) * BMERGE_CA)
            m = (ga_ref[...].astype(F32) * _dot(oa_ref[...], wa_ref[:, cols])
                 + gb_ref[...].astype(F32) * _dot(ob_ref[...], wb_ref[:, cols])
                 + gc_ref[...].astype(F32) * _dot(oc_ref[...], wc_ref[:, cols]))
            m_ref[...] = m.astype(BF16)


def _branch_merge(branches, gates, wa, wb, wc, layer):
    n = gates.shape[0]
    tm, ca = BMERGE_TM, BMERGE_CA
    gate = lambda k: pl.BlockSpec((pl.Element(tm), pl.Element(ca)),
                                  lambda i, s, k=k: (i * tm, pl.multiple_of(GATES_COL0 + k * D_MODEL + s * ca, LANES)))
    branch = lambda col: pl.BlockSpec((tm, A_DIM), lambda i, s: (i, col))
    resident = lambda rows: pl.BlockSpec((None, rows, D_MODEL), lambda i, s: (layer, 0, 0),
                                         pipeline_mode=pl.Buffered(1))
    oa, ob, oc = branches
    est = (2 * 3 * tm * A_DIM * 2 + 2 * 3 * tm * ca * 2 + 2 * tm * ca * 2 + 3 * A_DIM * D_MODEL * 2 + 8 * tm * ca * 4)
    return pl.pallas_call(
        _branch_merge_kernel,
        grid=(n // tm, BMERGE_NA),
        in_specs=[branch(oa[1]), branch(ob[1]), branch(oc[1]), gate(0), gate(1), gate(2),
                  resident(A_DIM), resident(B_Q_DIM), resident(POOL_DIM)],
        out_specs=pl.BlockSpec((tm, ca), lambda i, s: (i, s)),
        out_shape=jax.ShapeDtypeStruct((n, D_MODEL), BF16),
        compiler_params=_params(("parallel", "arbitrary"), est),
        name="branch_merge",
    )(oa[0], ob[0], oc[0], gates, gates, gates, wa, wb, wc)


def _outproj_kernel(m_ref, x_ref, mod_ref, nw_ref, wo_ref, wr_ref, x1_ref, h2_ref, lg_ref, y_scr):
    tiles = [slice(c * OUTPROJ_CB, (c + 1) * OUTPROJ_CB) for c in range(D_MODEL // OUTPROJ_CB)]
    rows = x_ref.shape[0]
    ss = jnp.zeros((rows, 1), F32)
    for cols in tiles:
        y = _dot(m_ref[...], wo_ref[:, cols])
        y_scr[:, cols] = y
        ss = ss + (y * y).sum(axis=-1, keepdims=True)
    r1 = lax.rsqrt(ss / D_MODEL + RMS_EPS)
    gain1 = mod_ref[2:3, :] * nw_ref[1:2, :]
    gain2 = nw_ref[2:3, :] * (1.0 + mod_ref[4:5, :])
    ss = jnp.zeros((rows, 1), F32)
    for cols in tiles:
        x1 = x_ref[:, cols] + (y_scr[:, cols] * r1) * gain1[:, cols]
        x1_ref[:, cols] = x1
        ss = ss + (x1 * x1).sum(axis=-1, keepdims=True)
    r2 = lax.rsqrt(ss / D_MODEL + RMS_EPS)
    lg = jnp.zeros(lg_ref.shape, F32)
    for cols in tiles:
        h2 = (x1_ref[:, cols] * r2) * gain2[:, cols] + mod_ref[3:4, cols]
        h2_ref[:, cols] = h2.astype(BF16)
        h_hi, h_lo = _split_bf16(h2)
        w_hi, w_lo = _split_bf16(wr_ref[:, cols])
        lg = lg + (_dot_nt(w_hi, h_hi) + (_dot_nt(w_hi, h_lo) + _dot_nt(w_lo, h_hi)))
    lg_ref[...] = lg


def _outproj(m, x, mod, norm_w, wo, wr_t, layer, group_of_tile):
    n = x.shape[0]
    tm = OUTPROJ_TM
    row = lambda: pl.BlockSpec((tm, D_MODEL), lambda i: (i, 0))
    est = (2 * tm * D_MODEL * (2 + 4 + 4 + 2) + D_MODEL * D_MODEL * 2 + tm * D_MODEL * 4 + 4 * tm * OUTPROJ_CB * 4)
    return pl.pallas_call(
        _outproj_kernel,
        grid=(n // tm,),
        in_specs=[row(), row(),
                  pl.BlockSpec((None, None, 6, D_MODEL), lambda i: (layer, group_of_tile(i), 0, 0)),
                  pl.BlockSpec((None, 4, D_MODEL), lambda i: (layer, 0, 0)),
                  pl.BlockSpec((None, D_MODEL, D_MODEL), lambda i: (layer, 0, 0), pipeline_mode=pl.Buffered(1)),
                  pl.BlockSpec((None, N_EXPERTS, D_MODEL), lambda i: (layer, 0, 0))],
        out_specs=[row(), row(), pl.BlockSpec((N_EXPERTS, tm), lambda i: (0, i))],
        out_shape=[jax.ShapeDtypeStruct((n, D_MODEL), F32), jax.ShapeDtypeStruct((n, D_MODEL), BF16),
                   jax.ShapeDtypeStruct((N_EXPERTS, n), F32)],
        scratch_shapes=[pltpu.VMEM((tm, D_MODEL), F32)],
        compiler_params=_params(("parallel",), est),
        name="outproj",
    )(m, x, mod, norm_w, wo, wr_t)


RANK_TILE = LANES


def _dispatch_kernel(lg_ref, h_ref, xs_ref, gate_ref, rc_ref, aff_scr, sel_scr, *, seq, cap):
    t = RANK_TILE
    nt = seq // t
    lg = lg_ref[...]
    e = jnp.exp(lg - lg.max(axis=0, keepdims=True))
    aff_scr[...] = e / e.sum(axis=0, keepdims=True)
    aff = aff_scr[...]
    ident = jnp.where(lax.broadcasted_iota(jnp.int32, (seq, seq), 0) == lax.broadcasted_iota(jnp.int32, (seq, seq), 1),
                      1.0, 0.0).astype(BF16)
    a1 = aff.astype(BF16)
    r1 = aff - a1.astype(F32)
    a2 = r1.astype(BF16)
    a3 = (r1 - a2.astype(F32)).astype(BF16)
    aff_col = (_dot_nt(ident, a1) + _dot_nt(ident, a2)) + _dot_nt(ident, a3)

    earlier = lax.broadcasted_iota(jnp.int32, (t, t), 0) < lax.broadcasted_iota(jnp.int32, (t, t), 1)
    slot = lax.broadcasted_iota(jnp.int32, (cap, seq), 0).astype(F32)
    ranks = []
    for ex in range(N_EXPERTS):
        row = aff[ex:ex + 1, :]
        cols = [jnp.broadcast_to(aff_col[r * t:(r + 1) * t, ex:ex + 1], (t, t)) for r in range(nt)]
        counts = []
        for c in range(nt):
            rowb = jnp.broadcast_to(row[:, c * t:(c + 1) * t], (t, t))
            acc = jnp.zeros((t, t), F32)
            for r in range(nt):
                if r < c:
                    beats = cols[r] >= rowb
                elif r > c:
                    beats = cols[r] > rowb
                else:
                    beats = (cols[r] > rowb) | (earlier & (cols[r] == rowb))
                acc = acc + jnp.where(beats, 1.0, 0.0)
            counts.append(acc.sum(axis=0, keepdims=True))
        rank_row = jnp.concatenate(counts, axis=1) if nt > 1 else counts[0]
        ranks.append(rank_row)
        sel = slot == rank_row
        sel_scr[ex * cap:(ex + 1) * cap, :] = sel.astype(BF16)
        gate = jnp.where(sel, jnp.broadcast_to(row, (cap, seq)), 0.0).sum(axis=1, keepdims=True)
        gate_ref[ex] = jnp.broadcast_to(gate, (cap, LANES))
    rank = jnp.concatenate(ranks + [jnp.zeros((LANES - N_EXPERTS, seq), F32)], axis=0)
    rc_ref[...] = _dot_nt(ident, jnp.minimum(rank, float(cap)).astype(BF16))
    xs = _dot(sel_scr[...], h_ref[...]).astype(BF16)
    xs_ref[...] = xs.reshape(N_EXPERTS, cap, D_MODEL)


def _dispatch(lg_t, h2, seq):
    n = h2.shape[0]
    nb = n // seq
    cap = EC_FACTOR * seq // N_EXPERTS
    est = (2 * seq * D_MODEL * 2 + 2 * N_EXPERTS * cap * D_MODEL * 2 + N_EXPERTS * cap * seq * 2
           + N_EXPERTS * cap * D_MODEL * 4 + seq * seq * (2 + 4) + 4 * seq * LANES * 4)
    return pl.pallas_call(
        functools.partial(_dispatch_kernel, seq=seq, cap=cap),
        grid=(nb,),
        in_specs=[pl.BlockSpec((N_EXPERTS, seq), lambda b: (0, b)),
                  pl.BlockSpec((seq, D_MODEL), lambda b: (b, 0))],
        out_specs=[pl.BlockSpec((N_EXPERTS, cap, D_MODEL), lambda b: (0, b, 0)),
                   pl.BlockSpec((N_EXPERTS, cap, LANES), lambda b: (0, b, 0)),
                   pl.BlockSpec((seq, LANES), lambda b: (b, 0))],
        out_shape=[jax.ShapeDtypeStruct((N_EXPERTS, nb * cap, D_MODEL), BF16),
                   jax.ShapeDtypeStruct((N_EXPERTS, nb * cap, LANES), F32),
                   jax.ShapeDtypeStruct((n, LANES), F32)],
        scratch_shapes=[pltpu.VMEM((N_EXPERTS, seq), F32), pltpu.VMEM((N_EXPERTS * cap, seq), BF16)],
        compiler_params=_params(("parallel",), est),
        name="dispatch",
    )(lg_t, h2)


EXPERT_NF = D_EXPERT // EXPERT_TF
EXPERT_ND = D_MODEL // EXPERT_TD


assert EXPERT_NF == EXPERT_ND


def _expert_kernel(xc_ref, xl_ref, gc_ref, gl_ref, wg_ref, wu_ref, wd_ref, yc_ref, yl_ref, hc_scr, hl_scr):
    e = pl.program_id(0)
    t = pl.program_id(1)
    cur = e % 2

    @pl.when(e >= 1)
    def _():
        wd = wd_ref[0].astype(BF16)
        for h_scr, g_ref, y_ref in ((hc_scr, gc_ref, yc_ref), (hl_scr, gl_ref, yl_ref)):
            acc = _dot(h_scr[1 - cur, 0], wd[0:EXPERT_TF])
            for f in range(1, EXPERT_NF):
                acc = acc + _dot(h_scr[1 - cur, f], wd[f * EXPERT_TF:(f + 1) * EXPERT_TF])
            y_ref[0] = (acc * g_ref[0, :, 0:1]).astype(BF16)

    @pl.when(e < N_EXPERTS)
    def _():
        wg = wg_ref[0].astype(BF16)
        wu = wu_ref[0].astype(BF16)
        for x_ref, h_scr in ((xc_ref, hc_scr), (xl_ref, hl_scr)):
            x = x_ref[0]
            a = _dot(x, wg)
            h_scr[cur, t] = ((a * jax.nn.sigmoid(a)) * _dot(x, wu)).astype(BF16)


def _experts(xs_c, xs_l, gate_c, gate_l, w_gate, w_up, w_down, layer):
    sc, sl = xs_c.shape[1], xs_l.shape[1]
    tf, td = EXPERT_TF, EXPERT_TD
    last = N_EXPERTS - 1
    up_expert = lambda e: jnp.minimum(e, last)
    down_expert = lambda e: jnp.maximum(e - 1, 0)
    up_tile = lambda e, t: (layer, up_expert(e), 0, jnp.where(e > last, EXPERT_NF - 1, t))
    down_tile = lambda e, t: (down_expert(e), 0, jnp.where(e == 0, 0, t))
    x_spec = lambda s: pl.BlockSpec((1, s, D_MODEL), lambda e, t: (up_expert(e), 0, 0))
    g_spec = lambda s: pl.BlockSpec((1, s, LANES), lambda e, t: (down_expert(e), 0, 0))
    est = (2 * (sc + sl) * D_MODEL * 2 + 2 * 2 * D_MODEL * tf * 4 + 2 * D_EXPERT * td * 4 + 2 * (sc + sl) * td * 4
           + 2 * (sc + sl) * D_EXPERT * 2 + 2 * D_MODEL * tf * 2 + D_EXPERT * td * 2 + 6 * sc * max(tf, td) * 4)
    return pl.pallas_call(
        _expert_kernel,
        grid=(N_EXPERTS + 1, EXPERT_NF),
        in_specs=[x_spec(sc), x_spec(sl), g_spec(sc), g_spec(sl),
                  pl.BlockSpec((None, 1, D_MODEL, tf), up_tile),
                  pl.BlockSpec((None, 1, D_MODEL, tf), up_tile),
                  pl.BlockSpec((None, 1, D_EXPERT, td), lambda e, t: (layer,) + down_tile(e, t))],
        out_specs=[pl.BlockSpec((1, sc, td), down_tile), pl.BlockSpec((1, sl, td), down_tile)],
        out_shape=[jax.ShapeDtypeStruct((N_EXPERTS, sc, D_MODEL), BF16),
                   jax.ShapeDtypeStruct((N_EXPERTS, sl, D_MODEL), BF16)],
        scratch_shapes=[pltpu.VMEM((2, EXPERT_NF, sc, tf), BF16), pltpu.VMEM((2, EXPERT_NF, sl, tf), BF16)],
        compiler_params=_params(("arbitrary", "arbitrary"), est),
        name="experts",
    )(xs_c, xs_l, gate_c, gate_l, w_gate, w_up, w_down)


def _combine_kernel(rc_ref, ye_ref, x1_ref, mod_ref, nw_ref, o_ref, *, cap):
    nslots = N_EXPERTS * cap
    shift = cap.bit_length() - 1
    expert_of_slot = lax.broadcasted_iota(jnp.int32, (LANES, nslots), 1) >> shift
    spread = jnp.where(lax.broadcasted_iota(jnp.int32, (LANES, nslots), 0) == expert_of_slot, 1.0, 0.0).astype(BF16)
    rank_of_slot = _dot(rc_ref[...].astype(BF16), spread)
    slot = (lax.broadcasted_iota(jnp.int32, (1, nslots), 1) & (cap - 1)).astype(F32)
    onehot = jnp.where(rank_of_slot == slot, 1.0, 0.0).astype(BF16)
    ffn = _dot(onehot, ye_ref[...].reshape(nslots, D_MODEL))
    o_ref[...] = x1_ref[...] + mod_ref[5:6, :] * _rms(ffn, nw_ref[3:4, :])


def _combine(rank_col, ye, x1, mod, norm_w, layer, seq, group_of_batch):
    n = x1.shape[0]
    cap = EC_FACTOR * seq // N_EXPERTS
    assert cap & (cap - 1) == 0 and cap <= LANES
    tr = COMBINE_TR
    per = seq // tr
    est = (2 * N_EXPERTS * cap * D_MODEL * 2 + 4 * tr * D_MODEL * 4 + (tr + LANES) * N_EXPERTS * cap * 6
           + 3 * tr * D_MODEL * 4)
    return pl.pallas_call(
        functools.partial(_combine_kernel, cap=cap),
        grid=(n // seq, per),
        in_specs=[pl.BlockSpec((tr, LANES), lambda b, i: (b * per + i, 0)),
                  pl.BlockSpec((N_EXPERTS, cap, D_MODEL), lambda b, i: (0, b, 0)),
                  pl.BlockSpec((tr, D_MODEL), lambda b, i: (b * per + i, 0)),
                  pl.BlockSpec((None, None, 6, D_MODEL), lambda b, i: (layer, group_of_batch(b), 0, 0)),
                  pl.BlockSpec((None, 4, D_MODEL), lambda b, i: (layer, 0, 0))],
        out_specs=pl.BlockSpec((tr, D_MODEL), lambda b, i: (b * per + i, 0)),
        out_shape=jax.ShapeDtypeStruct((n, D_MODEL), F32),
        compiler_params=_params(("parallel", "arbitrary"), est),
        name="combine",
    )(rank_col, ye, x1, mod, norm_w)


def kernel(x_prompt, x_sample, c, cache_a_k, cache_a_v, cache_b_k, cache_b_v, c_ctx, norm_w, w_ada, b_ada, w_in, a_rpb,
           b_sink, c_pool_w, c_scale, w_branch_a, w_branch_b, w_branch_c, w_out, w_router, w_gate_e, w_up_e, w_down_e):
    batch, seq_c, _ = x_prompt.shape
    dec_batch, seq_l, _ = x_sample.shape
    past = cache_a_k.shape[2]

    cond = jnp.zeros((ADA_ROWS, D_MODEL), F32).at[0].set(c_ctx).at[1:1 + dec_batch].set(c)
    mod = _adaln(cond, w_ada, b_ada).reshape(DEPTH, ADA_ROWS, 6, D_MODEL)

    ctx_group = lambda i: 0
    lat_group_inproj = lambda i: 1 + i // (seq_l // INPROJ_TM)
    lat_group_outproj = lambda i: 1 + i // (seq_l // OUTPROJ_TM)
    lat_group_batch = lambda b: 1 + b

    cbk = cache_b_k.reshape(dec_batch, DEPTH, past, B_KV_DIM)
    cbv = cache_b_v.reshape(dec_batch, DEPTH, past, B_KV_DIM)

    x_c = x_prompt.reshape(batch * seq_c, D_MODEL)
    x_l = x_sample.reshape(dec_batch * seq_l, D_MODEL)
    w_in_bf = w_in.astype(BF16)
    wa, wb, wc, wo = (w.astype(BF16) for w in (w_branch_a, w_branch_b, w_branch_c, w_out))
    wr_t = jnp.swapaxes(w_router, 1, 2)
    pool_scale = c_scale.reshape(DEPTH, 1, POOL_DIM)
    caches = ()
    for l in range(DEPTH):
        p_c, g_c = _inproj(x_c, mod, norm_w, w_in_bf, l, ctx_group)
        o_c, caches = _ctx_mix(p_c, b_sink[l], c_pool_w, pool_scale, l, seq_c, caches)
        branches_c = tuple((o_c, k) for k in range(N_BRANCH))
        m_c = _branch_merge(branches_c, g_c, wa, wb, wc, l)
        x1_c, h2_c, lg_c = _outproj(m_c, x_c, mod, norm_w, wo, wr_t, l, ctx_group)
        xs_c, gate_c, rc_c = _dispatch(lg_c, h2_c, seq_c)

        p_l, g_l = _inproj(x_l, mod, norm_w, w_in_bf, l, lat_group_inproj)
        table = _rpb_table(a_rpb[l])
        o_a = _na_latent(p_l, cache_a_k, cache_a_v, table, l, seq_l)
        o_b = _sw_latent(p_l, cbk, cbv, b_sink[l], l, seq_l)
        o_p = _pool_latent(p_l, c_pool_w, pool_scale, l, seq_l)
        m_l = _branch_merge(((o_a, 0), (o_b, 0), (o_p, 0)), g_l, wa, wb, wc, l)
        x1_l, h2_l, lg_l = _outproj(m_l, x_l, mod, norm_w, wo, wr_t, l, lat_group_outproj)
        xs_l, gate_l, rc_l = _dispatch(lg_l, h2_l, seq_l)

        ye_c, ye_l = _experts(xs_c, xs_l, gate_c, gate_l, w_gate_e, w_up_e, w_down_e, l)
        x_c = _combine(rc_c, ye_c, x1_c, mod, norm_w, l, seq_c, ctx_group)
        x_l = _combine(rc_l, ye_l, x1_l, mod, norm_w, l, seq_l, lat_group_batch)

    y_prompt = x_c.reshape(batch, seq_c, D_MODEL)
    y_sample = x_l.reshape(dec_batch, seq_l, D_MODEL)
    new_a_k, new_a_v, new_b_k, new_b_v = caches
    a_shape = (batch, DEPTH, seq_c, NA_HEADS, NA_HEAD_DIM)
    b_shape = (batch, DEPTH, seq_c, SW_KV_HEADS, SW_HEAD_DIM)
    return (y_prompt, y_sample, new_a_k.reshape(a_shape), new_a_v.reshape(a_shape),
            new_b_k.reshape(b_shape), new_b_v.reshape(b_shape))
```

```python
import functools

import numpy as np
import jax
import jax.numpy as jnp
from jax import lax
from jax.experimental import pallas as pl
from jax.experimental.pallas import tpu as pltpu

F32 = jnp.float32
BF16 = jnp.bfloat16

D_MODEL = 2048
DEPTH = 2
GRID_W = 64
NA_HEADS, NA_HEAD_DIM, NA_MAX_KH, NA_KW = 4, 128, 8, 16
SW_Q_HEADS, SW_KV_HEADS, SW_HEAD_DIM = 8, 2, 64
SW_GROUP = SW_Q_HEADS // SW_KV_HEADS
SW_WINDOW, SW_BLOCK = 128, 128
ROPE_THETA = 10000.0
POOL_WINDOWS = (2, 4, 8, 16)
POOL_GROUPS, POOL_GROUP_DIM = 4, 128
POOL_DIM = POOL_GROUPS * POOL_GROUP_DIM
A_DIM = NA_HEADS * NA_HEAD_DIM
B_Q_DIM = SW_Q_HEADS * SW_HEAD_DIM
B_KV_DIM = SW_KV_HEADS * SW_HEAD_DIM
N_BRANCH = 3
GATE_DIM = N_BRANCH * D_MODEL
QKVU_DIM = 3 * A_DIM + B_Q_DIM + 2 * B_KV_DIM + POOL_DIM
IN_DIM = QKVU_DIM + GATE_DIM
N_EXPERTS = 16
EC_FACTOR = 2
D_EXPERT = 1024
RMS_EPS = 1e-6
NEG_INF = -1e30

COL_QA = 0
COL_KB = (3 * A_DIM + B_Q_DIM) // B_KV_DIM
COL_U = COL_KB + 2

V7X_VMEM_BYTES = 64 * 1024 * 1024
V7X_VMEM_CEILING = 60000 * 1024
MIB = 1024 * 1024

ADA_ROWS = 16
ADA_TN = 1024
V7X_MXU_DIM = 256
LANES = 128
INPROJ_TM = 1024
INPROJ_TN = 1280
assert INPROJ_TN % V7X_MXU_DIM == 0 and IN_DIM % INPROJ_TN == 0
INPROJ_TILES = IN_DIM // INPROJ_TN
INPROJ_SPLIT = QKVU_DIM // INPROJ_TN
P_WIDTH = (INPROJ_SPLIT + 1) * INPROJ_TN
GATES_WIDTH = (INPROJ_TILES - INPROJ_SPLIT) * INPROJ_TN
GATES_COL0 = QKVU_DIM - INPROJ_SPLIT * INPROJ_TN
BMERGE_TM = 1024
BMERGE_CA = 1024
BMERGE_NA = D_MODEL // BMERGE_CA
OUTPROJ_TM = 512
OUTPROJ_CB = 512
EXPERT_TF = 256
EXPERT_TD = 512
COMBINE_TR = 256
ROW_CHUNK = 64


def _vmem_limit(estimate_bytes):
    return int(min(V7X_VMEM_CEILING, max(32 * MIB, estimate_bytes + 8 * MIB)))


def _params(semantics, estimate_bytes):
    return pltpu.CompilerParams(dimension_semantics=semantics,
                                vmem_limit_bytes=_vmem_limit(estimate_bytes))


def _rms(x, g):
    ms = jnp.mean(x * x, axis=-1, keepdims=True)
    return x * lax.rsqrt(ms + RMS_EPS) * g


def _sigmoid(x):
    return 0.5 * jnp.tanh(0.5 * x) + 0.5


def _dot(a, b):
    return jnp.dot(a, b, preferred_element_type=F32)


def _dot_nt(a, b):
    return lax.dot_general(a, b, (((1,), (1,)), ((), ())), preferred_element_type=F32)


def _adaln_kernel(c_ref, w_ref, b_ref, o_ref):
    c = c_ref[...]
    s = (c * jax.nn.sigmoid(c)).astype(BF16)
    o_ref[0] = _dot(s, w_ref[0].astype(BF16)) + b_ref[0]


def _adaln(cond, w_ada, b_ada):
    n_out = w_ada.shape[-1]
    return pl.pallas_call(
        _adaln_kernel,
        grid=(DEPTH, n_out // ADA_TN),
        in_specs=[pl.BlockSpec((ADA_ROWS, D_MODEL), lambda l, j: (0, 0)),
                  pl.BlockSpec((1, D_MODEL, ADA_TN), lambda l, j: (l, 0, j)),
                  pl.BlockSpec((1, 1, ADA_TN), lambda l, j: (l, 0, j))],
        out_specs=pl.BlockSpec((1, ADA_ROWS, ADA_TN), lambda l, j: (l, 0, j)),
        out_shape=jax.ShapeDtypeStruct((DEPTH, ADA_ROWS, n_out), F32),
        compiler_params=_params(("parallel", "parallel"), 2 * D_MODEL * ADA_TN * 4),
        name="adaln",
    )(cond, w_ada, b_ada.reshape(DEPTH, 1, n_out))


def _inproj_kernel(x_ref, mod_ref, nw_ref, w_ref, o_ref, gate_ref, h_scr):
    j = pl.program_id(1)

    @pl.when(j == 0)
    def _():
        g = nw_ref[0:1, :]
        sc = 1.0 + mod_ref[1:2, :]
        sh = mod_ref[0:1, :]

        def body(r, carry):
            rows = pl.ds(pl.multiple_of(r * ROW_CHUNK, ROW_CHUNK), ROW_CHUNK)
            h_scr[rows, :] = (_rms(x_ref[rows, :], g) * sc + sh).astype(BF16)
            return carry

        lax.fori_loop(0, INPROJ_TM // ROW_CHUNK, body, 0, unroll=4)

    @pl.when(j < INPROJ_SPLIT)
    def _():
        o_ref[...] = _dot(h_scr[...], w_ref[...])

    @pl.when(j == INPROJ_SPLIT)
    def _():
        acc = _dot(h_scr[...], w_ref[...])
        o_ref[...] = acc
        gate_ref[...] = _sigmoid(acc).astype(BF16)

    @pl.when(j > INPROJ_SPLIT)
    def _():
        gate_ref[...] = _sigmoid(_dot(h_scr[...], w_ref[...])).astype(BF16)


def _mod_spec(layer, group_of):
    return pl.BlockSpec((None, None, 6, D_MODEL), lambda i, j: (layer, group_of(i), 0, 0))


def _inproj(x, mod, norm_w, w_in_bf, layer, group_of_tile):
    n = x.shape[0]
    est = (2 * INPROJ_TM * D_MODEL * 4 + 2 * D_MODEL * INPROJ_TN * 2 + 2 * INPROJ_TM * INPROJ_TN * (4 + 2)
           + INPROJ_TM * D_MODEL * 2 + 2 * INPROJ_TM * INPROJ_TN * 4)
    return pl.pallas_call(
        _inproj_kernel,
        grid=(n // INPROJ_TM, INPROJ_TILES),
        in_specs=[pl.BlockSpec((INPROJ_TM, D_MODEL), lambda i, j: (i, 0)),
                  _mod_spec(layer, group_of_tile),
                  pl.BlockSpec((None, 4, D_MODEL), lambda i, j: (layer, 0, 0)),
                  pl.BlockSpec((None, D_MODEL, INPROJ_TN), lambda i, j: (layer, 0, j))],
        out_specs=[pl.BlockSpec((INPROJ_TM, INPROJ_TN), lambda i, j: (i, jnp.minimum(j, INPROJ_SPLIT))),
                   pl.BlockSpec((INPROJ_TM, INPROJ_TN), lambda i, j: (i, jnp.maximum(j - INPROJ_SPLIT, 0)))],
        out_shape=[jax.ShapeDtypeStruct((n, P_WIDTH), F32), jax.ShapeDtypeStruct((n, GATES_WIDTH), BF16)],
        scratch_shapes=[pltpu.VMEM((INPROJ_TM, D_MODEL), BF16)],
        compiler_params=_params(("parallel", "arbitrary"), est),
        name="inproj",
    )(x, mod, norm_w, w_in_bf)


def _joint_attention(q, segments, scale, sink=None):
    scores = []
    for k, _, bias, mask in segments:
        s = _dot_nt(q, k) * scale
        if bias is not None:
            s = s + bias
        if mask is not None:
            s = jnp.where(mask, s, NEG_INF)
        scores.append(s)
    m = scores[0].max(axis=-1, keepdims=True)
    for s in scores[1:]:
        m = jnp.maximum(m, s.max(axis=-1, keepdims=True))
    if sink is not None:
        m = jnp.maximum(m, sink)
    denom = jnp.exp(sink - m) if sink is not None else 0.0
    acc = None
    for s, (_, v, _, _) in zip(scores, segments):
        e = jnp.exp(s - m)
        denom = denom + e.sum(axis=-1, keepdims=True)
        pv = _dot(e.astype(BF16), v)
        acc = pv if acc is None else acc + pv
    return acc / denom


def _pool_group(u, window, pw_bf, scale_row):
    seq = u.shape[0]
    pad = 8
    n = seq + 2 * pad
    z = jnp.zeros((pad, POOL_GROUP_DIM), F32)
    p = jnp.concatenate([z, u, z], axis=0)
    k = 1
    while k < window:
        p = p + pltpu.roll(p, n - k, 0)
        k *= 2
    win = pltpu.roll(p, window // 2, 0)[pad:pad + seq]
    t = lax.broadcasted_iota(jnp.int32, (seq, 1), 0)
    lo = jnp.maximum(t - window // 2, 0)
    hi = jnp.minimum(t - window // 2 + window, seq)
    cnt = (hi - lo).astype(F32)
    pooled = win / cnt - u
    return _dot(pooled.astype(BF16), pw_bf) * scale_row


def _unstack_heads(o, rows_per_head):
    return jnp.concatenate([o[g * rows_per_head:(g + 1) * rows_per_head] for g in range(SW_GROUP)], axis=1).astype(BF16)


def _sink_column(sink_ref, kv_head, rows_per_head):
    r = lax.broadcasted_iota(jnp.int32, (SW_GROUP * rows_per_head, 1), 0)
    col = jnp.full((SW_GROUP * rows_per_head, 1), sink_ref[kv_head * SW_GROUP], F32)
    for g in range(1, SW_GROUP):
        col = jnp.where(r >= g * rows_per_head, sink_ref[kv_head * SW_GROUP + g], col)
    return col


def _ctx_mix_kernel(sink_ref, qa_ref, ka_ref, va_ref, qb_ref, kb_ref, vb_ref, u0_ref, u1_ref, u2_ref, u3_ref,
                    pw_ref, ps_ref, *rest, layer):
    o_ref = rest[-5]
    seq = qa_ref.shape[0]
    for cache_ref, src_ref in zip(rest[-4:], (ka_ref, va_ref, kb_ref, vb_ref)):
        if len(cache_ref.shape) == 2:
            cache_ref[...] = src_ref[...]
        else:
            for d in range(cache_ref.shape[0]):
                cache_ref[d] = src_ref[...] if d == layer else jnp.zeros(src_ref.shape, F32)
    for h in range(NA_HEADS):
        sl = slice(h * NA_HEAD_DIM, (h + 1) * NA_HEAD_DIM)
        o = _joint_attention(qa_ref[:, sl].astype(BF16),
                             [(ka_ref[:, sl].astype(BF16), va_ref[:, sl].astype(BF16), None, None)],
                             NA_HEAD_DIM ** -0.5)
        o_ref[:, sl] = o.astype(BF16)
    for hk in range(SW_KV_HEADS):
        ksl = slice(hk * SW_HEAD_DIM, (hk + 1) * SW_HEAD_DIM)
        q = jnp.concatenate(
            [qb_ref[:, (hk * SW_GROUP + g) * SW_HEAD_DIM:(hk * SW_GROUP + g + 1) * SW_HEAD_DIM] for g in range(SW_GROUP)],
            axis=0).astype(BF16)
        o = _joint_attention(q, [(kb_ref[:, ksl].astype(BF16), vb_ref[:, ksl].astype(BF16), None, None)],
                             SW_HEAD_DIM ** -0.5, sink=_sink_column(sink_ref, hk, seq))
        c0 = A_DIM + hk * SW_GROUP * SW_HEAD_DIM
        o_ref[:, c0:c0 + SW_GROUP * SW_HEAD_DIM] = _unstack_heads(o, seq)
    for gi, u_ref in enumerate((u0_ref, u1_ref, u2_ref, u3_ref)):
        c0 = A_DIM + B_Q_DIM + gi * POOL_GROUP_DIM
        o_ref[:, c0:c0 + POOL_GROUP_DIM] = _pool_group(
            u_ref[...], POOL_WINDOWS[gi], pw_ref[gi].astype(BF16),
            ps_ref[:, gi * POOL_GROUP_DIM:(gi + 1) * POOL_GROUP_DIM]).astype(BF16)


def _ctx_mix(p, sink_l, pool_w, pool_scale, layer, seq, caches):
    n = p.shape[0]
    batch = n // seq
    wide = lambda c: pl.BlockSpec((seq, A_DIM), lambda b, c=c: (b, c))
    narrow = lambda c: pl.BlockSpec((seq, B_KV_DIM), lambda b, c=c: (b, c))
    in_specs = [pl.BlockSpec(memory_space=pltpu.SMEM)]
    in_specs += [wide(COL_QA + i) for i in range(4)]
    in_specs += [narrow(COL_KB), narrow(COL_KB + 1)]
    in_specs += [narrow(COL_U + g) for g in range(POOL_GROUPS)]
    in_specs += [pl.BlockSpec((None, POOL_GROUPS, POOL_GROUP_DIM, POOL_GROUP_DIM), lambda b: (layer, 0, 0, 0)),
                 pl.BlockSpec((None, 1, POOL_DIM), lambda b: (layer, 0, 0))]
    n_fixed = len(in_specs)
    in_specs += [pl.BlockSpec(memory_space=pl.ANY)] * len(caches)
    if caches:
        cache_spec = lambda width: pl.BlockSpec((None, None, seq, width), lambda b: (b, layer, 0, 0))
    else:
        cache_spec = lambda width: pl.BlockSpec((None, DEPTH, seq, width), lambda b: (b, 0, 0, 0))
    cache_shape = lambda width: jax.ShapeDtypeStruct((batch, DEPTH, seq, width), F32)
    outs = pl.pallas_call(
        functools.partial(_ctx_mix_kernel, layer=layer),
        grid=(batch,),
        in_specs=in_specs,
        out_specs=[pl.BlockSpec((seq, 3 * A_DIM), lambda b: (b, 0)),
                   cache_spec(A_DIM), cache_spec(A_DIM), cache_spec(B_KV_DIM), cache_spec(B_KV_DIM)],
        out_shape=[jax.ShapeDtypeStruct((n, 3 * A_DIM), BF16),
                   cache_shape(A_DIM), cache_shape(A_DIM), cache_shape(B_KV_DIM), cache_shape(B_KV_DIM)],
        input_output_aliases={n_fixed + k: 1 + k for k in range(len(caches))},
        compiler_params=_params(("parallel",), 24 * MIB),
        name="ctx_mix",
    )(sink_l, p, p, p, p, p, p, p, p, p, p, pool_w, pool_scale, *caches)
    return outs[0], tuple(outs[1:])


NA_PAIR_ROWS = 2 * NA_MAX_KH - 2


def _rpb_table_kernel(rpb_ref, t_ref):
    lane = lax.broadcasted_iota(jnp.int32, (GRID_W, 2 * GRID_W), 1)
    qc = lax.broadcasted_iota(jnp.int32, (GRID_W, 2 * GRID_W), 0)
    kc = lane & (GRID_W - 1)
    upper = lane >= GRID_W
    dcm = jnp.clip(kc - qc + NA_KW - 1, 0, 2 * NA_KW - 2)
    col0 = jnp.clip(qc - NA_KW // 2, 0, GRID_W - NA_KW)
    inside = (kc >= col0) & (kc < col0 + NA_KW)
    n_dc = 2 * NA_KW - 1
    n_dr = 2 * NA_MAX_KH - 1

    def body(i, carry):
        h = i // NA_PAIR_ROWS
        dr = i - h * NA_PAIR_ROWS
        base = (h * n_dr + dr) * n_dc
        acc = jnp.zeros((GRID_W, 2 * GRID_W), F32)
        for dc in range(n_dc):
            val = jnp.where(upper, rpb_ref[base + n_dc + dc], rpb_ref[base + dc])
            acc = jnp.where(dcm == dc, val, acc)
        t_ref[i] = jnp.where(inside, acc, NEG_INF)
        return carry

    lax.fori_loop(0, NA_HEADS * NA_PAIR_ROWS, body, 0)


def _rpb_table(rpb_l):
    return pl.pallas_call(
        _rpb_table_kernel,
        in_specs=[pl.BlockSpec(memory_space=pltpu.SMEM)],
        out_specs=pl.BlockSpec(memory_space=pltpu.VMEM),
        out_shape=jax.ShapeDtypeStruct((NA_HEADS * NA_PAIR_ROWS, GRID_W, 2 * GRID_W), F32),
        name="rpb_table",
    )(rpb_l.reshape(-1))


def _na_kernel(q_ref, k_ref, v_ref, kc_ref, vc_ref, t_ref, o_ref, kb_scr, vb_scr, kcb_scr, vcb_scr, *, rows):
    qr = pl.program_id(1)

    @pl.when(qr == 0)
    def _():
        kb_scr[...] = k_ref[...].astype(BF16)
        vb_scr[...] = v_ref[...].astype(BF16)
        for h in range(NA_HEADS):
            sl = slice(h * NA_HEAD_DIM, (h + 1) * NA_HEAD_DIM)
            kcb_scr[:, sl] = kc_ref[:, h, :].astype(BF16)
            vcb_scr[:, sl] = vc_ref[:, h, :].astype(BF16)

    row0 = jnp.clip(qr - NA_MAX_KH // 2, 0, rows - NA_MAX_KH)
    start = pl.multiple_of(row0 * GRID_W, GRID_W)
    nkeys = NA_MAX_KH * GRID_W
    d0 = row0 - qr + NA_MAX_KH - 1
    for h in range(NA_HEADS):
        sl = slice(h * NA_HEAD_DIM, (h + 1) * NA_HEAD_DIM)
        bias = jnp.concatenate([t_ref[h * NA_PAIR_ROWS + d0 + 2 * i] for i in range(NA_MAX_KH // 2)], axis=1)
        o = _joint_attention(
            q_ref[:, sl].astype(BF16),
            [(kb_scr[pl.ds(start, nkeys), sl], vb_scr[pl.ds(start, nkeys), sl], bias, None),
             (kcb_scr[:, sl], vcb_scr[:, sl], None, None)],
            NA_HEAD_DIM ** -0.5)
        o_ref[:, sl] = o.astype(BF16)


def _na_latent(p, cache_k, cache_v, table, layer, seq):
    n = p.shape[0]
    rows = seq // GRID_W
    past = cache_k.shape[2]
    ctx_spec = pl.BlockSpec((None, None, past, NA_HEADS, NA_HEAD_DIM), lambda b, r: (b, layer, 0, 0, 0))
    return pl.pallas_call(
        functools.partial(_na_kernel, rows=rows),
        grid=(n // seq, rows),
        in_specs=[pl.BlockSpec((GRID_W, A_DIM), lambda b, r: (b * rows + r, COL_QA)),
                  pl.BlockSpec((seq, A_DIM), lambda b, r: (b, COL_QA + 1)),
                  pl.BlockSpec((seq, A_DIM), lambda b, r: (b, COL_QA + 2)),
                  ctx_spec, ctx_spec,
                  pl.BlockSpec(table.shape, lambda b, r: (0, 0, 0))],
        out_specs=pl.BlockSpec((GRID_W, A_DIM), lambda b, r: (b * rows + r, 0)),
        out_shape=jax.ShapeDtypeStruct((n, A_DIM), BF16),
        scratch_shapes=[pltpu.VMEM((seq, A_DIM), BF16), pltpu.VMEM((seq, A_DIM), BF16),
                        pltpu.VMEM((past, A_DIM), BF16), pltpu.VMEM((past, A_DIM), BF16)],
        compiler_params=_params(("parallel", "arbitrary"), 16 * MIB),
        name="na_latent",
    )(p, p, p, cache_k, cache_v, table)


def _rope_tables(seq):
    nfreq = SW_HEAD_DIM // 4
    inv = 1.0 / (ROPE_THETA ** (np.arange(nfreq, dtype=np.float32) / np.float32(nfreq)))
    t = np.arange(seq)
    pos = (t // GRID_W, t % GRID_W)
    cos = np.zeros((seq, SW_HEAD_DIM), np.float32)
    sin_next = np.zeros((seq, SW_HEAD_DIM), np.float32)
    sin_prev = np.zeros((seq, SW_HEAD_DIM), np.float32)
    for a in range(2):
        ang = pos[a].astype(np.float32)[:, None] * inv[None, :].astype(np.float32)
        c, s = np.cos(ang).astype(np.float32), np.sin(ang).astype(np.float32)
        lo = 2 * a * nfreq
        cos[:, lo:lo + nfreq] = c
        cos[:, lo + nfreq:lo + 2 * nfreq] = c
        sin_next[:, lo:lo + nfreq] = -s
        sin_prev[:, lo + nfreq:lo + 2 * nfreq] = s
    tile = lambda x: jnp.asarray(np.tile(x, (1, 128 // SW_HEAD_DIM)))
    return tile(cos), tile(sin_next), tile(sin_prev)


def _rope(x, cos, sin_next, sin_prev):
    nfreq = SW_HEAD_DIM // 4
    return x * cos + pltpu.roll(x, 128 - nfreq, 1) * sin_next + pltpu.roll(x, nfreq, 1) * sin_prev


def _sw_kernel(sink_ref, q_ref, k_ref, v_ref, kc_ref, vc_ref, cq_ref, snq_ref, spq_ref, ck_ref, snk_ref, spk_ref,
               o_ref, kr_scr, *, seq):
    n = pl.program_id(1)

    @pl.when(n == 0)
    def _():
        kr_scr[...] = _rope(k_ref[...], ck_ref[...], snk_ref[...], spk_ref[...]).astype(BF16)

    nwin = 3 * SW_BLOCK
    kstart = pl.multiple_of(jnp.clip((n - 1) * SW_BLOCK, 0, seq - nwin), SW_BLOCK)
    cq, snq, spq = cq_ref[...], snq_ref[...], spq_ref[...]
    q = jnp.concatenate([_rope(q_ref[:, c * 128:(c + 1) * 128], cq, snq, spq) for c in range(B_Q_DIM // 128)],
                        axis=1).astype(BF16)
    rows = SW_GROUP * SW_BLOCK
    qpos = n * SW_BLOCK + (lax.broadcasted_iota(jnp.int32, (rows, nwin), 0) & (SW_BLOCK - 1))
    kpos = kstart + lax.broadcasted_iota(jnp.int32, (rows, nwin), 1)
    band = jnp.abs(qpos - kpos) <= SW_WINDOW
    for hk in range(SW_KV_HEADS):
        ksl = slice(hk * SW_HEAD_DIM, (hk + 1) * SW_HEAD_DIM)
        qs = jnp.concatenate(
            [q[:, (hk * SW_GROUP + g) * SW_HEAD_DIM:(hk * SW_GROUP + g + 1) * SW_HEAD_DIM] for g in range(SW_GROUP)], axis=0)
        o = _joint_attention(
            qs,
            [(kr_scr[pl.ds(kstart, nwin), ksl], v_ref[pl.ds(kstart, nwin), ksl].astype(BF16), None, band),
             (kc_ref[:, ksl].astype(BF16), vc_ref[:, ksl].astype(BF16), None, None)],
            SW_HEAD_DIM ** -0.5, sink=_sink_column(sink_ref, hk, SW_BLOCK))
        c0 = hk * SW_GROUP * SW_HEAD_DIM
        o_ref[:, c0:c0 + SW_GROUP * SW_HEAD_DIM] = _unstack_heads(o, SW_BLOCK)


def _sw_latent(p, cache_k, cache_v, sink_l, layer, seq):
    n = p.shape[0]
    nb = seq // SW_BLOCK
    past = cache_k.shape[2]
    cos, sin_next, sin_prev = _rope_tables(seq)
    ctx_spec = pl.BlockSpec((None, None, past, B_KV_DIM), lambda b, i: (b, layer, 0, 0))
    tab_q = pl.BlockSpec((SW_BLOCK, 128), lambda b, i: (i, 0))
    tab_k = pl.BlockSpec((seq, 128), lambda b, i: (0, 0))
    return pl.pallas_call(
        functools.partial(_sw_kernel, seq=seq),
        grid=(n // seq, nb),
        in_specs=[pl.BlockSpec(memory_space=pltpu.SMEM),
                  pl.BlockSpec((SW_BLOCK, B_Q_DIM), lambda b, i: (b * nb + i, COL_QA + 3)),
                  pl.BlockSpec((seq, B_KV_DIM), lambda b, i: (b, COL_KB)),
                  pl.BlockSpec((seq, B_KV_DIM), lambda b, i: (b, COL_KB + 1)),
                  ctx_spec, ctx_spec, tab_q, tab_q, tab_q, tab_k, tab_k, tab_k],
        out_specs=pl.BlockSpec((SW_BLOCK, B_Q_DIM), lambda b, i: (b * nb + i, 0)),
        out_shape=jax.ShapeDtypeStruct((n, B_Q_DIM), BF16),
        scratch_shapes=[pltpu.VMEM((seq, B_KV_DIM), BF16)],
        compiler_params=_params(("parallel", "arbitrary"), 16 * MIB),
        name="sw_latent",
    )(sink_l, p, p, p, cache_k, cache_v, cos, sin_next, sin_prev, cos, sin_next, sin_prev)


def _pool_kernel(u0_ref, u1_ref, u2_ref, u3_ref, pw_ref, ps_ref, o_ref):
    for gi, u_ref in enumerate((u0_ref, u1_ref, u2_ref, u3_ref)):
        sl = slice(gi * POOL_GROUP_DIM, (gi + 1) * POOL_GROUP_DIM)
        o_ref[:, sl] = _pool_group(u_ref[...], POOL_WINDOWS[gi], pw_ref[gi].astype(BF16), ps_ref[:, sl]).astype(BF16)


def _pool_latent(p, pool_w, pool_scale, layer, seq):
    n = p.shape[0]
    return pl.pallas_call(
        _pool_kernel,
        grid=(n // seq,),
        in_specs=[pl.BlockSpec((seq, POOL_GROUP_DIM), lambda b, g=g: (b, COL_U + g)) for g in range(POOL_GROUPS)]
        + [pl.BlockSpec((None, POOL_GROUPS, POOL_GROUP_DIM, POOL_GROUP_DIM), lambda b: (layer, 0, 0, 0)),
           pl.BlockSpec((None, 1, POOL_DIM), lambda b: (layer, 0, 0))],
        out_specs=pl.BlockSpec((seq, POOL_DIM), lambda b: (b, 0)),
        out_shape=jax.ShapeDtypeStruct((n, POOL_DIM), BF16),
        compiler_params=_params(("parallel",), 16 * MIB),
        name="pool_latent",
    )(p, p, p, p, pool_w, pool_scale)


def _split_bf16(x):
    hi = x.astype(BF16)
    return hi, (x - hi.astype(F32)).astype(BF16)


def _branch_merge_kernel(oa_ref, ob_ref, oc_ref, ga_ref, gb_ref, gc_ref, wa_ref, wb_ref, wc_ref, m_ref):
    s = pl.program_id(1)
    for t in range(BMERGE_NA):
        @pl.when(s == t)
        def _(t=t):
            cols = slice(t * BMERGE_CA, (t + 1) * BMERGE_CA)
            m = (ga_ref[...].astype(F32) * _dot(oa_ref[...], wa_ref[:, cols])
                 + gb_ref[...].astype(F32) * _dot(ob_ref[...], wb_ref[:, cols])
                 + gc_ref[...].astype(F32) * _dot(oc_ref[...], wc_ref[:, cols]))
            m_ref[...] = m.astype(BF16)


def _branch_merge(branches, gates, wa, wb, wc, layer):
    n = gates.shape[0]
    tm, ca = BMERGE_TM, BMERGE_CA
    gate = lambda k: pl.BlockSpec((pl.Element(tm), pl.Element(ca)),
                                  lambda i, s, k=k: (i * tm, pl.multiple_of(GATES_COL0 + k * D_MODEL + s * ca, LANES)))
    branch = lambda col: pl.BlockSpec((tm, A_DIM), lambda i, s: (i, col))
    resident = lambda rows: pl.BlockSpec((None, rows, D_MODEL), lambda i, s: (layer, 0, 0),
                                         pipeline_mode=pl.Buffered(1))
    oa, ob, oc = branches
    est = (2 * 3 * tm * A_DIM * 2 + 2 * 3 * tm * ca * 2 + 2 * tm * ca * 2 + 3 * A_DIM * D_MODEL * 2 + 8 * tm * ca * 4)
    return pl.pallas_call(
        _branch_merge_kernel,
        grid=(n // tm, BMERGE_NA),
        in_specs=[branch(oa[1]), branch(ob[1]), branch(oc[1]), gate(0), gate(1), gate(2),
                  resident(A_DIM), resident(B_Q_DIM), resident(POOL_DIM)],
        out_specs=pl.BlockSpec((tm, ca), lambda i, s: (i, s)),
        out_shape=jax.ShapeDtypeStruct((n, D_MODEL), BF16),
        compiler_params=_params(("parallel", "arbitrary"), est),
        name="branch_merge",
    )(oa[0], ob[0], oc[0], gates, gates, gates, wa, wb, wc)


def _outproj_kernel(m_ref, x_ref, mod_ref, nw_ref, wo_ref, wr_ref, x1_ref, h2_ref, lg_ref, y_scr):
    tiles = [slice(c * OUTPROJ_CB, (c + 1) * OUTPROJ_CB) for c in range(D_MODEL // OUTPROJ_CB)]
    rows = x_ref.shape[0]
    ss = jnp.zeros((rows, 1), F32)
    for cols in tiles:
        y = _dot(m_ref[...], wo_ref[:, cols])
        y_scr[:, cols] = y
        ss = ss + (y * y).sum(axis=-1, keepdims=True)
    r1 = lax.rsqrt(ss / D_MODEL + RMS_EPS)
    gain1 = mod_ref[2:3, :] * nw_ref[1:2, :]
    gain2 = nw_ref[2:3, :] * (1.0 + mod_ref[4:5, :])
    ss = jnp.zeros((rows, 1), F32)
    for cols in tiles:
        x1 = x_ref[:, cols] + (y_scr[:, cols] * r1) * gain1[:, cols]
        x1_ref[:, cols] = x1
        ss = ss + (x1 * x1).sum(axis=-1, keepdims=True)
    r2 = lax.rsqrt(ss / D_MODEL + RMS_EPS)
    lg = jnp.zeros(lg_ref.shape, F32)
    for cols in tiles:
        h2 = (x1_ref[:, cols] * r2) * gain2[:, cols] + mod_ref[3:4, cols]
        h2_ref[:, cols] = h2.astype(BF16)
        h_hi, h_lo = _split_bf16(h2)
        w_hi, w_lo = _split_bf16(wr_ref[:, cols])
        lg = lg + (_dot_nt(w_hi, h_hi) + (_dot_nt(w_hi, h_lo) + _dot_nt(w_lo, h_hi)))
    lg_ref[...] = lg


def _outproj(m, x, mod, norm_w, wo, wr_t, layer, group_of_tile):
    n = x.shape[0]
    tm = OUTPROJ_TM
    row = lambda: pl.BlockSpec((tm, D_MODEL), lambda i: (i, 0))
    est = (2 * tm * D_MODEL * (2 + 4 + 4 + 2) + D_MODEL * D_MODEL * 2 + tm * D_MODEL * 4 + 4 * tm * OUTPROJ_CB * 4)
    return pl.pallas_call(
        _outproj_kernel,
        grid=(n // tm,),
        in_specs=[row(), row(),
                  pl.BlockSpec((None, None, 6, D_MODEL), lambda i: (layer, group_of_tile(i), 0, 0)),
                  pl.BlockSpec((None, 4, D_MODEL), lambda i: (layer, 0, 0)),
                  pl.BlockSpec((None, D_MODEL, D_MODEL), lambda i: (layer, 0, 0), pipeline_mode=pl.Buffered(1)),
                  pl.BlockSpec((None, N_EXPERTS, D_MODEL), lambda i: (layer, 0, 0))],
        out_specs=[row(), row(), pl.BlockSpec((N_EXPERTS, tm), lambda i: (0, i))],
        out_shape=[jax.ShapeDtypeStruct((n, D_MODEL), F32), jax.ShapeDtypeStruct((n, D_MODEL), BF16),
                   jax.ShapeDtypeStruct((N_EXPERTS, n), F32)],
        scratch_shapes=[pltpu.VMEM((tm, D_MODEL), F32)],
        compiler_params=_params(("parallel",), est),
        name="outproj",
    )(m, x, mod, norm_w, wo, wr_t)


RANK_TILE = LANES


def _dispatch_kernel(lg_ref, h_ref, xs_ref, gate_ref, rc_ref, aff_scr, sel_scr, *, seq, cap):
    t = RANK_TILE
    nt = seq // t
    lg = lg_ref[...]
    e = jnp.exp(lg - lg.max(axis=0, keepdims=True))
    aff_scr[...] = e / e.sum(axis=0, keepdims=True)
    aff = aff_scr[...]
    ident = jnp.where(lax.broadcasted_iota(jnp.int32, (seq, seq), 0) == lax.broadcasted_iota(jnp.int32, (seq, seq), 1),
                      1.0, 0.0).astype(BF16)
    a1 = aff.astype(BF16)
    r1 = aff - a1.astype(F32)
    a2 = r1.astype(BF16)
    a3 = (r1 - a2.astype(F32)).astype(BF16)
    aff_col = (_dot_nt(ident, a1) + _dot_nt(ident, a2)) + _dot_nt(ident, a3)

    earlier = lax.broadcasted_iota(jnp.int32, (t, t), 0) < lax.broadcasted_iota(jnp.int32, (t, t), 1)
    slot = lax.broadcasted_iota(jnp.int32, (cap, seq), 0).astype(F32)
    ranks = []
    for ex in range(N_EXPERTS):
        row = aff[ex:ex + 1, :]
        cols = [jnp.broadcast_to(aff_col[r * t:(r + 1) * t, ex:ex + 1], (t, t)) for r in range(nt)]
        counts = []
        for c in range(nt):
            rowb = jnp.broadcast_to(row[:, c * t:(c + 1) * t], (t, t))
            acc = jnp.zeros((t, t), F32)
            for r in range(nt):
                if r < c:
                    beats = cols[r] >= rowb
                elif r > c:
                    beats = cols[r] > rowb
                else:
                    beats = (cols[r] > rowb) | (earlier & (cols[r] == rowb))
                acc = acc + jnp.where(beats, 1.0, 0.0)
            counts.append(acc.sum(axis=0, keepdims=True))
        rank_row = jnp.concatenate(counts, axis=1) if nt > 1 else counts[0]
        ranks.append(rank_row)
        sel = slot == rank_row
        sel_scr[ex * cap:(ex + 1) * cap, :] = sel.astype(BF16)
        gate = jnp.where(sel, jnp.broadcast_to(row, (cap, seq)), 0.0).sum(axis=1, keepdims=True)
        gate_ref[ex] = jnp.broadcast_to(gate, (cap, LANES))
    rank = jnp.concatenate(ranks + [jnp.zeros((LANES - N_EXPERTS, seq), F32)], axis=0)
    rc_ref[...] = _dot_nt(ident, jnp.minimum(rank, float(cap)).astype(BF16))
    xs = _dot(sel_scr[...], h_ref[...]).astype(BF16)
    xs_ref[...] = xs.reshape(N_EXPERTS, cap, D_MODEL)


def _dispatch(lg_t, h2, seq):
    n = h2.shape[0]
    nb = n // seq
    cap = EC_FACTOR * seq // N_EXPERTS
    est = (2 * seq * D_MODEL * 2 + 2 * N_EXPERTS * cap * D_MODEL * 2 + N_EXPERTS * cap * seq * 2
           + N_EXPERTS * cap * D_MODEL * 4 + seq * seq * (2 + 4) + 4 * seq * LANES * 4)
    return pl.pallas_call(
        functools.partial(_dispatch_kernel, seq=seq, cap=cap),
        grid=(nb,),
        in_specs=[pl.BlockSpec((N_EXPERTS, seq), lambda b: (0, b)),
                  pl.BlockSpec((seq, D_MODEL), lambda b: (b, 0))],
        out_specs=[pl.BlockSpec((N_EXPERTS, cap, D_MODEL), lambda b: (0, b, 0)),
                   pl.BlockSpec((N_EXPERTS, cap, LANES), lambda b: (0, b, 0)),
                   pl.BlockSpec((seq, LANES), lambda b: (b, 0))],
        out_shape=[jax.ShapeDtypeStruct((N_EXPERTS, nb * cap, D_MODEL), BF16),
                   jax.ShapeDtypeStruct((N_EXPERTS, nb * cap, LANES), F32),
                   jax.ShapeDtypeStruct((n, LANES), F32)],
        scratch_shapes=[pltpu.VMEM((N_EXPERTS, seq), F32), pltpu.VMEM((N_EXPERTS * cap, seq), BF16)],
        compiler_params=_params(("parallel",), est),
        name="dispatch",
    )(lg_t, h2)


EXPERT_NF = D_EXPERT // EXPERT_TF
EXPERT_ND = D_MODEL // EXPERT_TD


assert EXPERT_NF == EXPERT_ND


def _expert_kernel(xc_ref, xl_ref, gc_ref, gl_ref, wg_ref, wu_ref, wd_ref, yc_ref, yl_ref, hc_scr, hl_scr):
    e = pl.program_id(0)
    t = pl.program_id(1)
    cur = e % 2

    @pl.when(e >= 1)
    def _():
        wd = wd_ref[0].astype(BF16)
        for h_scr, g_ref, y_ref in ((hc_scr, gc_ref, yc_ref), (hl_scr, gl_ref, yl_ref)):
            acc = _dot(h_scr[1 - cur, 0], wd[0:EXPERT_TF])
            for f in range(1, EXPERT_NF):
                acc = acc + _dot(h_scr[1 - cur, f], wd[f * EXPERT_TF:(f + 1) * EXPERT_TF])
            y_ref[0] = (acc * g_ref[0, :, 0:1]).astype(BF16)

    @pl.when(e < N_EXPERTS)
    def _():
        wg = wg_ref[0].astype(BF16)
        wu = wu_ref[0].astype(BF16)
        for x_ref, h_scr in ((xc_ref, hc_scr), (xl_ref, hl_scr)):
            x = x_ref[0]
            a = _dot(x, wg)
            h_scr[cur, t] = ((a * jax.nn.sigmoid(a)) * _dot(x, wu)).astype(BF16)


def _experts(xs_c, xs_l, gate_c, gate_l, w_gate, w_up, w_down, layer):
    sc, sl = xs_c.shape[1], xs_l.shape[1]
    tf, td = EXPERT_TF, EXPERT_TD
    last = N_EXPERTS - 1
    up_expert = lambda e: jnp.minimum(e, last)
    down_expert = lambda e: jnp.maximum(e - 1, 0)
    up_tile = lambda e, t: (layer, up_expert(e), 0, jnp.where(e > last, EXPERT_NF - 1, t))
    down_tile = lambda e, t: (down_expert(e), 0, jnp.where(e == 0, 0, t))
    x_spec = lambda s: pl.BlockSpec((1, s, D_MODEL), lambda e, t: (up_expert(e), 0, 0))
    g_spec = lambda s: pl.BlockSpec((1, s, LANES), lambda e, t: (down_expert(e), 0, 0))
    est = (2 * (sc + sl) * D_MODEL * 2 + 2 * 2 * D_MODEL * tf * 4 + 2 * D_EXPERT * td * 4 + 2 * (sc + sl) * td * 4
           + 2 * (sc + sl) * D_EXPERT * 2 + 2 * D_MODEL * tf * 2 + D_EXPERT * td * 2 + 6 * sc * max(tf, td) * 4)
    return pl.pallas_call(
        _expert_kernel,
        grid=(N_EXPERTS + 1, EXPERT_NF),
        in_specs=[x_spec(sc), x_spec(sl), g_spec(sc), g_spec(sl),
                  pl.BlockSpec((None, 1, D_MODEL, tf), up_tile),
                  pl.BlockSpec((None, 1, D_MODEL, tf), up_tile),
                  pl.BlockSpec((None, 1, D_EXPERT, td), lambda e, t: (layer,) + down_tile(e, t))],
        out_specs=[pl.BlockSpec((1, sc, td), down_tile), pl.BlockSpec((1, sl, td), down_tile)],
        out_shape=[jax.ShapeDtypeStruct((N_EXPERTS, sc, D_MODEL), BF16),
                   jax.ShapeDtypeStruct((N_EXPERTS, sl, D_MODEL), BF16)],
        scratch_shapes=[pltpu.VMEM((2, EXPERT_NF, sc, tf), BF16), pltpu.VMEM((2, EXPERT_NF, sl, tf), BF16)],
        compiler_params=_params(("arbitrary", "arbitrary"), est),
        name="experts",
    )(xs_c, xs_l, gate_c, gate_l, w_gate, w_up, w_down)


def _combine_kernel(rc_ref, ye_ref, x1_ref, mod_ref, nw_ref, o_ref, *, cap):
    nslots = N_EXPERTS * cap
    shift = cap.bit_length() - 1
    expert_of_slot = lax.broadcasted_iota(jnp.int32, (LANES, nslots), 1) >> shift
    spread = jnp.where(lax.broadcasted_iota(jnp.int32, (LANES, nslots), 0) == expert_of_slot, 1.0, 0.0).astype(BF16)
    rank_of_slot = _dot(rc_ref[...].astype(BF16), spread)
    slot = (lax.broadcasted_iota(jnp.int32, (1, nslots), 1) & (cap - 1)).astype(F32)
    onehot = jnp.where(rank_of_slot == slot, 1.0, 0.0).astype(BF16)
    ffn = _dot(onehot, ye_ref[...].reshape(nslots, D_MODEL))
    o_ref[...] = x1_ref[...] + mod_ref[5:6, :] * _rms(ffn, nw_ref[3:4, :])


def _combine(rank_col, ye, x1, mod, norm_w, layer, seq, group_of_batch):
    n = x1.shape[0]
    cap = EC_FACTOR * seq // N_EXPERTS
    assert cap & (cap - 1) == 0 and cap <= LANES
    tr = COMBINE_TR
    per = seq // tr
    est = (2 * N_EXPERTS * cap * D_MODEL * 2 + 4 * tr * D_MODEL * 4 + (tr + LANES) * N_EXPERTS * cap * 6
           + 3 * tr * D_MODEL * 4)
    return pl.pallas_call(
        functools.partial(_combine_kernel, cap=cap),
        grid=(n // seq, per),
        in_specs=[pl.BlockSpec((tr, LANES), lambda b, i: (b * per + i, 0)),
                  pl.BlockSpec((N_EXPERTS, cap, D_MODEL), lambda b, i: (0, b, 0)),
                  pl.BlockSpec((tr, D_MODEL), lambda b, i: (b * per + i, 0)),
                  pl.BlockSpec((None, None, 6, D_MODEL), lambda b, i: (layer, group_of_batch(b), 0, 0)),
                  pl.BlockSpec((None, 4, D_MODEL), lambda b, i: (layer, 0, 0))],
        out_specs=pl.BlockSpec((tr, D_MODEL), lambda b, i: (b * per + i, 0)),
        out_shape=jax.ShapeDtypeStruct((n, D_MODEL), F32),
        compiler_params=_params(("parallel", "arbitrary"), est),
        name="combine",
    )(rank_col, ye, x1, mod, norm_w)


def kernel(x_prompt, x_sample, c, cache_a_k, cache_a_v, cache_b_k, cache_b_v, c_ctx, norm_w, w_ada, b_ada, w_in, a_rpb,
           b_sink, c_pool_w, c_scale, w_branch_a, w_branch_b, w_branch_c, w_out, w_router, w_gate_e, w_up_e, w_down_e):
    batch, seq_c, _ = x_prompt.shape
    dec_batch, seq_l, _ = x_sample.shape
    past = cache_a_k.shape[2]

    cond = jnp.zeros((ADA_ROWS, D_MODEL), F32).at[0].set(c_ctx).at[1:1 + dec_batch].set(c)
    mod = _adaln(cond, w_ada, b_ada).reshape(DEPTH, ADA_ROWS, 6, D_MODEL)

    ctx_group = lambda i: 0
    lat_group_inproj = lambda i: 1 + i // (seq_l // INPROJ_TM)
    lat_group_outproj = lambda i: 1 + i // (seq_l // OUTPROJ_TM)
    lat_group_batch = lambda b: 1 + b

    cbk = cache_b_k.reshape(dec_batch, DEPTH, past, B_KV_DIM)
    cbv = cache_b_v.reshape(dec_batch, DEPTH, past, B_KV_DIM)

    x_c = x_prompt.reshape(batch * seq_c, D_MODEL)
    x_l = x_sample.reshape(dec_batch * seq_l, D_MODEL)
    w_in_bf = w_in.astype(BF16)
    wa, wb, wc, wo = (w.astype(BF16) for w in (w_branch_a, w_branch_b, w_branch_c, w_out))
    wr_t = jnp.swapaxes(w_router, 1, 2)
    pool_scale = c_scale.reshape(DEPTH, 1, POOL_DIM)
    caches = ()
    for l in range(DEPTH):
        p_c, g_c = _inproj(x_c, mod, norm_w, w_in_bf, l, ctx_group)
        o_c, caches = _ctx_mix(p_c, b_sink[l], c_pool_w, pool_scale, l, seq_c, caches)
        branches_c = tuple((o_c, k) for k in range(N_BRANCH))
        m_c = _branch_merge(branches_c, g_c, wa, wb, wc, l)
        x1_c, h2_c, lg_c = _outproj(m_c, x_c, mod, norm_w, wo, wr_t, l, ctx_group)
        xs_c, gate_c, rc_c = _dispatch(lg_c, h2_c, seq_c)

        p_l, g_l = _inproj(x_l, mod, norm_w, w_in_bf, l, lat_group_inproj)
        table = _rpb_table(a_rpb[l])
        o_a = _na_latent(p_l, cache_a_k, cache_a_v, table, l, seq_l)
        o_b = _sw_latent(p_l, cbk, cbv, b_sink[l], l, seq_l)
        o_p = _pool_latent(p_l, c_pool_w, pool_scale, l, seq_l)
        m_l = _branch_merge(((o_a, 0), (o_b, 0), (o_p, 0)), g_l, wa, wb, wc, l)
        x1_l, h2_l, lg_l = _outproj(m_l, x_l, mod, norm_w, wo, wr_t, l, lat_group_outproj)
        xs_l, gate_l, rc_l = _dispatch(lg_l, h2_l, seq_l)

        ye_c, ye_l = _experts(xs_c, xs_l, gate_c, gate_l, w_gate_e, w_up_e, w_down_e, l)
        x_c = _combine(rc_c, ye_c, x1_c, mod, norm_w, l, seq_c, ctx_group)
        x_l = _combine(rc_l, ye_l, x1_l, mod, norm_w, l, seq_l, lat_group_batch)

    y_prompt = x_c.reshape(batch, seq_c, D_MODEL)
    y_sample = x_l.reshape(dec_batch, seq_l, D_MODEL)
    new_a_k, new_a_v, new_b_k, new_b_v = caches
    a_shape = (batch, DEPTH, seq_c, NA_HEADS, NA_HEAD_DIM)
    b_shape = (batch, DEPTH, seq_c, SW_KV_HEADS, SW_HEAD_DIM)
    return (y_prompt, y_sample, new_a_k.reshape(a_shape), new_a_v.reshape(a_shape),
            new_b_k.reshape(b_shape), new_b_v.reshape(b_shape))
```

```python
import functools

import numpy as np
import jax
import jax.numpy as jnp
from jax import lax
from jax.experimental import pallas as pl
from jax.experimental.pallas import tpu as pltpu

F32 = jnp.float32
BF16 = jnp.bfloat16

D_MODEL = 2048
DEPTH = 2
GRID_W = 64
NA_HEADS, NA_HEAD_DIM, NA_MAX_KH, NA_KW = 4, 128, 8, 16
SW_Q_HEADS, SW_KV_HEADS, SW_HEAD_DIM = 8, 2, 64
SW_GROUP = SW_Q_HEADS // SW_KV_HEADS
SW_WINDOW, SW_BLOCK = 128, 128
ROPE_THETA = 10000.0
POOL_WINDOWS = (2, 4, 8, 16)
POOL_GROUPS, POOL_GROUP_DIM = 4, 128
POOL_DIM = POOL_GROUPS * POOL_GROUP_DIM
A_DIM = NA_HEADS * NA_HEAD_DIM
B_Q_DIM = SW_Q_HEADS * SW_HEAD_DIM
B_KV_DIM = SW_KV_HEADS * SW_HEAD_DIM
N_BRANCH = 3
GATE_DIM = N_BRANCH * D_MODEL
QKVU_DIM = 3 * A_DIM + B_Q_DIM + 2 * B_KV_DIM + POOL_DIM
IN_DIM = QKVU_DIM + GATE_DIM
N_EXPERTS = 16
EC_FACTOR = 2
D_EXPERT = 1024
RMS_EPS = 1e-6
NEG_INF = -1e30

COL_QA = 0
COL_KB = (3 * A_DIM + B_Q_DIM) // B_KV_DIM
COL_U = COL_KB + 2

V7X_VMEM_BYTES = 64 * 1024 * 1024
V7X_VMEM_CEILING = 60000 * 1024
MIB = 1024 * 1024

ADA_ROWS = 16
ADA_TN = 1024
V7X_MXU_DIM = 256
LANES = 128
INPROJ_TM = 1024
INPROJ_TN = 1280
assert INPROJ_TN % V7X_MXU_DIM == 0 and IN_DIM % INPROJ_TN == 0
INPROJ_TILES = IN_DIM // INPROJ_TN
INPROJ_SPLIT = QKVU_DIM // INPROJ_TN
P_WIDTH = (INPROJ_SPLIT + 1) * INPROJ_TN
GATES_WIDTH = (INPROJ_TILES - INPROJ_SPLIT) * INPROJ_TN
GATES_COL0 = QKVU_DIM - INPROJ_SPLIT * INPROJ_TN
BMERGE_TM = 1024
BMERGE_CA = 1024
BMERGE_NA = D_MODEL // BMERGE_CA
OUTPROJ_TM = 512
OUTPROJ_CB = 512
EXPERT_TF = 256
EXPERT_TD = 512
COMBINE_TR = 256
ROW_CHUNK = 64


def _vmem_limit(estimate_bytes):
    return int(min(V7X_VMEM_CEILING, max(32 * MIB, estimate_bytes + 8 * MIB)))


def _params(semantics, estimate_bytes):
    return pltpu.CompilerParams(dimension_semantics=semantics,
                                vmem_limit_bytes=_vmem_limit(estimate_bytes))


def _rms(x, g):
    ms = jnp.mean(x * x, axis=-1, keepdims=True)
    return x * lax.rsqrt(ms + RMS_EPS) * g


def _sigmoid(x):
    return 0.5 * jnp.tanh(0.5 * x) + 0.5


def _dot(a, b):
    return jnp.dot(a, b, preferred_element_type=F32)


def _dot_nt(a, b):
    return lax.dot_general(a, b, (((1,), (1,)), ((), ())), preferred_element_type=F32)


def _adaln_kernel(c_ref, w_ref, b_ref, o_ref):
    c = c_ref[...]
    s = (c * jax.nn.sigmoid(c)).astype(BF16)
    o_ref[0] = _dot(s, w_ref[0].astype(BF16)) + b_ref[0]


def _adaln(cond, w_ada, b_ada):
    n_out = w_ada.shape[-1]
    return pl.pallas_call(
        _adaln_kernel,
        grid=(DEPTH, n_out // ADA_TN),
        in_specs=[pl.BlockSpec((ADA_ROWS, D_MODEL), lambda l, j: (0, 0)),
                  pl.BlockSpec((1, D_MODEL, ADA_TN), lambda l, j: (l, 0, j)),
                  pl.BlockSpec((1, 1, ADA_TN), lambda l, j: (l, 0, j))],
        out_specs=pl.BlockSpec((1, ADA_ROWS, ADA_TN), lambda l, j: (l, 0, j)),
        out_shape=jax.ShapeDtypeStruct((DEPTH, ADA_ROWS, n_out), F32),
        compiler_params=_params(("parallel", "parallel"), 2 * D_MODEL * ADA_TN * 4),
        name="adaln",
    )(cond, w_ada, b_ada.reshape(DEPTH, 1, n_out))


def _inproj_kernel(x_ref, mod_ref, nw_ref, w_ref, o_ref, gate_ref, h_scr):
    j = pl.program_id(1)

    @pl.when(j == 0)
    def _():
        g = nw_ref[0:1, :]
        sc = 1.0 + mod_ref[1:2, :]
        sh = mod_ref[0:1, :]

        def body(r, carry):
            rows = pl.ds(pl.multiple_of(r * ROW_CHUNK, ROW_CHUNK), ROW_CHUNK)
            h_scr[rows, :] = (_rms(x_ref[rows, :], g) * sc + sh).astype(BF16)
            return carry

        lax.fori_loop(0, INPROJ_TM // ROW_CHUNK, body, 0, unroll=4)

    @pl.when(j < INPROJ_SPLIT)
    def _():
        o_ref[...] = _dot(h_scr[...], w_ref[...])

    @pl.when(j == INPROJ_SPLIT)
    def _():
        acc = _dot(h_scr[...], w_ref[...])
        o_ref[...] = acc
        gate_ref[...] = _sigmoid(acc).astype(BF16)

    @pl.when(j > INPROJ_SPLIT)
    def _():
        gate_ref[...] = _sigmoid(_dot(h_scr[...], w_ref[...])).astype(BF16)


def _mod_spec(layer, group_of):
    return pl.BlockSpec((None, None, 6, D_MODEL), lambda i, j: (layer, group_of(i), 0, 0))


def _inproj(x, mod, norm_w, w_in_bf, layer, group_of_tile):
    n = x.shape[0]
    est = (2 * INPROJ_TM * D_MODEL * 4 + 2 * D_MODEL * INPROJ_TN * 2 + 2 * INPROJ_TM * INPROJ_TN * (4 + 2)
           + INPROJ_TM * D_MODEL * 2 + 2 * INPROJ_TM * INPROJ_TN * 4)
    return pl.pallas_call(
        _inproj_kernel,
        grid=(n // INPROJ_TM, INPROJ_TILES),
        in_specs=[pl.BlockSpec((INPROJ_TM, D_MODEL), lambda i, j: (i, 0)),
                  _mod_spec(layer, group_of_tile),
                  pl.BlockSpec((None, 4, D_MODEL), lambda i, j: (layer, 0, 0)),
                  pl.BlockSpec((None, D_MODEL, INPROJ_TN), lambda i, j: (layer, 0, j))],
        out_specs=[pl.BlockSpec((INPROJ_TM, INPROJ_TN), lambda i, j: (i, jnp.minimum(j, INPROJ_SPLIT))),
                   pl.BlockSpec((INPROJ_TM, INPROJ_TN), lambda i, j: (i, jnp.maximum(j - INPROJ_SPLIT, 0)))],
        out_shape=[jax.ShapeDtypeStruct((n, P_WIDTH), F32), jax.ShapeDtypeStruct((n, GATES_WIDTH), BF16)],
        scratch_shapes=[pltpu.VMEM((INPROJ_TM, D_MODEL), BF16)],
        compiler_params=_params(("parallel", "arbitrary"), est),
        name="inproj",
    )(x, mod, norm_w, w_in_bf)


def _joint_attention(q, segments, scale, sink=None):
    scores = []
    for k, _, bias, mask in segments:
        s = _dot_nt(q, k) * scale
        if bias is not None:
            s = s + bias
        if mask is not None:
            s = jnp.where(mask, s, NEG_INF)
        scores.append(s)
    m = scores[0].max(axis=-1, keepdims=True)
    for s in scores[1:]:
        m = jnp.maximum(m, s.max(axis=-1, keepdims=True))
    if sink is not None:
        m = jnp.maximum(m, sink)
    denom = jnp.exp(sink - m) if sink is not None else 0.0
    acc = None
    for s, (_, v, _, _) in zip(scores, segments):
        e = jnp.exp(s - m)
        denom = denom + e.sum(axis=-1, keepdims=True)
        pv = _dot(e.astype(BF16), v)
        acc = pv if acc is None else acc + pv
    return acc / denom


def _pool_group(u, window, pw_bf, scale_row):
    seq = u.shape[0]
    pad = 8
    n = seq + 2 * pad
    z = jnp.zeros((pad, POOL_GROUP_DIM), F32)
    p = jnp.concatenate([z, u, z], axis=0)
    k = 1
    while k < window:
        p = p + pltpu.roll(p, n - k, 0)
        k *= 2
    win = pltpu.roll(p, window // 2, 0)[pad:pad + seq]
    t = lax.broadcasted_iota(jnp.int32, (seq, 1), 0)
    lo = jnp.maximum(t - window // 2, 0)
    hi = jnp.minimum(t - window // 2 + window, seq)
    cnt = (hi - lo).astype(F32)
    pooled = win / cnt - u
    return _dot(pooled.astype(BF16), pw_bf) * scale_row


def _unstack_heads(o, rows_per_head):
    return jnp.concatenate([o[g * rows_per_head:(g + 1) * rows_per_head] for g in range(SW_GROUP)], axis=1).astype(BF16)


def _sink_column(sink_ref, kv_head, rows_per_head):
    r = lax.broadcasted_iota(jnp.int32, (SW_GROUP * rows_per_head, 1), 0)
    col = jnp.full((SW_GROUP * rows_per_head, 1), sink_ref[kv_head * SW_GROUP], F32)
    for g in range(1, SW_GROUP):
        col = jnp.where(r >= g * rows_per_head, sink_ref[kv_head * SW_GROUP + g], col)
    return col


def _ctx_mix_kernel(sink_ref, qa_ref, ka_ref, va_ref, qb_ref, kb_ref, vb_ref, u0_ref, u1_ref, u2_ref, u3_ref,
                    pw_ref, ps_ref, *rest, layer):
    o_ref = rest[-5]
    seq = qa_ref.shape[0]
    for cache_ref, src_ref in zip(rest[-4:], (ka_ref, va_ref, kb_ref, vb_ref)):
        if len(cache_ref.shape) == 2:
            cache_ref[...] = src_ref[...]
        else:
            for d in range(cache_ref.shape[0]):
                cache_ref[d] = src_ref[...] if d == layer else jnp.zeros(src_ref.shape, F32)
    for h in range(NA_HEADS):
        sl = slice(h * NA_HEAD_DIM, (h + 1) * NA_HEAD_DIM)
        o = _joint_attention(qa_ref[:, sl].astype(BF16),
                             [(ka_ref[:, sl].astype(BF16), va_ref[:, sl].astype(BF16), None, None)],
                             NA_HEAD_DIM ** -0.5)
        o_ref[:, sl] = o.astype(BF16)
    for hk in range(SW_KV_HEADS):
        ksl = slice(hk * SW_HEAD_DIM, (hk + 1) * SW_HEAD_DIM)
        q = jnp.concatenate(
            [qb_ref[:, (hk * SW_GROUP + g) * SW_HEAD_DIM:(hk * SW_GROUP + g + 1) * SW_HEAD_DIM] for g in range(SW_GROUP)],
            axis=0).astype(BF16)
        o = _joint_attention(q, [(kb_ref[:, ksl].astype(BF16), vb_ref[:, ksl].astype(BF16), None, None)],
                             SW_HEAD_DIM ** -0.5, sink=_sink_column(sink_ref, hk, seq))
        c0 = A_DIM + hk * SW_GROUP * SW_HEAD_DIM
        o_ref[:, c0:c0 + SW_GROUP * SW_HEAD_DIM] = _unstack_heads(o, seq)
    for gi, u_ref in enumerate((u0_ref, u1_ref, u2_ref, u3_ref)):
        c0 = A_DIM + B_Q_DIM + gi * POOL_GROUP_DIM
        o_ref[:, c0:c0 + POOL_GROUP_DIM] = _pool_group(
            u_ref[...], POOL_WINDOWS[gi], pw_ref[gi].astype(BF16),
            ps_ref[:, gi * POOL_GROUP_DIM:(gi + 1) * POOL_GROUP_DIM]).astype(BF16)


def _ctx_mix(p, sink_l, pool_w, pool_scale, layer, seq, caches):
    n = p.shape[0]
    batch = n // seq
    wide = lambda c: pl.BlockSpec((seq, A_DIM), lambda b, c=c: (b, c))
    narrow = lambda c: pl.BlockSpec((seq, B_KV_DIM), lambda b, c=c: (b, c))
    in_specs = [pl.BlockSpec(memory_space=pltpu.SMEM)]
    in_specs += [wide(COL_QA + i) for i in range(4)]
    in_specs += [narrow(COL_KB), narrow(COL_KB + 1)]
    in_specs += [narrow(COL_U + g) for g in range(POOL_GROUPS)]
    in_specs += [pl.BlockSpec((None, POOL_GROUPS, POOL_GROUP_DIM, POOL_GROUP_DIM), lambda b: (layer, 0, 0, 0)),
                 pl.BlockSpec((None, 1, POOL_DIM), lambda b: (layer, 0, 0))]
    n_fixed = len(in_specs)
    in_specs += [pl.BlockSpec(memory_space=pl.ANY)] * len(caches)
    if caches:
        cache_spec = lambda width: pl.BlockSpec((None, None, seq, width), lambda b: (b, layer, 0, 0))
    else:
        cache_spec = lambda width: pl.BlockSpec((None, DEPTH, seq, width), lambda b: (b, 0, 0, 0))
    cache_shape = lambda width: jax.ShapeDtypeStruct((batch, DEPTH, seq, width), F32)
    outs = pl.pallas_call(
        functools.partial(_ctx_mix_kernel, layer=layer),
        grid=(batch,),
        in_specs=in_specs,
        out_specs=[pl.BlockSpec((seq, 3 * A_DIM), lambda b: (b, 0)),
                   cache_spec(A_DIM), cache_spec(A_DIM), cache_spec(B_KV_DIM), cache_spec(B_KV_DIM)],
        out_shape=[jax.ShapeDtypeStruct((n, 3 * A_DIM), BF16),
                   cache_shape(A_DIM), cache_shape(A_DIM), cache_shape(B_KV_DIM), cache_shape(B_KV_DIM)],
        input_output_aliases={n_fixed + k: 1 + k for k in range(len(caches))},
        compiler_params=_params(("parallel",), 24 * MIB),
        name="ctx_mix",
    )(sink_l, p, p, p, p, p, p, p, p, p, p, pool_w, pool_scale, *caches)
    return outs[0], tuple(outs[1:])


NA_PAIR_ROWS = 2 * NA_MAX_KH - 2


def _rpb_table_kernel(rpb_ref, t_ref):
    lane = lax.broadcasted_iota(jnp.int32, (GRID_W, 2 * GRID_W), 1)
    qc = lax.broadcasted_iota(jnp.int32, (GRID_W, 2 * GRID_W), 0)
    kc = lane & (GRID_W - 1)
    upper = lane >= GRID_W
    dcm = jnp.clip(kc - qc + NA_KW - 1, 0, 2 * NA_KW - 2)
    col0 = jnp.clip(qc - NA_KW // 2, 0, GRID_W - NA_KW)
    inside = (kc >= col0) & (kc < col0 + NA_KW)
    n_dc = 2 * NA_KW - 1
    n_dr = 2 * NA_MAX_KH - 1

    def body(i, carry):
        h = i // NA_PAIR_ROWS
        dr = i - h * NA_PAIR_ROWS
        base = (h * n_dr + dr) * n_dc
        acc = jnp.zeros((GRID_W, 2 * GRID_W), F32)
        for dc in range(n_dc):
            val = jnp.where(upper, rpb_ref[base + n_dc + dc], rpb_ref[base + dc])
            acc = jnp.where(dcm == dc, val, acc)
        t_ref[i] = jnp.where(inside, acc, NEG_INF)
        return carry

    lax.fori_loop(0, NA_HEADS * NA_PAIR_ROWS, body, 0)


def _rpb_table(rpb_l):
    return pl.pallas_call(
        _rpb_table_kernel,
        in_specs=[pl.BlockSpec(memory_space=pltpu.SMEM)],
        out_specs=pl.BlockSpec(memory_space=pltpu.VMEM),
        out_shape=jax.ShapeDtypeStruct((NA_HEADS * NA_PAIR_ROWS, GRID_W, 2 * GRID_W), F32),
        name="rpb_table",
    )(rpb_l.reshape(-1))


def _na_kernel(q_ref, k_ref, v_ref, kc_ref, vc_ref, t_ref, o_ref, kb_scr, vb_scr, kcb_scr, vcb_scr, *, rows):
    qr = pl.program_id(1)

    @pl.when(qr == 0)
    def _():
        kb_scr[...] = k_ref[...].astype(BF16)
        vb_scr[...] = v_ref[...].astype(BF16)
        for h in range(NA_HEADS):
            sl = slice(h * NA_HEAD_DIM, (h + 1) * NA_HEAD_DIM)
            kcb_scr[:, sl] = kc_ref[:, h, :].astype(BF16)
            vcb_scr[:, sl] = vc_ref[:, h, :].astype(BF16)

    row0 = jnp.clip(qr - NA_MAX_KH // 2, 0, rows - NA_MAX_KH)
    start = pl.multiple_of(row0 * GRID_W, GRID_W)
    nkeys = NA_MAX_KH * GRID_W
    d0 = row0 - qr + NA_MAX_KH - 1
    for h in range(NA_HEADS):
        sl = slice(h * NA_HEAD_DIM, (h + 1) * NA_HEAD_DIM)
        bias = jnp.concatenate([t_ref[h * NA_PAIR_ROWS + d0 + 2 * i] for i in range(NA_MAX_KH // 2)], axis=1)
        o = _joint_attention(
            q_ref[:, sl].astype(BF16),
            [(kb_scr[pl.ds(start, nkeys), sl], vb_scr[pl.ds(start, nkeys), sl], bias, None),
             (kcb_scr[:, sl], vcb_scr[:, sl], None, None)],
            NA_HEAD_DIM ** -0.5)
        o_ref[:, sl] = o.astype(BF16)


def _na_latent(p, cache_k, cache_v, table, layer, seq):
    n = p.shape[0]
    rows = seq // GRID_W
    past = cache_k.shape[2]
    ctx_spec = pl.BlockSpec((None, None, past, NA_HEADS, NA_HEAD_DIM), lambda b, r: (b, layer, 0, 0, 0))
    return pl.pallas_call(
        functools.partial(_na_kernel, rows=rows),
        grid=(n // seq, rows),
        in_specs=[pl.BlockSpec((GRID_W, A_DIM), lambda b, r: (b * rows + r, COL_QA)),
                  pl.BlockSpec((seq, A_DIM), lambda b, r: (b, COL_QA + 1)),
                  pl.BlockSpec((seq, A_DIM), lambda b, r: (b, COL_QA + 2)),
                  ctx_spec, ctx_spec,
                  pl.BlockSpec(table.shape, lambda b, r: (0, 0, 0))],
        out_specs=pl.BlockSpec((GRID_W, A_DIM), lambda b, r: (b * rows + r, 0)),
        out_shape=jax.ShapeDtypeStruct((n, A_DIM), BF16),
        scratch_shapes=[pltpu.VMEM((seq, A_DIM), BF16), pltpu.VMEM((seq, A_DIM), BF16),
                        pltpu.VMEM((past, A_DIM), BF16), pltpu.VMEM((past, A_DIM), BF16)],
        compiler_params=_params(("parallel", "arbitrary"), 16 * MIB),
        name="na_latent",
    )(p, p, p, cache_k, cache_v, table)


def _rope_tables(seq):
    nfreq = SW_HEAD_DIM // 4
    inv = 1.0 / (ROPE_THETA ** (np.arange(nfreq, dtype=np.float32) / np.float32(nfreq)))
    t = np.arange(seq)
    pos = (t // GRID_W, t % GRID_W)
    cos = np.zeros((seq, SW_HEAD_DIM), np.float32)
    sin_next = np.zeros((seq, SW_HEAD_DIM), np.float32)
    sin_prev = np.zeros((seq, SW_HEAD_DIM), np.float32)
    for a in range(2):
        ang = pos[a].astype(np.float32)[:, None] * inv[None, :].astype(np.float32)
        c, s = np.cos(ang).astype(np.float32), np.sin(ang).astype(np.float32)
        lo = 2 * a * nfreq
        cos[:, lo:lo + nfreq] = c
        cos[:, lo + nfreq:lo + 2 * nfreq] = c
        sin_next[:, lo:lo + nfreq] = -s
        sin_prev[:, lo + nfreq:lo + 2 * nfreq] = s
    tile = lambda x: jnp.asarray(np.tile(x, (1, 128 // SW_HEAD_DIM)))
    return tile(cos), tile(sin_next), tile(sin_prev)


def _rope(x, cos, sin_next, sin_prev):
    nfreq = SW_HEAD_DIM // 4
    return x * cos + pltpu.roll(x, 128 - nfreq, 1) * sin_next + pltpu.roll(x, nfreq, 1) * sin_prev


def _sw_kernel(sink_ref, q_ref, k_ref, v_ref, kc_ref, vc_ref, cq_ref, snq_ref, spq_ref, ck_ref, snk_ref, spk_ref,
               o_ref, kr_scr, *, seq):
    n = pl.program_id(1)

    @pl.when(n == 0)
    def _():
        kr_scr[...] = _rope(k_ref[...], ck_ref[...], snk_ref[...], spk_ref[...]).astype(BF16)

    nwin = 3 * SW_BLOCK
    kstart = pl.multiple_of(jnp.clip((n - 1) * SW_BLOCK, 0, seq - nwin), SW_BLOCK)
    cq, snq, spq = cq_ref[...], snq_ref[...], spq_ref[...]
    q = jnp.concatenate([_rope(q_ref[:, c * 128:(c + 1) * 128], cq, snq, spq) for c in range(B_Q_DIM // 128)],
                        axis=1).astype(BF16)
    rows = SW_GROUP * SW_BLOCK
    qpos = n * SW_BLOCK + (lax.broadcasted_iota(jnp.int32, (rows, nwin), 0) & (SW_BLOCK - 1))
    kpos = kstart + lax.broadcasted_iota(jnp.int32, (rows, nwin), 1)
    band = jnp.abs(qpos - kpos) <= SW_WINDOW
    for hk in range(SW_KV_HEADS):
        ksl = slice(hk * SW_HEAD_DIM, (hk + 1) * SW_HEAD_DIM)
        qs = jnp.concatenate(
            [q[:, (hk * SW_GROUP + g) * SW_HEAD_DIM:(hk * SW_GROUP + g + 1) * SW_HEAD_DIM] for g in range(SW_GROUP)], axis=0)
        o = _joint_attention(
            qs,
            [(kr_scr[pl.ds(kstart, nwin), ksl], v_ref[pl.ds(kstart, nwin), ksl].astype(BF16), None, band),
             (kc_ref[:, ksl].astype(BF16), vc_ref[:, ksl].astype(BF16), None, None)],
            SW_HEAD_DIM ** -0.5, sink=_sink_column(sink_ref, hk, SW_BLOCK))
        c0 = hk * SW_GROUP * SW_HEAD_DIM
        o_ref[:, c0:c0 + SW_GROUP * SW_HEAD_DIM] = _unstack_heads(o, SW_BLOCK)


def _sw_latent(p, cache_k, cache_v, sink_l, layer, seq):
    n = p.shape[0]
    nb = seq // SW_BLOCK
    past = cache_k.shape[2]
    cos, sin_next, sin_prev = _rope_tables(seq)
    ctx_spec = pl.BlockSpec((None, None, past, B_KV_DIM), lambda b, i: (b, layer, 0, 0))
    tab_q = pl.BlockSpec((SW_BLOCK, 128), lambda b, i: (i, 0))
    tab_k = pl.BlockSpec((seq, 128), lambda b, i: (0, 0))
    return pl.pallas_call(
        functools.partial(_sw_kernel, seq=seq),
        grid=(n // seq, nb),
        in_specs=[pl.BlockSpec(memory_space=pltpu.SMEM),
                  pl.BlockSpec((SW_BLOCK, B_Q_DIM), lambda b, i: (b * nb + i, COL_QA + 3)),
                  pl.BlockSpec((seq, B_KV_DIM), lambda b, i: (b, COL_KB)),
                  pl.BlockSpec((seq, B_KV_DIM), lambda b, i: (b, COL_KB + 1)),
                  ctx_spec, ctx_spec, tab_q, tab_q, tab_q, tab_k, tab_k, tab_k],
        out_specs=pl.BlockSpec((SW_BLOCK, B_Q_DIM), lambda b, i: (b * nb + i, 0)),
        out_shape=jax.ShapeDtypeStruct((n, B_Q_DIM), BF16),
        scratch_shapes=[pltpu.VMEM((seq, B_KV_DIM), BF16)],
        compiler_params=_params(("parallel", "arbitrary"), 16 * MIB),
        name="sw_latent",
    )(sink_l, p, p, p, cache_k, cache_v, cos, sin_next, sin_prev, cos, sin_next, sin_prev)


def _pool_kernel(u0_ref, u1_ref, u2_ref, u3_ref, pw_ref, ps_ref, o_ref):
    for gi, u_ref in enumerate((u0_ref, u1_ref, u2_ref, u3_ref)):
        sl = slice(gi * POOL_GROUP_DIM, (gi + 1) * POOL_GROUP_DIM)
        o_ref[:, sl] = _pool_group(u_ref[...], POOL_WINDOWS[gi], pw_ref[gi].astype(BF16), ps_ref[:, sl]).astype(BF16)


def _pool_latent(p, pool_w, pool_scale, layer, seq):
    n = p.shape[0]
    return pl.pallas_call(
        _pool_kernel,
        grid=(n // seq,),
        in_specs=[pl.BlockSpec((seq, POOL_GROUP_DIM), lambda b, g=g: (b, COL_U + g)) for g in range(POOL_GROUPS)]
        + [pl.BlockSpec((None, POOL_GROUPS, POOL_GROUP_DIM, POOL_GROUP_DIM), lambda b: (layer, 0, 0, 0)),
           pl.BlockSpec((None, 1, POOL_DIM), lambda b: (layer, 0, 0))],
        out_specs=pl.BlockSpec((seq, POOL_DIM), lambda b: (b, 0)),
        out_shape=jax.ShapeDtypeStruct((n, POOL_DIM), BF16),
        compiler_params=_params(("parallel",), 16 * MIB),
        name="pool_latent",
    )(p, p, p, p, pool_w, pool_scale)


def _split_bf16(x):
    hi = x.astype(BF16)
    return hi, (x - hi.astype(F32)).astype(BF16)


def _branch_merge_kernel(oa_ref, ob_ref, oc_ref, ga_ref, gb_ref, gc_ref, wa_ref, wb_ref, wc_ref, m_ref):
    s = pl.program_id(1)
    for t in range(BMERGE_NA):
        @pl.when(s == t)
        def _(t=t):
            cols = slice(t * BMERGE_CA, (t + 1) * BMERGE_CA)
            m = (ga_ref[...].astype(F32) * _dot(oa_ref[...], wa_ref[:, cols])
                 + gb_ref[...].astype(F32) * _dot(ob_ref[...], wb_ref[:, cols])
                 + gc_ref[...].astype(F32) * _dot(oc_ref[...], wc_ref[:, cols]))
            m_ref[...] = m.astype(BF16)


def _branch_merge(branches, gates, wa, wb, wc, layer):
    n = gates.shape[0]
    tm, ca = BMERGE_TM, BMERGE_CA
    gate = lambda k: pl.BlockSpec((pl.Element(tm), pl.Element(ca)),
                                  lambda i, s, k=k: (i * tm, pl.multiple_of(GATES_COL0 + k * D_MODEL + s * ca, LANES)))
    branch = lambda col: pl.BlockSpec((tm, A_DIM), lambda i, s: (i, col))
    resident = lambda rows: pl.BlockSpec((None, rows, D_MODEL), lambda i, s: (layer, 0, 0),
                                         pipeline_mode=pl.Buffered(1))
    oa, ob, oc = branches
    est = (2 * 3 * tm * A_DIM * 2 + 2 * 3 * tm * ca * 2 + 2 * tm * ca * 2 + 3 * A_DIM * D_MODEL * 2 + 8 * tm * ca * 4)
    return pl.pallas_call(
        _branch_merge_kernel,
        grid=(n // tm, BMERGE_NA),
        in_specs=[branch(oa[1]), branch(ob[1]), branch(oc[1]), gate(0), gate(1), gate(2),
                  resident(A_DIM), resident(B_Q_DIM), resident(POOL_DIM)],
        out_specs=pl.BlockSpec((tm, ca), lambda i, s: (i, s)),
        out_shape=jax.ShapeDtypeStruct((n, D_MODEL), BF16),
        compiler_params=_params(("parallel", "arbitrary"), est),
        name="branch_merge",
    )(oa[0], ob[0], oc[0], gates, gates, gates, wa, wb, wc)


def _outproj_kernel(m_ref, x_ref, mod_ref, nw_ref, wo_ref, wr_ref, x1_ref, h2_ref, lg_ref, y_scr):
    tiles = [slice(c * OUTPROJ_CB, (c + 1) * OUTPROJ_CB) for c in range(D_MODEL // OUTPROJ_CB)]
    rows = x_ref.shape[0]
    ss = jnp.zeros((rows, 1), F32)
    for cols in tiles:
        y = _dot(m_ref[...], wo_ref[:, cols])
        y_scr[:, cols] = y
        ss = ss + (y * y).sum(axis=-1, keepdims=True)
    r1 = lax.rsqrt(ss / D_MODEL + RMS_EPS)
    gain1 = mod_ref[2:3, :] * nw_ref[1:2, :]
    gain2 = nw_ref[2:3, :] * (1.0 + mod_ref[4:5, :])
    ss = jnp.zeros((rows, 1), F32)
    for cols in tiles:
        x1 = x_ref[:, cols] + (y_scr[:, cols] * r1) * gain1[:, cols]
        x1_ref[:, cols] = x1
        ss = ss + (x1 * x1).sum(axis=-1, keepdims=True)
    r2 = lax.rsqrt(ss / D_MODEL + RMS_EPS)
    lg = jnp.zeros(lg_ref.shape, F32)
    for cols in tiles:
        h2 = (x1_ref[:, cols] * r2) * gain2[:, cols] + mod_ref[3:4, cols]
        h2_ref[:, cols] = h2.astype(BF16)
        h_hi, h_lo = _split_bf16(h2)
        w_hi, w_lo = _split_bf16(wr_ref[:, cols])
        lg = lg + (_dot_nt(w_hi, h_hi) + (_dot_nt(w_hi, h_lo) + _dot_nt(w_lo, h_hi)))
    lg_ref[...] = lg


def _outproj(m, x, mod, norm_w, wo, wr_t, layer, group_of_tile):
    n = x.shape[0]
    tm = OUTPROJ_TM
    row = lambda: pl.BlockSpec((tm, D_MODEL), lambda i: (i, 0))
    est = (2 * tm * D_MODEL * (2 + 4 + 4 + 2) + D_MODEL * D_MODEL * 2 + tm * D_MODEL * 4 + 4 * tm * OUTPROJ_CB * 4)
    return pl.pallas_call(
        _outproj_kernel,
        grid=(n // tm,),
        in_specs=[row(), row(),
                  pl.BlockSpec((None, None, 6, D_MODEL), lambda i: (layer, group_of_tile(i), 0, 0)),
                  pl.BlockSpec((None, 4, D_MODEL), lambda i: (layer, 0, 0)),
                  pl.BlockSpec((None, D_MODEL, D_MODEL), lambda i: (layer, 0, 0), pipeline_mode=pl.Buffered(1)),
                  pl.BlockSpec((None, N_EXPERTS, D_MODEL), lambda i: (layer, 0, 0))],
        out_specs=[row(), row(), pl.BlockSpec((N_EXPERTS, tm), lambda i: (0, i))],
        out_shape=[jax.ShapeDtypeStruct((n, D_MODEL), F32), jax.ShapeDtypeStruct((n, D_MODEL), BF16),
                   jax.ShapeDtypeStruct((N_EXPERTS, n), F32)],
        scratch_shapes=[pltpu.VMEM((tm, D_MODEL), F32)],
        compiler_params=_params(("parallel",), est),
        name="outproj",
    )(m, x, mod, norm_w, wo, wr_t)


RANK_TILE = LANES


def _dispatch_kernel(lg_ref, h_ref, xs_ref, gate_ref, rc_ref, aff_scr, sel_scr, *, seq, cap):
    t = RANK_TILE
    nt = seq // t
    lg = lg_ref[...]
    e = jnp.exp(lg - lg.max(axis=0, keepdims=True))
    aff_scr[...] = e / e.sum(axis=0, keepdims=True)
    aff = aff_scr[...]
    ident = jnp.where(lax.broadcasted_iota(jnp.int32, (t, t), 0) == lax.broadcasted_iota(jnp.int32, (t, t), 1),
                      1.0, 0.0).astype(BF16)

    def to_sublanes(pieces, r):
        tile = slice(r * t, (r + 1) * t)
        out = _dot_nt(ident, pieces[0][:, tile])
        for piece in pieces[1:]:
            out = out + _dot_nt(ident, piece[:, tile])
        return out

    a1 = aff.astype(BF16)
    r1 = aff - a1.astype(F32)
    a2 = r1.astype(BF16)
    a3 = (r1 - a2.astype(F32)).astype(BF16)
    aff_cols = [to_sublanes((a1, a2, a3), r) for r in range(nt)]

    earlier = lax.broadcasted_iota(jnp.int32, (t, t), 0) < lax.broadcasted_iota(jnp.int32, (t, t), 1)
    slot = lax.broadcasted_iota(jnp.int32, (cap, seq), 0).astype(F32)
    ranks = []
    for ex in range(N_EXPERTS):
        row = aff[ex:ex + 1, :]
        cols = [jnp.broadcast_to(aff_cols[r][:, ex:ex + 1], (t, t)) for r in range(nt)]
        counts = []
        for c in range(nt):
            rowb = jnp.broadcast_to(row[:, c * t:(c + 1) * t], (t, t))
            acc = jnp.zeros((t, t), F32)
            for r in range(nt):
                if r < c:
                    beats = cols[r] >= rowb
                elif r > c:
                    beats = cols[r] > rowb
                else:
                    beats = (cols[r] > rowb) | (earlier & (cols[r] == rowb))
                acc = acc + jnp.where(beats, 1.0, 0.0)
            counts.append(acc.sum(axis=0, keepdims=True))
        rank_row = jnp.concatenate(counts, axis=1) if nt > 1 else counts[0]
        ranks.append(rank_row)
        sel = slot == rank_row
        sel_scr[ex * cap:(ex + 1) * cap, :] = sel.astype(BF16)
        gate = jnp.where(sel, jnp.broadcast_to(row, (cap, seq)), 0.0).sum(axis=1, keepdims=True)
        gate_ref[ex] = jnp.broadcast_to(gate, (cap, LANES))
    rank = jnp.concatenate(ranks + [jnp.zeros((LANES - N_EXPERTS, seq), F32)], axis=0)
    rank = jnp.minimum(rank, float(cap)).astype(BF16)
    for r in range(nt):
        rc_ref[r * t:(r + 1) * t, :] = to_sublanes((rank,), r)
    xs = _dot(sel_scr[...], h_ref[...]).astype(BF16)
    xs_ref[...] = xs.reshape(N_EXPERTS, cap, D_MODEL)


def _dispatch(lg_t, h2, seq):
    n = h2.shape[0]
    nb = n // seq
    cap = EC_FACTOR * seq // N_EXPERTS
    est = (2 * seq * D_MODEL * 2 + 2 * N_EXPERTS * cap * D_MODEL * 2 + N_EXPERTS * cap * seq * 2
           + N_EXPERTS * cap * D_MODEL * 4 + 8 * seq * LANES * 4)
    return pl.pallas_call(
        functools.partial(_dispatch_kernel, seq=seq, cap=cap),
        grid=(nb,),
        in_specs=[pl.BlockSpec((N_EXPERTS, seq), lambda b: (0, b)),
                  pl.BlockSpec((seq, D_MODEL), lambda b: (b, 0))],
        out_specs=[pl.BlockSpec((N_EXPERTS, cap, D_MODEL), lambda b: (0, b, 0)),
                   pl.BlockSpec((N_EXPERTS, cap, LANES), lambda b: (0, b, 0)),
                   pl.BlockSpec((seq, LANES), lambda b: (b, 0))],
        out_shape=[jax.ShapeDtypeStruct((N_EXPERTS, nb * cap, D_MODEL), BF16),
                   jax.ShapeDtypeStruct((N_EXPERTS, nb * cap, LANES), F32),
                   jax.ShapeDtypeStruct((n, LANES), F32)],
        scratch_shapes=[pltpu.VMEM((N_EXPERTS, seq), F32), pltpu.VMEM((N_EXPERTS * cap, seq), BF16)],
        compiler_params=_params(("parallel",), est),
        name="dispatch",
    )(lg_t, h2)


EXPERT_NF = D_EXPERT // EXPERT_TF
EXPERT_ND = D_MODEL // EXPERT_TD


assert EXPERT_NF == EXPERT_ND


def _expert_kernel(xc_ref, xl_ref, gc_ref, gl_ref, wg_ref, wu_ref, wd_ref, yc_ref, yl_ref, hc_scr, hl_scr):
    e = pl.program_id(0)
    t = pl.program_id(1)
    cur = e % 2

    @pl.when(e >= 1)
    def _():
        wd = wd_ref[0].astype(BF16)
        for h_scr, g_ref, y_ref in ((hc_scr, gc_ref, yc_ref), (hl_scr, gl_ref, yl_ref)):
            acc = _dot(h_scr[1 - cur, 0], wd[0:EXPERT_TF])
            for f in range(1, EXPERT_NF):
                acc = acc + _dot(h_scr[1 - cur, f], wd[f * EXPERT_TF:(f + 1) * EXPERT_TF])
            y_ref[0] = (acc * g_ref[0, :, 0:1]).astype(BF16)

    @pl.when(e < N_EXPERTS)
    def _():
        wg = wg_ref[0].astype(BF16)
        wu = wu_ref[0].astype(BF16)
        for x_ref, h_scr in ((xc_ref, hc_scr), (xl_ref, hl_scr)):
            x = x_ref[0]
            a = _dot(x, wg)
            h_scr[cur, t] = ((a * jax.nn.sigmoid(a)) * _dot(x, wu)).astype(BF16)


def _experts(xs_c, xs_l, gate_c, gate_l, w_gate, w_up, w_down, layer):
    sc, sl = xs_c.shape[1], xs_l.shape[1]
    tf, td = EXPERT_TF, EXPERT_TD
    last = N_EXPERTS - 1
    up_expert = lambda e: jnp.minimum(e, last)
    down_expert = lambda e: jnp.maximum(e - 1, 0)
    up_tile = lambda e, t: (layer, up_expert(e), 0, jnp.where(e > last, EXPERT_NF - 1, t))
    down_tile = lambda e, t: (down_expert(e), 0, jnp.where(e == 0, 0, t))
    x_spec = lambda s: pl.BlockSpec((1, s, D_MODEL), lambda e, t: (up_expert(e), 0, 0))
    g_spec = lambda s: pl.BlockSpec((1, s, LANES), lambda e, t: (down_expert(e), 0, 0))
    est = (2 * (sc + sl) * D_MODEL * 2 + 2 * 2 * D_MODEL * tf * 4 + 2 * D_EXPERT * td * 4 + 2 * (sc + sl) * td * 4
           + 2 * (sc + sl) * D_EXPERT * 2 + 2 * D_MODEL * tf * 2 + D_EXPERT * td * 2 + 6 * sc * max(tf, td) * 4)
    return pl.pallas_call(
        _expert_kernel,
        grid=(N_EXPERTS + 1, EXPERT_NF),
        in_specs=[x_spec(sc), x_spec(sl), g_spec(sc), g_spec(sl),
                  pl.BlockSpec((None, 1, D_MODEL, tf), up_tile),
                  pl.BlockSpec((None, 1, D_MODEL, tf), up_tile),
                  pl.BlockSpec((None, 1, D_EXPERT, td), lambda e, t: (layer,) + down_tile(e, t))],
        out_specs=[pl.BlockSpec((1, sc, td), down_tile), pl.BlockSpec((1, sl, td), down_tile)],
        out_shape=[jax.ShapeDtypeStruct((N_EXPERTS, sc, D_MODEL), BF16),
                   jax.ShapeDtypeStruct((N_EXPERTS, sl, D_MODEL), BF16)],
        scratch_shapes=[pltpu.VMEM((2, EXPERT_NF, sc, tf), BF16), pltpu.VMEM((2, EXPERT_NF, sl, tf), BF16)],
        compiler_params=_params(("arbitrary", "arbitrary"), est),
        name="experts",
    )(xs_c, xs_l, gate_c, gate_l, w_gate, w_up, w_down)


def _combine_kernel(rc_ref, ye_ref, x1_ref, mod_ref, nw_ref, o_ref, *, cap):
    nslots = N_EXPERTS * cap
    shift = cap.bit_length() - 1
    expert_of_slot = lax.broadcasted_iota(jnp.int32, (LANES, nslots), 1) >> shift
    spread = jnp.where(lax.broadcasted_iota(jnp.int32, (LANES, nslots), 0) == expert_of_slot, 1.0, 0.0).astype(BF16)
    rank_of_slot = _dot(rc_ref[...].astype(BF16), spread)
    slot = (lax.broadcasted_iota(jnp.int32, (1, nslots), 1) & (cap - 1)).astype(F32)
    onehot = jnp.where(rank_of_slot == slot, 1.0, 0.0).astype(BF16)
    ffn = _dot(onehot, ye_ref[...].reshape(nslots, D_MODEL))
    o_ref[...] = x1_ref[...] + mod_ref[5:6, :] * _rms(ffn, nw_ref[3:4, :])


def _combine(rank_col, ye, x1, mod, norm_w, layer, seq, group_of_batch):
    n = x1.shape[0]
    cap = EC_FACTOR * seq // N_EXPERTS
    assert cap & (cap - 1) == 0 and cap <= LANES
    tr = COMBINE_TR
    per = seq // tr
    est = (2 * N_EXPERTS * cap * D_MODEL * 2 + 4 * tr * D_MODEL * 4 + (tr + LANES) * N_EXPERTS * cap * 6
           + 3 * tr * D_MODEL * 4)
    return pl.pallas_call(
        functools.partial(_combine_kernel, cap=cap),
        grid=(n // seq, per),
        in_specs=[pl.BlockSpec((tr, LANES), lambda b, i: (b * per + i, 0)),
                  pl.BlockSpec((N_EXPERTS, cap, D_MODEL), lambda b, i: (0, b, 0)),
                  pl.BlockSpec((tr, D_MODEL), lambda b, i: (b * per + i, 0)),
                  pl.BlockSpec((None, None, 6, D_MODEL), lambda b, i: (layer, group_of_batch(b), 0, 0)),
                  pl.BlockSpec((None, 4, D_MODEL), lambda b, i: (layer, 0, 0))],
        out_specs=pl.BlockSpec((tr, D_MODEL), lambda b, i: (b * per + i, 0)),
        out_shape=jax.ShapeDtypeStruct((n, D_MODEL), F32),
        compiler_params=_params(("parallel", "arbitrary"), est),
        name="combine",
    )(rank_col, ye, x1, mod, norm_w)


def kernel(x_prompt, x_sample, c, cache_a_k, cache_a_v, cache_b_k, cache_b_v, c_ctx, norm_w, w_ada, b_ada, w_in, a_rpb,
           b_sink, c_pool_w, c_scale, w_branch_a, w_branch_b, w_branch_c, w_out, w_router, w_gate_e, w_up_e, w_down_e):
    batch, seq_c, _ = x_prompt.shape
    dec_batch, seq_l, _ = x_sample.shape
    past = cache_a_k.shape[2]

    cond = jnp.zeros((ADA_ROWS, D_MODEL), F32).at[0].set(c_ctx).at[1:1 + dec_batch].set(c)
    mod = _adaln(cond, w_ada, b_ada).reshape(DEPTH, ADA_ROWS, 6, D_MODEL)

    ctx_group = lambda i: 0
    lat_group_inproj = lambda i: 1 + i // (seq_l // INPROJ_TM)
    lat_group_outproj = lambda i: 1 + i // (seq_l // OUTPROJ_TM)
    lat_group_batch = lambda b: 1 + b

    cbk = cache_b_k.reshape(dec_batch, DEPTH, past, B_KV_DIM)
    cbv = cache_b_v.reshape(dec_batch, DEPTH, past, B_KV_DIM)

    x_c = x_prompt.reshape(batch * seq_c, D_MODEL)
    x_l = x_sample.reshape(dec_batch * seq_l, D_MODEL)
    w_in_bf = w_in.astype(BF16)
    wa, wb, wc, wo = (w.astype(BF16) for w in (w_branch_a, w_branch_b, w_branch_c, w_out))
    wr_t = jnp.swapaxes(w_router, 1, 2)
    pool_scale = c_scale.reshape(DEPTH, 1, POOL_DIM)
    caches = ()
    for l in range(DEPTH):
        p_c, g_c = _inproj(x_c, mod, norm_w, w_in_bf, l, ctx_group)
        o_c, caches = _ctx_mix(p_c, b_sink[l], c_pool_w, pool_scale, l, seq_c, caches)
        branches_c = tuple((o_c, k) for k in range(N_BRANCH))
        m_c = _branch_merge(branches_c, g_c, wa, wb, wc, l)
        x1_c, h2_c, lg_c = _outproj(m_c, x_c, mod, norm_w, wo, wr_t, l, ctx_group)
        xs_c, gate_c, rc_c = _dispatch(lg_c, h2_c, seq_c)

        p_l, g_l = _inproj(x_l, mod, norm_w, w_in_bf, l, lat_group_inproj)
        table = _rpb_table(a_rpb[l])
        o_a = _na_latent(p_l, cache_a_k, cache_a_v, table, l, seq_l)
        o_b = _sw_latent(p_l, cbk, cbv, b_sink[l], l, seq_l)
        o_p = _pool_latent(p_l, c_pool_w, pool_scale, l, seq_l)
        m_l = _branch_merge(((o_a, 0), (o_b, 0), (o_p, 0)), g_l, wa, wb, wc, l)
        x1_l, h2_l, lg_l = _outproj(m_l, x_l, mod, norm_w, wo, wr_t, l, lat_group_outproj)
        xs_l, gate_l, rc_l = _dispatch(lg_l, h2_l, seq_l)

        ye_c, ye_l = _experts(xs_c, xs_l, gate_c, gate_l, w_gate_e, w_up_e, w_down_e, l)
        x_c = _combine(rc_c, ye_c, x1_c, mod, norm_w, l, seq_c, ctx_group)
        x_l = _combine(rc_l, ye_l, x1_l, mod, norm_w, l, seq_l, lat_group_batch)

    y_prompt = x_c.reshape(batch, seq_c, D_MODEL)
    y_sample = x_l.reshape(dec_batch, seq_l, D_MODEL)
    new_a_k, new_a_v, new_b_k, new_b_v = caches
    a_shape = (batch, DEPTH, seq_c, NA_HEADS, NA_HEAD_DIM)
    b_shape = (batch, DEPTH, seq_c, SW_KV_HEADS, SW_HEAD_DIM)
    return (y_prompt, y_sample, new_a_k.reshape(a_shape), new_a_v.reshape(a_shape),
            new_b_k.reshape(b_shape), new_b_v.reshape(b_shape))
```

```python
import functools

import numpy as np
import jax
import jax.numpy as jnp
from jax import lax
from jax.experimental import pallas as pl
from jax.experimental.pallas import tpu as pltpu

F32 = jnp.float32
BF16 = jnp.bfloat16

D_MODEL = 2048
DEPTH = 2
GRID_W = 64
NA_HEADS, NA_HEAD_DIM, NA_MAX_KH, NA_KW = 4, 128, 8, 16
SW_Q_HEADS, SW_KV_HEADS, SW_HEAD_DIM = 8, 2, 64
SW_GROUP = SW_Q_HEADS // SW_KV_HEADS
SW_WINDOW, SW_BLOCK = 128, 128
ROPE_THETA = 10000.0
POOL_WINDOWS = (2, 4, 8, 16)
POOL_GROUPS, POOL_GROUP_DIM = 4, 128
POOL_DIM = POOL_GROUPS * POOL_GROUP_DIM
A_DIM = NA_HEADS * NA_HEAD_DIM
B_Q_DIM = SW_Q_HEADS * SW_HEAD_DIM
B_KV_DIM = SW_KV_HEADS * SW_HEAD_DIM
N_BRANCH = 3
GATE_DIM = N_BRANCH * D_MODEL
QKVU_DIM = 3 * A_DIM + B_Q_DIM + 2 * B_KV_DIM + POOL_DIM
IN_DIM = QKVU_DIM + GATE_DIM
N_EXPERTS = 16
EC_FACTOR = 2
D_EXPERT = 1024
RMS_EPS = 1e-6
NEG_INF = -1e30

COL_QA = 0
COL_KB = (3 * A_DIM + B_Q_DIM) // B_KV_DIM
COL_U = COL_KB + 2

MIB = 1024 * 1024
V7X_VMEM_BYTES = 64 * MIB
V7X_VMEM_CEILING = 60000 * 1024
VMEM_FLOOR = 32 * MIB
VMEM_COMPILER_SCRATCH = 8 * MIB
assert V7X_VMEM_CEILING < V7X_VMEM_BYTES

ADA_ROWS = 16
ADA_TN = 1024
V7X_MXU_DIM = 256
LANES = 128
INPROJ_TM = 1024
INPROJ_TN = 1280
assert INPROJ_TN % V7X_MXU_DIM == 0 and IN_DIM % INPROJ_TN == 0
INPROJ_TILES = IN_DIM // INPROJ_TN
INPROJ_SPLIT = QKVU_DIM // INPROJ_TN
P_WIDTH = (INPROJ_SPLIT + 1) * INPROJ_TN
GATES_WIDTH = (INPROJ_TILES - INPROJ_SPLIT) * INPROJ_TN
GATES_COL0 = QKVU_DIM - INPROJ_SPLIT * INPROJ_TN
BMERGE_TM = 1024
BMERGE_CA = 1024
BMERGE_NA = D_MODEL // BMERGE_CA
OUTPROJ_TM = 512
OUTPROJ_CB = 512
EXPERT_TF = 256
EXPERT_TD = 512
COMBINE_TR = 256
ROW_CHUNK = 64


def _vmem_limit(estimate_bytes):
    return int(min(V7X_VMEM_CEILING, max(VMEM_FLOOR, estimate_bytes + VMEM_COMPILER_SCRATCH)))


def _params(semantics, estimate_bytes):
    return pltpu.CompilerParams(dimension_semantics=semantics,
                                vmem_limit_bytes=_vmem_limit(estimate_bytes))


def _rms(x, g):
    ms = jnp.mean(x * x, axis=-1, keepdims=True)
    return x * lax.rsqrt(ms + RMS_EPS) * g


def _sigmoid(x):
    return 0.5 * jnp.tanh(0.5 * x) + 0.5


def _dot(a, b):
    return jnp.dot(a, b, preferred_element_type=F32)


def _dot_nt(a, b):
    return lax.dot_general(a, b, (((1,), (1,)), ((), ())), preferred_element_type=F32)


def _adaln_kernel(c_ref, w_ref, b_ref, o_ref):
    c = c_ref[...]
    s = (c * jax.nn.sigmoid(c)).astype(BF16)
    o_ref[0] = _dot(s, w_ref[0].astype(BF16)) + b_ref[0]


def _adaln(cond, w_ada, b_ada):
    n_out = w_ada.shape[-1]
    return pl.pallas_call(
        _adaln_kernel,
        grid=(DEPTH, n_out // ADA_TN),
        in_specs=[pl.BlockSpec((ADA_ROWS, D_MODEL), lambda l, j: (0, 0)),
                  pl.BlockSpec((1, D_MODEL, ADA_TN), lambda l, j: (l, 0, j)),
                  pl.BlockSpec((1, 1, ADA_TN), lambda l, j: (l, 0, j))],
        out_specs=pl.BlockSpec((1, ADA_ROWS, ADA_TN), lambda l, j: (l, 0, j)),
        out_shape=jax.ShapeDtypeStruct((DEPTH, ADA_ROWS, n_out), F32),
        compiler_params=_params(("parallel", "parallel"), 2 * D_MODEL * ADA_TN * 4),
        name="adaln",
    )(cond, w_ada, b_ada.reshape(DEPTH, 1, n_out))


def _inproj_kernel(x_ref, mod_ref, nw_ref, w_ref, o_ref, gate_ref, h_scr):
    j = pl.program_id(1)

    @pl.when(j == 0)
    def _():
        g = nw_ref[0:1, :]
        sc = 1.0 + mod_ref[1:2, :]
        sh = mod_ref[0:1, :]

        def body(r, carry):
            rows = pl.ds(pl.multiple_of(r * ROW_CHUNK, ROW_CHUNK), ROW_CHUNK)
            h_scr[rows, :] = (_rms(x_ref[rows, :], g) * sc + sh).astype(BF16)
            return carry

        lax.fori_loop(0, INPROJ_TM // ROW_CHUNK, body, 0, unroll=4)

    @pl.when(j < INPROJ_SPLIT)
    def _():
        o_ref[...] = _dot(h_scr[...], w_ref[...])

    @pl.when(j == INPROJ_SPLIT)
    def _():
        acc = _dot(h_scr[...], w_ref[...])
        o_ref[...] = acc
        gate_ref[...] = _sigmoid(acc).astype(BF16)

    @pl.when(j > INPROJ_SPLIT)
    def _():
        gate_ref[...] = _sigmoid(_dot(h_scr[...], w_ref[...])).astype(BF16)


def _mod_spec(layer, group_of):
    return pl.BlockSpec((None, None, 6, D_MODEL), lambda i, j: (layer, group_of(i), 0, 0))


def _inproj(x, mod, norm_w, w_in_bf, layer, group_of_tile):
    n = x.shape[0]
    est = (2 * INPROJ_TM * D_MODEL * 4 + 2 * D_MODEL * INPROJ_TN * 2 + 2 * INPROJ_TM * INPROJ_TN * (4 + 2)
           + INPROJ_TM * D_MODEL * 2 + 2 * INPROJ_TM * INPROJ_TN * 4)
    return pl.pallas_call(
        _inproj_kernel,
        grid=(n // INPROJ_TM, INPROJ_TILES),
        in_specs=[pl.BlockSpec((INPROJ_TM, D_MODEL), lambda i, j: (i, 0)),
                  _mod_spec(layer, group_of_tile),
                  pl.BlockSpec((None, 4, D_MODEL), lambda i, j: (layer, 0, 0)),
                  pl.BlockSpec((None, D_MODEL, INPROJ_TN), lambda i, j: (layer, 0, j))],
        out_specs=[pl.BlockSpec((INPROJ_TM, INPROJ_TN), lambda i, j: (i, jnp.minimum(j, INPROJ_SPLIT))),
                   pl.BlockSpec((INPROJ_TM, INPROJ_TN), lambda i, j: (i, jnp.maximum(j - INPROJ_SPLIT, 0)))],
        out_shape=[jax.ShapeDtypeStruct((n, P_WIDTH), F32), jax.ShapeDtypeStruct((n, GATES_WIDTH), BF16)],
        scratch_shapes=[pltpu.VMEM((INPROJ_TM, D_MODEL), BF16)],
        compiler_params=_params(("parallel", "arbitrary"), est),
        name="inproj",
    )(x, mod, norm_w, w_in_bf)


def _joint_attention(q, segments, scale, sink=None):
    scores = []
    for k, _, bias, mask in segments:
        s = _dot_nt(q, k) * scale
        if bias is not None:
            s = s + bias
        if mask is not None:
            s = jnp.where(mask, s, NEG_INF)
        scores.append(s)
    m = scores[0].max(axis=-1, keepdims=True)
    for s in scores[1:]:
        m = jnp.maximum(m, s.max(axis=-1, keepdims=True))
    if sink is not None:
        m = jnp.maximum(m, sink)
    denom = jnp.exp(sink - m) if sink is not None else 0.0
    acc = None
    for s, (_, v, _, _) in zip(scores, segments):
        e = jnp.exp(s - m)
        denom = denom + e.sum(axis=-1, keepdims=True)
        pv = _dot(e.astype(BF16), v)
        acc = pv if acc is None else acc + pv
    return acc / denom


def _pool_group(u, window, pw_bf, scale_row):
    seq = u.shape[0]
    pad = 8
    n = seq + 2 * pad
    z = jnp.zeros((pad, POOL_GROUP_DIM), F32)
    p = jnp.concatenate([z, u, z], axis=0)
    k = 1
    while k < window:
        p = p + pltpu.roll(p, n - k, 0)
        k *= 2
    win = pltpu.roll(p, window // 2, 0)[pad:pad + seq]
    t = lax.broadcasted_iota(jnp.int32, (seq, 1), 0)
    lo = jnp.maximum(t - window // 2, 0)
    hi = jnp.minimum(t - window // 2 + window, seq)
    cnt = (hi - lo).astype(F32)
    pooled = win / cnt - u
    return _dot(pooled.astype(BF16), pw_bf) * scale_row


def _unstack_heads(o, rows_per_head):
    return jnp.concatenate([o[g * rows_per_head:(g + 1) * rows_per_head] for g in range(SW_GROUP)], axis=1).astype(BF16)


def _sink_column(sink_ref, kv_head, rows_per_head):
    r = lax.broadcasted_iota(jnp.int32, (SW_GROUP * rows_per_head, 1), 0)
    col = jnp.full((SW_GROUP * rows_per_head, 1), sink_ref[kv_head * SW_GROUP], F32)
    for g in range(1, SW_GROUP):
        col = jnp.where(r >= g * rows_per_head, sink_ref[kv_head * SW_GROUP + g], col)
    return col


def _ctx_mix_kernel(sink_ref, qa_ref, ka_ref, va_ref, qb_ref, kb_ref, vb_ref, u0_ref, u1_ref, u2_ref, u3_ref,
                    pw_ref, ps_ref, *rest, layer):
    o_ref = rest[-5]
    seq = qa_ref.shape[0]
    for cache_ref, src_ref in zip(rest[-4:], (ka_ref, va_ref, kb_ref, vb_ref)):
        if len(cache_ref.shape) == 2:
            cache_ref[...] = src_ref[...]
        else:
            for d in range(cache_ref.shape[0]):
                cache_ref[d] = src_ref[...] if d == layer else jnp.zeros(src_ref.shape, F32)
    for h in range(NA_HEADS):
        sl = slice(h * NA_HEAD_DIM, (h + 1) * NA_HEAD_DIM)
        o = _joint_attention(qa_ref[:, sl].astype(BF16),
                             [(ka_ref[:, sl].astype(BF16), va_ref[:, sl].astype(BF16), None, None)],
                             NA_HEAD_DIM ** -0.5)
        o_ref[:, sl] = o.astype(BF16)
    for hk in range(SW_KV_HEADS):
        ksl = slice(hk * SW_HEAD_DIM, (hk + 1) * SW_HEAD_DIM)
        q = jnp.concatenate(
            [qb_ref[:, (hk * SW_GROUP + g) * SW_HEAD_DIM:(hk * SW_GROUP + g + 1) * SW_HEAD_DIM] for g in range(SW_GROUP)],
            axis=0).astype(BF16)
        o = _joint_attention(q, [(kb_ref[:, ksl].astype(BF16), vb_ref[:, ksl].astype(BF16), None, None)],
                             SW_HEAD_DIM ** -0.5, sink=_sink_column(sink_ref, hk, seq))
        c0 = A_DIM + hk * SW_GROUP * SW_HEAD_DIM
        o_ref[:, c0:c0 + SW_GROUP * SW_HEAD_DIM] = _unstack_heads(o, seq)
    for gi, u_ref in enumerate((u0_ref, u1_ref, u2_ref, u3_ref)):
        c0 = A_DIM + B_Q_DIM + gi * POOL_GROUP_DIM
        o_ref[:, c0:c0 + POOL_GROUP_DIM] = _pool_group(
            u_ref[...], POOL_WINDOWS[gi], pw_ref[gi].astype(BF16),
            ps_ref[:, gi * POOL_GROUP_DIM:(gi + 1) * POOL_GROUP_DIM]).astype(BF16)


def _ctx_mix(p, sink_l, pool_w, pool_scale, layer, seq, caches):
    n = p.shape[0]
    batch = n // seq
    wide = lambda c: pl.BlockSpec((seq, A_DIM), lambda b, c=c: (b, c))
    narrow = lambda c: pl.BlockSpec((seq, B_KV_DIM), lambda b, c=c: (b, c))
    in_specs = [pl.BlockSpec(memory_space=pltpu.SMEM)]
    in_specs += [wide(COL_QA + i) for i in range(4)]
    in_specs += [narrow(COL_KB), narrow(COL_KB + 1)]
    in_specs += [narrow(COL_U + g) for g in range(POOL_GROUPS)]
    in_specs += [pl.BlockSpec((None, POOL_GROUPS, POOL_GROUP_DIM, POOL_GROUP_DIM), lambda b: (layer, 0, 0, 0)),
                 pl.BlockSpec((None, 1, POOL_DIM), lambda b: (layer, 0, 0))]
    n_fixed = len(in_specs)
    in_specs += [pl.BlockSpec(memory_space=pl.ANY)] * len(caches)
    if caches:
        cache_spec = lambda width: pl.BlockSpec((None, None, seq, width), lambda b: (b, layer, 0, 0))
    else:
        cache_spec = lambda width: pl.BlockSpec((None, DEPTH, seq, width), lambda b: (b, 0, 0, 0))
    cache_shape = lambda width: jax.ShapeDtypeStruct((batch, DEPTH, seq, width), F32)
    outs = pl.pallas_call(
        functools.partial(_ctx_mix_kernel, layer=layer),
        grid=(batch,),
        in_specs=in_specs,
        out_specs=[pl.BlockSpec((seq, 3 * A_DIM), lambda b: (b, 0)),
                   cache_spec(A_DIM), cache_spec(A_DIM), cache_spec(B_KV_DIM), cache_spec(B_KV_DIM)],
        out_shape=[jax.ShapeDtypeStruct((n, 3 * A_DIM), BF16),
                   cache_shape(A_DIM), cache_shape(A_DIM), cache_shape(B_KV_DIM), cache_shape(B_KV_DIM)],
        input_output_aliases={n_fixed + k: 1 + k for k in range(len(caches))},
        compiler_params=_params(("parallel",), 24 * MIB),
        name="ctx_mix",
    )(sink_l, p, p, p, p, p, p, p, p, p, p, pool_w, pool_scale, *caches)
    return outs[0], tuple(outs[1:])


NA_PAIR_ROWS = 2 * NA_MAX_KH - 2


def _rpb_table_kernel(rpb_ref, t_ref):
    lane = lax.broadcasted_iota(jnp.int32, (GRID_W, 2 * GRID_W), 1)
    qc = lax.broadcasted_iota(jnp.int32, (GRID_W, 2 * GRID_W), 0)
    kc = lane & (GRID_W - 1)
    upper = lane >= GRID_W
    dcm = jnp.clip(kc - qc + NA_KW - 1, 0, 2 * NA_KW - 2)
    col0 = jnp.clip(qc - NA_KW // 2, 0, GRID_W - NA_KW)
    inside = (kc >= col0) & (kc < col0 + NA_KW)
    n_dc = 2 * NA_KW - 1
    n_dr = 2 * NA_MAX_KH - 1

    def body(i, carry):
        h = i // NA_PAIR_ROWS
        dr = i - h * NA_PAIR_ROWS
        base = (h * n_dr + dr) * n_dc
        acc = jnp.zeros((GRID_W, 2 * GRID_W), F32)
        for dc in range(n_dc):
            val = jnp.where(upper, rpb_ref[base + n_dc + dc], rpb_ref[base + dc])
            acc = jnp.where(dcm == dc, val, acc)
        t_ref[i] = jnp.where(inside, acc, NEG_INF)
        return carry

    lax.fori_loop(0, NA_HEADS * NA_PAIR_ROWS, body, 0)


def _rpb_table(rpb_l):
    return pl.pallas_call(
        _rpb_table_kernel,
        in_specs=[pl.BlockSpec(memory_space=pltpu.SMEM)],
        out_specs=pl.BlockSpec(memory_space=pltpu.VMEM),
        out_shape=jax.ShapeDtypeStruct((NA_HEADS * NA_PAIR_ROWS, GRID_W, 2 * GRID_W), F32),
        name="rpb_table",
    )(rpb_l.reshape(-1))


def _na_kernel(q_ref, k_ref, v_ref, kc_ref, vc_ref, t_ref, o_ref, kb_scr, vb_scr, kcb_scr, vcb_scr, *, rows):
    qr = pl.program_id(1)

    @pl.when(qr == 0)
    def _():
        kb_scr[...] = k_ref[...].astype(BF16)
        vb_scr[...] = v_ref[...].astype(BF16)
        for h in range(NA_HEADS):
            sl = slice(h * NA_HEAD_DIM, (h + 1) * NA_HEAD_DIM)
            kcb_scr[:, sl] = kc_ref[:, h, :].astype(BF16)
            vcb_scr[:, sl] = vc_ref[:, h, :].astype(BF16)

    row0 = jnp.clip(qr - NA_MAX_KH // 2, 0, rows - NA_MAX_KH)
    start = pl.multiple_of(row0 * GRID_W, GRID_W)
    nkeys = NA_MAX_KH * GRID_W
    d0 = row0 - qr + NA_MAX_KH - 1
    for h in range(NA_HEADS):
        sl = slice(h * NA_HEAD_DIM, (h + 1) * NA_HEAD_DIM)
        bias = jnp.concatenate([t_ref[h * NA_PAIR_ROWS + d0 + 2 * i] for i in range(NA_MAX_KH // 2)], axis=1)
        o = _joint_attention(
            q_ref[:, sl].astype(BF16),
            [(kb_scr[pl.ds(start, nkeys), sl], vb_scr[pl.ds(start, nkeys), sl], bias, None),
             (kcb_scr[:, sl], vcb_scr[:, sl], None, None)],
            NA_HEAD_DIM ** -0.5)
        o_ref[:, sl] = o.astype(BF16)


def _na_latent(p, cache_k, cache_v, table, layer, seq):
    n = p.shape[0]
    rows = seq // GRID_W
    past = cache_k.shape[2]
    ctx_spec = pl.BlockSpec((None, None, past, NA_HEADS, NA_HEAD_DIM), lambda b, r: (b, layer, 0, 0, 0))
    return pl.pallas_call(
        functools.partial(_na_kernel, rows=rows),
        grid=(n // seq, rows),
        in_specs=[pl.BlockSpec((GRID_W, A_DIM), lambda b, r: (b * rows + r, COL_QA)),
                  pl.BlockSpec((seq, A_DIM), lambda b, r: (b, COL_QA + 1)),
                  pl.BlockSpec((seq, A_DIM), lambda b, r: (b, COL_QA + 2)),
                  ctx_spec, ctx_spec,
                  pl.BlockSpec(table.shape, lambda b, r: (0, 0, 0))],
        out_specs=pl.BlockSpec((GRID_W, A_DIM), lambda b, r: (b * rows + r, 0)),
        out_shape=jax.ShapeDtypeStruct((n, A_DIM), BF16),
        scratch_shapes=[pltpu.VMEM((seq, A_DIM), BF16), pltpu.VMEM((seq, A_DIM), BF16),
                        pltpu.VMEM((past, A_DIM), BF16), pltpu.VMEM((past, A_DIM), BF16)],
        compiler_params=_params(("parallel", "arbitrary"), 16 * MIB),
        name="na_latent",
    )(p, p, p, cache_k, cache_v, table)


def _rope_tables(seq):
    nfreq = SW_HEAD_DIM // 4
    inv = 1.0 / (ROPE_THETA ** (np.arange(nfreq, dtype=np.float32) / np.float32(nfreq)))
    t = np.arange(seq)
    pos = (t // GRID_W, t % GRID_W)
    cos = np.zeros((seq, SW_HEAD_DIM), np.float32)
    sin_next = np.zeros((seq, SW_HEAD_DIM), np.float32)
    sin_prev = np.zeros((seq, SW_HEAD_DIM), np.float32)
    for a in range(2):
        ang = pos[a].astype(np.float32)[:, None] * inv[None, :].astype(np.float32)
        c, s = np.cos(ang).astype(np.float32), np.sin(ang).astype(np.float32)
        lo = 2 * a * nfreq
        cos[:, lo:lo + nfreq] = c
        cos[:, lo + nfreq:lo + 2 * nfreq] = c
        sin_next[:, lo:lo + nfreq] = -s
        sin_prev[:, lo + nfreq:lo + 2 * nfreq] = s
    tile = lambda x: jnp.asarray(np.tile(x, (1, 128 // SW_HEAD_DIM)))
    return tile(cos), tile(sin_next), tile(sin_prev)


def _rope(x, cos, sin_next, sin_prev):
    nfreq = SW_HEAD_DIM // 4
    return x * cos + pltpu.roll(x, 128 - nfreq, 1) * sin_next + pltpu.roll(x, nfreq, 1) * sin_prev


def _sw_kernel(sink_ref, q_ref, k_ref, v_ref, kc_ref, vc_ref, cq_ref, snq_ref, spq_ref, ck_ref, snk_ref, spk_ref,
               o_ref, kr_scr, *, seq):
    n = pl.program_id(1)

    @pl.when(n == 0)
    def _():
        kr_scr[...] = _rope(k_ref[...], ck_ref[...], snk_ref[...], spk_ref[...]).astype(BF16)

    nwin = 3 * SW_BLOCK
    kstart = pl.multiple_of(jnp.clip((n - 1) * SW_BLOCK, 0, seq - nwin), SW_BLOCK)
    cq, snq, spq = cq_ref[...], snq_ref[...], spq_ref[...]
    q = jnp.concatenate([_rope(q_ref[:, c * 128:(c + 1) * 128], cq, snq, spq) for c in range(B_Q_DIM // 128)],
                        axis=1).astype(BF16)
    rows = SW_GROUP * SW_BLOCK
    qpos = n * SW_BLOCK + (lax.broadcasted_iota(jnp.int32, (rows, nwin), 0) & (SW_BLOCK - 1))
    kpos = kstart + lax.broadcasted_iota(jnp.int32, (rows, nwin), 1)
    band = jnp.abs(qpos - kpos) <= SW_WINDOW
    for hk in range(SW_KV_HEADS):
        ksl = slice(hk * SW_HEAD_DIM, (hk + 1) * SW_HEAD_DIM)
        qs = jnp.concatenate(
            [q[:, (hk * SW_GROUP + g) * SW_HEAD_DIM:(hk * SW_GROUP + g + 1) * SW_HEAD_DIM] for g in range(SW_GROUP)], axis=0)
        o = _joint_attention(
            qs,
            [(kr_scr[pl.ds(kstart, nwin), ksl], v_ref[pl.ds(kstart, nwin), ksl].astype(BF16), None, band),
             (kc_ref[:, ksl].astype(BF16), vc_ref[:, ksl].astype(BF16), None, None)],
            SW_HEAD_DIM ** -0.5, sink=_sink_column(sink_ref, hk, SW_BLOCK))
        c0 = hk * SW_GROUP * SW_HEAD_DIM
        o_ref[:, c0:c0 + SW_GROUP * SW_HEAD_DIM] = _unstack_heads(o, SW_BLOCK)


def _sw_latent(p, cache_k, cache_v, sink_l, layer, seq):
    n = p.shape[0]
    nb = seq // SW_BLOCK
    past = cache_k.shape[2]
    cos, sin_next, sin_prev = _rope_tables(seq)
    ctx_spec = pl.BlockSpec((None, None, past, B_KV_DIM), lambda b, i: (b, layer, 0, 0))
    tab_q = pl.BlockSpec((SW_BLOCK, 128), lambda b, i: (i, 0))
    tab_k = pl.BlockSpec((seq, 128), lambda b, i: (0, 0))
    return pl.pallas_call(
        functools.partial(_sw_kernel, seq=seq),
        grid=(n // seq, nb),
        in_specs=[pl.BlockSpec(memory_space=pltpu.SMEM),
                  pl.BlockSpec((SW_BLOCK, B_Q_DIM), lambda b, i: (b * nb + i, COL_QA + 3)),
                  pl.BlockSpec((seq, B_KV_DIM), lambda b, i: (b, COL_KB)),
                  pl.BlockSpec((seq, B_KV_DIM), lambda b, i: (b, COL_KB + 1)),
                  ctx_spec, ctx_spec, tab_q, tab_q, tab_q, tab_k, tab_k, tab_k],
        out_specs=pl.BlockSpec((SW_BLOCK, B_Q_DIM), lambda b, i: (b * nb + i, 0)),
        out_shape=jax.ShapeDtypeStruct((n, B_Q_DIM), BF16),
        scratch_shapes=[pltpu.VMEM((seq, B_KV_DIM), BF16)],
        compiler_params=_params(("parallel", "arbitrary"), 16 * MIB),
        name="sw_latent",
    )(sink_l, p, p, p, cache_k, cache_v, cos, sin_next, sin_prev, cos, sin_next, sin_prev)


def _pool_kernel(u0_ref, u1_ref, u2_ref, u3_ref, pw_ref, ps_ref, o_ref):
    for gi, u_ref in enumerate((u0_ref, u1_ref, u2_ref, u3_ref)):
        sl = slice(gi * POOL_GROUP_DIM, (gi + 1) * POOL_GROUP_DIM)
        o_ref[:, sl] = _pool_group(u_ref[...], POOL_WINDOWS[gi], pw_ref[gi].astype(BF16), ps_ref[:, sl]).astype(BF16)


def _pool_latent(p, pool_w, pool_scale, layer, seq):
    n = p.shape[0]
    return pl.pallas_call(
        _pool_kernel,
        grid=(n // seq,),
        in_specs=[pl.BlockSpec((seq, POOL_GROUP_DIM), lambda b, g=g: (b, COL_U + g)) for g in range(POOL_GROUPS)]
        + [pl.BlockSpec((None, POOL_GROUPS, POOL_GROUP_DIM, POOL_GROUP_DIM), lambda b: (layer, 0, 0, 0)),
           pl.BlockSpec((None, 1, POOL_DIM), lambda b: (layer, 0, 0))],
        out_specs=pl.BlockSpec((seq, POOL_DIM), lambda b: (b, 0)),
        out_shape=jax.ShapeDtypeStruct((n, POOL_DIM), BF16),
        compiler_params=_params(("parallel",), 16 * MIB),
        name="pool_latent",
    )(p, p, p, p, pool_w, pool_scale)


def _split_bf16(x):
    hi = x.astype(BF16)
    return hi, (x - hi.astype(F32)).astype(BF16)


def _branch_merge_kernel(oa_ref, ob_ref, oc_ref, ga_ref, gb_ref, gc_ref, wa_ref, wb_ref, wc_ref, m_ref):
    s = pl.program_id(1)
    for t in range(BMERGE_NA):
        @pl.when(s == t)
        def _(t=t):
            cols = slice(t * BMERGE_CA, (t + 1) * BMERGE_CA)
            m = (ga_ref[...].astype(F32) * _dot(oa_ref[...], wa_ref[:, cols])
                 + gb_ref[...].astype(F32) * _dot(ob_ref[...], wb_ref[:, cols])
                 + gc_ref[...].astype(F32) * _dot(oc_ref[...], wc_ref[:, cols]))
            m_ref[...] = m.astype(BF16)


def _branch_merge(branches, gates, wa, wb, wc, layer):
    n = gates.shape[0]
    tm, ca = BMERGE_TM, BMERGE_CA
    gate = lambda k: pl.BlockSpec((pl.Element(tm), pl.Element(ca)),
                                  lambda i, s, k=k: (i * tm, pl.multiple_of(GATES_COL0 + k * D_MODEL + s * ca, LANES)))
    branch = lambda col: pl.BlockSpec((tm, A_DIM), lambda i, s: (i, col))
    resident = lambda rows: pl.BlockSpec((None, rows, D_MODEL), lambda i, s: (layer, 0, 0),
                                         pipeline_mode=pl.Buffered(1))
    oa, ob, oc = branches
    est = (2 * 3 * tm * A_DIM * 2 + 2 * 3 * tm * ca * 2 + 2 * tm * ca * 2 + 3 * A_DIM * D_MODEL * 2 + 8 * tm * ca * 4)
    return pl.pallas_call(
        _branch_merge_kernel,
        grid=(n // tm, BMERGE_NA),
        in_specs=[branch(oa[1]), branch(ob[1]), branch(oc[1]), gate(0), gate(1), gate(2),
                  resident(A_DIM), resident(B_Q_DIM), resident(POOL_DIM)],
        out_specs=pl.BlockSpec((tm, ca), lambda i, s: (i, s)),
        out_shape=jax.ShapeDtypeStruct((n, D_MODEL), BF16),
        compiler_params=_params(("parallel", "arbitrary"), est),
        name="branch_merge",
    )(oa[0], ob[0], oc[0], gates, gates, gates, wa, wb, wc)


def _outproj_kernel(m_ref, x_ref, mod_ref, nw_ref, wo_ref, wr_ref, x1_ref, h2_ref, lg_ref, y_scr):
    tiles = [slice(c * OUTPROJ_CB, (c + 1) * OUTPROJ_CB) for c in range(D_MODEL // OUTPROJ_CB)]
    rows = x_ref.shape[0]
    ss = jnp.zeros((rows, 1), F32)
    for cols in tiles:
        y = _dot(m_ref[...], wo_ref[:, cols])
        y_scr[:, cols] = y
        ss = ss + (y * y).sum(axis=-1, keepdims=True)
    r1 = lax.rsqrt(ss / D_MODEL + RMS_EPS)
    gain1 = mod_ref[2:3, :] * nw_ref[1:2, :]
    gain2 = nw_ref[2:3, :] * (1.0 + mod_ref[4:5, :])
    ss = jnp.zeros((rows, 1), F32)
    for cols in tiles:
        x1 = x_ref[:, cols] + (y_scr[:, cols] * r1) * gain1[:, cols]
        x1_ref[:, cols] = x1
        ss = ss + (x1 * x1).sum(axis=-1, keepdims=True)
    r2 = lax.rsqrt(ss / D_MODEL + RMS_EPS)
    lg = jnp.zeros(lg_ref.shape, F32)
    for cols in tiles:
        h2 = (x1_ref[:, cols] * r2) * gain2[:, cols] + mod_ref[3:4, cols]
        h2_ref[:, cols] = h2.astype(BF16)
        h_hi, h_lo = _split_bf16(h2)
        w_hi, w_lo = _split_bf16(wr_ref[:, cols])
        lg = lg + (_dot_nt(w_hi, h_hi) + (_dot_nt(w_hi, h_lo) + _dot_nt(w_lo, h_hi)))
    lg_ref[...] = lg


def _outproj(m, x, mod, norm_w, wo, wr_t, layer, group_of_tile):
    n = x.shape[0]
    tm = OUTPROJ_TM
    row = lambda: pl.BlockSpec((tm, D_MODEL), lambda i: (i, 0))
    est = (2 * tm * D_MODEL * (2 + 4 + 4 + 2) + D_MODEL * D_MODEL * 2 + tm * D_MODEL * 4 + 4 * tm * OUTPROJ_CB * 4)
    return pl.pallas_call(
        _outproj_kernel,
        grid=(n // tm,),
        in_specs=[row(), row(),
                  pl.BlockSpec((None, None, 6, D_MODEL), lambda i: (layer, group_of_tile(i), 0, 0)),
                  pl.BlockSpec((None, 4, D_MODEL), lambda i: (layer, 0, 0)),
                  pl.BlockSpec((None, D_MODEL, D_MODEL), lambda i: (layer, 0, 0), pipeline_mode=pl.Buffered(1)),
                  pl.BlockSpec((None, N_EXPERTS, D_MODEL), lambda i: (layer, 0, 0))],
        out_specs=[row(), row(), pl.BlockSpec((N_EXPERTS, tm), lambda i: (0, i))],
        out_shape=[jax.ShapeDtypeStruct((n, D_MODEL), F32), jax.ShapeDtypeStruct((n, D_MODEL), BF16),
                   jax.ShapeDtypeStruct((N_EXPERTS, n), F32)],
        scratch_shapes=[pltpu.VMEM((tm, D_MODEL), F32)],
        compiler_params=_params(("parallel",), est),
        name="outproj",
    )(m, x, mod, norm_w, wo, wr_t)


RANK_TILE = LANES


def _dispatch_kernel(lg_ref, h_ref, xs_ref, gate_ref, rc_ref, aff_scr, sel_scr, *, seq, cap):
    t = RANK_TILE
    nt = seq // t
    lg = lg_ref[...]
    e = jnp.exp(lg - lg.max(axis=0, keepdims=True))
    aff_scr[...] = e / e.sum(axis=0, keepdims=True)
    aff = aff_scr[...]
    ident = jnp.where(lax.broadcasted_iota(jnp.int32, (t, t), 0) == lax.broadcasted_iota(jnp.int32, (t, t), 1),
                      1.0, 0.0).astype(BF16)

    def to_sublanes(pieces, r):
        tile = slice(r * t, (r + 1) * t)
        out = _dot_nt(ident, pieces[0][:, tile])
        for piece in pieces[1:]:
            out = out + _dot_nt(ident, piece[:, tile])
        return out

    a1 = aff.astype(BF16)
    r1 = aff - a1.astype(F32)
    a2 = r1.astype(BF16)
    a3 = (r1 - a2.astype(F32)).astype(BF16)
    aff_cols = [to_sublanes((a1, a2, a3), r) for r in range(nt)]

    earlier = lax.broadcasted_iota(jnp.int32, (t, t), 0) < lax.broadcasted_iota(jnp.int32, (t, t), 1)
    slot = lax.broadcasted_iota(jnp.int32, (cap, seq), 0).astype(F32)
    ranks = []
    for ex in range(N_EXPERTS):
        row = aff[ex:ex + 1, :]
        cols = [jnp.broadcast_to(aff_cols[r][:, ex:ex + 1], (t, t)) for r in range(nt)]
        counts = []
        for c in range(nt):
            rowb = jnp.broadcast_to(row[:, c * t:(c + 1) * t], (t, t))
            acc = jnp.zeros((t, t), F32)
            for r in range(nt):
                if r < c:
                    beats = cols[r] >= rowb
                elif r > c:
                    beats = cols[r] > rowb
                else:
                    beats = (cols[r] > rowb) | (earlier & (cols[r] == rowb))
                acc = acc + jnp.where(beats, 1.0, 0.0)
            counts.append(acc.sum(axis=0, keepdims=True))
        rank_row = jnp.concatenate(counts, axis=1) if nt > 1 else counts[0]
        ranks.append(rank_row)
        sel = slot == rank_row
        sel_scr[ex * cap:(ex + 1) * cap, :] = sel.astype(BF16)
        gate = jnp.where(sel, jnp.broadcast_to(row, (cap, seq)), 0.0).sum(axis=1, keepdims=True)
        gate_ref[ex] = jnp.broadcast_to(gate, (cap, LANES))
    rank = jnp.concatenate(ranks + [jnp.zeros((LANES - N_EXPERTS, seq), F32)], axis=0)
    rank = jnp.minimum(rank, float(cap)).astype(BF16)
    for r in range(nt):
        rc_ref[r * t:(r + 1) * t, :] = to_sublanes((rank,), r)
    xs = _dot(sel_scr[...], h_ref[...]).astype(BF16)
    xs_ref[...] = xs.reshape(N_EXPERTS, cap, D_MODEL)


def _dispatch(lg_t, h2, seq):
    n = h2.shape[0]
    nb = n // seq
    cap = EC_FACTOR * seq // N_EXPERTS
    est = (2 * seq * D_MODEL * 2 + 2 * N_EXPERTS * cap * D_MODEL * 2 + N_EXPERTS * cap * seq * 2
           + N_EXPERTS * cap * D_MODEL * 4 + 8 * seq * LANES * 4)
    return pl.pallas_call(
        functools.partial(_dispatch_kernel, seq=seq, cap=cap),
        grid=(nb,),
        in_specs=[pl.BlockSpec((N_EXPERTS, seq), lambda b: (0, b)),
                  pl.BlockSpec((seq, D_MODEL), lambda b: (b, 0))],
        out_specs=[pl.BlockSpec((N_EXPERTS, cap, D_MODEL), lambda b: (0, b, 0)),
                   pl.BlockSpec((N_EXPERTS, cap, LANES), lambda b: (0, b, 0)),
                   pl.BlockSpec((seq, LANES), lambda b: (b, 0))],
        out_shape=[jax.ShapeDtypeStruct((N_EXPERTS, nb * cap, D_MODEL), BF16),
                   jax.ShapeDtypeStruct((N_EXPERTS, nb * cap, LANES), F32),
                   jax.ShapeDtypeStruct((n, LANES), F32)],
        scratch_shapes=[pltpu.VMEM((N_EXPERTS, seq), F32), pltpu.VMEM((N_EXPERTS * cap, seq), BF16)],
        compiler_params=_params(("parallel",), est),
        name="dispatch",
    )(lg_t, h2)


EXPERT_NF = D_EXPERT // EXPERT_TF
EXPERT_ND = D_MODEL // EXPERT_TD


assert EXPERT_NF == EXPERT_ND


def _expert_kernel(xc_ref, xl_ref, gc_ref, gl_ref, wg_ref, wu_ref, wd_ref, yc_ref, yl_ref, hc_scr, hl_scr):
    e = pl.program_id(0)
    t = pl.program_id(1)
    cur = e % 2

    @pl.when(e >= 1)
    def _():
        wd = wd_ref[0].astype(BF16)
        for h_scr, g_ref, y_ref in ((hc_scr, gc_ref, yc_ref), (hl_scr, gl_ref, yl_ref)):
            acc = _dot(h_scr[1 - cur, 0], wd[0:EXPERT_TF])
            for f in range(1, EXPERT_NF):
                acc = acc + _dot(h_scr[1 - cur, f], wd[f * EXPERT_TF:(f + 1) * EXPERT_TF])
            y_ref[0] = (acc * g_ref[0, :, 0:1]).astype(BF16)

    @pl.when(e < N_EXPERTS)
    def _():
        wg = wg_ref[0].astype(BF16)
        wu = wu_ref[0].astype(BF16)
        for x_ref, h_scr in ((xc_ref, hc_scr), (xl_ref, hl_scr)):
            x = x_ref[0]
            a = _dot(x, wg)
            h_scr[cur, t] = ((a * jax.nn.sigmoid(a)) * _dot(x, wu)).astype(BF16)


def _experts(xs_c, xs_l, gate_c, gate_l, w_gate, w_up, w_down, layer):
    sc, sl = xs_c.shape[1], xs_l.shape[1]
    tf, td = EXPERT_TF, EXPERT_TD
    last = N_EXPERTS - 1
    up_expert = lambda e: jnp.minimum(e, last)
    down_expert = lambda e: jnp.maximum(e - 1, 0)
    up_tile = lambda e, t: (layer, up_expert(e), 0, jnp.where(e > last, EXPERT_NF - 1, t))
    down_tile = lambda e, t: (down_expert(e), 0, jnp.where(e == 0, 0, t))
    x_spec = lambda s: pl.BlockSpec((1, s, D_MODEL), lambda e, t: (up_expert(e), 0, 0))
    g_spec = lambda s: pl.BlockSpec((1, s, LANES), lambda e, t: (down_expert(e), 0, 0))
    est = (2 * (sc + sl) * D_MODEL * 2 + 2 * 2 * D_MODEL * tf * 4 + 2 * D_EXPERT * td * 4 + 2 * (sc + sl) * td * 4
           + 2 * (sc + sl) * D_EXPERT * 2 + 2 * D_MODEL * tf * 2 + D_EXPERT * td * 2 + 6 * sc * max(tf, td) * 4)
    return pl.pallas_call(
        _expert_kernel,
        grid=(N_EXPERTS + 1, EXPERT_NF),
        in_specs=[x_spec(sc), x_spec(sl), g_spec(sc), g_spec(sl),
                  pl.BlockSpec((None, 1, D_MODEL, tf), up_tile),
                  pl.BlockSpec((None, 1, D_MODEL, tf), up_tile),
                  pl.BlockSpec((None, 1, D_EXPERT, td), lambda e, t: (layer,) + down_tile(e, t))],
        out_specs=[pl.BlockSpec((1, sc, td), down_tile), pl.BlockSpec((1, sl, td), down_tile)],
        out_shape=[jax.ShapeDtypeStruct((N_EXPERTS, sc, D_MODEL), BF16),
                   jax.ShapeDtypeStruct((N_EXPERTS, sl, D_MODEL), BF16)],
        scratch_shapes=[pltpu.VMEM((2, EXPERT_NF, sc, tf), BF16), pltpu.VMEM((2, EXPERT_NF, sl, tf), BF16)],
        compiler_params=_params(("arbitrary", "arbitrary"), est),
        name="experts",
    )(xs_c, xs_l, gate_c, gate_l, w_gate, w_up, w_down)


def _combine_kernel(rc_ref, ye_ref, x1_ref, mod_ref, nw_ref, o_ref, *, cap):
    nslots = N_EXPERTS * cap
    shift = cap.bit_length() - 1
    expert_of_slot = lax.broadcasted_iota(jnp.int32, (LANES, nslots), 1) >> shift
    spread = jnp.where(lax.broadcasted_iota(jnp.int32, (LANES, nslots), 0) == expert_of_slot, 1.0, 0.0).astype(BF16)
    rank_of_slot = _dot(rc_ref[...].astype(BF16), spread)
    slot = (lax.broadcasted_iota(jnp.int32, (1, nslots), 1) & (cap - 1)).astype(F32)
    onehot = jnp.where(rank_of_slot == slot, 1.0, 0.0).astype(BF16)
    ffn = _dot(onehot, ye_ref[...].reshape(nslots, D_MODEL))
    o_ref[...] = x1_ref[...] + mod_ref[5:6, :] * _rms(ffn, nw_ref[3:4, :])


def _combine(rank_col, ye, x1, mod, norm_w, layer, seq, group_of_batch):
    n = x1.shape[0]
    cap = EC_FACTOR * seq // N_EXPERTS
    assert cap & (cap - 1) == 0 and cap <= LANES
    tr = COMBINE_TR
    per = seq // tr
    est = (2 * N_EXPERTS * cap * D_MODEL * 2 + 4 * tr * D_MODEL * 4 + (tr + LANES) * N_EXPERTS * cap * 6
           + 3 * tr * D_MODEL * 4)
    return pl.pallas_call(
        functools.partial(_combine_kernel, cap=cap),
        grid=(n // seq, per),
        in_specs=[pl.BlockSpec((tr, LANES), lambda b, i: (b * per + i, 0)),
                  pl.BlockSpec((N_EXPERTS, cap, D_MODEL), lambda b, i: (0, b, 0)),
                  pl.BlockSpec((tr, D_MODEL), lambda b, i: (b * per + i, 0)),
                  pl.BlockSpec((None, None, 6, D_MODEL), lambda b, i: (layer, group_of_batch(b), 0, 0)),
                  pl.BlockSpec((None, 4, D_MODEL), lambda b, i: (layer, 0, 0))],
        out_specs=pl.BlockSpec((tr, D_MODEL), lambda b, i: (b * per + i, 0)),
        out_shape=jax.ShapeDtypeStruct((n, D_MODEL), F32),
        compiler_params=_params(("parallel", "arbitrary"), est),
        name="combine",
    )(rank_col, ye, x1, mod, norm_w)


def kernel(x_prompt, x_sample, c, cache_a_k, cache_a_v, cache_b_k, cache_b_v, c_ctx, norm_w, w_ada, b_ada, w_in, a_rpb,
           b_sink, c_pool_w, c_scale, w_branch_a, w_branch_b, w_branch_c, w_out, w_router, w_gate_e, w_up_e, w_down_e):
    batch, seq_c, _ = x_prompt.shape
    dec_batch, seq_l, _ = x_sample.shape
    past = cache_a_k.shape[2]

    cond = jnp.zeros((ADA_ROWS, D_MODEL), F32).at[0].set(c_ctx).at[1:1 + dec_batch].set(c)
    mod = _adaln(cond, w_ada, b_ada).reshape(DEPTH, ADA_ROWS, 6, D_MODEL)

    ctx_group = lambda i: 0
    lat_group_inproj = lambda i: 1 + i // (seq_l // INPROJ_TM)
    lat_group_outproj = lambda i: 1 + i // (seq_l // OUTPROJ_TM)
    lat_group_batch = lambda b: 1 + b

    cbk = cache_b_k.reshape(dec_batch, DEPTH, past, B_KV_DIM)
    cbv = cache_b_v.reshape(dec_batch, DEPTH, past, B_KV_DIM)

    x_c = x_prompt.reshape(batch * seq_c, D_MODEL)
    x_l = x_sample.reshape(dec_batch * seq_l, D_MODEL)
    w_in_bf = w_in.astype(BF16)
    wa, wb, wc, wo = (w.astype(BF16) for w in (w_branch_a, w_branch_b, w_branch_c, w_out))
    wr_t = jnp.swapaxes(w_router, 1, 2)
    pool_scale = c_scale.reshape(DEPTH, 1, POOL_DIM)
    caches = ()
    for l in range(DEPTH):
        p_c, g_c = _inproj(x_c, mod, norm_w, w_in_bf, l, ctx_group)
        o_c, caches = _ctx_mix(p_c, b_sink[l], c_pool_w, pool_scale, l, seq_c, caches)
        branches_c = tuple((o_c, k) for k in range(N_BRANCH))
        m_c = _branch_merge(branches_c, g_c, wa, wb, wc, l)
        x1_c, h2_c, lg_c = _outproj(m_c, x_c, mod, norm_w, wo, wr_t, l, ctx_group)
        xs_c, gate_c, rc_c = _dispatch(lg_c, h2_c, seq_c)

        p_l, g_l = _inproj(x_l, mod, norm_w, w_in_bf, l, lat_group_inproj)
        table = _rpb_table(a_rpb[l])
        o_a = _na_latent(p_l, cache_a_k, cache_a_v, table, l, seq_l)
        o_b = _sw_latent(p_l, cbk, cbv, b_sink[l], l, seq_l)
        o_p = _pool_latent(p_l, c_pool_w, pool_scale, l, seq_l)
        m_l = _branch_merge(((o_a, 0), (o_b, 0), (o_p, 0)), g_l, wa, wb, wc, l)
        x1_l, h2_l, lg_l = _outproj(m_l, x_l, mod, norm_w, wo, wr_t, l, lat_group_outproj)
        xs_l, gate_l, rc_l = _dispatch(lg_l, h2_l, seq_l)

        ye_c, ye_l = _experts(xs_c, xs_l, gate_c, gate_l, w_gate_e, w_up_e, w_down_e, l)
        x_c = _combine(rc_c, ye_c, x1_c, mod, norm_w, l, seq_c, ctx_group)
        x_l = _combine(rc_l, ye_l, x1_l, mod, norm_w, l, seq_l, lat_group_batch)

    y_prompt = x_c.reshape(batch, seq_c, D_MODEL)
    y_sample = x_l.reshape(dec_batch, seq_l, D_MODEL)
    new_a_k, new_a_v, new_b_k, new_b_v = caches
    a_shape = (batch, DEPTH, seq_c, NA_HEADS, NA_HEAD_DIM)
    b_shape = (batch, DEPTH, seq_c, SW_KV_HEADS, SW_HEAD_DIM)
    return (y_prompt, y_sample, new_a_k.reshape(a_shape), new_a_v.reshape(a_shape),
            new_b_k.reshape(b_shape), new_b_v.reshape(b_shape))
```

```python
import functools

import numpy as np
import jax
import jax.numpy as jnp
from jax import lax
from jax.experimental import pallas as pl
from jax.experimental.pallas import tpu as pltpu

F32 = jnp.float32
BF16 = jnp.bfloat16

D_MODEL = 2048
DEPTH = 2
GRID_W = 64
NA_HEADS, NA_HEAD_DIM, NA_MAX_KH, NA_KW = 4, 128, 8, 16
SW_Q_HEADS, SW_KV_HEADS, SW_HEAD_DIM = 8, 2, 64
SW_GROUP = SW_Q_HEADS // SW_KV_HEADS
SW_WINDOW, SW_BLOCK = 128, 128
ROPE_THETA = 10000.0
POOL_WINDOWS = (2, 4, 8, 16)
POOL_GROUPS, POOL_GROUP_DIM = 4, 128
POOL_DIM = POOL_GROUPS * POOL_GROUP_DIM
A_DIM = NA_HEADS * NA_HEAD_DIM
B_Q_DIM = SW_Q_HEADS * SW_HEAD_DIM
B_KV_DIM = SW_KV_HEADS * SW_HEAD_DIM
N_BRANCH = 3
GATE_DIM = N_BRANCH * D_MODEL
QKVU_DIM = 3 * A_DIM + B_Q_DIM + 2 * B_KV_DIM + POOL_DIM
IN_DIM = QKVU_DIM + GATE_DIM
N_EXPERTS = 16
EC_FACTOR = 2
D_EXPERT = 1024
RMS_EPS = 1e-6
NEG_INF = -1e30

COL_QA = 0
COL_KB = (3 * A_DIM + B_Q_DIM) // B_KV_DIM
COL_U = COL_KB + 2

MIB = 1024 * 1024
V7X_VMEM_BYTES = 64 * MIB
V7X_VMEM_CEILING = 60000 * 1024
VMEM_FLOOR = 32 * MIB
VMEM_COMPILER_SCRATCH = 8 * MIB
assert V7X_VMEM_CEILING < V7X_VMEM_BYTES

ADA_ROWS = 16
ADA_TN = 1024
V7X_MXU_DIM = 256
LANES = 128
INPROJ_TM = 1024
INPROJ_TN = 1280
assert INPROJ_TN % V7X_MXU_DIM == 0 and IN_DIM % INPROJ_TN == 0
INPROJ_TILES = IN_DIM // INPROJ_TN
INPROJ_SPLIT = QKVU_DIM // INPROJ_TN
P_WIDTH = (INPROJ_SPLIT + 1) * INPROJ_TN
GATES_WIDTH = (INPROJ_TILES - INPROJ_SPLIT) * INPROJ_TN
GATES_COL0 = QKVU_DIM - INPROJ_SPLIT * INPROJ_TN
BMERGE_TM = 1024
BMERGE_CA = 1024
BMERGE_NA = D_MODEL // BMERGE_CA
OUTPROJ_TM = 512
OUTPROJ_CB = 512
EXPERT_TF = 256
EXPERT_TD = 512
COMBINE_TR = 256
ROW_CHUNK = 64


def _vmem_limit(estimate_bytes):
    return int(min(V7X_VMEM_CEILING, max(VMEM_FLOOR, estimate_bytes + VMEM_COMPILER_SCRATCH)))


def _params(semantics, estimate_bytes):
    return pltpu.CompilerParams(dimension_semantics=semantics,
                                vmem_limit_bytes=_vmem_limit(estimate_bytes))


def _rms(x, g):
    ms = jnp.mean(x * x, axis=-1, keepdims=True)
    return x * lax.rsqrt(ms + RMS_EPS) * g


def _sigmoid(x):
    return 0.5 * jnp.tanh(0.5 * x) + 0.5


def _dot(a, b):
    return jnp.dot(a, b, preferred_element_type=F32)


def _dot_nt(a, b):
    return lax.dot_general(a, b, (((1,), (1,)), ((), ())), preferred_element_type=F32)


def _adaln_kernel(c_ref, w_ref, b_ref, o_ref):
    c = c_ref[...]
    s = (c * jax.nn.sigmoid(c)).astype(BF16)
    o_ref[0] = _dot(s, w_ref[0].astype(BF16)) + b_ref[0]


def _adaln(cond, w_ada, b_ada):
    n_out = w_ada.shape[-1]
    return pl.pallas_call(
        _adaln_kernel,
        grid=(DEPTH, n_out // ADA_TN),
        in_specs=[pl.BlockSpec((ADA_ROWS, D_MODEL), lambda l, j: (0, 0)),
                  pl.BlockSpec((1, D_MODEL, ADA_TN), lambda l, j: (l, 0, j)),
                  pl.BlockSpec((1, 1, ADA_TN), lambda l, j: (l, 0, j))],
        out_specs=pl.BlockSpec((1, ADA_ROWS, ADA_TN), lambda l, j: (l, 0, j)),
        out_shape=jax.ShapeDtypeStruct((DEPTH, ADA_ROWS, n_out), F32),
        compiler_params=_params(("parallel", "parallel"), 2 * D_MODEL * ADA_TN * 4),
        name="adaln",
    )(cond, w_ada, b_ada.reshape(DEPTH, 1, n_out))


def _inproj_kernel(x_ref, mod_ref, nw_ref, w_ref, o_ref, gate_ref, h_scr):
    j = pl.program_id(1)

    @pl.when(j == 0)
    def _():
        g = nw_ref[0:1, :]
        sc = 1.0 + mod_ref[1:2, :]
        sh = mod_ref[0:1, :]

        def body(r, carry):
            rows = pl.ds(pl.multiple_of(r * ROW_CHUNK, ROW_CHUNK), ROW_CHUNK)
            h_scr[rows, :] = (_rms(x_ref[rows, :], g) * sc + sh).astype(BF16)
            return carry

        lax.fori_loop(0, INPROJ_TM // ROW_CHUNK, body, 0, unroll=4)

    @pl.when(j < INPROJ_SPLIT)
    def _():
        o_ref[...] = _dot(h_scr[...], w_ref[...])

    @pl.when(j == INPROJ_SPLIT)
    def _():
        acc = _dot(h_scr[...], w_ref[...])
        o_ref[...] = acc
        gate_ref[...] = _sigmoid(acc).astype(BF16)

    @pl.when(j > INPROJ_SPLIT)
    def _():
        gate_ref[...] = _sigmoid(_dot(h_scr[...], w_ref[...])).astype(BF16)


def _mod_spec(layer, group_of):
    return pl.BlockSpec((None, None, 6, D_MODEL), lambda i, j: (layer, group_of(i), 0, 0))


def _inproj(x, mod, norm_w, w_in_bf, layer, group_of_tile):
    n = x.shape[0]
    est = (2 * INPROJ_TM * D_MODEL * 4 + 2 * D_MODEL * INPROJ_TN * 2 + 2 * INPROJ_TM * INPROJ_TN * (4 + 2)
           + INPROJ_TM * D_MODEL * 2 + 2 * INPROJ_TM * INPROJ_TN * 4)
    return pl.pallas_call(
        _inproj_kernel,
        grid=(n // INPROJ_TM, INPROJ_TILES),
        in_specs=[pl.BlockSpec((INPROJ_TM, D_MODEL), lambda i, j: (i, 0)),
                  _mod_spec(layer, group_of_tile),
                  pl.BlockSpec((None, 4, D_MODEL), lambda i, j: (layer, 0, 0)),
                  pl.BlockSpec((None, D_MODEL, INPROJ_TN), lambda i, j: (layer, 0, j))],
        out_specs=[pl.BlockSpec((INPROJ_TM, INPROJ_TN), lambda i, j: (i, jnp.minimum(j, INPROJ_SPLIT))),
                   pl.BlockSpec((INPROJ_TM, INPROJ_TN), lambda i, j: (i, jnp.maximum(j - INPROJ_SPLIT, 0)))],
        out_shape=[jax.ShapeDtypeStruct((n, P_WIDTH), F32), jax.ShapeDtypeStruct((n, GATES_WIDTH), BF16)],
        scratch_shapes=[pltpu.VMEM((INPROJ_TM, D_MODEL), BF16)],
        compiler_params=_params(("parallel", "arbitrary"), est),
        name="inproj",
    )(x, mod, norm_w, w_in_bf)


def _joint_attention(q, segments, scale, sink=None):
    scores = []
    for k, _, bias, mask in segments:
        s = _dot_nt(q, k) * scale
        if bias is not None:
            s = s + bias
        if mask is not None:
            s = jnp.where(mask, s, NEG_INF)
        scores.append(s)
    m = scores[0].max(axis=-1, keepdims=True)
    for s in scores[1:]:
        m = jnp.maximum(m, s.max(axis=-1, keepdims=True))
    if sink is not None:
        m = jnp.maximum(m, sink)
    denom = jnp.exp(sink - m) if sink is not None else 0.0
    acc = None
    for s, (_, v, _, _) in zip(scores, segments):
        e = jnp.exp(s - m)
        denom = denom + e.sum(axis=-1, keepdims=True)
        pv = _dot(e.astype(BF16), v)
        acc = pv if acc is None else acc + pv
    return acc / denom


def _pool_group(u, window, pw_bf, scale_row):
    seq = u.shape[0]
    pad = 8
    n = seq + 2 * pad
    z = jnp.zeros((pad, POOL_GROUP_DIM), F32)
    p = jnp.concatenate([z, u, z], axis=0)
    k = 1
    while k < window:
        p = p + pltpu.roll(p, n - k, 0)
        k *= 2
    win = pltpu.roll(p, window // 2, 0)[pad:pad + seq]
    t = lax.broadcasted_iota(jnp.int32, (seq, 1), 0)
    lo = jnp.maximum(t - window // 2, 0)
    hi = jnp.minimum(t - window // 2 + window, seq)
    cnt = (hi - lo).astype(F32)
    pooled = win / cnt - u
    return _dot(pooled.astype(BF16), pw_bf) * scale_row


def _unstack_heads(o, rows_per_head):
    return jnp.concatenate([o[g * rows_per_head:(g + 1) * rows_per_head] for g in range(SW_GROUP)], axis=1).astype(BF16)


def _sink_column(sink_ref, kv_head, rows_per_head):
    r = lax.broadcasted_iota(jnp.int32, (SW_GROUP * rows_per_head, 1), 0)
    col = jnp.full((SW_GROUP * rows_per_head, 1), sink_ref[kv_head * SW_GROUP], F32)
    for g in range(1, SW_GROUP):
        col = jnp.where(r >= g * rows_per_head, sink_ref[kv_head * SW_GROUP + g], col)
    return col


def _ctx_mix_kernel(sink_ref, qa_ref, ka_ref, va_ref, qb_ref, kb_ref, vb_ref, u0_ref, u1_ref, u2_ref, u3_ref,
                    pw_ref, ps_ref, *rest, layer):
    o_ref = rest[-5]
    seq = qa_ref.shape[0]
    def put(cache_ref, lead, val):
        if cache_ref.shape[-2] == seq:
            cache_ref[lead + (slice(None), slice(None))] = val
        else:
            for h in range(NA_HEADS):
                cache_ref[lead + (pl.ds(h, seq, stride=NA_HEADS), slice(None))] = \
                    val[:, h * NA_HEAD_DIM:(h + 1) * NA_HEAD_DIM]

    for cache_ref, src_ref in zip(rest[-4:], (ka_ref, va_ref, kb_ref, vb_ref)):
        if len(cache_ref.shape) == 2:
            put(cache_ref, (), src_ref[...])
        else:
            for d in range(cache_ref.shape[0]):
                put(cache_ref, (d,), src_ref[...] if d == layer else jnp.zeros(src_ref.shape, F32))
    for h in range(NA_HEADS):
        sl = slice(h * NA_HEAD_DIM, (h + 1) * NA_HEAD_DIM)
        o = _joint_attention(qa_ref[:, sl].astype(BF16),
                             [(ka_ref[:, sl].astype(BF16), va_ref[:, sl].astype(BF16), None, None)],
                             NA_HEAD_DIM ** -0.5)
        o_ref[:, sl] = o.astype(BF16)
    for hk in range(SW_KV_HEADS):
        ksl = slice(hk * SW_HEAD_DIM, (hk + 1) * SW_HEAD_DIM)
        q = jnp.concatenate(
            [qb_ref[:, (hk * SW_GROUP + g) * SW_HEAD_DIM:(hk * SW_GROUP + g + 1) * SW_HEAD_DIM] for g in range(SW_GROUP)],
            axis=0).astype(BF16)
        o = _joint_attention(q, [(kb_ref[:, ksl].astype(BF16), vb_ref[:, ksl].astype(BF16), None, None)],
                             SW_HEAD_DIM ** -0.5, sink=_sink_column(sink_ref, hk, seq))
        c0 = A_DIM + hk * SW_GROUP * SW_HEAD_DIM
        o_ref[:, c0:c0 + SW_GROUP * SW_HEAD_DIM] = _unstack_heads(o, seq)
    for gi, u_ref in enumerate((u0_ref, u1_ref, u2_ref, u3_ref)):
        c0 = A_DIM + B_Q_DIM + gi * POOL_GROUP_DIM
        o_ref[:, c0:c0 + POOL_GROUP_DIM] = _pool_group(
            u_ref[...], POOL_WINDOWS[gi], pw_ref[gi].astype(BF16),
            ps_ref[:, gi * POOL_GROUP_DIM:(gi + 1) * POOL_GROUP_DIM]).astype(BF16)


def _ctx_mix(p, sink_l, pool_w, pool_scale, layer, seq, caches):
    n = p.shape[0]
    batch = n // seq
    wide = lambda c: pl.BlockSpec((seq, A_DIM), lambda b, c=c: (b, c))
    narrow = lambda c: pl.BlockSpec((seq, B_KV_DIM), lambda b, c=c: (b, c))
    in_specs = [pl.BlockSpec(memory_space=pltpu.SMEM)]
    in_specs += [wide(COL_QA + i) for i in range(4)]
    in_specs += [narrow(COL_KB), narrow(COL_KB + 1)]
    in_specs += [narrow(COL_U + g) for g in range(POOL_GROUPS)]
    in_specs += [pl.BlockSpec((None, POOL_GROUPS, POOL_GROUP_DIM, POOL_GROUP_DIM), lambda b: (layer, 0, 0, 0)),
                 pl.BlockSpec((None, 1, POOL_DIM), lambda b: (layer, 0, 0))]
    n_fixed = len(in_specs)
    in_specs += [pl.BlockSpec(memory_space=pl.ANY)] * len(caches)
    if caches:
        cache_spec = lambda rows, width: pl.BlockSpec((None, None, rows, width), lambda b: (b, layer, 0, 0))
    else:
        cache_spec = lambda rows, width: pl.BlockSpec((None, DEPTH, rows, width), lambda b: (b, 0, 0, 0))
    cache_shape = lambda rows, width: jax.ShapeDtypeStruct((batch, DEPTH, rows, width), F32)
    a_cache = (seq * NA_HEADS, NA_HEAD_DIM)
    b_cache = (seq, B_KV_DIM)
    outs = pl.pallas_call(
        functools.partial(_ctx_mix_kernel, layer=layer),
        grid=(batch,),
        in_specs=in_specs,
        out_specs=[pl.BlockSpec((seq, 3 * A_DIM), lambda b: (b, 0)),
                   cache_spec(*a_cache), cache_spec(*a_cache), cache_spec(*b_cache), cache_spec(*b_cache)],
        out_shape=[jax.ShapeDtypeStruct((n, 3 * A_DIM), BF16),
                   cache_shape(*a_cache), cache_shape(*a_cache), cache_shape(*b_cache), cache_shape(*b_cache)],
        input_output_aliases={n_fixed + k: 1 + k for k in range(len(caches))},
        compiler_params=_params(("parallel",), 24 * MIB),
        name="ctx_mix",
    )(sink_l, p, p, p, p, p, p, p, p, p, p, pool_w, pool_scale, *caches)
    return outs[0], tuple(outs[1:])


NA_PAIR_ROWS = 2 * NA_MAX_KH - 2


def _rpb_table_kernel(rpb_ref, t_ref):
    lane = lax.broadcasted_iota(jnp.int32, (GRID_W, 2 * GRID_W), 1)
    qc = lax.broadcasted_iota(jnp.int32, (GRID_W, 2 * GRID_W), 0)
    kc = lane & (GRID_W - 1)
    upper = lane >= GRID_W
    dcm = jnp.clip(kc - qc + NA_KW - 1, 0, 2 * NA_KW - 2)
    col0 = jnp.clip(qc - NA_KW // 2, 0, GRID_W - NA_KW)
    inside = (kc >= col0) & (kc < col0 + NA_KW)
    n_dc = 2 * NA_KW - 1
    n_dr = 2 * NA_MAX_KH - 1

    def body(i, carry):
        h = i // NA_PAIR_ROWS
        dr = i - h * NA_PAIR_ROWS
        base = (h * n_dr + dr) * n_dc
        acc = jnp.zeros((GRID_W, 2 * GRID_W), F32)
        for dc in range(n_dc):
            val = jnp.where(upper, rpb_ref[base + n_dc + dc], rpb_ref[base + dc])
            acc = jnp.where(dcm == dc, val, acc)
        t_ref[i] = jnp.where(inside, acc, NEG_INF)
        return carry

    lax.fori_loop(0, NA_HEADS * NA_PAIR_ROWS, body, 0)


def _rpb_table(rpb_l):
    return pl.pallas_call(
        _rpb_table_kernel,
        in_specs=[pl.BlockSpec(memory_space=pltpu.SMEM)],
        out_specs=pl.BlockSpec(memory_space=pltpu.VMEM),
        out_shape=jax.ShapeDtypeStruct((NA_HEADS * NA_PAIR_ROWS, GRID_W, 2 * GRID_W), F32),
        name="rpb_table",
    )(rpb_l.reshape(-1))


def _na_kernel(q_ref, k_ref, v_ref, kc_ref, vc_ref, t_ref, o_ref, kb_scr, vb_scr, kcb_scr, vcb_scr, *, rows):
    qr = pl.program_id(1)

    @pl.when(qr == 0)
    def _():
        kb_scr[...] = k_ref[...].astype(BF16)
        vb_scr[...] = v_ref[...].astype(BF16)
        for h in range(NA_HEADS):
            sl = slice(h * NA_HEAD_DIM, (h + 1) * NA_HEAD_DIM)
            kcb_scr[:, sl] = kc_ref[:, h, :].astype(BF16)
            vcb_scr[:, sl] = vc_ref[:, h, :].astype(BF16)

    row0 = jnp.clip(qr - NA_MAX_KH // 2, 0, rows - NA_MAX_KH)
    start = pl.multiple_of(row0 * GRID_W, GRID_W)
    nkeys = NA_MAX_KH * GRID_W
    d0 = row0 - qr + NA_MAX_KH - 1
    for h in range(NA_HEADS):
        sl = slice(h * NA_HEAD_DIM, (h + 1) * NA_HEAD_DIM)
        bias = jnp.concatenate([t_ref[h * NA_PAIR_ROWS + d0 + 2 * i] for i in range(NA_MAX_KH // 2)], axis=1)
        o = _joint_attention(
            q_ref[:, sl].astype(BF16),
            [(kb_scr[pl.ds(start, nkeys), sl], vb_scr[pl.ds(start, nkeys), sl], bias, None),
             (kcb_scr[:, sl], vcb_scr[:, sl], None, None)],
            NA_HEAD_DIM ** -0.5)
        o_ref[:, sl] = o.astype(BF16)


def _na_latent(p, cache_k, cache_v, table, layer, seq):
    n = p.shape[0]
    rows = seq // GRID_W
    past = cache_k.shape[2]
    ctx_spec = pl.BlockSpec((None, None, past, NA_HEADS, NA_HEAD_DIM), lambda b, r: (b, layer, 0, 0, 0))
    return pl.pallas_call(
        functools.partial(_na_kernel, rows=rows),
        grid=(n // seq, rows),
        in_specs=[pl.BlockSpec((GRID_W, A_DIM), lambda b, r: (b * rows + r, COL_QA)),
                  pl.BlockSpec((seq, A_DIM), lambda b, r: (b, COL_QA + 1)),
                  pl.BlockSpec((seq, A_DIM), lambda b, r: (b, COL_QA + 2)),
                  ctx_spec, ctx_spec,
                  pl.BlockSpec(table.shape, lambda b, r: (0, 0, 0))],
        out_specs=pl.BlockSpec((GRID_W, A_DIM), lambda b, r: (b * rows + r, 0)),
        out_shape=jax.ShapeDtypeStruct((n, A_DIM), BF16),
        scratch_shapes=[pltpu.VMEM((seq, A_DIM), BF16), pltpu.VMEM((seq, A_DIM), BF16),
                        pltpu.VMEM((past, A_DIM), BF16), pltpu.VMEM((past, A_DIM), BF16)],
        compiler_params=_params(("parallel", "arbitrary"), 16 * MIB),
        name="na_latent",
    )(p, p, p, cache_k, cache_v, table)


def _rope_tables(seq):
    nfreq = SW_HEAD_DIM // 4
    inv = 1.0 / (ROPE_THETA ** (np.arange(nfreq, dtype=np.float32) / np.float32(nfreq)))
    t = np.arange(seq)
    pos = (t // GRID_W, t % GRID_W)
    cos = np.zeros((seq, SW_HEAD_DIM), np.float32)
    sin_next = np.zeros((seq, SW_HEAD_DIM), np.float32)
    sin_prev = np.zeros((seq, SW_HEAD_DIM), np.float32)
    for a in range(2):
        ang = pos[a].astype(np.float32)[:, None] * inv[None, :].astype(np.float32)
        c, s = np.cos(ang).astype(np.float32), np.sin(ang).astype(np.float32)
        lo = 2 * a * nfreq
        cos[:, lo:lo + nfreq] = c
        cos[:, lo + nfreq:lo + 2 * nfreq] = c
        sin_next[:, lo:lo + nfreq] = -s
        sin_prev[:, lo + nfreq:lo + 2 * nfreq] = s
    tile = lambda x: jnp.asarray(np.tile(x, (1, 128 // SW_HEAD_DIM)))
    return tile(cos), tile(sin_next), tile(sin_prev)


def _rope(x, cos, sin_next, sin_prev):
    nfreq = SW_HEAD_DIM // 4
    return x * cos + pltpu.roll(x, 128 - nfreq, 1) * sin_next + pltpu.roll(x, nfreq, 1) * sin_prev


def _sw_kernel(sink_ref, q_ref, k_ref, v_ref, kc_ref, vc_ref, cq_ref, snq_ref, spq_ref, ck_ref, snk_ref, spk_ref,
               o_ref, kr_scr, *, seq):
    n = pl.program_id(1)

    @pl.when(n == 0)
    def _():
        kr_scr[...] = _rope(k_ref[...], ck_ref[...], snk_ref[...], spk_ref[...]).astype(BF16)

    nwin = 3 * SW_BLOCK
    kstart = pl.multiple_of(jnp.clip((n - 1) * SW_BLOCK, 0, seq - nwin), SW_BLOCK)
    cq, snq, spq = cq_ref[...], snq_ref[...], spq_ref[...]
    q = jnp.concatenate([_rope(q_ref[:, c * 128:(c + 1) * 128], cq, snq, spq) for c in range(B_Q_DIM // 128)],
                        axis=1).astype(BF16)
    rows = SW_GROUP * SW_BLOCK
    qpos = n * SW_BLOCK + (lax.broadcasted_iota(jnp.int32, (rows, nwin), 0) & (SW_BLOCK - 1))
    kpos = kstart + lax.broadcasted_iota(jnp.int32, (rows, nwin), 1)
    band = jnp.abs(qpos - kpos) <= SW_WINDOW
    for hk in range(SW_KV_HEADS):
        ksl = slice(hk * SW_HEAD_DIM, (hk + 1) * SW_HEAD_DIM)
        qs = jnp.concatenate(
            [q[:, (hk * SW_GROUP + g) * SW_HEAD_DIM:(hk * SW_GROUP + g + 1) * SW_HEAD_DIM] for g in range(SW_GROUP)], axis=0)
        o = _joint_attention(
            qs,
            [(kr_scr[pl.ds(kstart, nwin), ksl], v_ref[pl.ds(kstart, nwin), ksl].astype(BF16), None, band),
             (kc_ref[:, ksl].astype(BF16), vc_ref[:, ksl].astype(BF16), None, None)],
            SW_HEAD_DIM ** -0.5, sink=_sink_column(sink_ref, hk, SW_BLOCK))
        c0 = hk * SW_GROUP * SW_HEAD_DIM
        o_ref[:, c0:c0 + SW_GROUP * SW_HEAD_DIM] = _unstack_heads(o, SW_BLOCK)


def _sw_latent(p, cache_k, cache_v, sink_l, layer, seq):
    n = p.shape[0]
    nb = seq // SW_BLOCK
    past = cache_k.shape[2]
    cos, sin_next, sin_prev = _rope_tables(seq)
    ctx_spec = pl.BlockSpec((None, None, past, B_KV_DIM), lambda b, i: (b, layer, 0, 0))
    tab_q = pl.BlockSpec((SW_BLOCK, 128), lambda b, i: (i, 0))
    tab_k = pl.BlockSpec((seq, 128), lambda b, i: (0, 0))
    return pl.pallas_call(
        functools.partial(_sw_kernel, seq=seq),
        grid=(n // seq, nb),
        in_specs=[pl.BlockSpec(memory_space=pltpu.SMEM),
                  pl.BlockSpec((SW_BLOCK, B_Q_DIM), lambda b, i: (b * nb + i, COL_QA + 3)),
                  pl.BlockSpec((seq, B_KV_DIM), lambda b, i: (b, COL_KB)),
                  pl.BlockSpec((seq, B_KV_DIM), lambda b, i: (b, COL_KB + 1)),
                  ctx_spec, ctx_spec, tab_q, tab_q, tab_q, tab_k, tab_k, tab_k],
        out_specs=pl.BlockSpec((SW_BLOCK, B_Q_DIM), lambda b, i: (b * nb + i, 0)),
        out_shape=jax.ShapeDtypeStruct((n, B_Q_DIM), BF16),
        scratch_shapes=[pltpu.VMEM((seq, B_KV_DIM), BF16)],
        compiler_params=_params(("parallel", "arbitrary"), 16 * MIB),
        name="sw_latent",
    )(sink_l, p, p, p, cache_k, cache_v, cos, sin_next, sin_prev, cos, sin_next, sin_prev)


def _pool_kernel(u0_ref, u1_ref, u2_ref, u3_ref, pw_ref, ps_ref, o_ref):
    for gi, u_ref in enumerate((u0_ref, u1_ref, u2_ref, u3_ref)):
        sl = slice(gi * POOL_GROUP_DIM, (gi + 1) * POOL_GROUP_DIM)
        o_ref[:, sl] = _pool_group(u_ref[...], POOL_WINDOWS[gi], pw_ref[gi].astype(BF16), ps_ref[:, sl]).astype(BF16)


def _pool_latent(p, pool_w, pool_scale, layer, seq):
    n = p.shape[0]
    return pl.pallas_call(
        _pool_kernel,
        grid=(n // seq,),
        in_specs=[pl.BlockSpec((seq, POOL_GROUP_DIM), lambda b, g=g: (b, COL_U + g)) for g in range(POOL_GROUPS)]
        + [pl.BlockSpec((None, POOL_GROUPS, POOL_GROUP_DIM, POOL_GROUP_DIM), lambda b: (layer, 0, 0, 0)),
           pl.BlockSpec((None, 1, POOL_DIM), lambda b: (layer, 0, 0))],
        out_specs=pl.BlockSpec((seq, POOL_DIM), lambda b: (b, 0)),
        out_shape=jax.ShapeDtypeStruct((n, POOL_DIM), BF16),
        compiler_params=_params(("parallel",), 16 * MIB),
        name="pool_latent",
    )(p, p, p, p, pool_w, pool_scale)


def _split_bf16(x):
    hi = x.astype(BF16)
    return hi, (x - hi.astype(F32)).astype(BF16)


def _branch_merge_kernel(oa_ref, ob_ref, oc_ref, ga_ref, gb_ref, gc_ref, wa_ref, wb_ref, wc_ref, m_ref):
    s = pl.program_id(1)
    for t in range(BMERGE_NA):
        @pl.when(s == t)
        def _(t=t):
            cols = slice(t * BMERGE_CA, (t + 1) * BMERGE_CA)
            m = (ga_ref[...].astype(F32) * _dot(oa_ref[...], wa_ref[:, cols])
                 + gb_ref[...].astype(F32) * _dot(ob_ref[...], wb_ref[:, cols])
                 + gc_ref[...].astype(F32) * _dot(oc_ref[...], wc_ref[:, cols]))
            m_ref[...] = m.astype(BF16)


def _branch_merge(branches, gates, wa, wb, wc, layer):
    n = gates.shape[0]
    tm, ca = BMERGE_TM, BMERGE_CA
    gate = lambda k: pl.BlockSpec((pl.Element(tm), pl.Element(ca)),
                                  lambda i, s, k=k: (i * tm, pl.multiple_of(GATES_COL0 + k * D_MODEL + s * ca, LANES)))
    branch = lambda col: pl.BlockSpec((tm, A_DIM), lambda i, s: (i, col))
    resident = lambda rows: pl.BlockSpec((None, rows, D_MODEL), lambda i, s: (layer, 0, 0),
                                         pipeline_mode=pl.Buffered(1))
    oa, ob, oc = branches
    est = (2 * 3 * tm * A_DIM * 2 + 2 * 3 * tm * ca * 2 + 2 * tm * ca * 2 + 3 * A_DIM * D_MODEL * 2 + 8 * tm * ca * 4)
    return pl.pallas_call(
        _branch_merge_kernel,
        grid=(n // tm, BMERGE_NA),
        in_specs=[branch(oa[1]), branch(ob[1]), branch(oc[1]), gate(0), gate(1), gate(2),
                  resident(A_DIM), resident(B_Q_DIM), resident(POOL_DIM)],
        out_specs=pl.BlockSpec((tm, ca), lambda i, s: (i, s)),
        out_shape=jax.ShapeDtypeStruct((n, D_MODEL), BF16),
        compiler_params=_params(("parallel", "arbitrary"), est),
        name="branch_merge",
    )(oa[0], ob[0], oc[0], gates, gates, gates, wa, wb, wc)


def _outproj_kernel(m_ref, x_ref, mod_ref, nw_ref, wo_ref, wr_ref, x1_ref, h2_ref, lg_ref, y_scr):
    tiles = [slice(c * OUTPROJ_CB, (c + 1) * OUTPROJ_CB) for c in range(D_MODEL // OUTPROJ_CB)]
    rows = x_ref.shape[0]
    ss = jnp.zeros((rows, 1), F32)
    for cols in tiles:
        y = _dot(m_ref[...], wo_ref[:, cols])
        y_scr[:, cols] = y
        ss = ss + (y * y).sum(axis=-1, keepdims=True)
    r1 = lax.rsqrt(ss / D_MODEL + RMS_EPS)
    gain1 = mod_ref[2:3, :] * nw_ref[1:2, :]
    gain2 = nw_ref[2:3, :] * (1.0 + mod_ref[4:5, :])
    ss = jnp.zeros((rows, 1), F32)
    for cols in tiles:
        x1 = x_ref[:, cols] + (y_scr[:, cols] * r1) * gain1[:, cols]
        x1_ref[:, cols] = x1
        ss = ss + (x1 * x1).sum(axis=-1, keepdims=True)
    r2 = lax.rsqrt(ss / D_MODEL + RMS_EPS)
    lg = jnp.zeros(lg_ref.shape, F32)
    for cols in tiles:
        h2 = (x1_ref[:, cols] * r2) * gain2[:, cols] + mod_ref[3:4, cols]
        h2_ref[:, cols] = h2.astype(BF16)
        h_hi, h_lo = _split_bf16(h2)
        w_hi, w_lo = _split_bf16(wr_ref[:, cols])
        lg = lg + (_dot_nt(w_hi, h_hi) + (_dot_nt(w_hi, h_lo) + _dot_nt(w_lo, h_hi)))
    lg_ref[...] = lg


def _outproj(m, x, mod, norm_w, wo, wr_t, layer, group_of_tile):
    n = x.shape[0]
    tm = OUTPROJ_TM
    row = lambda: pl.BlockSpec((tm, D_MODEL), lambda i: (i, 0))
    est = (2 * tm * D_MODEL * (2 + 4 + 4 + 2) + D_MODEL * D_MODEL * 2 + tm * D_MODEL * 4 + 4 * tm * OUTPROJ_CB * 4)
    return pl.pallas_call(
        _outproj_kernel,
        grid=(n // tm,),
        in_specs=[row(), row(),
                  pl.BlockSpec((None, None, 6, D_MODEL), lambda i: (layer, group_of_tile(i), 0, 0)),
                  pl.BlockSpec((None, 4, D_MODEL), lambda i: (layer, 0, 0)),
                  pl.BlockSpec((None, D_MODEL, D_MODEL), lambda i: (layer, 0, 0), pipeline_mode=pl.Buffered(1)),
                  pl.BlockSpec((None, N_EXPERTS, D_MODEL), lambda i: (layer, 0, 0))],
        out_specs=[row(), row(), pl.BlockSpec((N_EXPERTS, tm), lambda i: (0, i))],
        out_shape=[jax.ShapeDtypeStruct((n, D_MODEL), F32), jax.ShapeDtypeStruct((n, D_MODEL), BF16),
                   jax.ShapeDtypeStruct((N_EXPERTS, n), F32)],
        scratch_shapes=[pltpu.VMEM((tm, D_MODEL), F32)],
        compiler_params=_params(("parallel",), est),
        name="outproj",
    )(m, x, mod, norm_w, wo, wr_t)


RANK_TILE = LANES


def _dispatch_kernel(lg_ref, h_ref, xs_ref, gate_ref, rc_ref, aff_scr, sel_scr, *, seq, cap):
    t = RANK_TILE
    nt = seq // t
    lg = lg_ref[...]
    e = jnp.exp(lg - lg.max(axis=0, keepdims=True))
    aff_scr[...] = e / e.sum(axis=0, keepdims=True)
    aff = aff_scr[...]
    ident = jnp.where(lax.broadcasted_iota(jnp.int32, (t, t), 0) == lax.broadcasted_iota(jnp.int32, (t, t), 1),
                      1.0, 0.0).astype(BF16)

    def to_sublanes(pieces, r):
        tile = slice(r * t, (r + 1) * t)
        out = _dot_nt(ident, pieces[0][:, tile])
        for piece in pieces[1:]:
            out = out + _dot_nt(ident, piece[:, tile])
        return out

    a1 = aff.astype(BF16)
    r1 = aff - a1.astype(F32)
    a2 = r1.astype(BF16)
    a3 = (r1 - a2.astype(F32)).astype(BF16)
    aff_cols = [to_sublanes((a1, a2, a3), r) for r in range(nt)]

    earlier = lax.broadcasted_iota(jnp.int32, (t, t), 0) < lax.broadcasted_iota(jnp.int32, (t, t), 1)
    slot = lax.broadcasted_iota(jnp.int32, (cap, seq), 0).astype(F32)
    ranks = []
    for ex in range(N_EXPERTS):
        row = aff[ex:ex + 1, :]
        cols = [jnp.broadcast_to(aff_cols[r][:, ex:ex + 1], (t, t)) for r in range(nt)]
        counts = []
        for c in range(nt):
            rowb = jnp.broadcast_to(row[:, c * t:(c + 1) * t], (t, t))
            acc = jnp.zeros((t, t), F32)
            for r in range(nt):
                if r < c:
                    beats = cols[r] >= rowb
                elif r > c:
                    beats = cols[r] > rowb
                else:
                    beats = (cols[r] > rowb) | (earlier & (cols[r] == rowb))
                acc = acc + jnp.where(beats, 1.0, 0.0)
            counts.append(acc.sum(axis=0, keepdims=True))
        rank_row = jnp.concatenate(counts, axis=1) if nt > 1 else counts[0]
        ranks.append(rank_row)
        sel = slot == rank_row
        sel_scr[ex * cap:(ex + 1) * cap, :] = sel.astype(BF16)
        gate = jnp.where(sel, jnp.broadcast_to(row, (cap, seq)), 0.0).sum(axis=1, keepdims=True)
        gate_ref[ex] = jnp.broadcast_to(gate, (cap, LANES))
    rank = jnp.concatenate(ranks + [jnp.zeros((LANES - N_EXPERTS, seq), F32)], axis=0)
    rank = jnp.minimum(rank, float(cap)).astype(BF16)
    for r in range(nt):
        rc_ref[r * t:(r + 1) * t, :] = to_sublanes((rank,), r)
    xs = _dot(sel_scr[...], h_ref[...]).astype(BF16)
    xs_ref[...] = xs.reshape(N_EXPERTS, cap, D_MODEL)


def _dispatch(lg_t, h2, seq):
    n = h2.shape[0]
    nb = n // seq
    cap = EC_FACTOR * seq // N_EXPERTS
    est = (2 * seq * D_MODEL * 2 + 2 * N_EXPERTS * cap * D_MODEL * 2 + N_EXPERTS * cap * seq * 2
           + N_EXPERTS * cap * D_MODEL * 4 + 8 * seq * LANES * 4)
    return pl.pallas_call(
        functools.partial(_dispatch_kernel, seq=seq, cap=cap),
        grid=(nb,),
        in_specs=[pl.BlockSpec((N_EXPERTS, seq), lambda b: (0, b)),
                  pl.BlockSpec((seq, D_MODEL), lambda b: (b, 0))],
        out_specs=[pl.BlockSpec((N_EXPERTS, cap, D_MODEL), lambda b: (0, b, 0)),
                   pl.BlockSpec((N_EXPERTS, cap, LANES), lambda b: (0, b, 0)),
                   pl.BlockSpec((seq, LANES), lambda b: (b, 0))],
        out_shape=[jax.ShapeDtypeStruct((N_EXPERTS, nb * cap, D_MODEL), BF16),
                   jax.ShapeDtypeStruct((N_EXPERTS, nb * cap, LANES), F32),
                   jax.ShapeDtypeStruct((n, LANES), F32)],
        scratch_shapes=[pltpu.VMEM((N_EXPERTS, seq), F32), pltpu.VMEM((N_EXPERTS * cap, seq), BF16)],
        compiler_params=_params(("parallel",), est),
        name="dispatch",
    )(lg_t, h2)


EXPERT_NF = D_EXPERT // EXPERT_TF
EXPERT_ND = D_MODEL // EXPERT_TD


assert EXPERT_NF == EXPERT_ND


def _expert_kernel(xc_ref, xl_ref, gc_ref, gl_ref, wg_ref, wu_ref, wd_ref, yc_ref, yl_ref, hc_scr, hl_scr):
    e = pl.program_id(0)
    t = pl.program_id(1)
    cur = e % 2

    @pl.when(e >= 1)
    def _():
        wd = wd_ref[0].astype(BF16)
        for h_scr, g_ref, y_ref in ((hc_scr, gc_ref, yc_ref), (hl_scr, gl_ref, yl_ref)):
            acc = _dot(h_scr[1 - cur, 0], wd[0:EXPERT_TF])
            for f in range(1, EXPERT_NF):
                acc = acc + _dot(h_scr[1 - cur, f], wd[f * EXPERT_TF:(f + 1) * EXPERT_TF])
            y_ref[0] = (acc * g_ref[0, :, 0:1]).astype(BF16)

    @pl.when(e < N_EXPERTS)
    def _():
        wg = wg_ref[0].astype(BF16)
        wu = wu_ref[0].astype(BF16)
        for x_ref, h_scr in ((xc_ref, hc_scr), (xl_ref, hl_scr)):
            x = x_ref[0]
            a = _dot(x, wg)
            h_scr[cur, t] = ((a * jax.nn.sigmoid(a)) * _dot(x, wu)).astype(BF16)


def _experts(xs_c, xs_l, gate_c, gate_l, w_gate, w_up, w_down, layer):
    sc, sl = xs_c.shape[1], xs_l.shape[1]
    tf, td = EXPERT_TF, EXPERT_TD
    last = N_EXPERTS - 1
    up_expert = lambda e: jnp.minimum(e, last)
    down_expert = lambda e: jnp.maximum(e - 1, 0)
    up_tile = lambda e, t: (layer, up_expert(e), 0, jnp.where(e > last, EXPERT_NF - 1, t))
    down_tile = lambda e, t: (down_expert(e), 0, jnp.where(e == 0, 0, t))
    x_spec = lambda s: pl.BlockSpec((1, s, D_MODEL), lambda e, t: (up_expert(e), 0, 0))
    g_spec = lambda s: pl.BlockSpec((1, s, LANES), lambda e, t: (down_expert(e), 0, 0))
    est = (2 * (sc + sl) * D_MODEL * 2 + 2 * 2 * D_MODEL * tf * 4 + 2 * D_EXPERT * td * 4 + 2 * (sc + sl) * td * 4
           + 2 * (sc + sl) * D_EXPERT * 2 + 2 * D_MODEL * tf * 2 + D_EXPERT * td * 2 + 6 * sc * max(tf, td) * 4)
    return pl.pallas_call(
        _expert_kernel,
        grid=(N_EXPERTS + 1, EXPERT_NF),
        in_specs=[x_spec(sc), x_spec(sl), g_spec(sc), g_spec(sl),
                  pl.BlockSpec((None, 1, D_MODEL, tf), up_tile),
                  pl.BlockSpec((None, 1, D_MODEL, tf), up_tile),
                  pl.BlockSpec((None, 1, D_EXPERT, td), lambda e, t: (layer,) + down_tile(e, t))],
        out_specs=[pl.BlockSpec((1, sc, td), down_tile), pl.BlockSpec((1, sl, td), down_tile)],
        out_shape=[jax.ShapeDtypeStruct((N_EXPERTS, sc, D_MODEL), BF16),
                   jax.ShapeDtypeStruct((N_EXPERTS, sl, D_MODEL), BF16)],
        scratch_shapes=[pltpu.VMEM((2, EXPERT_NF, sc, tf), BF16), pltpu.VMEM((2, EXPERT_NF, sl, tf), BF16)],
        compiler_params=_params(("arbitrary", "arbitrary"), est),
        name="experts",
    )(xs_c, xs_l, gate_c, gate_l, w_gate, w_up, w_down)


def _combine_kernel(rc_ref, ye_ref, x1_ref, mod_ref, nw_ref, o_ref, *, cap):
    nslots = N_EXPERTS * cap
    shift = cap.bit_length() - 1
    expert_of_slot = lax.broadcasted_iota(jnp.int32, (LANES, nslots), 1) >> shift
    spread = jnp.where(lax.broadcasted_iota(jnp.int32, (LANES, nslots), 0) == expert_of_slot, 1.0, 0.0).astype(BF16)
    rank_of_slot = _dot(rc_ref[...].astype(BF16), spread)
    slot = (lax.broadcasted_iota(jnp.int32, (1, nslots), 1) & (cap - 1)).astype(F32)
    onehot = jnp.where(rank_of_slot == slot, 1.0, 0.0).astype(BF16)
    ffn = _dot(onehot, ye_ref[...].reshape(nslots, D_MODEL))
    o_ref[...] = x1_ref[...] + mod_ref[5:6, :] * _rms(ffn, nw_ref[3:4, :])


def _combine(rank_col, ye, x1, mod, norm_w, layer, seq, group_of_batch):
    n = x1.shape[0]
    cap = EC_FACTOR * seq // N_EXPERTS
    assert cap & (cap - 1) == 0 and cap <= LANES
    tr = COMBINE_TR
    per = seq // tr
    est = (2 * N_EXPERTS * cap * D_MODEL * 2 + 4 * tr * D_MODEL * 4 + (tr + LANES) * N_EXPERTS * cap * 6
           + 3 * tr * D_MODEL * 4)
    return pl.pallas_call(
        functools.partial(_combine_kernel, cap=cap),
        grid=(n // seq, per),
        in_specs=[pl.BlockSpec((tr, LANES), lambda b, i: (b * per + i, 0)),
                  pl.BlockSpec((N_EXPERTS, cap, D_MODEL), lambda b, i: (0, b, 0)),
                  pl.BlockSpec((tr, D_MODEL), lambda b, i: (b * per + i, 0)),
                  pl.BlockSpec((None, None, 6, D_MODEL), lambda b, i: (layer, group_of_batch(b), 0, 0)),
                  pl.BlockSpec((None, 4, D_MODEL), lambda b, i: (layer, 0, 0))],
        out_specs=pl.BlockSpec((tr, D_MODEL), lambda b, i: (b * per + i, 0)),
        out_shape=jax.ShapeDtypeStruct((n, D_MODEL), F32),
        compiler_params=_params(("parallel", "arbitrary"), est),
        name="combine",
    )(rank_col, ye, x1, mod, norm_w)


def kernel(x_prompt, x_sample, c, cache_a_k, cache_a_v, cache_b_k, cache_b_v, c_ctx, norm_w, w_ada, b_ada, w_in, a_rpb,
           b_sink, c_pool_w, c_scale, w_branch_a, w_branch_b, w_branch_c, w_out, w_router, w_gate_e, w_up_e, w_down_e):
    batch, seq_c, _ = x_prompt.shape
    dec_batch, seq_l, _ = x_sample.shape
    past = cache_a_k.shape[2]

    cond = jnp.zeros((ADA_ROWS, D_MODEL), F32).at[0].set(c_ctx).at[1:1 + dec_batch].set(c)
    mod = _adaln(cond, w_ada, b_ada).reshape(DEPTH, ADA_ROWS, 6, D_MODEL)

    ctx_group = lambda i: 0
    lat_group_inproj = lambda i: 1 + i // (seq_l // INPROJ_TM)
    lat_group_outproj = lambda i: 1 + i // (seq_l // OUTPROJ_TM)
    lat_group_batch = lambda b: 1 + b

    cbk = cache_b_k.reshape(dec_batch, DEPTH, past, B_KV_DIM)
    cbv = cache_b_v.reshape(dec_batch, DEPTH, past, B_KV_DIM)

    x_c = x_prompt.reshape(batch * seq_c, D_MODEL)
    x_l = x_sample.reshape(dec_batch * seq_l, D_MODEL)
    w_in_bf = w_in.astype(BF16)
    wa, wb, wc, wo = (w.astype(BF16) for w in (w_branch_a, w_branch_b, w_branch_c, w_out))
    wr_t = jnp.swapaxes(w_router, 1, 2)
    pool_scale = c_scale.reshape(DEPTH, 1, POOL_DIM)
    caches = ()
    for l in range(DEPTH):
        p_c, g_c = _inproj(x_c, mod, norm_w, w_in_bf, l, ctx_group)
        o_c, caches = _ctx_mix(p_c, b_sink[l], c_pool_w, pool_scale, l, seq_c, caches)
        branches_c = tuple((o_c, k) for k in range(N_BRANCH))
        m_c = _branch_merge(branches_c, g_c, wa, wb, wc, l)
        x1_c, h2_c, lg_c = _outproj(m_c, x_c, mod, norm_w, wo, wr_t, l, ctx_group)
        xs_c, gate_c, rc_c = _dispatch(lg_c, h2_c, seq_c)

        p_l, g_l = _inproj(x_l, mod, norm_w, w_in_bf, l, lat_group_inproj)
        table = _rpb_table(a_rpb[l])
        o_a = _na_latent(p_l, cache_a_k, cache_a_v, table, l, seq_l)
        o_b = _sw_latent(p_l, cbk, cbv, b_sink[l], l, seq_l)
        o_p = _pool_latent(p_l, c_pool_w, pool_scale, l, seq_l)
        m_l = _branch_merge(((o_a, 0), (o_b, 0), (o_p, 0)), g_l, wa, wb, wc, l)
        x1_l, h2_l, lg_l = _outproj(m_l, x_l, mod, norm_w, wo, wr_t, l, lat_group_outproj)
        xs_l, gate_l, rc_l = _dispatch(lg_l, h2_l, seq_l)

        ye_c, ye_l = _experts(xs_c, xs_l, gate_c, gate_l, w_gate_e, w_up_e, w_down_e, l)
        x_c = _combine(rc_c, ye_c, x1_c, mod, norm_w, l, seq_c, ctx_group)
        x_l = _combine(rc_l, ye_l, x1_l, mod, norm_w, l, seq_l, lat_group_batch)

    y_prompt = x_c.reshape(batch, seq_c, D_MODEL)
    y_sample = x_l.reshape(dec_batch, seq_l, D_MODEL)
    new_a_k, new_a_v, new_b_k, new_b_v = caches
    a_shape = (batch, DEPTH, seq_c, NA_HEADS, NA_HEAD_DIM)
    b_shape = (batch, DEPTH, seq_c, SW_KV_HEADS, SW_HEAD_DIM)
    return (y_prompt, y_sample, new_a_k.reshape(a_shape), new_a_v.reshape(a_shape),
            new_b_k.reshape(b_shape), new_b_v.reshape(b_shape))
```

```python
import functools

import numpy as np
import jax
import jax.numpy as jnp
from jax import lax
from jax.experimental import pallas as pl
from jax.experimental.pallas import tpu as pltpu

F32 = jnp.float32
BF16 = jnp.bfloat16

D_MODEL = 2048
DEPTH = 2
GRID_W = 64
NA_HEADS, NA_HEAD_DIM, NA_MAX_KH, NA_KW = 4, 128, 8, 16
SW_Q_HEADS, SW_KV_HEADS, SW_HEAD_DIM = 8, 2, 64
SW_GROUP = SW_Q_HEADS // SW_KV_HEADS
SW_WINDOW, SW_BLOCK = 128, 128
ROPE_THETA = 10000.0
POOL_WINDOWS = (2, 4, 8, 16)
POOL_GROUPS, POOL_GROUP_DIM = 4, 128
POOL_DIM = POOL_GROUPS * POOL_GROUP_DIM
A_DIM = NA_HEADS * NA_HEAD_DIM
B_Q_DIM = SW_Q_HEADS * SW_HEAD_DIM
B_KV_DIM = SW_KV_HEADS * SW_HEAD_DIM
N_BRANCH = 3
GATE_DIM = N_BRANCH * D_MODEL
QKVU_DIM = 3 * A_DIM + B_Q_DIM + 2 * B_KV_DIM + POOL_DIM
IN_DIM = QKVU_DIM + GATE_DIM
N_EXPERTS = 16
EC_FACTOR = 2
D_EXPERT = 1024
RMS_EPS = 1e-6
NEG_INF = -1e30

COL_QA = 0
COL_KB = (3 * A_DIM + B_Q_DIM) // B_KV_DIM
COL_U = COL_KB + 2

MIB = 1024 * 1024
V7X_VMEM_BYTES = 64 * MIB
V7X_VMEM_CEILING = 60000 * 1024
VMEM_FLOOR = 32 * MIB
VMEM_COMPILER_SCRATCH = 8 * MIB
assert V7X_VMEM_CEILING < V7X_VMEM_BYTES

ADA_ROWS = 16
ADA_TN = 1024
V7X_MXU_DIM = 256
LANES = 128
INPROJ_TM = 1024
INPROJ_TN = 1280
assert INPROJ_TN % V7X_MXU_DIM == 0 and IN_DIM % INPROJ_TN == 0
INPROJ_TILES = IN_DIM // INPROJ_TN
INPROJ_SPLIT = QKVU_DIM // INPROJ_TN
P_WIDTH = (INPROJ_SPLIT + 1) * INPROJ_TN
GATES_WIDTH = (INPROJ_TILES - INPROJ_SPLIT) * INPROJ_TN
GATES_COL0 = QKVU_DIM - INPROJ_SPLIT * INPROJ_TN
BMERGE_TM = 1024
BMERGE_CA = 1024
BMERGE_NA = D_MODEL // BMERGE_CA
OUTPROJ_TM = 512
OUTPROJ_CB = 512
EXPERT_TF = 256
EXPERT_TD = 512
COMBINE_TR = 256
ROW_CHUNK = 64


def _vmem_limit(estimate_bytes):
    return int(min(V7X_VMEM_CEILING, max(VMEM_FLOOR, estimate_bytes + VMEM_COMPILER_SCRATCH)))


def _params(semantics, estimate_bytes):
    return pltpu.CompilerParams(dimension_semantics=semantics,
                                vmem_limit_bytes=_vmem_limit(estimate_bytes))


def _rms(x, g):
    ms = jnp.mean(x * x, axis=-1, keepdims=True)
    return x * lax.rsqrt(ms + RMS_EPS) * g


def _sigmoid(x):
    return 0.5 * jnp.tanh(0.5 * x) + 0.5


def _dot(a, b):
    return jnp.dot(a, b, preferred_element_type=F32)


def _dot_nt(a, b):
    return lax.dot_general(a, b, (((1,), (1,)), ((), ())), preferred_element_type=F32)


def _adaln_kernel(c_ref, w_ref, b_ref, o_ref):
    c = c_ref[...]
    s = (c * jax.nn.sigmoid(c)).astype(BF16)
    o_ref[0] = _dot(s, w_ref[0].astype(BF16)) + b_ref[0]


def _adaln(cond, w_ada, b_ada):
    n_out = w_ada.shape[-1]
    return pl.pallas_call(
        _adaln_kernel,
        grid=(DEPTH, n_out // ADA_TN),
        in_specs=[pl.BlockSpec((ADA_ROWS, D_MODEL), lambda l, j: (0, 0)),
                  pl.BlockSpec((1, D_MODEL, ADA_TN), lambda l, j: (l, 0, j)),
                  pl.BlockSpec((1, 1, ADA_TN), lambda l, j: (l, 0, j))],
        out_specs=pl.BlockSpec((1, ADA_ROWS, ADA_TN), lambda l, j: (l, 0, j)),
        out_shape=jax.ShapeDtypeStruct((DEPTH, ADA_ROWS, n_out), F32),
        compiler_params=_params(("parallel", "parallel"), 2 * D_MODEL * ADA_TN * 4),
        name="adaln",
    )(cond, w_ada, b_ada.reshape(DEPTH, 1, n_out))


def _inproj_kernel(x_ref, mod_ref, nw_ref, w_ref, o_ref, gate_ref, h_scr):
    j = pl.program_id(1)

    @pl.when(j == 0)
    def _():
        g = nw_ref[0:1, :]
        sc = 1.0 + mod_ref[1:2, :]
        sh = mod_ref[0:1, :]

        def body(r, carry):
            rows = pl.ds(pl.multiple_of(r * ROW_CHUNK, ROW_CHUNK), ROW_CHUNK)
            h_scr[rows, :] = (_rms(x_ref[rows, :], g) * sc + sh).astype(BF16)
            return carry

        lax.fori_loop(0, INPROJ_TM // ROW_CHUNK, body, 0, unroll=4)

    @pl.when(j < INPROJ_SPLIT)
    def _():
        o_ref[...] = _dot(h_scr[...], w_ref[...])

    @pl.when(j == INPROJ_SPLIT)
    def _():
        acc = _dot(h_scr[...], w_ref[...])
        o_ref[...] = acc
        gate_ref[...] = _sigmoid(acc).astype(BF16)

    @pl.when(j > INPROJ_SPLIT)
    def _():
        gate_ref[...] = _sigmoid(_dot(h_scr[...], w_ref[...])).astype(BF16)


def _mod_spec(layer, group_of):
    return pl.BlockSpec((None, None, 6, D_MODEL), lambda i, j: (layer, group_of(i), 0, 0))


def _inproj(x, mod, norm_w, w_in_bf, layer, group_of_tile):
    n = x.shape[0]
    est = (2 * INPROJ_TM * D_MODEL * 4 + 2 * D_MODEL * INPROJ_TN * 2 + 2 * INPROJ_TM * INPROJ_TN * (4 + 2)
           + INPROJ_TM * D_MODEL * 2 + 2 * INPROJ_TM * INPROJ_TN * 4)
    return pl.pallas_call(
        _inproj_kernel,
        grid=(n // INPROJ_TM, INPROJ_TILES),
        in_specs=[pl.BlockSpec((INPROJ_TM, D_MODEL), lambda i, j: (i, 0)),
                  _mod_spec(layer, group_of_tile),
                  pl.BlockSpec((None, 4, D_MODEL), lambda i, j: (layer, 0, 0)),
                  pl.BlockSpec((None, D_MODEL, INPROJ_TN), lambda i, j: (layer, 0, j))],
        out_specs=[pl.BlockSpec((INPROJ_TM, INPROJ_TN), lambda i, j: (i, jnp.minimum(j, INPROJ_SPLIT))),
                   pl.BlockSpec((INPROJ_TM, INPROJ_TN), lambda i, j: (i, jnp.maximum(j - INPROJ_SPLIT, 0)))],
        out_shape=[jax.ShapeDtypeStruct((n, P_WIDTH), F32), jax.ShapeDtypeStruct((n, GATES_WIDTH), BF16)],
        scratch_shapes=[pltpu.VMEM((INPROJ_TM, D_MODEL), BF16)],
        compiler_params=_params(("parallel", "arbitrary"), est),
        name="inproj",
    )(x, mod, norm_w, w_in_bf)


def _joint_attention(q, segments, scale, sink=None):
    scores = []
    for k, _, bias, mask in segments:
        s = _dot_nt(q, k) * scale
        if bias is not None:
            s = s + bias
        if mask is not None:
            s = jnp.where(mask, s, NEG_INF)
        scores.append(s)
    m = scores[0].max(axis=-1, keepdims=True)
    for s in scores[1:]:
        m = jnp.maximum(m, s.max(axis=-1, keepdims=True))
    if sink is not None:
        m = jnp.maximum(m, sink)
    denom = jnp.exp(sink - m) if sink is not None else 0.0
    acc = None
    for s, (_, v, _, _) in zip(scores, segments):
        e = jnp.exp(s - m)
        denom = denom + e.sum(axis=-1, keepdims=True)
        pv = _dot(e.astype(BF16), v)
        acc = pv if acc is None else acc + pv
    return acc / denom


def _pool_group(u, window, pw_bf, scale_row):
    seq = u.shape[0]
    pad = 8
    n = seq + 2 * pad
    z = jnp.zeros((pad, POOL_GROUP_DIM), F32)
    p = jnp.concatenate([z, u, z], axis=0)
    k = 1
    while k < window:
        p = p + pltpu.roll(p, n - k, 0)
        k *= 2
    win = pltpu.roll(p, window // 2, 0)[pad:pad + seq]
    t = lax.broadcasted_iota(jnp.int32, (seq, 1), 0)
    lo = jnp.maximum(t - window // 2, 0)
    hi = jnp.minimum(t - window // 2 + window, seq)
    cnt = (hi - lo).astype(F32)
    pooled = win / cnt - u
    return _dot(pooled.astype(BF16), pw_bf) * scale_row


def _unstack_heads(o, rows_per_head):
    return jnp.concatenate([o[g * rows_per_head:(g + 1) * rows_per_head] for g in range(SW_GROUP)], axis=1).astype(BF16)


def _sink_column(sink_ref, kv_head, rows_per_head):
    r = lax.broadcasted_iota(jnp.int32, (SW_GROUP * rows_per_head, 1), 0)
    col = jnp.full((SW_GROUP * rows_per_head, 1), sink_ref[kv_head * SW_GROUP], F32)
    for g in range(1, SW_GROUP):
        col = jnp.where(r >= g * rows_per_head, sink_ref[kv_head * SW_GROUP + g], col)
    return col


def _ctx_mix_kernel(sink_ref, qa_ref, ka_ref, va_ref, qb_ref, kb_ref, vb_ref, u0_ref, u1_ref, u2_ref, u3_ref,
                    pw_ref, ps_ref, *rest, layer):
    o_ref = rest[-5]
    seq = qa_ref.shape[0]

    def put(cache_ref, lead, val):
        if cache_ref.shape[-2] == seq:
            cache_ref[lead + (slice(None), slice(None))] = val
        else:
            for h in range(NA_HEADS):
                cache_ref[lead + (pl.ds(h, seq, stride=NA_HEADS), slice(None))] = \
                    val[:, h * NA_HEAD_DIM:(h + 1) * NA_HEAD_DIM]

    for cache_ref, src_ref in zip(rest[-4:], (ka_ref, va_ref, kb_ref, vb_ref)):
        if len(cache_ref.shape) == 2:
            put(cache_ref, (), src_ref[...])
        else:
            for d in range(cache_ref.shape[0]):
                put(cache_ref, (d,), src_ref[...] if d == layer else jnp.zeros(src_ref.shape, F32))
    for h in range(NA_HEADS):
        sl = slice(h * NA_HEAD_DIM, (h + 1) * NA_HEAD_DIM)
        o = _joint_attention(qa_ref[:, sl].astype(BF16),
                             [(ka_ref[:, sl].astype(BF16), va_ref[:, sl].astype(BF16), None, None)],
                             NA_HEAD_DIM ** -0.5)
        o_ref[:, sl] = o.astype(BF16)
    for hk in range(SW_KV_HEADS):
        ksl = slice(hk * SW_HEAD_DIM, (hk + 1) * SW_HEAD_DIM)
        q = jnp.concatenate(
            [qb_ref[:, (hk * SW_GROUP + g) * SW_HEAD_DIM:(hk * SW_GROUP + g + 1) * SW_HEAD_DIM] for g in range(SW_GROUP)],
            axis=0).astype(BF16)
        o = _joint_attention(q, [(kb_ref[:, ksl].astype(BF16), vb_ref[:, ksl].astype(BF16), None, None)],
                             SW_HEAD_DIM ** -0.5, sink=_sink_column(sink_ref, hk, seq))
        c0 = A_DIM + hk * SW_GROUP * SW_HEAD_DIM
        o_ref[:, c0:c0 + SW_GROUP * SW_HEAD_DIM] = _unstack_heads(o, seq)
    for gi, u_ref in enumerate((u0_ref, u1_ref, u2_ref, u3_ref)):
        c0 = A_DIM + B_Q_DIM + gi * POOL_GROUP_DIM
        o_ref[:, c0:c0 + POOL_GROUP_DIM] = _pool_group(
            u_ref[...], POOL_WINDOWS[gi], pw_ref[gi].astype(BF16),
            ps_ref[:, gi * POOL_GROUP_DIM:(gi + 1) * POOL_GROUP_DIM]).astype(BF16)


def _ctx_mix(p, sink_l, pool_w, pool_scale, layer, seq, caches):
    n = p.shape[0]
    batch = n // seq
    wide = lambda c: pl.BlockSpec((seq, A_DIM), lambda b, c=c: (b, c))
    narrow = lambda c: pl.BlockSpec((seq, B_KV_DIM), lambda b, c=c: (b, c))
    in_specs = [pl.BlockSpec(memory_space=pltpu.SMEM)]
    in_specs += [wide(COL_QA + i) for i in range(4)]
    in_specs += [narrow(COL_KB), narrow(COL_KB + 1)]
    in_specs += [narrow(COL_U + g) for g in range(POOL_GROUPS)]
    in_specs += [pl.BlockSpec((None, POOL_GROUPS, POOL_GROUP_DIM, POOL_GROUP_DIM), lambda b: (layer, 0, 0, 0)),
                 pl.BlockSpec((None, 1, POOL_DIM), lambda b: (layer, 0, 0))]
    n_fixed = len(in_specs)
    in_specs += [pl.BlockSpec(memory_space=pl.ANY)] * len(caches)
    if caches:
        cache_spec = lambda rows, width: pl.BlockSpec((None, None, rows, width), lambda b: (b, layer, 0, 0))
    else:
        cache_spec = lambda rows, width: pl.BlockSpec((None, DEPTH, rows, width), lambda b: (b, 0, 0, 0))
    cache_shape = lambda rows, width: jax.ShapeDtypeStruct((batch, DEPTH, rows, width), F32)
    a_cache = (seq * NA_HEADS, NA_HEAD_DIM)
    b_cache = (seq, B_KV_DIM)
    outs = pl.pallas_call(
        functools.partial(_ctx_mix_kernel, layer=layer),
        grid=(batch,),
        in_specs=in_specs,
        out_specs=[pl.BlockSpec((seq, 3 * A_DIM), lambda b: (b, 0)),
                   cache_spec(*a_cache), cache_spec(*a_cache), cache_spec(*b_cache), cache_spec(*b_cache)],
        out_shape=[jax.ShapeDtypeStruct((n, 3 * A_DIM), BF16),
                   cache_shape(*a_cache), cache_shape(*a_cache), cache_shape(*b_cache), cache_shape(*b_cache)],
        input_output_aliases={n_fixed + k: 1 + k for k in range(len(caches))},
        compiler_params=_params(("parallel",), 24 * MIB),
        name="ctx_mix",
    )(sink_l, p, p, p, p, p, p, p, p, p, p, pool_w, pool_scale, *caches)
    return outs[0], tuple(outs[1:])


NA_PAIR_ROWS = 2 * NA_MAX_KH - 2


def _rpb_table_kernel(rpb_ref, t_ref):
    lane = lax.broadcasted_iota(jnp.int32, (GRID_W, 2 * GRID_W), 1)
    qc = lax.broadcasted_iota(jnp.int32, (GRID_W, 2 * GRID_W), 0)
    kc = lane & (GRID_W - 1)
    upper = lane >= GRID_W
    dcm = jnp.clip(kc - qc + NA_KW - 1, 0, 2 * NA_KW - 2)
    col0 = jnp.clip(qc - NA_KW // 2, 0, GRID_W - NA_KW)
    inside = (kc >= col0) & (kc < col0 + NA_KW)
    n_dc = 2 * NA_KW - 1
    n_dr = 2 * NA_MAX_KH - 1

    def body(i, carry):
        h = i // NA_PAIR_ROWS
        dr = i - h * NA_PAIR_ROWS
        base = (h * n_dr + dr) * n_dc
        acc = jnp.zeros((GRID_W, 2 * GRID_W), F32)
        for dc in range(n_dc):
            val = jnp.where(upper, rpb_ref[base + n_dc + dc], rpb_ref[base + dc])
            acc = jnp.where(dcm == dc, val, acc)
        t_ref[i] = jnp.where(inside, acc, NEG_INF)
        return carry

    lax.fori_loop(0, NA_HEADS * NA_PAIR_ROWS, body, 0)


def _rpb_table(rpb_l):
    return pl.pallas_call(
        _rpb_table_kernel,
        in_specs=[pl.BlockSpec(memory_space=pltpu.SMEM)],
        out_specs=pl.BlockSpec(memory_space=pltpu.VMEM),
        out_shape=jax.ShapeDtypeStruct((NA_HEADS * NA_PAIR_ROWS, GRID_W, 2 * GRID_W), F32),
        name="rpb_table",
    )(rpb_l.reshape(-1))


def _na_kernel(q_ref, k_ref, v_ref, kc_ref, vc_ref, t_ref, o_ref, kb_scr, vb_scr, kcb_scr, vcb_scr, *, rows):
    qr = pl.program_id(1)

    @pl.when(qr == 0)
    def _():
        kb_scr[...] = k_ref[...].astype(BF16)
        vb_scr[...] = v_ref[...].astype(BF16)
        for h in range(NA_HEADS):
            sl = slice(h * NA_HEAD_DIM, (h + 1) * NA_HEAD_DIM)
            kcb_scr[:, sl] = kc_ref[:, h, :].astype(BF16)
            vcb_scr[:, sl] = vc_ref[:, h, :].astype(BF16)

    row0 = jnp.clip(qr - NA_MAX_KH // 2, 0, rows - NA_MAX_KH)
    start = pl.multiple_of(row0 * GRID_W, GRID_W)
    nkeys = NA_MAX_KH * GRID_W
    d0 = row0 - qr + NA_MAX_KH - 1
    for h in range(NA_HEADS):
        sl = slice(h * NA_HEAD_DIM, (h + 1) * NA_HEAD_DIM)
        bias = jnp.concatenate([t_ref[h * NA_PAIR_ROWS + d0 + 2 * i] for i in range(NA_MAX_KH // 2)], axis=1)
        o = _joint_attention(
            q_ref[:, sl].astype(BF16),
            [(kb_scr[pl.ds(start, nkeys), sl], vb_scr[pl.ds(start, nkeys), sl], bias, None),
             (kcb_scr[:, sl], vcb_scr[:, sl], None, None)],
            NA_HEAD_DIM ** -0.5)
        o_ref[:, sl] = o.astype(BF16)


def _na_latent(p, cache_k, cache_v, table, layer, seq):
    n = p.shape[0]
    rows = seq // GRID_W
    past = cache_k.shape[2]
    ctx_spec = pl.BlockSpec((None, None, past, NA_HEADS, NA_HEAD_DIM), lambda b, r: (b, layer, 0, 0, 0))
    return pl.pallas_call(
        functools.partial(_na_kernel, rows=rows),
        grid=(n // seq, rows),
        in_specs=[pl.BlockSpec((GRID_W, A_DIM), lambda b, r: (b * rows + r, COL_QA)),
                  pl.BlockSpec((seq, A_DIM), lambda b, r: (b, COL_QA + 1)),
                  pl.BlockSpec((seq, A_DIM), lambda b, r: (b, COL_QA + 2)),
                  ctx_spec, ctx_spec,
                  pl.BlockSpec(table.shape, lambda b, r: (0, 0, 0))],
        out_specs=pl.BlockSpec((GRID_W, A_DIM), lambda b, r: (b * rows + r, 0)),
        out_shape=jax.ShapeDtypeStruct((n, A_DIM), BF16),
        scratch_shapes=[pltpu.VMEM((seq, A_DIM), BF16), pltpu.VMEM((seq, A_DIM), BF16),
                        pltpu.VMEM((past, A_DIM), BF16), pltpu.VMEM((past, A_DIM), BF16)],
        compiler_params=_params(("parallel", "arbitrary"), 16 * MIB),
        name="na_latent",
    )(p, p, p, cache_k, cache_v, table)


def _rope_tables(seq):
    nfreq = SW_HEAD_DIM // 4
    inv = 1.0 / (ROPE_THETA ** (np.arange(nfreq, dtype=np.float32) / np.float32(nfreq)))
    t = np.arange(seq)
    pos = (t // GRID_W, t % GRID_W)
    cos = np.zeros((seq, SW_HEAD_DIM), np.float32)
    sin_next = np.zeros((seq, SW_HEAD_DIM), np.float32)
    sin_prev = np.zeros((seq, SW_HEAD_DIM), np.float32)
    for a in range(2):
        ang = pos[a].astype(np.float32)[:, None] * inv[None, :].astype(np.float32)
        c, s = np.cos(ang).astype(np.float32), np.sin(ang).astype(np.float32)
        lo = 2 * a * nfreq
        cos[:, lo:lo + nfreq] = c
        cos[:, lo + nfreq:lo + 2 * nfreq] = c
        sin_next[:, lo:lo + nfreq] = -s
        sin_prev[:, lo + nfreq:lo + 2 * nfreq] = s
    tile = lambda x: jnp.asarray(np.tile(x, (1, 128 // SW_HEAD_DIM)))
    return tile(cos), tile(sin_next), tile(sin_prev)


def _rope(x, cos, sin_next, sin_prev):
    nfreq = SW_HEAD_DIM // 4
    return x * cos + pltpu.roll(x, 128 - nfreq, 1) * sin_next + pltpu.roll(x, nfreq, 1) * sin_prev


def _sw_kernel(sink_ref, q_ref, k_ref, v_ref, kc_ref, vc_ref, cq_ref, snq_ref, spq_ref, ck_ref, snk_ref, spk_ref,
               o_ref, kr_scr, *, seq):
    n = pl.program_id(1)

    @pl.when(n == 0)
    def _():
        kr_scr[...] = _rope(k_ref[...], ck_ref[...], snk_ref[...], spk_ref[...]).astype(BF16)

    nwin = 3 * SW_BLOCK
    kstart = pl.multiple_of(jnp.clip((n - 1) * SW_BLOCK, 0, seq - nwin), SW_BLOCK)
    cq, snq, spq = cq_ref[...], snq_ref[...], spq_ref[...]
    q = jnp.concatenate([_rope(q_ref[:, c * 128:(c + 1) * 128], cq, snq, spq) for c in range(B_Q_DIM // 128)],
                        axis=1).astype(BF16)
    rows = SW_GROUP * SW_BLOCK
    qpos = n * SW_BLOCK + (lax.broadcasted_iota(jnp.int32, (rows, nwin), 0) & (SW_BLOCK - 1))
    kpos = kstart + lax.broadcasted_iota(jnp.int32, (rows, nwin), 1)
    band = jnp.abs(qpos - kpos) <= SW_WINDOW
    for hk in range(SW_KV_HEADS):
        ksl = slice(hk * SW_HEAD_DIM, (hk + 1) * SW_HEAD_DIM)
        qs = jnp.concatenate(
            [q[:, (hk * SW_GROUP + g) * SW_HEAD_DIM:(hk * SW_GROUP + g + 1) * SW_HEAD_DIM] for g in range(SW_GROUP)], axis=0)
        o = _joint_attention(
            qs,
            [(kr_scr[pl.ds(kstart, nwin), ksl], v_ref[pl.ds(kstart, nwin), ksl].astype(BF16), None, band),
             (kc_ref[:, ksl].astype(BF16), vc_ref[:, ksl].astype(BF16), None, None)],
            SW_HEAD_DIM ** -0.5, sink=_sink_column(sink_ref, hk, SW_BLOCK))
        c0 = hk * SW_GROUP * SW_HEAD_DIM
        o_ref[:, c0:c0 + SW_GROUP * SW_HEAD_DIM] = _unstack_heads(o, SW_BLOCK)


def _sw_latent(p, cache_k, cache_v, sink_l, layer, seq):
    n = p.shape[0]
    nb = seq // SW_BLOCK
    past = cache_k.shape[2]
    cos, sin_next, sin_prev = _rope_tables(seq)
    ctx_spec = pl.BlockSpec((None, None, past, B_KV_DIM), lambda b, i: (b, layer, 0, 0))
    tab_q = pl.BlockSpec((SW_BLOCK, 128), lambda b, i: (i, 0))
    tab_k = pl.BlockSpec((seq, 128), lambda b, i: (0, 0))
    return pl.pallas_call(
        functools.partial(_sw_kernel, seq=seq),
        grid=(n // seq, nb),
        in_specs=[pl.BlockSpec(memory_space=pltpu.SMEM),
                  pl.BlockSpec((SW_BLOCK, B_Q_DIM), lambda b, i: (b * nb + i, COL_QA + 3)),
                  pl.BlockSpec((seq, B_KV_DIM), lambda b, i: (b, COL_KB)),
                  pl.BlockSpec((seq, B_KV_DIM), lambda b, i: (b, COL_KB + 1)),
                  ctx_spec, ctx_spec, tab_q, tab_q, tab_q, tab_k, tab_k, tab_k],
        out_specs=pl.BlockSpec((SW_BLOCK, B_Q_DIM), lambda b, i: (b * nb + i, 0)),
        out_shape=jax.ShapeDtypeStruct((n, B_Q_DIM), BF16),
        scratch_shapes=[pltpu.VMEM((seq, B_KV_DIM), BF16)],
        compiler_params=_params(("parallel", "arbitrary"), 16 * MIB),
        name="sw_latent",
    )(sink_l, p, p, p, cache_k, cache_v, cos, sin_next, sin_prev, cos, sin_next, sin_prev)


def _pool_kernel(u0_ref, u1_ref, u2_ref, u3_ref, pw_ref, ps_ref, o_ref):
    for gi, u_ref in enumerate((u0_ref, u1_ref, u2_ref, u3_ref)):
        sl = slice(gi * POOL_GROUP_DIM, (gi + 1) * POOL_GROUP_DIM)
        o_ref[:, sl] = _pool_group(u_ref[...], POOL_WINDOWS[gi], pw_ref[gi].astype(BF16), ps_ref[:, sl]).astype(BF16)


def _pool_latent(p, pool_w, pool_scale, layer, seq):
    n = p.shape[0]
    return pl.pallas_call(
        _pool_kernel,
        grid=(n // seq,),
        in_specs=[pl.BlockSpec((seq, POOL_GROUP_DIM), lambda b, g=g: (b, COL_U + g)) for g in range(POOL_GROUPS)]
        + [pl.BlockSpec((None, POOL_GROUPS, POOL_GROUP_DIM, POOL_GROUP_DIM), lambda b: (layer, 0, 0, 0)),
           pl.BlockSpec((None, 1, POOL_DIM), lambda b: (layer, 0, 0))],
        out_specs=pl.BlockSpec((seq, POOL_DIM), lambda b: (b, 0)),
        out_shape=jax.ShapeDtypeStruct((n, POOL_DIM), BF16),
        compiler_params=_params(("parallel",), 16 * MIB),
        name="pool_latent",
    )(p, p, p, p, pool_w, pool_scale)


def _split_bf16(x):
    hi = x.astype(BF16)
    return hi, (x - hi.astype(F32)).astype(BF16)


def _branch_merge_kernel(oa_ref, ob_ref, oc_ref, ga_ref, gb_ref, gc_ref, wa_ref, wb_ref, wc_ref, m_ref):
    s = pl.program_id(1)
    for t in range(BMERGE_NA):
        @pl.when(s == t)
        def _(t=t):
            cols = slice(t * BMERGE_CA, (t + 1) * BMERGE_CA)
            m = (ga_ref[...].astype(F32) * _dot(oa_ref[...], wa_ref[:, cols])
                 + gb_ref[...].astype(F32) * _dot(ob_ref[...], wb_ref[:, cols])
                 + gc_ref[...].astype(F32) * _dot(oc_ref[...], wc_ref[:, cols]))
            m_ref[...] = m.astype(BF16)


def _branch_merge(branches, gates, wa, wb, wc, layer):
    n = gates.shape[0]
    tm, ca = BMERGE_TM, BMERGE_CA
    gate = lambda k: pl.BlockSpec((pl.Element(tm), pl.Element(ca)),
                                  lambda i, s, k=k: (i * tm, pl.multiple_of(GATES_COL0 + k * D_MODEL + s * ca, LANES)))
    branch = lambda col: pl.BlockSpec((tm, A_DIM), lambda i, s: (i, col))
    resident = lambda rows: pl.BlockSpec((None, rows, D_MODEL), lambda i, s: (layer, 0, 0),
                                         pipeline_mode=pl.Buffered(1))
    oa, ob, oc = branches
    est = (2 * 3 * tm * A_DIM * 2 + 2 * 3 * tm * ca * 2 + 2 * tm * ca * 2 + 3 * A_DIM * D_MODEL * 2 + 8 * tm * ca * 4)
    return pl.pallas_call(
        _branch_merge_kernel,
        grid=(n // tm, BMERGE_NA),
        in_specs=[branch(oa[1]), branch(ob[1]), branch(oc[1]), gate(0), gate(1), gate(2),
                  resident(A_DIM), resident(B_Q_DIM), resident(POOL_DIM)],
        out_specs=pl.BlockSpec((tm, ca), lambda i, s: (i, s)),
        out_shape=jax.ShapeDtypeStruct((n, D_MODEL), BF16),
        compiler_params=_params(("parallel", "arbitrary"), est),
        name="branch_merge",
    )(oa[0], ob[0], oc[0], gates, gates, gates, wa, wb, wc)


def _outproj_kernel(m_ref, x_ref, mod_ref, nw_ref, wo_ref, wr_ref, x1_ref, h2_ref, lg_ref, y_scr):
    tiles = [slice(c * OUTPROJ_CB, (c + 1) * OUTPROJ_CB) for c in range(D_MODEL // OUTPROJ_CB)]
    rows = x_ref.shape[0]
    ss = jnp.zeros((rows, 1), F32)
    for cols in tiles:
        y = _dot(m_ref[...], wo_ref[:, cols])
        y_scr[:, cols] = y
        ss = ss + (y * y).sum(axis=-1, keepdims=True)
    r1 = lax.rsqrt(ss / D_MODEL + RMS_EPS)
    gain1 = mod_ref[2:3, :] * nw_ref[1:2, :]
    gain2 = nw_ref[2:3, :] * (1.0 + mod_ref[4:5, :])
    ss = jnp.zeros((rows, 1), F32)
    for cols in tiles:
        x1 = x_ref[:, cols] + (y_scr[:, cols] * r1) * gain1[:, cols]
        x1_ref[:, cols] = x1
        ss = ss + (x1 * x1).sum(axis=-1, keepdims=True)
    r2 = lax.rsqrt(ss / D_MODEL + RMS_EPS)
    lg = jnp.zeros(lg_ref.shape, F32)
    for cols in tiles:
        h2 = (x1_ref[:, cols] * r2) * gain2[:, cols] + mod_ref[3:4, cols]
        h2_ref[:, cols] = h2.astype(BF16)
        h_hi, h_lo = _split_bf16(h2)
        w_hi, w_lo = _split_bf16(wr_ref[:, cols])
        lg = lg + (_dot_nt(w_hi, h_hi) + (_dot_nt(w_hi, h_lo) + _dot_nt(w_lo, h_hi)))
    lg_ref[...] = lg


def _outproj(m, x, mod, norm_w, wo, wr_t, layer, group_of_tile):
    n = x.shape[0]
    tm = OUTPROJ_TM
    row = lambda: pl.BlockSpec((tm, D_MODEL), lambda i: (i, 0))
    est = (2 * tm * D_MODEL * (2 + 4 + 4 + 2) + D_MODEL * D_MODEL * 2 + tm * D_MODEL * 4 + 4 * tm * OUTPROJ_CB * 4)
    return pl.pallas_call(
        _outproj_kernel,
        grid=(n // tm,),
        in_specs=[row(), row(),
                  pl.BlockSpec((None, None, 6, D_MODEL), lambda i: (layer, group_of_tile(i), 0, 0)),
                  pl.BlockSpec((None, 4, D_MODEL), lambda i: (layer, 0, 0)),
                  pl.BlockSpec((None, D_MODEL, D_MODEL), lambda i: (layer, 0, 0), pipeline_mode=pl.Buffered(1)),
                  pl.BlockSpec((None, N_EXPERTS, D_MODEL), lambda i: (layer, 0, 0))],
        out_specs=[row(), row(), pl.BlockSpec((N_EXPERTS, tm), lambda i: (0, i))],
        out_shape=[jax.ShapeDtypeStruct((n, D_MODEL), F32), jax.ShapeDtypeStruct((n, D_MODEL), BF16),
                   jax.ShapeDtypeStruct((N_EXPERTS, n), F32)],
        scratch_shapes=[pltpu.VMEM((tm, D_MODEL), F32)],
        compiler_params=_params(("parallel",), est),
        name="outproj",
    )(m, x, mod, norm_w, wo, wr_t)


RANK_TILE = LANES


def _dispatch_kernel(lg_ref, h_ref, xs_ref, gate_ref, rc_ref, aff_scr, sel_scr, *, seq, cap):
    t = RANK_TILE
    nt = seq // t
    lg = lg_ref[...]
    e = jnp.exp(lg - lg.max(axis=0, keepdims=True))
    aff_scr[...] = e / e.sum(axis=0, keepdims=True)
    aff = aff_scr[...]
    ident = jnp.where(lax.broadcasted_iota(jnp.int32, (t, t), 0) == lax.broadcasted_iota(jnp.int32, (t, t), 1),
                      1.0, 0.0).astype(BF16)

    def to_sublanes(pieces, r):
        tile = slice(r * t, (r + 1) * t)
        out = _dot_nt(ident, pieces[0][:, tile])
        for piece in pieces[1:]:
            out = out + _dot_nt(ident, piece[:, tile])
        return out

    a1 = aff.astype(BF16)
    r1 = aff - a1.astype(F32)
    a2 = r1.astype(BF16)
    a3 = (r1 - a2.astype(F32)).astype(BF16)
    aff_cols = [to_sublanes((a1, a2, a3), r) for r in range(nt)]

    earlier = lax.broadcasted_iota(jnp.int32, (t, t), 0) < lax.broadcasted_iota(jnp.int32, (t, t), 1)
    slot = lax.broadcasted_iota(jnp.int32, (cap, seq), 0).astype(F32)
    ranks = []
    for ex in range(N_EXPERTS):
        row = aff[ex:ex + 1, :]
        cols = [jnp.broadcast_to(aff_cols[r][:, ex:ex + 1], (t, t)) for r in range(nt)]
        counts = []
        for c in range(nt):
            rowb = jnp.broadcast_to(row[:, c * t:(c + 1) * t], (t, t))
            acc = jnp.zeros((t, t), F32)
            for r in range(nt):
                if r < c:
                    beats = cols[r] >= rowb
                elif r > c:
                    beats = cols[r] > rowb
                else:
                    beats = (cols[r] > rowb) | (earlier & (cols[r] == rowb))
                acc = acc + jnp.where(beats, 1.0, 0.0)
            counts.append(acc.sum(axis=0, keepdims=True))
        rank_row = jnp.concatenate(counts, axis=1) if nt > 1 else counts[0]
        ranks.append(rank_row)
        sel = slot == rank_row
        sel_scr[ex * cap:(ex + 1) * cap, :] = sel.astype(BF16)
        gate = jnp.where(sel, jnp.broadcast_to(row, (cap, seq)), 0.0).sum(axis=1, keepdims=True)
        gate_ref[ex] = jnp.broadcast_to(gate, (cap, LANES))
    rank = jnp.concatenate(ranks + [jnp.zeros((LANES - N_EXPERTS, seq), F32)], axis=0)
    rank = jnp.minimum(rank, float(cap)).astype(BF16)
    for r in range(nt):
        rc_ref[r * t:(r + 1) * t, :] = to_sublanes((rank,), r)
    xs = _dot(sel_scr[...], h_ref[...]).astype(BF16)
    xs_ref[...] = xs.reshape(N_EXPERTS, cap, D_MODEL)


def _dispatch(lg_t, h2, seq):
    n = h2.shape[0]
    nb = n // seq
    cap = EC_FACTOR * seq // N_EXPERTS
    est = (2 * seq * D_MODEL * 2 + 2 * N_EXPERTS * cap * D_MODEL * 2 + N_EXPERTS * cap * seq * 2
           + N_EXPERTS * cap * D_MODEL * 4 + 8 * seq * LANES * 4)
    return pl.pallas_call(
        functools.partial(_dispatch_kernel, seq=seq, cap=cap),
        grid=(nb,),
        in_specs=[pl.BlockSpec((N_EXPERTS, seq), lambda b: (0, b)),
                  pl.BlockSpec((seq, D_MODEL), lambda b: (b, 0))],
        out_specs=[pl.BlockSpec((N_EXPERTS, cap, D_MODEL), lambda b: (0, b, 0)),
                   pl.BlockSpec((N_EXPERTS, cap, LANES), lambda b: (0, b, 0)),
                   pl.BlockSpec((seq, LANES), lambda b: (b, 0))],
        out_shape=[jax.ShapeDtypeStruct((N_EXPERTS, nb * cap, D_MODEL), BF16),
                   jax.ShapeDtypeStruct((N_EXPERTS, nb * cap, LANES), F32),
                   jax.ShapeDtypeStruct((n, LANES), F32)],
        scratch_shapes=[pltpu.VMEM((N_EXPERTS, seq), F32), pltpu.VMEM((N_EXPERTS * cap, seq), BF16)],
        compiler_params=_params(("parallel",), est),
        name="dispatch",
    )(lg_t, h2)


EXPERT_NF = D_EXPERT // EXPERT_TF
EXPERT_ND = D_MODEL // EXPERT_TD


assert EXPERT_NF == EXPERT_ND


def _expert_kernel(xc_ref, xl_ref, gc_ref, gl_ref, wg_ref, wu_ref, wd_ref, yc_ref, yl_ref, hc_scr, hl_scr):
    e = pl.program_id(0)
    t = pl.program_id(1)
    cur = e % 2

    @pl.when(e >= 1)
    def _():
        wd = wd_ref[0].astype(BF16)
        for h_scr, g_ref, y_ref in ((hc_scr, gc_ref, yc_ref), (hl_scr, gl_ref, yl_ref)):
            acc = _dot(h_scr[1 - cur, 0], wd[0:EXPERT_TF])
            for f in range(1, EXPERT_NF):
                acc = acc + _dot(h_scr[1 - cur, f], wd[f * EXPERT_TF:(f + 1) * EXPERT_TF])
            y_ref[0] = (acc * g_ref[0, :, 0:1]).astype(BF16)

    @pl.when(e < N_EXPERTS)
    def _():
        wg = wg_ref[0].astype(BF16)
        wu = wu_ref[0].astype(BF16)
        for x_ref, h_scr in ((xc_ref, hc_scr), (xl_ref, hl_scr)):
            x = x_ref[0]
            a = _dot(x, wg)
            h_scr[cur, t] = ((a * jax.nn.sigmoid(a)) * _dot(x, wu)).astype(BF16)


def _experts(xs_c, xs_l, gate_c, gate_l, w_gate, w_up, w_down, layer):
    sc, sl = xs_c.shape[1], xs_l.shape[1]
    tf, td = EXPERT_TF, EXPERT_TD
    last = N_EXPERTS - 1
    up_expert = lambda e: jnp.minimum(e, last)
    down_expert = lambda e: jnp.maximum(e - 1, 0)
    up_tile = lambda e, t: (layer, up_expert(e), 0, jnp.where(e > last, EXPERT_NF - 1, t))
    down_tile = lambda e, t: (down_expert(e), 0, jnp.where(e == 0, 0, t))
    x_spec = lambda s: pl.BlockSpec((1, s, D_MODEL), lambda e, t: (up_expert(e), 0, 0))
    g_spec = lambda s: pl.BlockSpec((1, s, LANES), lambda e, t: (down_expert(e), 0, 0))
    est = (2 * (sc + sl) * D_MODEL * 2 + 2 * 2 * D_MODEL * tf * 4 + 2 * D_EXPERT * td * 4 + 2 * (sc + sl) * td * 4
           + 2 * (sc + sl) * D_EXPERT * 2 + 2 * D_MODEL * tf * 2 + D_EXPERT * td * 2 + 6 * sc * max(tf, td) * 4)
    return pl.pallas_call(
        _expert_kernel,
        grid=(N_EXPERTS + 1, EXPERT_NF),
        in_specs=[x_spec(sc), x_spec(sl), g_spec(sc), g_spec(sl),
                  pl.BlockSpec((None, 1, D_MODEL, tf), up_tile),
                  pl.BlockSpec((None, 1, D_MODEL, tf), up_tile),
                  pl.BlockSpec((None, 1, D_EXPERT, td), lambda e, t: (layer,) + down_tile(e, t))],
        out_specs=[pl.BlockSpec((1, sc, td), down_tile), pl.BlockSpec((1, sl, td), down_tile)],
        out_shape=[jax.ShapeDtypeStruct((N_EXPERTS, sc, D_MODEL), BF16),
                   jax.ShapeDtypeStruct((N_EXPERTS, sl, D_MODEL), BF16)],
        scratch_shapes=[pltpu.VMEM((2, EXPERT_NF, sc, tf), BF16), pltpu.VMEM((2, EXPERT_NF, sl, tf), BF16)],
        compiler_params=_params(("arbitrary", "arbitrary"), est),
        name="experts",
    )(xs_c, xs_l, gate_c, gate_l, w_gate, w_up, w_down)


def _combine_kernel(rc_ref, ye_ref, x1_ref, mod_ref, nw_ref, o_ref, *, cap):
    nslots = N_EXPERTS * cap
    shift = cap.bit_length() - 1
    expert_of_slot = lax.broadcasted_iota(jnp.int32, (LANES, nslots), 1) >> shift
    spread = jnp.where(lax.broadcasted_iota(jnp.int32, (LANES, nslots), 0) == expert_of_slot, 1.0, 0.0).astype(BF16)
    rank_of_slot = _dot(rc_ref[...].astype(BF16), spread)
    slot = (lax.broadcasted_iota(jnp.int32, (1, nslots), 1) & (cap - 1)).astype(F32)
    onehot = jnp.where(rank_of_slot == slot, 1.0, 0.0).astype(BF16)
    ffn = _dot(onehot, ye_ref[...].reshape(nslots, D_MODEL))
    o_ref[...] = x1_ref[...] + mod_ref[5:6, :] * _rms(ffn, nw_ref[3:4, :])


def _combine(rank_col, ye, x1, mod, norm_w, layer, seq, group_of_batch):
    n = x1.shape[0]
    cap = EC_FACTOR * seq // N_EXPERTS
    assert cap & (cap - 1) == 0 and cap <= LANES
    tr = COMBINE_TR
    per = seq // tr
    est = (2 * N_EXPERTS * cap * D_MODEL * 2 + 4 * tr * D_MODEL * 4 + (tr + LANES) * N_EXPERTS * cap * 6
           + 3 * tr * D_MODEL * 4)
    return pl.pallas_call(
        functools.partial(_combine_kernel, cap=cap),
        grid=(n // seq, per),
        in_specs=[pl.BlockSpec((tr, LANES), lambda b, i: (b * per + i, 0)),
                  pl.BlockSpec((N_EXPERTS, cap, D_MODEL), lambda b, i: (0, b, 0)),
                  pl.BlockSpec((tr, D_MODEL), lambda b, i: (b * per + i, 0)),
                  pl.BlockSpec((None, None, 6, D_MODEL), lambda b, i: (layer, group_of_batch(b), 0, 0)),
                  pl.BlockSpec((None, 4, D_MODEL), lambda b, i: (layer, 0, 0))],
        out_specs=pl.BlockSpec((tr, D_MODEL), lambda b, i: (b * per + i, 0)),
        out_shape=jax.ShapeDtypeStruct((n, D_MODEL), F32),
        compiler_params=_params(("parallel", "arbitrary"), est),
        name="combine",
    )(rank_col, ye, x1, mod, norm_w)


def kernel(x_prompt, x_sample, c, cache_a_k, cache_a_v, cache_b_k, cache_b_v, c_ctx, norm_w, w_ada, b_ada, w_in, a_rpb,
           b_sink, c_pool_w, c_scale, w_branch_a, w_branch_b, w_branch_c, w_out, w_router, w_gate_e, w_up_e, w_down_e):
    batch, seq_c, _ = x_prompt.shape
    dec_batch, seq_l, _ = x_sample.shape
    past = cache_a_k.shape[2]

    cond = jnp.zeros((ADA_ROWS, D_MODEL), F32).at[0].set(c_ctx).at[1:1 + dec_batch].set(c)
    mod = _adaln(cond, w_ada, b_ada).reshape(DEPTH, ADA_ROWS, 6, D_MODEL)

    ctx_group = lambda i: 0
    lat_group_inproj = lambda i: 1 + i // (seq_l // INPROJ_TM)
    lat_group_outproj = lambda i: 1 + i // (seq_l // OUTPROJ_TM)
    lat_group_batch = lambda b: 1 + b

    cbk = cache_b_k.reshape(dec_batch, DEPTH, past, B_KV_DIM)
    cbv = cache_b_v.reshape(dec_batch, DEPTH, past, B_KV_DIM)

    x_c = x_prompt.reshape(batch * seq_c, D_MODEL)
    x_l = x_sample.reshape(dec_batch * seq_l, D_MODEL)
    w_in_bf = w_in.astype(BF16)
    wa, wb, wc, wo = (w.astype(BF16) for w in (w_branch_a, w_branch_b, w_branch_c, w_out))
    wr_t = jnp.swapaxes(w_router, 1, 2)
    pool_scale = c_scale.reshape(DEPTH, 1, POOL_DIM)
    caches = ()
    for l in range(DEPTH):
        p_c, g_c = _inproj(x_c, mod, norm_w, w_in_bf, l, ctx_group)
        o_c, caches = _ctx_mix(p_c, b_sink[l], c_pool_w, pool_scale, l, seq_c, caches)
        branches_c = tuple((o_c, k) for k in range(N_BRANCH))
        m_c = _branch_merge(branches_c, g_c, wa, wb, wc, l)
        x1_c, h2_c, lg_c = _outproj(m_c, x_c, mod, norm_w, wo, wr_t, l, ctx_group)
        xs_c, gate_c, rc_c = _dispatch(lg_c, h2_c, seq_c)

        p_l, g_l = _inproj(x_l, mod, norm_w, w_in_bf, l, lat_group_inproj)
        table = _rpb_table(a_rpb[l])
        o_a = _na_latent(p_l, cache_a_k, cache_a_v, table, l, seq_l)
        o_b = _sw_latent(p_l, cbk, cbv, b_sink[l], l, seq_l)
        o_p = _pool_latent(p_l, c_pool_w, pool_scale, l, seq_l)
        m_l = _branch_merge(((o_a, 0), (o_b, 0), (o_p, 0)), g_l, wa, wb, wc, l)
        x1_l, h2_l, lg_l = _outproj(m_l, x_l, mod, norm_w, wo, wr_t, l, lat_group_outproj)
        xs_l, gate_l, rc_l = _dispatch(lg_l, h2_l, seq_l)

        ye_c, ye_l = _experts(xs_c, xs_l, gate_c, gate_l, w_gate_e, w_up_e, w_down_e, l)
        x_c = _combine(rc_c, ye_c, x1_c, mod, norm_w, l, seq_c, ctx_group)
        x_l = _combine(rc_l, ye_l, x1_l, mod, norm_w, l, seq_l, lat_group_batch)

    y_prompt = x_c.reshape(batch, seq_c, D_MODEL)
    y_sample = x_l.reshape(dec_batch, seq_l, D_MODEL)
    new_a_k, new_a_v, new_b_k, new_b_v = caches
    a_shape = (batch, DEPTH, seq_c, NA_HEADS, NA_HEAD_DIM)
    b_shape = (batch, DEPTH, seq_c, SW_KV_HEADS, SW_HEAD_DIM)
    return (y_prompt, y_sample, new_a_k.reshape(a_shape), new_a_v.reshape(a_shape),
            new_b_k.reshape(b_shape), new_b_v.reshape(b_shape))
```

```python
import functools

import numpy as np
import jax
import jax.numpy as jnp
from jax import lax
from jax.experimental import pallas as pl
from jax.experimental.pallas import tpu as pltpu

F32 = jnp.float32
BF16 = jnp.bfloat16

D_MODEL = 2048
DEPTH = 2
GRID_W = 64
NA_HEADS, NA_HEAD_DIM, NA_MAX_KH, NA_KW = 4, 128, 8, 16
SW_Q_HEADS, SW_KV_HEADS, SW_HEAD_DIM = 8, 2, 64
SW_GROUP = SW_Q_HEADS // SW_KV_HEADS
SW_WINDOW, SW_BLOCK = 128, 128
ROPE_THETA = 10000.0
POOL_WINDOWS = (2, 4, 8, 16)
POOL_GROUPS, POOL_GROUP_DIM = 4, 128
POOL_DIM = POOL_GROUPS * POOL_GROUP_DIM
A_DIM = NA_HEADS * NA_HEAD_DIM
B_Q_DIM = SW_Q_HEADS * SW_HEAD_DIM
B_KV_DIM = SW_KV_HEADS * SW_HEAD_DIM
N_BRANCH = 3
GATE_DIM = N_BRANCH * D_MODEL
QKVU_DIM = 3 * A_DIM + B_Q_DIM + 2 * B_KV_DIM + POOL_DIM
IN_DIM = QKVU_DIM + GATE_DIM
N_EXPERTS = 16
EC_FACTOR = 2
D_EXPERT = 1024
RMS_EPS = 1e-6
NEG_INF = -1e30

COL_QA = 0
COL_KB = (3 * A_DIM + B_Q_DIM) // B_KV_DIM
COL_U = COL_KB + 2

MIB = 1024 * 1024
V7X_VMEM_BYTES = 64 * MIB
V7X_VMEM_CEILING = 60000 * 1024
VMEM_FLOOR = 32 * MIB
VMEM_COMPILER_SCRATCH = 8 * MIB
assert V7X_VMEM_CEILING < V7X_VMEM_BYTES

ADA_ROWS = 16
ADA_TN = 1024
V7X_MXU_DIM = 256
LANES = 128
INPROJ_TM = 1024
INPROJ_TN = 1280
assert INPROJ_TN % V7X_MXU_DIM == 0 and IN_DIM % INPROJ_TN == 0
INPROJ_TILES = IN_DIM // INPROJ_TN
INPROJ_SPLIT = QKVU_DIM // INPROJ_TN
P_WIDTH = (INPROJ_SPLIT + 1) * INPROJ_TN
GATES_WIDTH = (INPROJ_TILES - INPROJ_SPLIT) * INPROJ_TN
GATES_COL0 = QKVU_DIM - INPROJ_SPLIT * INPROJ_TN
BMERGE_TM = 1024
BMERGE_CA = 1024
BMERGE_NA = D_MODEL // BMERGE_CA
OUTPROJ_TM = 512
OUTPROJ_CB = 512
EXPERT_TF = 512
EXPERT_TD = 1024
COMBINE_TR = 256
ROW_CHUNK = 64


def _vmem_limit(estimate_bytes):
    return int(min(V7X_VMEM_CEILING, max(VMEM_FLOOR, estimate_bytes + VMEM_COMPILER_SCRATCH)))


def _params(semantics, estimate_bytes):
    return pltpu.CompilerParams(dimension_semantics=semantics,
                                vmem_limit_bytes=_vmem_limit(estimate_bytes))


def _rms(x, g):
    ms = jnp.mean(x * x, axis=-1, keepdims=True)
    return x * lax.rsqrt(ms + RMS_EPS) * g


def _sigmoid(x):
    return 0.5 * jnp.tanh(0.5 * x) + 0.5


def _dot(a, b):
    return jnp.dot(a, b, preferred_element_type=F32)


def _dot_nt(a, b):
    return lax.dot_general(a, b, (((1,), (1,)), ((), ())), preferred_element_type=F32)


def _adaln_kernel(c_ref, w_ref, b_ref, o_ref):
    c = c_ref[...]
    s = (c * jax.nn.sigmoid(c)).astype(BF16)
    o_ref[0] = _dot(s, w_ref[0].astype(BF16)) + b_ref[0]


def _adaln(cond, w_ada, b_ada):
    n_out = w_ada.shape[-1]
    return pl.pallas_call(
        _adaln_kernel,
        grid=(DEPTH, n_out // ADA_TN),
        in_specs=[pl.BlockSpec((ADA_ROWS, D_MODEL), lambda l, j: (0, 0)),
                  pl.BlockSpec((1, D_MODEL, ADA_TN), lambda l, j: (l, 0, j)),
                  pl.BlockSpec((1, 1, ADA_TN), lambda l, j: (l, 0, j))],
        out_specs=pl.BlockSpec((1, ADA_ROWS, ADA_TN), lambda l, j: (l, 0, j)),
        out_shape=jax.ShapeDtypeStruct((DEPTH, ADA_ROWS, n_out), F32),
        compiler_params=_params(("parallel", "parallel"), 2 * D_MODEL * ADA_TN * 4),
        name="adaln",
    )(cond, w_ada, b_ada.reshape(DEPTH, 1, n_out))


def _inproj_kernel(x_ref, mod_ref, nw_ref, w_ref, o_ref, gate_ref, h_scr):
    j = pl.program_id(1)

    @pl.when(j == 0)
    def _():
        g = nw_ref[0:1, :]
        sc = 1.0 + mod_ref[1:2, :]
        sh = mod_ref[0:1, :]

        def body(r, carry):
            rows = pl.ds(pl.multiple_of(r * ROW_CHUNK, ROW_CHUNK), ROW_CHUNK)
            h_scr[rows, :] = (_rms(x_ref[rows, :], g) * sc + sh).astype(BF16)
            return carry

        lax.fori_loop(0, INPROJ_TM // ROW_CHUNK, body, 0, unroll=4)

    @pl.when(j < INPROJ_SPLIT)
    def _():
        o_ref[...] = _dot(h_scr[...], w_ref[...])

    @pl.when(j == INPROJ_SPLIT)
    def _():
        acc = _dot(h_scr[...], w_ref[...])
        o_ref[...] = acc
        gate_ref[...] = _sigmoid(acc).astype(BF16)

    @pl.when(j > INPROJ_SPLIT)
    def _():
        gate_ref[...] = _sigmoid(_dot(h_scr[...], w_ref[...])).astype(BF16)


def _mod_spec(layer, group_of):
    return pl.BlockSpec((None, None, 6, D_MODEL), lambda i, j: (layer, group_of(i), 0, 0))


def _inproj(x, mod, norm_w, w_in_bf, layer, group_of_tile):
    n = x.shape[0]
    est = (2 * INPROJ_TM * D_MODEL * 4 + 2 * D_MODEL * INPROJ_TN * 2 + 2 * INPROJ_TM * INPROJ_TN * (4 + 2)
           + INPROJ_TM * D_MODEL * 2 + 2 * INPROJ_TM * INPROJ_TN * 4)
    return pl.pallas_call(
        _inproj_kernel,
        grid=(n // INPROJ_TM, INPROJ_TILES),
        in_specs=[pl.BlockSpec((INPROJ_TM, D_MODEL), lambda i, j: (i, 0)),
                  _mod_spec(layer, group_of_tile),
                  pl.BlockSpec((None, 4, D_MODEL), lambda i, j: (layer, 0, 0)),
                  pl.BlockSpec((None, D_MODEL, INPROJ_TN), lambda i, j: (layer, 0, j))],
        out_specs=[pl.BlockSpec((INPROJ_TM, INPROJ_TN), lambda i, j: (i, jnp.minimum(j, INPROJ_SPLIT))),
                   pl.BlockSpec((INPROJ_TM, INPROJ_TN), lambda i, j: (i, jnp.maximum(j - INPROJ_SPLIT, 0)))],
        out_shape=[jax.ShapeDtypeStruct((n, P_WIDTH), F32), jax.ShapeDtypeStruct((n, GATES_WIDTH), BF16)],
        scratch_shapes=[pltpu.VMEM((INPROJ_TM, D_MODEL), BF16)],
        compiler_params=_params(("parallel", "arbitrary"), est),
        name="inproj",
    )(x, mod, norm_w, w_in_bf)


def _joint_attention(q, segments, scale, sink=None):
    scores = []
    for k, _, bias, mask in segments:
        s = _dot_nt(q, k) * scale
        if bias is not None:
            s = s + bias
        if mask is not None:
            s = jnp.where(mask, s, NEG_INF)
        scores.append(s)
    m = scores[0].max(axis=-1, keepdims=True)
    for s in scores[1:]:
        m = jnp.maximum(m, s.max(axis=-1, keepdims=True))
    if sink is not None:
        m = jnp.maximum(m, sink)
    denom = jnp.exp(sink - m) if sink is not None else 0.0
    acc = None
    for s, (_, v, _, _) in zip(scores, segments):
        e = jnp.exp(s - m)
        denom = denom + e.sum(axis=-1, keepdims=True)
        pv = _dot(e.astype(BF16), v)
        acc = pv if acc is None else acc + pv
    return acc / denom


def _pool_group(u, window, pw_bf, scale_row):
    seq = u.shape[0]
    pad = 8
    n = seq + 2 * pad
    z = jnp.zeros((pad, POOL_GROUP_DIM), F32)
    p = jnp.concatenate([z, u, z], axis=0)
    k = 1
    while k < window:
        p = p + pltpu.roll(p, n - k, 0)
        k *= 2
    win = pltpu.roll(p, window // 2, 0)[pad:pad + seq]
    t = lax.broadcasted_iota(jnp.int32, (seq, 1), 0)
    lo = jnp.maximum(t - window // 2, 0)
    hi = jnp.minimum(t - window // 2 + window, seq)
    cnt = (hi - lo).astype(F32)
    pooled = win / cnt - u
    return _dot(pooled.astype(BF16), pw_bf) * scale_row


def _unstack_heads(o, rows_per_head):
    return jnp.concatenate([o[g * rows_per_head:(g + 1) * rows_per_head] for g in range(SW_GROUP)], axis=1).astype(BF16)


def _sink_column(sink_ref, kv_head, rows_per_head):
    r = lax.broadcasted_iota(jnp.int32, (SW_GROUP * rows_per_head, 1), 0)
    col = jnp.full((SW_GROUP * rows_per_head, 1), sink_ref[kv_head * SW_GROUP], F32)
    for g in range(1, SW_GROUP):
        col = jnp.where(r >= g * rows_per_head, sink_ref[kv_head * SW_GROUP + g], col)
    return col


def _ctx_mix_kernel(sink_ref, qa_ref, ka_ref, va_ref, qb_ref, kb_ref, vb_ref, u0_ref, u1_ref, u2_ref, u3_ref,
                    pw_ref, ps_ref, *rest, layer):
    o_ref = rest[-5]
    seq = qa_ref.shape[0]

    def put(cache_ref, lead, val):
        if cache_ref.shape[-2] == seq:
            cache_ref[lead + (slice(None), slice(None))] = val
        else:
            for h in range(NA_HEADS):
                cache_ref[lead + (pl.ds(h, seq, stride=NA_HEADS), slice(None))] = \
                    val[:, h * NA_HEAD_DIM:(h + 1) * NA_HEAD_DIM]

    for cache_ref, src_ref in zip(rest[-4:], (ka_ref, va_ref, kb_ref, vb_ref)):
        if len(cache_ref.shape) == 2:
            put(cache_ref, (), src_ref[...])
        else:
            for d in range(cache_ref.shape[0]):
                put(cache_ref, (d,), src_ref[...] if d == layer else jnp.zeros(src_ref.shape, F32))
    for h in range(NA_HEADS):
        sl = slice(h * NA_HEAD_DIM, (h + 1) * NA_HEAD_DIM)
        o = _joint_attention(qa_ref[:, sl].astype(BF16),
                             [(ka_ref[:, sl].astype(BF16), va_ref[:, sl].astype(BF16), None, None)],
                             NA_HEAD_DIM ** -0.5)
        o_ref[:, sl] = o.astype(BF16)
    for hk in range(SW_KV_HEADS):
        ksl = slice(hk * SW_HEAD_DIM, (hk + 1) * SW_HEAD_DIM)
        q = jnp.concatenate(
            [qb_ref[:, (hk * SW_GROUP + g) * SW_HEAD_DIM:(hk * SW_GROUP + g + 1) * SW_HEAD_DIM] for g in range(SW_GROUP)],
            axis=0).astype(BF16)
        o = _joint_attention(q, [(kb_ref[:, ksl].astype(BF16), vb_ref[:, ksl].astype(BF16), None, None)],
                             SW_HEAD_DIM ** -0.5, sink=_sink_column(sink_ref, hk, seq))
        c0 = A_DIM + hk * SW_GROUP * SW_HEAD_DIM
        o_ref[:, c0:c0 + SW_GROUP * SW_HEAD_DIM] = _unstack_heads(o, seq)
    for gi, u_ref in enumerate((u0_ref, u1_ref, u2_ref, u3_ref)):
        c0 = A_DIM + B_Q_DIM + gi * POOL_GROUP_DIM
        o_ref[:, c0:c0 + POOL_GROUP_DIM] = _pool_group(
            u_ref[...], POOL_WINDOWS[gi], pw_ref[gi].astype(BF16),
            ps_ref[:, gi * POOL_GROUP_DIM:(gi + 1) * POOL_GROUP_DIM]).astype(BF16)


def _ctx_mix(p, sink_l, pool_w, pool_scale, layer, seq, caches):
    n = p.shape[0]
    batch = n // seq
    wide = lambda c: pl.BlockSpec((seq, A_DIM), lambda b, c=c: (b, c))
    narrow = lambda c: pl.BlockSpec((seq, B_KV_DIM), lambda b, c=c: (b, c))
    in_specs = [pl.BlockSpec(memory_space=pltpu.SMEM)]
    in_specs += [wide(COL_QA + i) for i in range(4)]
    in_specs += [narrow(COL_KB), narrow(COL_KB + 1)]
    in_specs += [narrow(COL_U + g) for g in range(POOL_GROUPS)]
    in_specs += [pl.BlockSpec((None, POOL_GROUPS, POOL_GROUP_DIM, POOL_GROUP_DIM), lambda b: (layer, 0, 0, 0)),
                 pl.BlockSpec((None, 1, POOL_DIM), lambda b: (layer, 0, 0))]
    n_fixed = len(in_specs)
    in_specs += [pl.BlockSpec(memory_space=pl.ANY)] * len(caches)
    if caches:
        cache_spec = lambda rows, width: pl.BlockSpec((None, None, rows, width), lambda b: (b, layer, 0, 0))
    else:
        cache_spec = lambda rows, width: pl.BlockSpec((None, DEPTH, rows, width), lambda b: (b, 0, 0, 0))
    cache_shape = lambda rows, width: jax.ShapeDtypeStruct((batch, DEPTH, rows, width), F32)
    a_cache = (seq * NA_HEADS, NA_HEAD_DIM)
    b_cache = (seq, B_KV_DIM)
    outs = pl.pallas_call(
        functools.partial(_ctx_mix_kernel, layer=layer),
        grid=(batch,),
        in_specs=in_specs,
        out_specs=[pl.BlockSpec((seq, 3 * A_DIM), lambda b: (b, 0)),
                   cache_spec(*a_cache), cache_spec(*a_cache), cache_spec(*b_cache), cache_spec(*b_cache)],
        out_shape=[jax.ShapeDtypeStruct((n, 3 * A_DIM), BF16),
                   cache_shape(*a_cache), cache_shape(*a_cache), cache_shape(*b_cache), cache_shape(*b_cache)],
        input_output_aliases={n_fixed + k: 1 + k for k in range(len(caches))},
        compiler_params=_params(("parallel",), 24 * MIB),
        name="ctx_mix",
    )(sink_l, p, p, p, p, p, p, p, p, p, p, pool_w, pool_scale, *caches)
    return outs[0], tuple(outs[1:])


NA_PAIR_ROWS = 2 * NA_MAX_KH - 2


def _rpb_table_kernel(rpb_ref, t_ref):
    lane = lax.broadcasted_iota(jnp.int32, (GRID_W, 2 * GRID_W), 1)
    qc = lax.broadcasted_iota(jnp.int32, (GRID_W, 2 * GRID_W), 0)
    kc = lane & (GRID_W - 1)
    upper = lane >= GRID_W
    dcm = jnp.clip(kc - qc + NA_KW - 1, 0, 2 * NA_KW - 2)
    col0 = jnp.clip(qc - NA_KW // 2, 0, GRID_W - NA_KW)
    inside = (kc >= col0) & (kc < col0 + NA_KW)
    n_dc = 2 * NA_KW - 1
    n_dr = 2 * NA_MAX_KH - 1

    def body(i, carry):
        h = i // NA_PAIR_ROWS
        dr = i - h * NA_PAIR_ROWS
        base = (h * n_dr + dr) * n_dc
        acc = jnp.zeros((GRID_W, 2 * GRID_W), F32)
        for dc in range(n_dc):
            val = jnp.where(upper, rpb_ref[base + n_dc + dc], rpb_ref[base + dc])
            acc = jnp.where(dcm == dc, val, acc)
        t_ref[i] = jnp.where(inside, acc, NEG_INF)
        return carry

    lax.fori_loop(0, NA_HEADS * NA_PAIR_ROWS, body, 0)


def _rpb_table(rpb_l):
    return pl.pallas_call(
        _rpb_table_kernel,
        in_specs=[pl.BlockSpec(memory_space=pltpu.SMEM)],
        out_specs=pl.BlockSpec(memory_space=pltpu.VMEM),
        out_shape=jax.ShapeDtypeStruct((NA_HEADS * NA_PAIR_ROWS, GRID_W, 2 * GRID_W), F32),
        name="rpb_table",
    )(rpb_l.reshape(-1))


def _na_kernel(q_ref, k_ref, v_ref, kc_ref, vc_ref, t_ref, o_ref, kb_scr, vb_scr, kcb_scr, vcb_scr, *, rows):
    qr = pl.program_id(1)

    @pl.when(qr == 0)
    def _():
        kb_scr[...] = k_ref[...].astype(BF16)
        vb_scr[...] = v_ref[...].astype(BF16)
        for h in range(NA_HEADS):
            sl = slice(h * NA_HEAD_DIM, (h + 1) * NA_HEAD_DIM)
            kcb_scr[:, sl] = kc_ref[:, h, :].astype(BF16)
            vcb_scr[:, sl] = vc_ref[:, h, :].astype(BF16)

    row0 = jnp.clip(qr - NA_MAX_KH // 2, 0, rows - NA_MAX_KH)
    start = pl.multiple_of(row0 * GRID_W, GRID_W)
    nkeys = NA_MAX_KH * GRID_W
    d0 = row0 - qr + NA_MAX_KH - 1
    for h in range(NA_HEADS):
        sl = slice(h * NA_HEAD_DIM, (h + 1) * NA_HEAD_DIM)
        bias = jnp.concatenate([t_ref[h * NA_PAIR_ROWS + d0 + 2 * i] for i in range(NA_MAX_KH // 2)], axis=1)
        o = _joint_attention(
            q_ref[:, sl].astype(BF16),
            [(kb_scr[pl.ds(start, nkeys), sl], vb_scr[pl.ds(start, nkeys), sl], bias, None),
             (kcb_scr[:, sl], vcb_scr[:, sl], None, None)],
            NA_HEAD_DIM ** -0.5)
        o_ref[:, sl] = o.astype(BF16)


def _na_latent(p, cache_k, cache_v, table, layer, seq):
    n = p.shape[0]
    rows = seq // GRID_W
    past = cache_k.shape[2]
    ctx_spec = pl.BlockSpec((None, None, past, NA_HEADS, NA_HEAD_DIM), lambda b, r: (b, layer, 0, 0, 0))
    return pl.pallas_call(
        functools.partial(_na_kernel, rows=rows),
        grid=(n // seq, rows),
        in_specs=[pl.BlockSpec((GRID_W, A_DIM), lambda b, r: (b * rows + r, COL_QA)),
                  pl.BlockSpec((seq, A_DIM), lambda b, r: (b, COL_QA + 1)),
                  pl.BlockSpec((seq, A_DIM), lambda b, r: (b, COL_QA + 2)),
                  ctx_spec, ctx_spec,
                  pl.BlockSpec(table.shape, lambda b, r: (0, 0, 0))],
        out_specs=pl.BlockSpec((GRID_W, A_DIM), lambda b, r: (b * rows + r, 0)),
        out_shape=jax.ShapeDtypeStruct((n, A_DIM), BF16),
        scratch_shapes=[pltpu.VMEM((seq, A_DIM), BF16), pltpu.VMEM((seq, A_DIM), BF16),
                        pltpu.VMEM((past, A_DIM), BF16), pltpu.VMEM((past, A_DIM), BF16)],
        compiler_params=_params(("parallel", "arbitrary"), 16 * MIB),
        name="na_latent",
    )(p, p, p, cache_k, cache_v, table)


def _rope_tables(seq):
    nfreq = SW_HEAD_DIM // 4
    inv = 1.0 / (ROPE_THETA ** (np.arange(nfreq, dtype=np.float32) / np.float32(nfreq)))
    t = np.arange(seq)
    pos = (t // GRID_W, t % GRID_W)
    cos = np.zeros((seq, SW_HEAD_DIM), np.float32)
    sin_next = np.zeros((seq, SW_HEAD_DIM), np.float32)
    sin_prev = np.zeros((seq, SW_HEAD_DIM), np.float32)
    for a in range(2):
        ang = pos[a].astype(np.float32)[:, None] * inv[None, :].astype(np.float32)
        c, s = np.cos(ang).astype(np.float32), np.sin(ang).astype(np.float32)
        lo = 2 * a * nfreq
        cos[:, lo:lo + nfreq] = c
        cos[:, lo + nfreq:lo + 2 * nfreq] = c
        sin_next[:, lo:lo + nfreq] = -s
        sin_prev[:, lo + nfreq:lo + 2 * nfreq] = s
    tile = lambda x: jnp.asarray(np.tile(x, (1, 128 // SW_HEAD_DIM)))
    return tile(cos), tile(sin_next), tile(sin_prev)


def _rope(x, cos, sin_next, sin_prev):
    nfreq = SW_HEAD_DIM // 4
    return x * cos + pltpu.roll(x, 128 - nfreq, 1) * sin_next + pltpu.roll(x, nfreq, 1) * sin_prev


def _sw_kernel(sink_ref, q_ref, k_ref, v_ref, kc_ref, vc_ref, cq_ref, snq_ref, spq_ref, ck_ref, snk_ref, spk_ref,
               o_ref, kr_scr, *, seq):
    n = pl.program_id(1)

    @pl.when(n == 0)
    def _():
        kr_scr[...] = _rope(k_ref[...], ck_ref[...], snk_ref[...], spk_ref[...]).astype(BF16)

    nwin = 3 * SW_BLOCK
    kstart = pl.multiple_of(jnp.clip((n - 1) * SW_BLOCK, 0, seq - nwin), SW_BLOCK)
    cq, snq, spq = cq_ref[...], snq_ref[...], spq_ref[...]
    q = jnp.concatenate([_rope(q_ref[:, c * 128:(c + 1) * 128], cq, snq, spq) for c in range(B_Q_DIM // 128)],
                        axis=1).astype(BF16)
    rows = SW_GROUP * SW_BLOCK
    qpos = n * SW_BLOCK + (lax.broadcasted_iota(jnp.int32, (rows, nwin), 0) & (SW_BLOCK - 1))
    kpos = kstart + lax.broadcasted_iota(jnp.int32, (rows, nwin), 1)
    band = jnp.abs(qpos - kpos) <= SW_WINDOW
    for hk in range(SW_KV_HEADS):
        ksl = slice(hk * SW_HEAD_DIM, (hk + 1) * SW_HEAD_DIM)
        qs = jnp.concatenate(
            [q[:, (hk * SW_GROUP + g) * SW_HEAD_DIM:(hk * SW_GROUP + g + 1) * SW_HEAD_DIM] for g in range(SW_GROUP)], axis=0)
        o = _joint_attention(
            qs,
            [(kr_scr[pl.ds(kstart, nwin), ksl], v_ref[pl.ds(kstart, nwin), ksl].astype(BF16), None, band),
             (kc_ref[:, ksl].astype(BF16), vc_ref[:, ksl].astype(BF16), None, None)],
            SW_HEAD_DIM ** -0.5, sink=_sink_column(sink_ref, hk, SW_BLOCK))
        c0 = hk * SW_GROUP * SW_HEAD_DIM
        o_ref[:, c0:c0 + SW_GROUP * SW_HEAD_DIM] = _unstack_heads(o, SW_BLOCK)


def _sw_latent(p, cache_k, cache_v, sink_l, layer, seq):
    n = p.shape[0]
    nb = seq // SW_BLOCK
    past = cache_k.shape[2]
    cos, sin_next, sin_prev = _rope_tables(seq)
    ctx_spec = pl.BlockSpec((None, None, past, B_KV_DIM), lambda b, i: (b, layer, 0, 0))
    tab_q = pl.BlockSpec((SW_BLOCK, 128), lambda b, i: (i, 0))
    tab_k = pl.BlockSpec((seq, 128), lambda b, i: (0, 0))
    return pl.pallas_call(
        functools.partial(_sw_kernel, seq=seq),
        grid=(n // seq, nb),
        in_specs=[pl.BlockSpec(memory_space=pltpu.SMEM),
                  pl.BlockSpec((SW_BLOCK, B_Q_DIM), lambda b, i: (b * nb + i, COL_QA + 3)),
                  pl.BlockSpec((seq, B_KV_DIM), lambda b, i: (b, COL_KB)),
                  pl.BlockSpec((seq, B_KV_DIM), lambda b, i: (b, COL_KB + 1)),
                  ctx_spec, ctx_spec, tab_q, tab_q, tab_q, tab_k, tab_k, tab_k],
        out_specs=pl.BlockSpec((SW_BLOCK, B_Q_DIM), lambda b, i: (b * nb + i, 0)),
        out_shape=jax.ShapeDtypeStruct((n, B_Q_DIM), BF16),
        scratch_shapes=[pltpu.VMEM((seq, B_KV_DIM), BF16)],
        compiler_params=_params(("parallel", "arbitrary"), 16 * MIB),
        name="sw_latent",
    )(sink_l, p, p, p, cache_k, cache_v, cos, sin_next, sin_prev, cos, sin_next, sin_prev)


def _pool_kernel(u0_ref, u1_ref, u2_ref, u3_ref, pw_ref, ps_ref, o_ref):
    for gi, u_ref in enumerate((u0_ref, u1_ref, u2_ref, u3_ref)):
        sl = slice(gi * POOL_GROUP_DIM, (gi + 1) * POOL_GROUP_DIM)
        o_ref[:, sl] = _pool_group(u_ref[...], POOL_WINDOWS[gi], pw_ref[gi].astype(BF16), ps_ref[:, sl]).astype(BF16)


def _pool_latent(p, pool_w, pool_scale, layer, seq):
    n = p.shape[0]
    return pl.pallas_call(
        _pool_kernel,
        grid=(n // seq,),
        in_specs=[pl.BlockSpec((seq, POOL_GROUP_DIM), lambda b, g=g: (b, COL_U + g)) for g in range(POOL_GROUPS)]
        + [pl.BlockSpec((None, POOL_GROUPS, POOL_GROUP_DIM, POOL_GROUP_DIM), lambda b: (layer, 0, 0, 0)),
           pl.BlockSpec((None, 1, POOL_DIM), lambda b: (layer, 0, 0))],
        out_specs=pl.BlockSpec((seq, POOL_DIM), lambda b: (b, 0)),
        out_shape=jax.ShapeDtypeStruct((n, POOL_DIM), BF16),
        compiler_params=_params(("parallel",), 16 * MIB),
        name="pool_latent",
    )(p, p, p, p, pool_w, pool_scale)


def _split_bf16(x):
    hi = x.astype(BF16)
    return hi, (x - hi.astype(F32)).astype(BF16)


def _branch_merge_kernel(oa_ref, ob_ref, oc_ref, ga_ref, gb_ref, gc_ref, wa_ref, wb_ref, wc_ref, m_ref):
    s = pl.program_id(1)
    for t in range(BMERGE_NA):
        @pl.when(s == t)
        def _(t=t):
            cols = slice(t * BMERGE_CA, (t + 1) * BMERGE_CA)
            m = (ga_ref[...].astype(F32) * _dot(oa_ref[...], wa_ref[:, cols])
                 + gb_ref[...].astype(F32) * _dot(ob_ref[...], wb_ref[:, cols])
                 + gc_ref[...].astype(F32) * _dot(oc_ref[...], wc_ref[:, cols]))
            m_ref[...] = m.astype(BF16)


def _branch_merge(branches, gates, wa, wb, wc, layer):
    n = gates.shape[0]
    tm, ca = BMERGE_TM, BMERGE_CA
    gate = lambda k: pl.BlockSpec((pl.Element(tm), pl.Element(ca)),
                                  lambda i, s, k=k: (i * tm, pl.multiple_of(GATES_COL0 + k * D_MODEL + s * ca, LANES)))
    branch = lambda col: pl.BlockSpec((tm, A_DIM), lambda i, s: (i, col))
    resident = lambda rows: pl.BlockSpec((None, rows, D_MODEL), lambda i, s: (layer, 0, 0),
                                         pipeline_mode=pl.Buffered(1))
    oa, ob, oc = branches
    est = (2 * 3 * tm * A_DIM * 2 + 2 * 3 * tm * ca * 2 + 2 * tm * ca * 2 + 3 * A_DIM * D_MODEL * 2 + 8 * tm * ca * 4)
    return pl.pallas_call(
        _branch_merge_kernel,
        grid=(n // tm, BMERGE_NA),
        in_specs=[branch(oa[1]), branch(ob[1]), branch(oc[1]), gate(0), gate(1), gate(2),
                  resident(A_DIM), resident(B_Q_DIM), resident(POOL_DIM)],
        out_specs=pl.BlockSpec((tm, ca), lambda i, s: (i, s)),
        out_shape=jax.ShapeDtypeStruct((n, D_MODEL), BF16),
        compiler_params=_params(("parallel", "arbitrary"), est),
        name="branch_merge",
    )(oa[0], ob[0], oc[0], gates, gates, gates, wa, wb, wc)


def _outproj_kernel(m_ref, x_ref, mod_ref, nw_ref, wo_ref, wr_ref, x1_ref, h2_ref, lg_ref, y_scr):
    tiles = [slice(c * OUTPROJ_CB, (c + 1) * OUTPROJ_CB) for c in range(D_MODEL // OUTPROJ_CB)]
    rows = x_ref.shape[0]
    ss = jnp.zeros((rows, 1), F32)
    for cols in tiles:
        y = _dot(m_ref[...], wo_ref[:, cols])
        y_scr[:, cols] = y
        ss = ss + (y * y).sum(axis=-1, keepdims=True)
    r1 = lax.rsqrt(ss / D_MODEL + RMS_EPS)
    gain1 = mod_ref[2:3, :] * nw_ref[1:2, :]
    gain2 = nw_ref[2:3, :] * (1.0 + mod_ref[4:5, :])
    ss = jnp.zeros((rows, 1), F32)
    for cols in tiles:
        x1 = x_ref[:, cols] + (y_scr[:, cols] * r1) * gain1[:, cols]
        x1_ref[:, cols] = x1
        ss = ss + (x1 * x1).sum(axis=-1, keepdims=True)
    r2 = lax.rsqrt(ss / D_MODEL + RMS_EPS)
    lg = jnp.zeros(lg_ref.shape, F32)
    for cols in tiles:
        h2 = (x1_ref[:, cols] * r2) * gain2[:, cols] + mod_ref[3:4, cols]
        h2_ref[:, cols] = h2.astype(BF16)
        h_hi, h_lo = _split_bf16(h2)
        w_hi, w_lo = _split_bf16(wr_ref[:, cols])
        lg = lg + (_dot_nt(w_hi, h_hi) + (_dot_nt(w_hi, h_lo) + _dot_nt(w_lo, h_hi)))
    lg_ref[...] = lg


def _outproj(m, x, mod, norm_w, wo, wr_t, layer, group_of_tile):
    n = x.shape[0]
    tm = OUTPROJ_TM
    row = lambda: pl.BlockSpec((tm, D_MODEL), lambda i: (i, 0))
    est = (2 * tm * D_MODEL * (2 + 4 + 4 + 2) + D_MODEL * D_MODEL * 2 + tm * D_MODEL * 4 + 4 * tm * OUTPROJ_CB * 4)
    return pl.pallas_call(
        _outproj_kernel,
        grid=(n // tm,),
        in_specs=[row(), row(),
                  pl.BlockSpec((None, None, 6, D_MODEL), lambda i: (layer, group_of_tile(i), 0, 0)),
                  pl.BlockSpec((None, 4, D_MODEL), lambda i: (layer, 0, 0)),
                  pl.BlockSpec((None, D_MODEL, D_MODEL), lambda i: (layer, 0, 0), pipeline_mode=pl.Buffered(1)),
                  pl.BlockSpec((None, N_EXPERTS, D_MODEL), lambda i: (layer, 0, 0))],
        out_specs=[row(), row(), pl.BlockSpec((N_EXPERTS, tm), lambda i: (0, i))],
        out_shape=[jax.ShapeDtypeStruct((n, D_MODEL), F32), jax.ShapeDtypeStruct((n, D_MODEL), BF16),
                   jax.ShapeDtypeStruct((N_EXPERTS, n), F32)],
        scratch_shapes=[pltpu.VMEM((tm, D_MODEL), F32)],
        compiler_params=_params(("parallel",), est),
        name="outproj",
    )(m, x, mod, norm_w, wo, wr_t)


RANK_TILE = LANES


def _dispatch_kernel(lg_ref, h_ref, xs_ref, gate_ref, rc_ref, aff_scr, sel_scr, *, seq, cap):
    t = RANK_TILE
    nt = seq // t
    lg = lg_ref[...]
    e = jnp.exp(lg - lg.max(axis=0, keepdims=True))
    aff_scr[...] = e / e.sum(axis=0, keepdims=True)
    aff = aff_scr[...]
    ident = jnp.where(lax.broadcasted_iota(jnp.int32, (t, t), 0) == lax.broadcasted_iota(jnp.int32, (t, t), 1),
                      1.0, 0.0).astype(BF16)

    def to_sublanes(pieces, r):
        tile = slice(r * t, (r + 1) * t)
        out = _dot_nt(ident, pieces[0][:, tile])
        for piece in pieces[1:]:
            out = out + _dot_nt(ident, piece[:, tile])
        return out

    a1 = aff.astype(BF16)
    r1 = aff - a1.astype(F32)
    a2 = r1.astype(BF16)
    a3 = (r1 - a2.astype(F32)).astype(BF16)
    aff_cols = [to_sublanes((a1, a2, a3), r) for r in range(nt)]

    earlier = lax.broadcasted_iota(jnp.int32, (t, t), 0) < lax.broadcasted_iota(jnp.int32, (t, t), 1)
    slot = lax.broadcasted_iota(jnp.int32, (cap, seq), 0).astype(F32)
    ranks = []
    for ex in range(N_EXPERTS):
        row = aff[ex:ex + 1, :]
        cols = [jnp.broadcast_to(aff_cols[r][:, ex:ex + 1], (t, t)) for r in range(nt)]
        counts = []
        for c in range(nt):
            rowb = jnp.broadcast_to(row[:, c * t:(c + 1) * t], (t, t))
            acc = jnp.zeros((t, t), F32)
            for r in range(nt):
                if r < c:
                    beats = cols[r] >= rowb
                elif r > c:
                    beats = cols[r] > rowb
                else:
                    beats = (cols[r] > rowb) | (earlier & (cols[r] == rowb))
                acc = acc + jnp.where(beats, 1.0, 0.0)
            counts.append(acc.sum(axis=0, keepdims=True))
        rank_row = jnp.concatenate(counts, axis=1) if nt > 1 else counts[0]
        ranks.append(rank_row)
        sel = slot == rank_row
        sel_scr[ex * cap:(ex + 1) * cap, :] = sel.astype(BF16)
        gate = jnp.where(sel, jnp.broadcast_to(row, (cap, seq)), 0.0).sum(axis=1, keepdims=True)
        gate_ref[ex] = jnp.broadcast_to(gate, (cap, LANES))
    rank = jnp.concatenate(ranks + [jnp.zeros((LANES - N_EXPERTS, seq), F32)], axis=0)
    rank = jnp.minimum(rank, float(cap)).astype(BF16)
    for r in range(nt):
        rc_ref[r * t:(r + 1) * t, :] = to_sublanes((rank,), r)
    xs = _dot(sel_scr[...], h_ref[...]).astype(BF16)
    xs_ref[...] = xs.reshape(N_EXPERTS, cap, D_MODEL)


def _dispatch(lg_t, h2, seq):
    n = h2.shape[0]
    nb = n // seq
    cap = EC_FACTOR * seq // N_EXPERTS
    est = (2 * seq * D_MODEL * 2 + 2 * N_EXPERTS * cap * D_MODEL * 2 + N_EXPERTS * cap * seq * 2
           + N_EXPERTS * cap * D_MODEL * 4 + 8 * seq * LANES * 4)
    return pl.pallas_call(
        functools.partial(_dispatch_kernel, seq=seq, cap=cap),
        grid=(nb,),
        in_specs=[pl.BlockSpec((N_EXPERTS, seq), lambda b: (0, b)),
                  pl.BlockSpec((seq, D_MODEL), lambda b: (b, 0))],
        out_specs=[pl.BlockSpec((N_EXPERTS, cap, D_MODEL), lambda b: (0, b, 0)),
                   pl.BlockSpec((N_EXPERTS, cap, LANES), lambda b: (0, b, 0)),
                   pl.BlockSpec((seq, LANES), lambda b: (b, 0))],
        out_shape=[jax.ShapeDtypeStruct((N_EXPERTS, nb * cap, D_MODEL), BF16),
                   jax.ShapeDtypeStruct((N_EXPERTS, nb * cap, LANES), F32),
                   jax.ShapeDtypeStruct((n, LANES), F32)],
        scratch_shapes=[pltpu.VMEM((N_EXPERTS, seq), F32), pltpu.VMEM((N_EXPERTS * cap, seq), BF16)],
        compiler_params=_params(("parallel",), est),
        name="dispatch",
    )(lg_t, h2)


EXPERT_NF = D_EXPERT // EXPERT_TF
EXPERT_ND = D_MODEL // EXPERT_TD


assert EXPERT_NF == EXPERT_ND


def _expert_kernel(xc_ref, xl_ref, gc_ref, gl_ref, wg_ref, wu_ref, wd_ref, yc_ref, yl_ref, hc_scr, hl_scr):
    e = pl.program_id(0)
    t = pl.program_id(1)
    cur = e % 2

    @pl.when(e >= 1)
    def _():
        wd = wd_ref[0].astype(BF16)
        for h_scr, g_ref, y_ref in ((hc_scr, gc_ref, yc_ref), (hl_scr, gl_ref, yl_ref)):
            acc = _dot(h_scr[1 - cur, 0], wd[0:EXPERT_TF])
            for f in range(1, EXPERT_NF):
                acc = acc + _dot(h_scr[1 - cur, f], wd[f * EXPERT_TF:(f + 1) * EXPERT_TF])
            y_ref[0] = (acc * g_ref[0, :, 0:1]).astype(BF16)

    @pl.when(e < N_EXPERTS)
    def _():
        wg = wg_ref[0].astype(BF16)
        wu = wu_ref[0].astype(BF16)
        for x_ref, h_scr in ((xc_ref, hc_scr), (xl_ref, hl_scr)):
            x = x_ref[0]
            a = _dot(x, wg)
            h_scr[cur, t] = ((a * jax.nn.sigmoid(a)) * _dot(x, wu)).astype(BF16)


def _experts(xs_c, xs_l, gate_c, gate_l, w_gate, w_up, w_down, layer):
    sc, sl = xs_c.shape[1], xs_l.shape[1]
    tf, td = EXPERT_TF, EXPERT_TD
    last = N_EXPERTS - 1
    up_expert = lambda e: jnp.minimum(e, last)
    down_expert = lambda e: jnp.maximum(e - 1, 0)
    up_tile = lambda e, t: (layer, up_expert(e), 0, jnp.where(e > last, EXPERT_NF - 1, t))
    down_tile = lambda e, t: (down_expert(e), 0, jnp.where(e == 0, 0, t))
    x_spec = lambda s: pl.BlockSpec((1, s, D_MODEL), lambda e, t: (up_expert(e), 0, 0))
    g_spec = lambda s: pl.BlockSpec((1, s, LANES), lambda e, t: (down_expert(e), 0, 0))
    est = (2 * (sc + sl) * D_MODEL * 2 + 2 * 2 * D_MODEL * tf * 4 + 2 * D_EXPERT * td * 4 + 2 * (sc + sl) * td * 4
           + 2 * (sc + sl) * D_EXPERT * 2 + 2 * D_MODEL * tf * 2 + D_EXPERT * td * 2 + 6 * sc * max(tf, td) * 4)
    return pl.pallas_call(
        _expert_kernel,
        grid=(N_EXPERTS + 1, EXPERT_NF),
        in_specs=[x_spec(sc), x_spec(sl), g_spec(sc), g_spec(sl),
                  pl.BlockSpec((None, 1, D_MODEL, tf), up_tile),
                  pl.BlockSpec((None, 1, D_MODEL, tf), up_tile),
                  pl.BlockSpec((None, 1, D_EXPERT, td), lambda e, t: (layer,) + down_tile(e, t))],
        out_specs=[pl.BlockSpec((1, sc, td), down_tile), pl.BlockSpec((1, sl, td), down_tile)],
        out_shape=[jax.ShapeDtypeStruct((N_EXPERTS, sc, D_MODEL), BF16),
                   jax.ShapeDtypeStruct((N_EXPERTS, sl, D_MODEL), BF16)],
        scratch_shapes=[pltpu.VMEM((2, EXPERT_NF, sc, tf), BF16), pltpu.VMEM((2, EXPERT_NF, sl, tf), BF16)],
        compiler_params=_params(("arbitrary", "arbitrary"), est),
        name="experts",
    )(xs_c, xs_l, gate_c, gate_l, w_gate, w_up, w_down)


def _combine_kernel(rc_ref, ye_ref, x1_ref, mod_ref, nw_ref, o_ref, *, cap):
    nslots = N_EXPERTS * cap
    shift = cap.bit_length() - 1
    expert_of_slot = lax.broadcasted_iota(jnp.int32, (LANES, nslots), 1) >> shift
    spread = jnp.where(lax.broadcasted_iota(jnp.int32, (LANES, nslots), 0) == expert_of_slot, 1.0, 0.0).astype(BF16)
    rank_of_slot = _dot(rc_ref[...].astype(BF16), spread)
    slot = (lax.broadcasted_iota(jnp.int32, (1, nslots), 1) & (cap - 1)).astype(F32)
    onehot = jnp.where(rank_of_slot == slot, 1.0, 0.0).astype(BF16)
    ffn = _dot(onehot, ye_ref[...].reshape(nslots, D_MODEL))
    o_ref[...] = x1_ref[...] + mod_ref[5:6, :] * _rms(ffn, nw_ref[3:4, :])


def _combine(rank_col, ye, x1, mod, norm_w, layer, seq, group_of_batch):
    n = x1.shape[0]
    cap = EC_FACTOR * seq // N_EXPERTS
    assert cap & (cap - 1) == 0 and cap <= LANES
    tr = COMBINE_TR
    per = seq // tr
    est = (2 * N_EXPERTS * cap * D_MODEL * 2 + 4 * tr * D_MODEL * 4 + (tr + LANES) * N_EXPERTS * cap * 6
           + 3 * tr * D_MODEL * 4)
    return pl.pallas_call(
        functools.partial(_combine_kernel, cap=cap),
        grid=(n // seq, per),
        in_specs=[pl.BlockSpec((tr, LANES), lambda b, i: (b * per + i, 0)),
                  pl.BlockSpec((N_EXPERTS, cap, D_MODEL), lambda b, i: (0, b, 0)),
                  pl.BlockSpec((tr, D_MODEL), lambda b, i: (b * per + i, 0)),
                  pl.BlockSpec((None, None, 6, D_MODEL), lambda b, i: (layer, group_of_batch(b), 0, 0)),
                  pl.BlockSpec((None, 4, D_MODEL), lambda b, i: (layer, 0, 0))],
        out_specs=pl.BlockSpec((tr, D_MODEL), lambda b, i: (b * per + i, 0)),
        out_shape=jax.ShapeDtypeStruct((n, D_MODEL), F32),
        compiler_params=_params(("parallel", "arbitrary"), est),
        name="combine",
    )(rank_col, ye, x1, mod, norm_w)


def kernel(x_prompt, x_sample, c, cache_a_k, cache_a_v, cache_b_k, cache_b_v, c_ctx, norm_w, w_ada, b_ada, w_in, a_rpb,
           b_sink, c_pool_w, c_scale, w_branch_a, w_branch_b, w_branch_c, w_out, w_router, w_gate_e, w_up_e, w_down_e):
    batch, seq_c, _ = x_prompt.shape
    dec_batch, seq_l, _ = x_sample.shape
    past = cache_a_k.shape[2]

    cond = jnp.zeros((ADA_ROWS, D_MODEL), F32).at[0].set(c_ctx).at[1:1 + dec_batch].set(c)
    mod = _adaln(cond, w_ada, b_ada).reshape(DEPTH, ADA_ROWS, 6, D_MODEL)

    ctx_group = lambda i: 0
    lat_group_inproj = lambda i: 1 + i // (seq_l // INPROJ_TM)
    lat_group_outproj = lambda i: 1 + i // (seq_l // OUTPROJ_TM)
    lat_group_batch = lambda b: 1 + b

    cbk = cache_b_k.reshape(dec_batch, DEPTH, past, B_KV_DIM)
    cbv = cache_b_v.reshape(dec_batch, DEPTH, past, B_KV_DIM)

    x_c = x_prompt.reshape(batch * seq_c, D_MODEL)
    x_l = x_sample.reshape(dec_batch * seq_l, D_MODEL)
    w_in_bf = w_in.astype(BF16)
    wa, wb, wc, wo = (w.astype(BF16) for w in (w_branch_a, w_branch_b, w_branch_c, w_out))
    wr_t = jnp.swapaxes(w_router, 1, 2)
    pool_scale = c_scale.reshape(DEPTH, 1, POOL_DIM)
    caches = ()
    for l in range(DEPTH):
        p_c, g_c = _inproj(x_c, mod, norm_w, w_in_bf, l, ctx_group)
        o_c, caches = _ctx_mix(p_c, b_sink[l], c_pool_w, pool_scale, l, seq_c, caches)
        branches_c = tuple((o_c, k) for k in range(N_BRANCH))
        m_c = _branch_merge(branches_c, g_c, wa, wb, wc, l)
        x1_c, h2_c, lg_c = _outproj(m_c, x_c, mod, norm_w, wo, wr_t, l, ctx_group)
        xs_c, gate_c, rc_c = _dispatch(lg_c, h2_c, seq_c)

        p_l, g_l = _inproj(x_l, mod, norm_w, w_in_bf, l, lat_group_inproj)
        table = _rpb_table(a_rpb[l])
        o_a = _na_latent(p_l, cache_a_k, cache_a_v, table, l, seq_l)
        o_b = _sw_latent(p_l, cbk, cbv, b_sink[l], l, seq_l)
        o_p = _pool_latent(p_l, c_pool_w, pool_scale, l, seq_l)
        m_l = _branch_merge(((o_a, 0), (o_b, 0), (o_p, 0)), g_l, wa, wb, wc, l)
        x1_l, h2_l, lg_l = _outproj(m_l, x_l, mod, norm_w, wo, wr_t, l, lat_group_outproj)
        xs_l, gate_l, rc_l = _dispatch(lg_l, h2_l, seq_l)

        ye_c, ye_l = _experts(xs_c, xs_l, gate_c, gate_l, w_gate_e, w_up_e, w_down_e, l)
        x_c = _combine(rc_c, ye_c, x1_c, mod, norm_w, l, seq_c, ctx_group)
        x_l = _combine(rc_l, ye_l, x1_l, mod, norm_w, l, seq_l, lat_group_batch)

    y_prompt = x_c.reshape(batch, seq_c, D_MODEL)
    y_sample = x_l.reshape(dec_batch, seq_l, D_MODEL)
    new_a_k, new_a_v, new_b_k, new_b_v = caches
    a_shape = (batch, DEPTH, seq_c, NA_HEADS, NA_HEAD_DIM)
    b_shape = (batch, DEPTH, seq_c, SW_KV_HEADS, SW_HEAD_DIM)
    return (y_prompt, y_sample, new_a_k.reshape(a_shape), new_a_v.reshape(a_shape),
            new_b_k.reshape(b_shape), new_b_v.reshape(b_shape))
```

```python
import functools

import numpy as np
import jax
import jax.numpy as jnp
from jax import lax
from jax.experimental import pallas as pl
from jax.experimental.pallas import tpu as pltpu

F32 = jnp.float32
BF16 = jnp.bfloat16

D_MODEL = 2048
DEPTH = 2
GRID_W = 64
NA_HEADS, NA_HEAD_DIM, NA_MAX_KH, NA_KW = 4, 128, 8, 16
SW_Q_HEADS, SW_KV_HEADS, SW_HEAD_DIM = 8, 2, 64
SW_GROUP = SW_Q_HEADS // SW_KV_HEADS
SW_WINDOW, SW_BLOCK = 128, 128
ROPE_THETA = 10000.0
POOL_WINDOWS = (2, 4, 8, 16)
POOL_GROUPS, POOL_GROUP_DIM = 4, 128
POOL_DIM = POOL_GROUPS * POOL_GROUP_DIM
A_DIM = NA_HEADS * NA_HEAD_DIM
B_Q_DIM = SW_Q_HEADS * SW_HEAD_DIM
B_KV_DIM = SW_KV_HEADS * SW_HEAD_DIM
N_BRANCH = 3
GATE_DIM = N_BRANCH * D_MODEL
QKVU_DIM = 3 * A_DIM + B_Q_DIM + 2 * B_KV_DIM + POOL_DIM
IN_DIM = QKVU_DIM + GATE_DIM
N_EXPERTS = 16
EC_FACTOR = 2
D_EXPERT = 1024
RMS_EPS = 1e-6
NEG_INF = -1e30

COL_QA = 0
COL_KB = (3 * A_DIM + B_Q_DIM) // B_KV_DIM
COL_U = COL_KB + 2

MIB = 1024 * 1024
V7X_VMEM_BYTES = 64 * MIB
V7X_VMEM_CEILING = 60000 * 1024
VMEM_FLOOR = 32 * MIB
VMEM_COMPILER_SCRATCH = 8 * MIB
assert V7X_VMEM_CEILING < V7X_VMEM_BYTES

ADA_ROWS = 16
ADA_TN = 1024
V7X_MXU_DIM = 256
LANES = 128
INPROJ_TM = 1024
INPROJ_TN = 1280
assert INPROJ_TN % V7X_MXU_DIM == 0 and IN_DIM % INPROJ_TN == 0
INPROJ_TILES = IN_DIM // INPROJ_TN
INPROJ_SPLIT = QKVU_DIM // INPROJ_TN
P_WIDTH = (INPROJ_SPLIT + 1) * INPROJ_TN
GATES_WIDTH = (INPROJ_TILES - INPROJ_SPLIT) * INPROJ_TN
GATES_COL0 = QKVU_DIM - INPROJ_SPLIT * INPROJ_TN
BMERGE_TM = 1024
BMERGE_CA = 1024
BMERGE_NA = D_MODEL // BMERGE_CA
OUTPROJ_TM = 512
OUTPROJ_CB = 512
EXPERT_TF = 512
EXPERT_TD = 1024
COMBINE_TR = 256
ROW_CHUNK = 64


def _vmem_limit(estimate_bytes):
    return int(min(V7X_VMEM_CEILING, max(VMEM_FLOOR, estimate_bytes + VMEM_COMPILER_SCRATCH)))


def _params(semantics, estimate_bytes):
    return pltpu.CompilerParams(dimension_semantics=semantics,
                                vmem_limit_bytes=_vmem_limit(estimate_bytes))


def _rms(x, g):
    ms = jnp.mean(x * x, axis=-1, keepdims=True)
    return x * lax.rsqrt(ms + RMS_EPS) * g


def _sigmoid(x):
    return 0.5 * jnp.tanh(0.5 * x) + 0.5


def _dot(a, b):
    return jnp.dot(a, b, preferred_element_type=F32)


def _dot_nt(a, b):
    return lax.dot_general(a, b, (((1,), (1,)), ((), ())), preferred_element_type=F32)


def _adaln_kernel(c_ref, w_ref, b_ref, o_ref):
    c = c_ref[...]
    s = (c * jax.nn.sigmoid(c)).astype(BF16)
    o_ref[0] = _dot(s, w_ref[0].astype(BF16)) + b_ref[0]


def _adaln(cond, w_ada, b_ada):
    n_out = w_ada.shape[-1]
    return pl.pallas_call(
        _adaln_kernel,
        grid=(DEPTH, n_out // ADA_TN),
        in_specs=[pl.BlockSpec((ADA_ROWS, D_MODEL), lambda l, j: (0, 0)),
                  pl.BlockSpec((1, D_MODEL, ADA_TN), lambda l, j: (l, 0, j)),
                  pl.BlockSpec((1, 1, ADA_TN), lambda l, j: (l, 0, j))],
        out_specs=pl.BlockSpec((1, ADA_ROWS, ADA_TN), lambda l, j: (l, 0, j)),
        out_shape=jax.ShapeDtypeStruct((DEPTH, ADA_ROWS, n_out), F32),
        compiler_params=_params(("parallel", "parallel"), 2 * D_MODEL * ADA_TN * 4),
        name="adaln",
    )(cond, w_ada, b_ada.reshape(DEPTH, 1, n_out))


def _inproj_kernel(x_ref, mod_ref, nw_ref, w_ref, o_ref, gate_ref, h_scr):
    j = pl.program_id(1)

    @pl.when(j == 0)
    def _():
        g = nw_ref[0:1, :]
        sc = 1.0 + mod_ref[1:2, :]
        sh = mod_ref[0:1, :]

        def body(r, carry):
            rows = pl.ds(pl.multiple_of(r * ROW_CHUNK, ROW_CHUNK), ROW_CHUNK)
            h_scr[rows, :] = (_rms(x_ref[rows, :], g) * sc + sh).astype(BF16)
            return carry

        lax.fori_loop(0, INPROJ_TM // ROW_CHUNK, body, 0, unroll=4)

    @pl.when(j < INPROJ_SPLIT)
    def _():
        o_ref[...] = _dot(h_scr[...], w_ref[...])

    @pl.when(j == INPROJ_SPLIT)
    def _():
        acc = _dot(h_scr[...], w_ref[...])
        o_ref[...] = acc
        gate_ref[...] = _sigmoid(acc).astype(BF16)

    @pl.when(j > INPROJ_SPLIT)
    def _():
        gate_ref[...] = _sigmoid(_dot(h_scr[...], w_ref[...])).astype(BF16)


def _mod_spec(layer, group_of):
    return pl.BlockSpec((None, None, 6, D_MODEL), lambda i, j: (layer, group_of(i), 0, 0))


def _inproj(x, mod, norm_w, w_in_bf, layer, group_of_tile):
    n = x.shape[0]
    est = (2 * INPROJ_TM * D_MODEL * 4 + 2 * D_MODEL * INPROJ_TN * 2 + 2 * INPROJ_TM * INPROJ_TN * (4 + 2)
           + INPROJ_TM * D_MODEL * 2 + 2 * INPROJ_TM * INPROJ_TN * 4)
    return pl.pallas_call(
        _inproj_kernel,
        grid=(n // INPROJ_TM, INPROJ_TILES),
        in_specs=[pl.BlockSpec((INPROJ_TM, D_MODEL), lambda i, j: (i, 0)),
                  _mod_spec(layer, group_of_tile),
                  pl.BlockSpec((None, 4, D_MODEL), lambda i, j: (layer, 0, 0)),
                  pl.BlockSpec((None, D_MODEL, INPROJ_TN), lambda i, j: (layer, 0, j))],
        out_specs=[pl.BlockSpec((INPROJ_TM, INPROJ_TN), lambda i, j: (i, jnp.minimum(j, INPROJ_SPLIT))),
                   pl.BlockSpec((INPROJ_TM, INPROJ_TN), lambda i, j: (i, jnp.maximum(j - INPROJ_SPLIT, 0)))],
        out_shape=[jax.ShapeDtypeStruct((n, P_WIDTH), F32), jax.ShapeDtypeStruct((n, GATES_WIDTH), BF16)],
        scratch_shapes=[pltpu.VMEM((INPROJ_TM, D_MODEL), BF16)],
        compiler_params=_params(("parallel", "arbitrary"), est),
        name="inproj",
    )(x, mod, norm_w, w_in_bf)


def _joint_attention(q, segments, scale, sink=None):
    scores = []
    for k, _, bias, mask in segments:
        s = _dot_nt(q, k) * scale
        if bias is not None:
            s = s + bias
        if mask is not None:
            s = jnp.where(mask, s, NEG_INF)
        scores.append(s)
    m = scores[0].max(axis=-1, keepdims=True)
    for s in scores[1:]:
        m = jnp.maximum(m, s.max(axis=-1, keepdims=True))
    if sink is not None:
        m = jnp.maximum(m, sink)
    denom = jnp.exp(sink - m) if sink is not None else 0.0
    acc = None
    for s, (_, v, _, _) in zip(scores, segments):
        e = jnp.exp(s - m)
        denom = denom + e.sum(axis=-1, keepdims=True)
        pv = _dot(e.astype(BF16), v)
        acc = pv if acc is None else acc + pv
    return acc / denom


def _pool_group(u, window, pw_bf, scale_row):
    seq = u.shape[0]
    pad = 8
    n = seq + 2 * pad
    z = jnp.zeros((pad, POOL_GROUP_DIM), F32)
    p = jnp.concatenate([z, u, z], axis=0)
    k = 1
    while k < window:
        p = p + pltpu.roll(p, n - k, 0)
        k *= 2
    win = pltpu.roll(p, window // 2, 0)[pad:pad + seq]
    t = lax.broadcasted_iota(jnp.int32, (seq, 1), 0)
    lo = jnp.maximum(t - window // 2, 0)
    hi = jnp.minimum(t - window // 2 + window, seq)
    cnt = (hi - lo).astype(F32)
    pooled = win / cnt - u
    return _dot(pooled.astype(BF16), pw_bf) * scale_row


def _unstack_heads(o, rows_per_head):
    return jnp.concatenate([o[g * rows_per_head:(g + 1) * rows_per_head] for g in range(SW_GROUP)], axis=1).astype(BF16)


def _sink_column(sink_ref, kv_head, rows_per_head):
    r = lax.broadcasted_iota(jnp.int32, (SW_GROUP * rows_per_head, 1), 0)
    col = jnp.full((SW_GROUP * rows_per_head, 1), sink_ref[kv_head * SW_GROUP], F32)
    for g in range(1, SW_GROUP):
        col = jnp.where(r >= g * rows_per_head, sink_ref[kv_head * SW_GROUP + g], col)
    return col


def _ctx_mix_kernel(sink_ref, qa_ref, ka_ref, va_ref, qb_ref, kb_ref, vb_ref, u0_ref, u1_ref, u2_ref, u3_ref,
                    pw_ref, ps_ref, *rest, layer):
    o_ref = rest[-5]
    seq = qa_ref.shape[0]

    def put(cache_ref, lead, val):
        if cache_ref.shape[-2] == seq:
            cache_ref[lead + (slice(None), slice(None))] = val
        else:
            for h in range(NA_HEADS):
                cache_ref[lead + (pl.ds(h, seq, stride=NA_HEADS), slice(None))] = \
                    val[:, h * NA_HEAD_DIM:(h + 1) * NA_HEAD_DIM]

    for cache_ref, src_ref in zip(rest[-4:], (ka_ref, va_ref, kb_ref, vb_ref)):
        if len(cache_ref.shape) == 2:
            put(cache_ref, (), src_ref[...])
        else:
            for d in range(cache_ref.shape[0]):
                put(cache_ref, (d,), src_ref[...] if d == layer else jnp.zeros(src_ref.shape, F32))
    for h in range(NA_HEADS):
        sl = slice(h * NA_HEAD_DIM, (h + 1) * NA_HEAD_DIM)
        o = _joint_attention(qa_ref[:, sl].astype(BF16),
                             [(ka_ref[:, sl].astype(BF16), va_ref[:, sl].astype(BF16), None, None)],
                             NA_HEAD_DIM ** -0.5)
        o_ref[:, sl] = o.astype(BF16)
    for hk in range(SW_KV_HEADS):
        ksl = slice(hk * SW_HEAD_DIM, (hk + 1) * SW_HEAD_DIM)
        q = jnp.concatenate(
            [qb_ref[:, (hk * SW_GROUP + g) * SW_HEAD_DIM:(hk * SW_GROUP + g + 1) * SW_HEAD_DIM] for g in range(SW_GROUP)],
            axis=0).astype(BF16)
        o = _joint_attention(q, [(kb_ref[:, ksl].astype(BF16), vb_ref[:, ksl].astype(BF16), None, None)],
                             SW_HEAD_DIM ** -0.5, sink=_sink_column(sink_ref, hk, seq))
        c0 = A_DIM + hk * SW_GROUP * SW_HEAD_DIM
        o_ref[:, c0:c0 + SW_GROUP * SW_HEAD_DIM] = _unstack_heads(o, seq)
    for gi, u_ref in enumerate((u0_ref, u1_ref, u2_ref, u3_ref)):
        c0 = A_DIM + B_Q_DIM + gi * POOL_GROUP_DIM
        o_ref[:, c0:c0 + POOL_GROUP_DIM] = _pool_group(
            u_ref[...], POOL_WINDOWS[gi], pw_ref[gi].astype(BF16),
            ps_ref[:, gi * POOL_GROUP_DIM:(gi + 1) * POOL_GROUP_DIM]).astype(BF16)


def _ctx_mix(p, sink_l, pool_w, pool_scale, layer, seq, caches):
    n = p.shape[0]
    batch = n // seq
    wide = lambda c: pl.BlockSpec((seq, A_DIM), lambda b, c=c: (b, c))
    narrow = lambda c: pl.BlockSpec((seq, B_KV_DIM), lambda b, c=c: (b, c))
    in_specs = [pl.BlockSpec(memory_space=pltpu.SMEM)]
    in_specs += [wide(COL_QA + i) for i in range(4)]
    in_specs += [narrow(COL_KB), narrow(COL_KB + 1)]
    in_specs += [narrow(COL_U + g) for g in range(POOL_GROUPS)]
    in_specs += [pl.BlockSpec((None, POOL_GROUPS, POOL_GROUP_DIM, POOL_GROUP_DIM), lambda b: (layer, 0, 0, 0)),
                 pl.BlockSpec((None, 1, POOL_DIM), lambda b: (layer, 0, 0))]
    n_fixed = len(in_specs)
    in_specs += [pl.BlockSpec(memory_space=pl.ANY)] * len(caches)
    if caches:
        cache_spec = lambda rows, width: pl.BlockSpec((None, None, rows, width), lambda b: (b, layer, 0, 0))
    else:
        cache_spec = lambda rows, width: pl.BlockSpec((None, DEPTH, rows, width), lambda b: (b, 0, 0, 0))
    cache_shape = lambda rows, width: jax.ShapeDtypeStruct((batch, DEPTH, rows, width), F32)
    a_cache = (seq * NA_HEADS, NA_HEAD_DIM)
    b_cache = (seq, B_KV_DIM)
    outs = pl.pallas_call(
        functools.partial(_ctx_mix_kernel, layer=layer),
        grid=(batch,),
        in_specs=in_specs,
        out_specs=[pl.BlockSpec((seq, 3 * A_DIM), lambda b: (b, 0)),
                   cache_spec(*a_cache), cache_spec(*a_cache), cache_spec(*b_cache), cache_spec(*b_cache)],
        out_shape=[jax.ShapeDtypeStruct((n, 3 * A_DIM), BF16),
                   cache_shape(*a_cache), cache_shape(*a_cache), cache_shape(*b_cache), cache_shape(*b_cache)],
        input_output_aliases={n_fixed + k: 1 + k for k in range(len(caches))},
        compiler_params=_params(("parallel",), 24 * MIB),
        name="ctx_mix",
    )(sink_l, p, p, p, p, p, p, p, p, p, p, pool_w, pool_scale, *caches)
    return outs[0], tuple(outs[1:])


NA_PAIR_ROWS = 2 * NA_MAX_KH - 2


def _rpb_table_kernel(rpb_ref, t_ref):
    lane = lax.broadcasted_iota(jnp.int32, (GRID_W, 2 * GRID_W), 1)
    qc = lax.broadcasted_iota(jnp.int32, (GRID_W, 2 * GRID_W), 0)
    kc = lane & (GRID_W - 1)
    upper = lane >= GRID_W
    dcm = jnp.clip(kc - qc + NA_KW - 1, 0, 2 * NA_KW - 2)
    col0 = jnp.clip(qc - NA_KW // 2, 0, GRID_W - NA_KW)
    inside = (kc >= col0) & (kc < col0 + NA_KW)
    n_dc = 2 * NA_KW - 1
    n_dr = 2 * NA_MAX_KH - 1

    def body(i, carry):
        h = i // NA_PAIR_ROWS
        dr = i - h * NA_PAIR_ROWS
        base = (h * n_dr + dr) * n_dc
        acc = jnp.zeros((GRID_W, 2 * GRID_W), F32)
        for dc in range(n_dc):
            val = jnp.where(upper, rpb_ref[base + n_dc + dc], rpb_ref[base + dc])
            acc = jnp.where(dcm == dc, val, acc)
        t_ref[i] = jnp.where(inside, acc, NEG_INF)
        return carry

    lax.fori_loop(0, NA_HEADS * NA_PAIR_ROWS, body, 0)


def _rpb_table(rpb_l):
    return pl.pallas_call(
        _rpb_table_kernel,
        in_specs=[pl.BlockSpec(memory_space=pltpu.SMEM)],
        out_specs=pl.BlockSpec(memory_space=pltpu.VMEM),
        out_shape=jax.ShapeDtypeStruct((NA_HEADS * NA_PAIR_ROWS, GRID_W, 2 * GRID_W), F32),
        name="rpb_table",
    )(rpb_l.reshape(-1))


def _na_kernel(q_ref, k_ref, v_ref, kc_ref, vc_ref, t_ref, o_ref, kb_scr, vb_scr, kcb_scr, vcb_scr, *, rows):
    qr = pl.program_id(1)

    @pl.when(qr == 0)
    def _():
        kb_scr[...] = k_ref[...].astype(BF16)
        vb_scr[...] = v_ref[...].astype(BF16)
        for h in range(NA_HEADS):
            sl = slice(h * NA_HEAD_DIM, (h + 1) * NA_HEAD_DIM)
            kcb_scr[:, sl] = kc_ref[:, h, :].astype(BF16)
            vcb_scr[:, sl] = vc_ref[:, h, :].astype(BF16)

    row0 = jnp.clip(qr - NA_MAX_KH // 2, 0, rows - NA_MAX_KH)
    start = pl.multiple_of(row0 * GRID_W, GRID_W)
    nkeys = NA_MAX_KH * GRID_W
    d0 = row0 - qr + NA_MAX_KH - 1
    for h in range(NA_HEADS):
        sl = slice(h * NA_HEAD_DIM, (h + 1) * NA_HEAD_DIM)
        bias = jnp.concatenate([t_ref[h * NA_PAIR_ROWS + d0 + 2 * i] for i in range(NA_MAX_KH // 2)], axis=1)
        o = _joint_attention(
            q_ref[:, sl].astype(BF16),
            [(kb_scr[pl.ds(start, nkeys), sl], vb_scr[pl.ds(start, nkeys), sl], bias, None),
             (kcb_scr[:, sl], vcb_scr[:, sl], None, None)],
            NA_HEAD_DIM ** -0.5)
        o_ref[:, sl] = o.astype(BF16)


def _na_latent(p, cache_k, cache_v, table, layer, seq):
    n = p.shape[0]
    rows = seq // GRID_W
    past = cache_k.shape[2]
    ctx_spec = pl.BlockSpec((None, None, past, NA_HEADS, NA_HEAD_DIM), lambda b, r: (b, layer, 0, 0, 0))
    return pl.pallas_call(
        functools.partial(_na_kernel, rows=rows),
        grid=(n // seq, rows),
        in_specs=[pl.BlockSpec((GRID_W, A_DIM), lambda b, r: (b * rows + r, COL_QA)),
                  pl.BlockSpec((seq, A_DIM), lambda b, r: (b, COL_QA + 1)),
                  pl.BlockSpec((seq, A_DIM), lambda b, r: (b, COL_QA + 2)),
                  ctx_spec, ctx_spec,
                  pl.BlockSpec(table.shape, lambda b, r: (0, 0, 0))],
        out_specs=pl.BlockSpec((GRID_W, A_DIM), lambda b, r: (b * rows + r, 0)),
        out_shape=jax.ShapeDtypeStruct((n, A_DIM), BF16),
        scratch_shapes=[pltpu.VMEM((seq, A_DIM), BF16), pltpu.VMEM((seq, A_DIM), BF16),
                        pltpu.VMEM((past, A_DIM), BF16), pltpu.VMEM((past, A_DIM), BF16)],
        compiler_params=_params(("parallel", "arbitrary"), 16 * MIB),
        name="na_latent",
    )(p, p, p, cache_k, cache_v, table)


def _rope_tables(seq):
    nfreq = SW_HEAD_DIM // 4
    inv = 1.0 / (ROPE_THETA ** (np.arange(nfreq, dtype=np.float32) / np.float32(nfreq)))
    t = np.arange(seq)
    pos = (t // GRID_W, t % GRID_W)
    cos = np.zeros((seq, SW_HEAD_DIM), np.float32)
    sin_next = np.zeros((seq, SW_HEAD_DIM), np.float32)
    sin_prev = np.zeros((seq, SW_HEAD_DIM), np.float32)
    for a in range(2):
        ang = pos[a].astype(np.float32)[:, None] * inv[None, :].astype(np.float32)
        c, s = np.cos(ang).astype(np.float32), np.sin(ang).astype(np.float32)
        lo = 2 * a * nfreq
        cos[:, lo:lo + nfreq] = c
        cos[:, lo + nfreq:lo + 2 * nfreq] = c
        sin_next[:, lo:lo + nfreq] = -s
        sin_prev[:, lo + nfreq:lo + 2 * nfreq] = s
    tile = lambda x: jnp.asarray(np.tile(x, (1, 128 // SW_HEAD_DIM)))
    return tile(cos), tile(sin_next), tile(sin_prev)


def _rope(x, cos, sin_next, sin_prev):
    nfreq = SW_HEAD_DIM // 4
    return x * cos + pltpu.roll(x, 128 - nfreq, 1) * sin_next + pltpu.roll(x, nfreq, 1) * sin_prev


def _sw_kernel(sink_ref, q_ref, k_ref, v_ref, kc_ref, vc_ref, cq_ref, snq_ref, spq_ref, ck_ref, snk_ref, spk_ref,
               o_ref, kr_scr, *, seq):
    n = pl.program_id(1)

    @pl.when(n == 0)
    def _():
        kr_scr[...] = _rope(k_ref[...], ck_ref[...], snk_ref[...], spk_ref[...]).astype(BF16)

    nwin = 3 * SW_BLOCK
    kstart = pl.multiple_of(jnp.clip((n - 1) * SW_BLOCK, 0, seq - nwin), SW_BLOCK)
    cq, snq, spq = cq_ref[...], snq_ref[...], spq_ref[...]
    q = jnp.concatenate([_rope(q_ref[:, c * 128:(c + 1) * 128], cq, snq, spq) for c in range(B_Q_DIM // 128)],
                        axis=1).astype(BF16)
    rows = SW_GROUP * SW_BLOCK
    qpos = n * SW_BLOCK + (lax.broadcasted_iota(jnp.int32, (rows, nwin), 0) & (SW_BLOCK - 1))
    kpos = kstart + lax.broadcasted_iota(jnp.int32, (rows, nwin), 1)
    band = jnp.abs(qpos - kpos) <= SW_WINDOW
    for hk in range(SW_KV_HEADS):
        ksl = slice(hk * SW_HEAD_DIM, (hk + 1) * SW_HEAD_DIM)
        qs = jnp.concatenate(
            [q[:, (hk * SW_GROUP + g) * SW_HEAD_DIM:(hk * SW_GROUP + g + 1) * SW_HEAD_DIM] for g in range(SW_GROUP)], axis=0)
        o = _joint_attention(
            qs,
            [(kr_scr[pl.ds(kstart, nwin), ksl], v_ref[pl.ds(kstart, nwin), ksl].astype(BF16), None, band),
             (kc_ref[:, ksl].astype(BF16), vc_ref[:, ksl].astype(BF16), None, None)],
            SW_HEAD_DIM ** -0.5, sink=_sink_column(sink_ref, hk, SW_BLOCK))
        c0 = hk * SW_GROUP * SW_HEAD_DIM
        o_ref[:, c0:c0 + SW_GROUP * SW_HEAD_DIM] = _unstack_heads(o, SW_BLOCK)


def _sw_latent(p, cache_k, cache_v, sink_l, layer, seq):
    n = p.shape[0]
    nb = seq // SW_BLOCK
    past = cache_k.shape[2]
    cos, sin_next, sin_prev = _rope_tables(seq)
    ctx_spec = pl.BlockSpec((None, None, past, B_KV_DIM), lambda b, i: (b, layer, 0, 0))
    tab_q = pl.BlockSpec((SW_BLOCK, 128), lambda b, i: (i, 0))
    tab_k = pl.BlockSpec((seq, 128), lambda b, i: (0, 0))
    return pl.pallas_call(
        functools.partial(_sw_kernel, seq=seq),
        grid=(n // seq, nb),
        in_specs=[pl.BlockSpec(memory_space=pltpu.SMEM),
                  pl.BlockSpec((SW_BLOCK, B_Q_DIM), lambda b, i: (b * nb + i, COL_QA + 3)),
                  pl.BlockSpec((seq, B_KV_DIM), lambda b, i: (b, COL_KB)),
                  pl.BlockSpec((seq, B_KV_DIM), lambda b, i: (b, COL_KB + 1)),
                  ctx_spec, ctx_spec, tab_q, tab_q, tab_q, tab_k, tab_k, tab_k],
        out_specs=pl.BlockSpec((SW_BLOCK, B_Q_DIM), lambda b, i: (b * nb + i, 0)),
        out_shape=jax.ShapeDtypeStruct((n, B_Q_DIM), BF16),
        scratch_shapes=[pltpu.VMEM((seq, B_KV_DIM), BF16)],
        compiler_params=_params(("parallel", "arbitrary"), 16 * MIB),
        name="sw_latent",
    )(sink_l, p, p, p, cache_k, cache_v, cos, sin_next, sin_prev, cos, sin_next, sin_prev)


def _pool_kernel(u0_ref, u1_ref, u2_ref, u3_ref, pw_ref, ps_ref, o_ref):
    for gi, u_ref in enumerate((u0_ref, u1_ref, u2_ref, u3_ref)):
        sl = slice(gi * POOL_GROUP_DIM, (gi + 1) * POOL_GROUP_DIM)
        o_ref[:, sl] = _pool_group(u_ref[...], POOL_WINDOWS[gi], pw_ref[gi].astype(BF16), ps_ref[:, sl]).astype(BF16)


def _pool_latent(p, pool_w, pool_scale, layer, seq):
    n = p.shape[0]
    return pl.pallas_call(
        _pool_kernel,
        grid=(n // seq,),
        in_specs=[pl.BlockSpec((seq, POOL_GROUP_DIM), lambda b, g=g: (b, COL_U + g)) for g in range(POOL_GROUPS)]
        + [pl.BlockSpec((None, POOL_GROUPS, POOL_GROUP_DIM, POOL_GROUP_DIM), lambda b: (layer, 0, 0, 0)),
           pl.BlockSpec((None, 1, POOL_DIM), lambda b: (layer, 0, 0))],
        out_specs=pl.BlockSpec((seq, POOL_DIM), lambda b: (b, 0)),
        out_shape=jax.ShapeDtypeStruct((n, POOL_DIM), BF16),
        compiler_params=_params(("parallel",), 16 * MIB),
        name="pool_latent",
    )(p, p, p, p, pool_w, pool_scale)


def _split_bf16(x):
    hi = x.astype(BF16)
    return hi, (x - hi.astype(F32)).astype(BF16)


def _branch_merge_kernel(oa_ref, ob_ref, oc_ref, ga_ref, gb_ref, gc_ref, wa_ref, wb_ref, wc_ref, m_ref):
    s = pl.program_id(1)
    for t in range(BMERGE_NA):
        @pl.when(s == t)
        def _(t=t):
            cols = slice(t * BMERGE_CA, (t + 1) * BMERGE_CA)
            m = (ga_ref[...].astype(F32) * _dot(oa_ref[...], wa_ref[:, cols])
                 + gb_ref[...].astype(F32) * _dot(ob_ref[...], wb_ref[:, cols])
                 + gc_ref[...].astype(F32) * _dot(oc_ref[...], wc_ref[:, cols]))
            m_ref[...] = m.astype(BF16)


def _branch_merge(branches, gates, wa, wb, wc, layer):
    n = gates.shape[0]
    tm, ca = BMERGE_TM, BMERGE_CA
    gate = lambda k: pl.BlockSpec((pl.Element(tm), pl.Element(ca)),
                                  lambda i, s, k=k: (i * tm, pl.multiple_of(GATES_COL0 + k * D_MODEL + s * ca, LANES)))
    branch = lambda col: pl.BlockSpec((tm, A_DIM), lambda i, s: (i, col))
    resident = lambda rows: pl.BlockSpec((None, rows, D_MODEL), lambda i, s: (layer, 0, 0),
                                         pipeline_mode=pl.Buffered(1))
    oa, ob, oc = branches
    est = (2 * 3 * tm * A_DIM * 2 + 2 * 3 * tm * ca * 2 + 2 * tm * ca * 2 + 3 * A_DIM * D_MODEL * 2 + 8 * tm * ca * 4)
    return pl.pallas_call(
        _branch_merge_kernel,
        grid=(n // tm, BMERGE_NA),
        in_specs=[branch(oa[1]), branch(ob[1]), branch(oc[1]), gate(0), gate(1), gate(2),
                  resident(A_DIM), resident(B_Q_DIM), resident(POOL_DIM)],
        out_specs=pl.BlockSpec((tm, ca), lambda i, s: (i, s)),
        out_shape=jax.ShapeDtypeStruct((n, D_MODEL), BF16),
        compiler_params=_params(("parallel", "arbitrary"), est),
        name="branch_merge",
    )(oa[0], ob[0], oc[0], gates, gates, gates, wa, wb, wc)


def _outproj_kernel(m_ref, x_ref, mod_ref, nw_ref, wo_ref, wr_ref, x1_ref, h2_ref, lg_ref, y_scr):
    tiles = [slice(c * OUTPROJ_CB, (c + 1) * OUTPROJ_CB) for c in range(D_MODEL // OUTPROJ_CB)]
    rows = x_ref.shape[0]
    ss = jnp.zeros((rows, 1), F32)
    for cols in tiles:
        y = _dot(m_ref[...], wo_ref[:, cols])
        y_scr[:, cols] = y
        ss = ss + (y * y).sum(axis=-1, keepdims=True)
    r1 = lax.rsqrt(ss / D_MODEL + RMS_EPS)
    gain1 = mod_ref[2:3, :] * nw_ref[1:2, :]
    gain2 = nw_ref[2:3, :] * (1.0 + mod_ref[4:5, :])
    ss = jnp.zeros((rows, 1), F32)
    for cols in tiles:
        x1 = x_ref[:, cols] + (y_scr[:, cols] * r1) * gain1[:, cols]
        x1_ref[:, cols] = x1
        ss = ss + (x1 * x1).sum(axis=-1, keepdims=True)
    r2 = lax.rsqrt(ss / D_MODEL + RMS_EPS)
    lg = jnp.zeros(lg_ref.shape, F32)
    for cols in tiles:
        h2 = (x1_ref[:, cols] * r2) * gain2[:, cols] + mod_ref[3:4, cols]
        h2_ref[:, cols] = h2.astype(BF16)
        h_hi, h_lo = _split_bf16(h2)
        w_hi, w_lo = _split_bf16(wr_ref[:, cols])
        lg = lg + (_dot_nt(w_hi, h_hi) + (_dot_nt(w_hi, h_lo) + _dot_nt(w_lo, h_hi)))
    lg_ref[...] = lg


def _outproj(m, x, mod, norm_w, wo, wr_t, layer, group_of_tile):
    n = x.shape[0]
    tm = OUTPROJ_TM
    row = lambda: pl.BlockSpec((tm, D_MODEL), lambda i: (i, 0))
    est = (2 * tm * D_MODEL * (2 + 4 + 4 + 2) + D_MODEL * D_MODEL * 2 + tm * D_MODEL * 4 + 4 * tm * OUTPROJ_CB * 4)
    return pl.pallas_call(
        _outproj_kernel,
        grid=(n // tm,),
        in_specs=[row(), row(),
                  pl.BlockSpec((None, None, 6, D_MODEL), lambda i: (layer, group_of_tile(i), 0, 0)),
                  pl.BlockSpec((None, 4, D_MODEL), lambda i: (layer, 0, 0)),
                  pl.BlockSpec((None, D_MODEL, D_MODEL), lambda i: (layer, 0, 0), pipeline_mode=pl.Buffered(1)),
                  pl.BlockSpec((None, N_EXPERTS, D_MODEL), lambda i: (layer, 0, 0))],
        out_specs=[row(), row(), pl.BlockSpec((N_EXPERTS, tm), lambda i: (0, i))],
        out_shape=[jax.ShapeDtypeStruct((n, D_MODEL), F32), jax.ShapeDtypeStruct((n, D_MODEL), BF16),
                   jax.ShapeDtypeStruct((N_EXPERTS, n), F32)],
        scratch_shapes=[pltpu.VMEM((tm, D_MODEL), F32)],
        compiler_params=_params(("parallel",), est),
        name="outproj",
    )(m, x, mod, norm_w, wo, wr_t)


RANK_TILE = LANES


def _dispatch_kernel(lg_ref, h_ref, xs_ref, gate_ref, rc_ref, aff_scr, sel_scr, *, seq, cap):
    t = RANK_TILE
    nt = seq // t
    lg = lg_ref[...]
    e = jnp.exp(lg - lg.max(axis=0, keepdims=True))
    aff_scr[...] = e / e.sum(axis=0, keepdims=True)
    aff = aff_scr[...]
    ident = jnp.where(lax.broadcasted_iota(jnp.int32, (t, t), 0) == lax.broadcasted_iota(jnp.int32, (t, t), 1),
                      1.0, 0.0).astype(BF16)

    def to_sublanes(pieces, r):
        tile = slice(r * t, (r + 1) * t)
        out = _dot_nt(ident, pieces[0][:, tile])
        for piece in pieces[1:]:
            out = out + _dot_nt(ident, piece[:, tile])
        return out

    a1 = aff.astype(BF16)
    r1 = aff - a1.astype(F32)
    a2 = r1.astype(BF16)
    a3 = (r1 - a2.astype(F32)).astype(BF16)
    aff_cols = [to_sublanes((a1, a2, a3), r) for r in range(nt)]

    earlier = lax.broadcasted_iota(jnp.int32, (t, t), 0) < lax.broadcasted_iota(jnp.int32, (t, t), 1)
    slot = lax.broadcasted_iota(jnp.int32, (cap, seq), 0).astype(F32)
    ranks = []
    for ex in range(N_EXPERTS):
        row = aff[ex:ex + 1, :]
        cols = [jnp.broadcast_to(aff_cols[r][:, ex:ex + 1], (t, t)) for r in range(nt)]
        counts = []
        for c in range(nt):
            rowb = jnp.broadcast_to(row[:, c * t:(c + 1) * t], (t, t))
            acc = jnp.zeros((t, t), F32)
            for r in range(nt):
                if r < c:
                    beats = cols[r] >= rowb
                elif r > c:
                    beats = cols[r] > rowb
                else:
                    beats = (cols[r] > rowb) | (earlier & (cols[r] == rowb))
                acc = acc + jnp.where(beats, 1.0, 0.0)
            counts.append(acc.sum(axis=0, keepdims=True))
        rank_row = jnp.concatenate(counts, axis=1) if nt > 1 else counts[0]
        ranks.append(rank_row)
        sel = slot == rank_row
        sel_scr[ex * cap:(ex + 1) * cap, :] = sel.astype(BF16)
        gate = jnp.where(sel, jnp.broadcast_to(row, (cap, seq)), 0.0).sum(axis=1, keepdims=True)
        gate_ref[ex] = jnp.broadcast_to(gate, (cap, LANES))
    rank = jnp.concatenate(ranks + [jnp.zeros((LANES - N_EXPERTS, seq), F32)], axis=0)
    rank = jnp.minimum(rank, float(cap)).astype(BF16)
    for r in range(nt):
        rc_ref[r * t:(r + 1) * t, :] = to_sublanes((rank,), r)
    xs = _dot(sel_scr[...], h_ref[...]).astype(BF16)
    xs_ref[...] = xs.reshape(N_EXPERTS, cap, D_MODEL)


def _dispatch(lg_t, h2, seq):
    n = h2.shape[0]
    nb = n // seq
    cap = EC_FACTOR * seq // N_EXPERTS
    est = (2 * seq * D_MODEL * 2 + 2 * N_EXPERTS * cap * D_MODEL * 2 + N_EXPERTS * cap * seq * 2
           + N_EXPERTS * cap * D_MODEL * 4 + 8 * seq * LANES * 4)
    return pl.pallas_call(
        functools.partial(_dispatch_kernel, seq=seq, cap=cap),
        grid=(nb,),
        in_specs=[pl.BlockSpec((N_EXPERTS, seq), lambda b: (0, b)),
                  pl.BlockSpec((seq, D_MODEL), lambda b: (b, 0))],
        out_specs=[pl.BlockSpec((N_EXPERTS, cap, D_MODEL), lambda b: (0, b, 0)),
                   pl.BlockSpec((N_EXPERTS, cap, LANES), lambda b: (0, b, 0)),
                   pl.BlockSpec((seq, LANES), lambda b: (b, 0))],
        out_shape=[jax.ShapeDtypeStruct((N_EXPERTS, nb * cap, D_MODEL), BF16),
                   jax.ShapeDtypeStruct((N_EXPERTS, nb * cap, LANES), F32),
                   jax.ShapeDtypeStruct((n, LANES), F32)],
        scratch_shapes=[pltpu.VMEM((N_EXPERTS, seq), F32), pltpu.VMEM((N_EXPERTS * cap, seq), BF16)],
        compiler_params=_params(("parallel",), est),
        name="dispatch",
    )(lg_t, h2)


EXPERT_NF = D_EXPERT // EXPERT_TF
EXPERT_ND = D_MODEL // EXPERT_TD


assert EXPERT_NF == EXPERT_ND


def _expert_kernel(xc_ref, xl_ref, gc_ref, gl_ref, wg_ref, wu_ref, wd_ref, yc_ref, yl_ref, hc_scr, hl_scr):
    e = pl.program_id(0)
    t = pl.program_id(1)
    cur = e % 2

    @pl.when(e >= 1)
    def _():
        wd = wd_ref[0].astype(BF16)
        for h_scr, g_ref, y_ref in ((hc_scr, gc_ref, yc_ref), (hl_scr, gl_ref, yl_ref)):
            acc = _dot(h_scr[1 - cur, 0], wd[0:EXPERT_TF])
            for f in range(1, EXPERT_NF):
                acc = acc + _dot(h_scr[1 - cur, f], wd[f * EXPERT_TF:(f + 1) * EXPERT_TF])
            y_ref[0] = (acc * g_ref[0, :, 0:1]).astype(BF16)

    @pl.when(e < N_EXPERTS)
    def _():
        wg = wg_ref[0].astype(BF16)
        wu = wu_ref[0].astype(BF16)
        for x_ref, h_scr in ((xc_ref, hc_scr), (xl_ref, hl_scr)):
            x = x_ref[0]
            a = _dot(x, wg)
            h_scr[cur, t] = ((a * jax.nn.sigmoid(a)) * _dot(x, wu)).astype(BF16)


def _experts(xs_c, xs_l, gate_c, gate_l, w_gate, w_up, w_down, layer):
    sc, sl = xs_c.shape[1], xs_l.shape[1]
    tf, td = EXPERT_TF, EXPERT_TD
    last = N_EXPERTS - 1
    up_expert = lambda e: jnp.minimum(e, last)
    down_expert = lambda e: jnp.maximum(e - 1, 0)
    up_tile = lambda e, t: (layer, up_expert(e), 0, jnp.where(e > last, EXPERT_NF - 1, t))
    down_tile = lambda e, t: (down_expert(e), 0, jnp.where(e == 0, 0, t))
    x_spec = lambda s: pl.BlockSpec((1, s, D_MODEL), lambda e, t: (up_expert(e), 0, 0))
    g_spec = lambda s: pl.BlockSpec((1, s, LANES), lambda e, t: (down_expert(e), 0, 0))
    est = (2 * (sc + sl) * D_MODEL * 2 + 2 * 2 * D_MODEL * tf * 4 + 2 * D_EXPERT * td * 4 + 2 * (sc + sl) * td * 4
           + 2 * (sc + sl) * D_EXPERT * 2 + 2 * D_MODEL * tf * 2 + D_EXPERT * td * 2 + 6 * sc * max(tf, td) * 4)
    return pl.pallas_call(
        _expert_kernel,
        grid=(N_EXPERTS + 1, EXPERT_NF),
        in_specs=[x_spec(sc), x_spec(sl), g_spec(sc), g_spec(sl),
                  pl.BlockSpec((None, 1, D_MODEL, tf), up_tile),
                  pl.BlockSpec((None, 1, D_MODEL, tf), up_tile),
                  pl.BlockSpec((None, 1, D_EXPERT, td), lambda e, t: (layer,) + down_tile(e, t))],
        out_specs=[pl.BlockSpec((1, sc, td), down_tile), pl.BlockSpec((1, sl, td), down_tile)],
        out_shape=[jax.ShapeDtypeStruct((N_EXPERTS, sc, D_MODEL), BF16),
                   jax.ShapeDtypeStruct((N_EXPERTS, sl, D_MODEL), BF16)],
        scratch_shapes=[pltpu.VMEM((2, EXPERT_NF, sc, tf), BF16), pltpu.VMEM((2, EXPERT_NF, sl, tf), BF16)],
        compiler_params=_params(("arbitrary", "arbitrary"), est),
        name="experts",
    )(xs_c, xs_l, gate_c, gate_l, w_gate, w_up, w_down)


COMBINE_RING = 3


def _combine_ring_kernel(rc_ref, ye_hbm, x1_hbm, mod_ref, nw_ref, o_ref, ye_buf, x1_buf, sem, *, cap, rows, steps):
    b = pl.program_id(0)

    def copies(step, slot):
        return (pltpu.make_async_copy(ye_hbm.at[:, pl.ds(step * cap, cap), :], ye_buf.at[slot], sem.at[0, slot]),
                pltpu.make_async_copy(x1_hbm.at[pl.ds(step * rows, rows), :], x1_buf.at[slot], sem.at[1, slot]))

    @pl.when(b == 0)
    def _():
        for s in range(min(COMBINE_RING - 1, steps)):
            for c in copies(s, s):
                c.start()

    ahead = b + (COMBINE_RING - 1)

    @pl.when(ahead < steps)
    def _():
        for c in copies(ahead, ahead % COMBINE_RING):
            c.start()

    slot = b % COMBINE_RING
    for c in copies(b, slot):
        c.wait()
    _combine_body(rc_ref[...], ye_buf[slot], x1_buf[slot], mod_ref, nw_ref, o_ref, cap)


def _combine_kernel(rc_ref, ye_ref, x1_ref, mod_ref, nw_ref, o_ref, *, cap):
    _combine_body(rc_ref[...], ye_ref[...], x1_ref[...], mod_ref, nw_ref, o_ref, cap)


def _combine_body(rc, ye, x1, mod_ref, nw_ref, o_ref, cap):
    nslots = N_EXPERTS * cap
    shift = cap.bit_length() - 1
    expert_of_slot = lax.broadcasted_iota(jnp.int32, (LANES, nslots), 1) >> shift
    spread = jnp.where(lax.broadcasted_iota(jnp.int32, (LANES, nslots), 0) == expert_of_slot, 1.0, 0.0).astype(BF16)
    rank_of_slot = _dot(rc.astype(BF16), spread)
    slot = (lax.broadcasted_iota(jnp.int32, (1, nslots), 1) & (cap - 1)).astype(F32)
    onehot = jnp.where(rank_of_slot == slot, 1.0, 0.0).astype(BF16)
    ffn = _dot(onehot, ye.reshape(nslots, D_MODEL))
    o_ref[...] = x1 + mod_ref[5:6, :] * _rms(ffn, nw_ref[3:4, :])


def _combine(rank_col, ye, x1, mod, norm_w, layer, seq, group_of_batch):
    n = x1.shape[0]
    cap = EC_FACTOR * seq // N_EXPERTS
    assert cap & (cap - 1) == 0 and cap <= LANES
    tr = COMBINE_TR
    per = seq // tr
    est = (2 * N_EXPERTS * cap * D_MODEL * 2 + 4 * tr * D_MODEL * 4 + (tr + LANES) * N_EXPERTS * cap * 6
           + 3 * tr * D_MODEL * 4)
    if per == 1:
        steps = n // seq
        ring = COMBINE_RING * (N_EXPERTS * cap * D_MODEL * 2 + tr * D_MODEL * 4)
        return pl.pallas_call(
            functools.partial(_combine_ring_kernel, cap=cap, rows=tr, steps=steps),
            grid=(steps,),
            in_specs=[pl.BlockSpec((tr, LANES), lambda b: (b, 0)),
                      pl.BlockSpec(memory_space=pl.ANY), pl.BlockSpec(memory_space=pl.ANY),
                      pl.BlockSpec((None, None, 6, D_MODEL), lambda b: (layer, group_of_batch(b), 0, 0)),
                      pl.BlockSpec((None, 4, D_MODEL), lambda b: (layer, 0, 0))],
            out_specs=pl.BlockSpec((tr, D_MODEL), lambda b: (b, 0)),
            out_shape=jax.ShapeDtypeStruct((n, D_MODEL), F32),
            scratch_shapes=[pltpu.VMEM((COMBINE_RING, N_EXPERTS, cap, D_MODEL), BF16),
                            pltpu.VMEM((COMBINE_RING, tr, D_MODEL), F32),
                            pltpu.SemaphoreType.DMA((2, COMBINE_RING))],
            compiler_params=_params(("arbitrary",), est + ring),
            name="combine_ring",
        )(rank_col, ye, x1, mod, norm_w)
    return pl.pallas_call(
        functools.partial(_combine_kernel, cap=cap),
        grid=(n // seq, per),
        in_specs=[pl.BlockSpec((tr, LANES), lambda b, i: (b * per + i, 0)),
                  pl.BlockSpec((N_EXPERTS, cap, D_MODEL), lambda b, i: (0, b, 0)),
                  pl.BlockSpec((tr, D_MODEL), lambda b, i: (b * per + i, 0)),
                  pl.BlockSpec((None, None, 6, D_MODEL), lambda b, i: (layer, group_of_batch(b), 0, 0)),
                  pl.BlockSpec((None, 4, D_MODEL), lambda b, i: (layer, 0, 0))],
        out_specs=pl.BlockSpec((tr, D_MODEL), lambda b, i: (b * per + i, 0)),
        out_shape=jax.ShapeDtypeStruct((n, D_MODEL), F32),
        compiler_params=_params(("parallel", "arbitrary"), est),
        name="combine",
    )(rank_col, ye, x1, mod, norm_w)


def kernel(x_prompt, x_sample, c, cache_a_k, cache_a_v, cache_b_k, cache_b_v, c_ctx, norm_w, w_ada, b_ada, w_in, a_rpb,
           b_sink, c_pool_w, c_scale, w_branch_a, w_branch_b, w_branch_c, w_out, w_router, w_gate_e, w_up_e, w_down_e):
    batch, seq_c, _ = x_prompt.shape
    dec_batch, seq_l, _ = x_sample.shape
    past = cache_a_k.shape[2]

    cond = jnp.zeros((ADA_ROWS, D_MODEL), F32).at[0].set(c_ctx).at[1:1 + dec_batch].set(c)
    mod = _adaln(cond, w_ada, b_ada).reshape(DEPTH, ADA_ROWS, 6, D_MODEL)

    ctx_group = lambda i: 0
    lat_group_inproj = lambda i: 1 + i // (seq_l // INPROJ_TM)
    lat_group_outproj = lambda i: 1 + i // (seq_l // OUTPROJ_TM)
    lat_group_batch = lambda b: 1 + b

    cbk = cache_b_k.reshape(dec_batch, DEPTH, past, B_KV_DIM)
    cbv = cache_b_v.reshape(dec_batch, DEPTH, past, B_KV_DIM)

    x_c = x_prompt.reshape(batch * seq_c, D_MODEL)
    x_l = x_sample.reshape(dec_batch * seq_l, D_MODEL)
    w_in_bf = w_in.astype(BF16)
    wa, wb, wc, wo = (w.astype(BF16) for w in (w_branch_a, w_branch_b, w_branch_c, w_out))
    wr_t = jnp.swapaxes(w_router, 1, 2)
    pool_scale = c_scale.reshape(DEPTH, 1, POOL_DIM)
    caches = ()
    for l in range(DEPTH):
        p_c, g_c = _inproj(x_c, mod, norm_w, w_in_bf, l, ctx_group)
        o_c, caches = _ctx_mix(p_c, b_sink[l], c_pool_w, pool_scale, l, seq_c, caches)
        branches_c = tuple((o_c, k) for k in range(N_BRANCH))
        m_c = _branch_merge(branches_c, g_c, wa, wb, wc, l)
        x1_c, h2_c, lg_c = _outproj(m_c, x_c, mod, norm_w, wo, wr_t, l, ctx_group)
        xs_c, gate_c, rc_c = _dispatch(lg_c, h2_c, seq_c)

        p_l, g_l = _inproj(x_l, mod, norm_w, w_in_bf, l, lat_group_inproj)
        table = _rpb_table(a_rpb[l])
        o_a = _na_latent(p_l, cache_a_k, cache_a_v, table, l, seq_l)
        o_b = _sw_latent(p_l, cbk, cbv, b_sink[l], l, seq_l)
        o_p = _pool_latent(p_l, c_pool_w, pool_scale, l, seq_l)
        m_l = _branch_merge(((o_a, 0), (o_b, 0), (o_p, 0)), g_l, wa, wb, wc, l)
        x1_l, h2_l, lg_l = _outproj(m_l, x_l, mod, norm_w, wo, wr_t, l, lat_group_outproj)
        xs_l, gate_l, rc_l = _dispatch(lg_l, h2_l, seq_l)

        ye_c, ye_l = _experts(xs_c, xs_l, gate_c, gate_l, w_gate_e, w_up_e, w_down_e, l)
        x_c = _combine(rc_c, ye_c, x1_c, mod, norm_w, l, seq_c, ctx_group)
        x_l = _combine(rc_l, ye_l, x1_l, mod, norm_w, l, seq_l, lat_group_batch)

    y_prompt = x_c.reshape(batch, seq_c, D_MODEL)
    y_sample = x_l.reshape(dec_batch, seq_l, D_MODEL)
    new_a_k, new_a_v, new_b_k, new_b_v = caches
    a_shape = (batch, DEPTH, seq_c, NA_HEADS, NA_HEAD_DIM)
    b_shape = (batch, DEPTH, seq_c, SW_KV_HEADS, SW_HEAD_DIM)
    return (y_prompt, y_sample, new_a_k.reshape(a_shape), new_a_v.reshape(a_shape),
            new_b_k.reshape(b_shape), new_b_v.reshape(b_shape))
```
